```python
import math
import jax, jax.numpy as jnp
from jax import lax
import numpy as np

D_MODEL = 2048
BATCH = 4
SEQ = 4096
DEPTH = 4

N_MIXERS = 2
N_MLA_LAYERS = (DEPTH + N_MIXERS - 1) // N_MIXERS
N_DIL_LAYERS = DEPTH // N_MIXERS

ROPE_THETA = 500000.0
NORM_EPS = 1e-6
ATTN_BLOCK = 128

MLA_HEADS = 16
MLA_Q_RANK = 512
MLA_KV_RANK = 512
MLA_NOPE = 128
MLA_ROPE = 64
MLA_V = 128

DIL_GROUPS = ((128, 1), (512, 4), (2048, 16))
DIL_HEADS = 16
DIL_HEAD_DIM = 128
DIL_ROT = DIL_HEAD_DIM // 4

FFN_HIDDEN = 5632
CONV_WIDTH = 3

kernel_name = "hybrid_mla_dilated_convffn"


def rms_norm(x, g):
    xf = x.astype(jnp.float32)
    y = xf * lax.rsqrt(jnp.mean(xf * xf, axis=-1, keepdims=True) + NORM_EPS)
    return (y * g.astype(jnp.float32)).astype(x.dtype)


def rope(x, positions):
    r = x.shape[-1]
    inv_freq = ROPE_THETA ** (-jnp.arange(0, r, 2, dtype=jnp.float32) / r)
    ang = positions.astype(jnp.float32)[..., None] * inv_freq
    cos = jnp.cos(ang)[:, :, None, :]
    sin = jnp.sin(ang)[:, :, None, :]
    xf = x.astype(jnp.float32)
    x1, x2 = xf[..., : r // 2], xf[..., r // 2:]
    out = jnp.concatenate([x1 * cos - x2 * sin, x2 * cos + x1 * sin], axis=-1)
    return out.astype(x.dtype)


def partial_rope(x, positions):
    return jnp.concatenate([rope(x[..., :DIL_ROT], positions), x[..., DIL_ROT:]], axis=-1)


def dense_causal_attention(q, k, v, scale):
    B, S, H, Dk = q.shape
    nb = S // ATTN_BLOCK
    qb = q.reshape(B, nb, ATTN_BLOCK, H, Dk).transpose(1, 0, 2, 3, 4)
    k_idx = jnp.arange(S)
    starts = jnp.arange(nb) * ATTN_BLOCK

    def one_block(args):
        q_blk, start = args
        s = jnp.einsum('bqhd,bkhd->bhqk', q_blk, k, preferred_element_type=jnp.float32) * scale
        q_idx = start + jnp.arange(ATTN_BLOCK)
        mask = k_idx[None, :] <= q_idx[:, None]
        s = jnp.where(mask, s, -jnp.inf)
        p = jax.nn.softmax(s, axis=-1).astype(v.dtype)
        return jnp.einsum('bhqk,bkhd->bqhd', p, v)

    out = lax.map(one_block, (qb, starts))
    return out.transpose(1, 0, 2, 3, 4).reshape(B, S, H, v.shape[-1])


def mla_mixer(h, positions, wq_a, q_norm, wq_b, wkv_a, kv_norm, wkv_b, wo):
    B, S, _ = h.shape
    q = (rms_norm(h @ wq_a, q_norm) @ wq_b).reshape(B, S, MLA_HEADS, MLA_NOPE + MLA_ROPE)
    q_nope, q_pe = q[..., :MLA_NOPE], rope(q[..., MLA_NOPE:], positions)
    kv_a = h @ wkv_a
    c_kv = kv_a[..., :MLA_KV_RANK]
    k_pe = rope(kv_a[..., None, MLA_KV_RANK:], positions)
    kv = (rms_norm(c_kv, kv_norm) @ wkv_b).reshape(B, S, MLA_HEADS, MLA_NOPE + MLA_V)
    k_nope, v = kv[..., :MLA_NOPE], kv[..., MLA_NOPE:]
    q_full = jnp.concatenate([q_nope, q_pe], axis=-1)
    k_full = jnp.concatenate(
        [k_nope, jnp.broadcast_to(k_pe, (B, S, MLA_HEADS, MLA_ROPE))], axis=-1)
    o = dense_causal_attention(q_full, k_full, v, (MLA_NOPE + MLA_ROPE) ** -0.5)
    return o.reshape(B, S, MLA_HEADS * MLA_V) @ wo


def dilated_group_attention(q, k, v, window, dilation, scale):
    B, S, H, hd = q.shape
    span = window // dilation
    L = S // dilation
    nb = -(-L // ATTN_BLOCK)
    Lp = nb * ATTN_BLOCK
    Q = ATTN_BLOCK

    def to_residue(t):
        t = t.reshape(B, L, dilation, H, hd).transpose(0, 2, 1, 3, 4).reshape(B * dilation, L, H, hd)
        t = jnp.pad(t, ((0, 0), (0, Lp - L), (0, 0), (0, 0)))
        return t.reshape(B * dilation, nb, Q, H, hd)

    def with_prev(t):
        prev = jnp.pad(t[:, :-1], ((0, 0), (1, 0), (0, 0), (0, 0), (0, 0)))
        return jnp.concatenate([prev, t], axis=2)

    qr = to_residue(q)
    kb = with_prev(to_residue(k))
    vb = with_prev(to_residue(v))
    s = jnp.einsum('nbqhd,nbkhd->nbhqk', qr, kb, preferred_element_type=jnp.float32) * scale
    qi = jnp.arange(Q)[:, None]
    ki = jnp.arange(2 * Q)[None, :]
    dist = qi + Q - ki
    band = (dist >= 0) & (dist <= span)
    has_prev = (jnp.arange(nb) > 0)[:, None, None] | (ki >= Q)[None]
    mask = band[None] & has_prev
    s = jnp.where(mask[None, :, None], s, -jnp.inf)
    lse = jax.nn.logsumexp(s, axis=-1)
    p = jnp.exp(s - lse[..., None]).astype(v.dtype)
    o = jnp.einsum('nbhqk,nbkhd->nbqhd', p, vb)
    o = o.reshape(B * dilation, Lp, H, hd)[:, :L]
    o = o.reshape(B, dilation, L, H, hd).transpose(0, 2, 1, 3, 4).reshape(B, S, H, hd)
    lse = lse.transpose(0, 1, 3, 2).reshape(B * dilation, Lp, H)[:, :L]
    lse = lse.reshape(B, dilation, L, H).transpose(0, 2, 1, 3).reshape(B, S, H)
    return o, lse


def dilated_mixer(h, positions, w_in, wo):
    B, S, _ = h.shape
    G = len(DIL_GROUPS)
    qkv = (h @ w_in).reshape(B, S, G, 3, DIL_HEADS, DIL_HEAD_DIM)
    outs, lses = [], []
    for g, (window, dilation) in enumerate(DIL_GROUPS):
        q = partial_rope(qkv[:, :, g, 0], positions)
        k = partial_rope(qkv[:, :, g, 1], positions)
        v = qkv[:, :, g, 2]
        o, l = dilated_group_attention(q, k, v, window, dilation, DIL_HEAD_DIM ** -0.5)
        outs.append(o)
        lses.append(l)
    alpha = jax.nn.softmax(jnp.stack(lses), axis=0)
    o = jnp.einsum('gbsh,gbshd->bshd', alpha, jnp.stack(outs).astype(jnp.float32)).astype(h.dtype)
    return o.reshape(B, S, DIL_HEADS * DIL_HEAD_DIM) @ wo


def conv_ffn(h, w_up, conv_w, conv_b, w_down):
    S = h.shape[1]
    u = h @ w_up
    up = jnp.pad(u, ((0, 0), (CONV_WIDTH - 1, 0), (0, 0)))
    c = conv_b + up[:, 0:S] * conv_w[0]
    for j in range(1, CONV_WIDTH):
        c = c + up[:, j:j + S] * conv_w[j]
    gate, val = c[..., :FFN_HIDDEN], c[..., FFN_HIDDEN:]
    return (jax.nn.silu(gate) * val) @ w_down


def setup_inputs(seed: int = 0) -> dict:
    key = jax.random.key(seed)
    ks = iter(jax.random.split(key, 32))

    def w(shape, fan_in):
        return jax.random.normal(next(ks), shape, jnp.float32) * fan_in ** -0.5

    def gain(shape):
        return 1.0 + 0.01 * jax.random.normal(next(ks), shape, jnp.float32)

    NA, ND = N_MLA_LAYERS, N_DIL_LAYERS
    G = len(DIL_GROUPS)
    return {
        "x": jax.random.normal(next(ks), (BATCH, SEQ, D_MODEL), jnp.float32),
        "positions": jnp.broadcast_to(jnp.arange(SEQ, dtype=jnp.int32), (BATCH, SEQ)),
        "attn_norm": gain((DEPTH, D_MODEL)),
        "ffn_norm": gain((DEPTH, D_MODEL)),
        "final_norm": gain((D_MODEL,)),
        "mla_wq_a": w((NA, D_MODEL, MLA_Q_RANK), D_MODEL),
        "mla_q_norm": gain((NA, MLA_Q_RANK)),
        "mla_wq_b": w((NA, MLA_Q_RANK, MLA_HEADS * (MLA_NOPE + MLA_ROPE)), MLA_Q_RANK),
        "mla_wkv_a": w((NA, D_MODEL, MLA_KV_RANK + MLA_ROPE), D_MODEL),
        "mla_kv_norm": gain((NA, MLA_KV_RANK)),
        "mla_wkv_b": w((NA, MLA_KV_RANK, MLA_HEADS * (MLA_NOPE + MLA_V)), MLA_KV_RANK),
        "mla_wo": w((NA, MLA_HEADS * MLA_V, D_MODEL), MLA_HEADS * MLA_V),
        "dil_w_in": w((ND, D_MODEL, G * 3 * DIL_HEADS * DIL_HEAD_DIM), D_MODEL),
        "dil_wo": w((ND, DIL_HEADS * DIL_HEAD_DIM, D_MODEL), DIL_HEADS * DIL_HEAD_DIM),
        "ffn_w_up": w((DEPTH, D_MODEL, 2 * FFN_HIDDEN), D_MODEL),
        "ffn_conv_w": w((DEPTH, CONV_WIDTH, 2 * FFN_HIDDEN), CONV_WIDTH),
        "ffn_conv_b": 0.01 * jax.random.normal(next(ks), (DEPTH, 2 * FFN_HIDDEN), jnp.float32),
        "ffn_w_down": w((DEPTH, FFN_HIDDEN, D_MODEL), FFN_HIDDEN),
    }


def reference(x, positions, attn_norm, ffn_norm, final_norm,
              mla_wq_a, mla_q_norm, mla_wq_b, mla_wkv_a, mla_kv_norm, mla_wkv_b, mla_wo,
              dil_w_in, dil_wo,
              ffn_w_up, ffn_conv_w, ffn_conv_b, ffn_w_down):
    for i in range(DEPTH):
        h = rms_norm(x, attn_norm[i])
        j = i // N_MIXERS
        if i % N_MIXERS == 0:
            x = x + mla_mixer(h, positions, mla_wq_a[j], mla_q_norm[j], mla_wq_b[j],
                              mla_wkv_a[j], mla_kv_norm[j], mla_wkv_b[j], mla_wo[j])
        else:
            x = x + dilated_mixer(h, positions, dil_w_in[j], dil_wo[j])
        h = rms_norm(x, ffn_norm[i])
        x = x + conv_ffn(h, ffn_w_up[i], ffn_conv_w[i], ffn_conv_b[i], ffn_w_down[i])
    return rms_norm(x, final_norm)
```

```python
import functools

import jax
import jax.numpy as jnp
from jax import lax
from jax.experimental import pallas as pl
from jax.experimental.pallas import tpu as pltpu

F32 = jnp.float32
BF16 = jnp.bfloat16

D_MODEL = 2048
DEPTH = 4
N_MIXERS = 2
ROPE_THETA = 500000.0
NORM_EPS = 1e-6

MLA_HEADS = 16
MLA_Q_RANK = 512
MLA_KV_RANK = 512
MLA_NOPE = 128
MLA_ROPE = 64
MLA_V = 128

DIL_GROUPS = ((128, 1), (512, 4), (2048, 16))
DIL_HEADS = 16
DIL_HEAD_DIM = 128
DIL_ROT = DIL_HEAD_DIM // 4
DIL_BLOCK = 128

FFN_HIDDEN = 5632
CONV_WIDTH = 3

LANES = 128
SUBLANES = 8
VMEM_BYTES_V7X = 64 * 1024 * 1024
VMEM_LIMIT = VMEM_BYTES_V7X - 8 * 1024 * 1024

TM_PROJ = 1024
TN_PROJ = 1024
TM_SMALL = 512
TM_MERGE = 256
TQ_MLA = 512
TM_FFN = 512
TF_FFN = 512
CARRY_ROWS = SUBLANES


def _params(*sem):
    return pltpu.CompilerParams(dimension_semantics=sem, vmem_limit_bytes=VMEM_LIMIT)


def _rms(x, g):
    ms = jnp.mean(x * x, axis=-1, keepdims=True)
    return (x * lax.rsqrt(ms + NORM_EPS)) * g


def _dot(a, b):
    return jnp.dot(a, b, preferred_element_type=F32)


def _dot_nt(a, b):
    return lax.dot_general(a, b, (((1,), (1,)), ((), ())), preferred_element_type=F32)


def _rope_tables_kernel(pos_ref, fm_ref, fd_ref, cm_ref, sm_ref, tk_ref, cd_ref, sa_ref, sb_ref):
    pos = pos_ref[...]
    lane = lax.broadcasted_iota(jnp.int32, (1, LANES), 1)
    am = pos * fm_ref[...]
    cm = jnp.cos(am)
    sm = jnp.sin(am)
    cm_ref[...] = cm
    sm_ref[...] = sm
    tk_ref[...] = jnp.where(lane < MLA_ROPE, cm, sm)
    ad = pos * fd_ref[...]
    cd = jnp.cos(ad)
    sd = jnp.sin(ad)
    half = DIL_ROT // 2
    cd_ref[...] = jnp.where(lane < DIL_ROT, cd, 1.0)
    sa_ref[...] = jnp.where(lane < half, 0.0, jnp.where(lane < DIL_ROT, sd, 0.0))
    sb_ref[...] = jnp.where(lane < half, -sd, 0.0)


def _rope_tables(positions):
    m = positions.size
    pos = positions.reshape(m, 1).astype(F32)
    inv_m = ROPE_THETA ** (-jnp.arange(0, MLA_ROPE, 2, dtype=F32) / MLA_ROPE)
    inv_d = ROPE_THETA ** (-jnp.arange(0, DIL_ROT, 2, dtype=F32) / DIL_ROT)
    fm = jnp.tile(inv_m, LANES // inv_m.size).reshape(1, LANES)
    fd = jnp.tile(inv_d, LANES // inv_d.size).reshape(1, LANES)
    tm = min(m, 2048)
    row = pl.BlockSpec((tm, LANES), lambda i: (i, 0))
    vec = pl.BlockSpec((1, LANES), lambda i: (0, 0))
    tab = jax.ShapeDtypeStruct((m, LANES), F32)
    return pl.pallas_call(
        _rope_tables_kernel,
        grid=(m // tm,),
        in_specs=[pl.BlockSpec((tm, 1), lambda i: (i, 0)), vec, vec],
        out_specs=[row] * 6,
        out_shape=[tab] * 6,
        compiler_params=_params("parallel"),
        name="rope_tables",
    )(pos, fm, fd)


def _mla_down_kernel(x_ref, g_ref, w_ref, qn_ref, kvn_ref, tk_ref, ql_ref, ckv_ref, kpe_ref):
    h = _rms(x_ref[...], g_ref[...]).astype(BF16)
    acc = _dot(h, w_ref[...])
    ql_ref[...] = _rms(acc[:, :MLA_Q_RANK], qn_ref[...]).astype(BF16)
    ckv_ref[...] = _rms(acc[:, MLA_Q_RANK:MLA_Q_RANK + MLA_KV_RANK], kvn_ref[...]).astype(BF16)
    y = acc[:, MLA_Q_RANK + MLA_KV_RANK:] * tk_ref[...]
    z = y + pltpu.roll(y, MLA_ROPE, 1)
    lane = lax.broadcasted_iota(jnp.int32, z.shape, 1)
    kpe_ref[:, :LANES] = jnp.where(lane < MLA_ROPE, z, 0.0).astype(BF16)
    kpe_ref[:, LANES:] = jnp.where(lane < MLA_ROPE, 0.0, z).astype(BF16)


def _mla_down(x2, gain, w1, q_norm, kv_norm, tk):
    m = x2.shape[0]
    tm = min(m, TM_SMALL)
    n1 = w1.shape[1]
    return pl.pallas_call(
        _mla_down_kernel,
        grid=(m // tm,),
        in_specs=[
            pl.BlockSpec((tm, D_MODEL), lambda i: (i, 0)),
            pl.BlockSpec((1, D_MODEL), lambda i: (0, 0)),
            pl.BlockSpec((D_MODEL, n1), lambda i: (0, 0)),
            pl.BlockSpec((1, MLA_Q_RANK), lambda i: (0, 0)),
            pl.BlockSpec((1, MLA_KV_RANK), lambda i: (0, 0)),
            pl.BlockSpec((tm, LANES), lambda i: (i, 0)),
        ],
        out_specs=[
            pl.BlockSpec((tm, MLA_Q_RANK), lambda i: (i, 0)),
            pl.BlockSpec((tm, MLA_KV_RANK), lambda i: (i, 0)),
            pl.BlockSpec((tm, 2 * LANES), lambda i: (i, 0)),
        ],
        out_shape=[
            jax.ShapeDtypeStruct((m, MLA_Q_RANK), BF16),
            jax.ShapeDtypeStruct((m, MLA_KV_RANK), BF16),
            jax.ShapeDtypeStruct((m, 2 * LANES), BF16),
        ],
        compiler_params=_params("parallel"),
        name="mla_down",
    )(x2, gain, w1, q_norm, kv_norm, tk)


MLA_PAIRS = MLA_HEADS // 2
Q_PAIR_IN = 2 * MLA_NOPE + 2 * LANES
Q_PAIR_OUT = 2 * MLA_NOPE + LANES


def _mla_q_up_kernel(a_ref, w_ref, c_ref, s_ref, o_ref):
    a = a_ref[...]
    c = c_ref[...]
    s = s_ref[...]
    for p in range(MLA_PAIRS):
        acc = _dot(a, w_ref[:, p * Q_PAIR_IN:(p + 1) * Q_PAIR_IN])
        o_ref[:, p * Q_PAIR_OUT:p * Q_PAIR_OUT + 2 * MLA_NOPE] = acc[:, :2 * MLA_NOPE].astype(BF16)
        pe = acc[:, 2 * MLA_NOPE:2 * MLA_NOPE + LANES] * c + acc[:, 2 * MLA_NOPE + LANES:] * s
        o_ref[:, p * Q_PAIR_OUT + 2 * MLA_NOPE:(p + 1) * Q_PAIR_OUT] = pe.astype(BF16)


def _mla_q_up(q_lat, wq, cm, sm):
    m = q_lat.shape[0]
    tm = min(m, TM_SMALL)
    n_out = MLA_PAIRS * Q_PAIR_OUT
    return pl.pallas_call(
        _mla_q_up_kernel,
        grid=(m // tm,),
        in_specs=[
            pl.BlockSpec((tm, MLA_Q_RANK), lambda i: (i, 0)),
            pl.BlockSpec(wq.shape, lambda i: (0, 0)),
            pl.BlockSpec((tm, LANES), lambda i: (i, 0)),
            pl.BlockSpec((tm, LANES), lambda i: (i, 0)),
        ],
        out_specs=pl.BlockSpec((tm, n_out), lambda i: (i, 0)),
        out_shape=jax.ShapeDtypeStruct((m, n_out), BF16),
        compiler_params=_params("parallel"),
        name="mla_q_up",
    )(q_lat, wq, cm, sm)


def _matmul_bf16_kernel(a_ref, w_ref, o_ref):
    o_ref[...] = _dot(a_ref[...], w_ref[...]).astype(BF16)


def _mla_kv_up(ckv, wkv):
    m = ckv.shape[0]
    tm = min(m, TM_SMALL)
    n = wkv.shape[1]
    return pl.pallas_call(
        _matmul_bf16_kernel,
        grid=(m // tm,),
        in_specs=[
            pl.BlockSpec((tm, MLA_KV_RANK), lambda i: (i, 0)),
            pl.BlockSpec(wkv.shape, lambda i: (0, 0)),
        ],
        out_specs=pl.BlockSpec((tm, n), lambda i: (i, 0)),
        out_shape=jax.ShapeDtypeStruct((m, n), BF16),
        compiler_params=_params("parallel"),
        name="mla_kv_up",
    )(ckv, wkv)


def _mla_attn_kernel(q_ref, kn_ref, kpe_ref, v_ref, o_ref, m_ref, l_ref, acc_ref, *, tq, scale):
    qi = pl.program_id(2)
    q_pe = q_ref[:, 2 * MLA_NOPE:]
    row = lax.broadcasted_iota(jnp.int32, (tq, tq), 0)
    col = lax.broadcasted_iota(jnp.int32, (tq, tq), 1)
    causal = col <= row
    for hh in range(2):
        lanes = slice(hh * LANES, (hh + 1) * LANES)
        q = jnp.concatenate([q_ref[:, lanes], q_pe], axis=1)
        m_ref[...] = jnp.full(m_ref.shape, -jnp.inf, F32)
        l_ref[...] = jnp.zeros(l_ref.shape, F32)
        acc_ref[...] = jnp.zeros(acc_ref.shape, F32)

        def step(kb, masked):
            ks = pl.multiple_of(kb * tq, tq)
            k = jnp.concatenate([kn_ref[pl.ds(ks, tq), lanes], kpe_ref[pl.ds(ks, tq), lanes]], axis=1)
            s = _dot_nt(q, k)
            if masked:
                s = jnp.where(causal, s, -jnp.inf)
            m_prev = m_ref[...]
            m_next = jnp.maximum(m_prev, jnp.max(s, axis=1, keepdims=True))
            p = jnp.exp((s - pltpu.repeat(m_next, tq // LANES, 1)) * scale)
            alpha = jnp.exp((m_prev - m_next) * scale)
            l_ref[...] = alpha * l_ref[...] + jnp.sum(p, axis=1, keepdims=True)
            m_ref[...] = m_next
            acc_ref[...] = alpha * acc_ref[...] + _dot(p.astype(BF16), v_ref[pl.ds(ks, tq), lanes])

        def body(kb, carry):
            step(kb, False)
            return carry

        lax.fori_loop(0, qi, body, 0)
        step(qi, True)
        o_ref[:, lanes] = (acc_ref[...] / l_ref[...]).astype(BF16)


def _mla_attention(q, kv, kpe, b, s):
    tq = min(s, TQ_MLA)
    scale = (MLA_NOPE + MLA_ROPE) ** -0.5
    kern = functools.partial(_mla_attn_kernel, tq=tq, scale=scale)
    pair_w = 2 * LANES
    return pl.pallas_call(
        kern,
        grid=(b, MLA_PAIRS, s // tq),
        in_specs=[
            pl.BlockSpec((None, tq, Q_PAIR_OUT), lambda bi, p, qi: (bi, qi, p)),
            pl.BlockSpec((None, s, pair_w), lambda bi, p, qi: (bi, 0, p)),
            pl.BlockSpec((None, s, pair_w), lambda bi, p, qi: (bi, 0, 0)),
            pl.BlockSpec((None, s, pair_w), lambda bi, p, qi: (bi, 0, MLA_PAIRS + p)),
        ],
        out_specs=pl.BlockSpec((None, tq, pair_w), lambda bi, p, qi: (bi, qi, p)),
        out_shape=jax.ShapeDtypeStruct((b, s, MLA_HEADS * MLA_V), BF16),
        scratch_shapes=[
            pltpu.VMEM((tq, LANES), F32),
            pltpu.VMEM((tq, LANES), F32),
            pltpu.VMEM((tq, LANES), F32),
        ],
        compiler_params=_params("parallel", "parallel", "arbitrary"),
        name="mla_attention",
    )(q, kv, kpe, kv)


def _matmul_resid_kernel(a_ref, w_ref, r_ref, o_ref):
    o_ref[...] = r_ref[...] + _dot(a_ref[...], w_ref[...])


def _matmul_resid(a, w, resid):
    m, k = a.shape
    n = w.shape[1]
    tm = min(m, TM_SMALL)
    return pl.pallas_call(
        _matmul_resid_kernel,
        grid=(m // tm,),
        in_specs=[
            pl.BlockSpec((tm, k), lambda i: (i, 0)),
            pl.BlockSpec((k, n), lambda i: (0, 0)),
            pl.BlockSpec((tm, n), lambda i: (i, 0)),
        ],
        out_specs=pl.BlockSpec((tm, n), lambda i: (i, 0)),
        out_shape=jax.ShapeDtypeStruct((m, n), F32),
        compiler_params=_params("parallel"),
        name="out_proj_resid",
    )(a, w, resid)


def _dil_rope(xc, c, sa, sb):
    half = DIL_ROT // 2
    return xc * c + pltpu.roll(xc, half, 1) * sa + pltpu.roll(xc, LANES - half, 1) * sb


def _dil_qkv_kernel(x_ref, g_ref, w_ref, c_ref, sa_ref, sb_ref, o_ref, h_ref, acc_ref, *, d, tm, n_rope):
    j = pl.program_id(2)

    @pl.when(j == 0)
    def _():
        h_ref[...] = _rms(x_ref[...], g_ref[...]).astype(BF16)

    acc = _dot(h_ref[...], w_ref[...])
    heads = acc.shape[1] // LANES
    for t in range(heads):
        acc_ref[t] = acc[:, t * LANES:(t + 1) * LANES]
    rows = tm // d

    def emit(rope):
        for r in range(d):
            sl = pl.ds(r, rows, stride=d) if d > 1 else slice(None)
            if rope:
                c, sa, sb = c_ref[sl, :], sa_ref[sl, :], sb_ref[sl, :]
            for t in range(heads):
                a = acc_ref[t, sl, :]
                if rope:
                    a = _dil_rope(a, c, sa, sb)
                o_ref[r, :, t * LANES:(t + 1) * LANES] = a.astype(BF16)

    @pl.when(j < n_rope)
    def _():
        emit(True)

    @pl.when(j >= n_rope)
    def _():
        emit(False)


def _dil_qkv(x, gain, w_in, g, cd, sa, sb, d):
    b, s, _ = x.shape
    n = 3 * DIL_HEADS * DIL_HEAD_DIM
    tm = min(s, TM_PROJ)
    tn = TN_PROJ
    col0 = g * (n // tn)
    n_rope = 2 * DIL_HEADS * DIL_HEAD_DIM // tn
    kern = functools.partial(_dil_qkv_kernel, d=d, tm=tm, n_rope=n_rope)
    tab = pl.BlockSpec((None, tm, LANES), lambda bi, i, j: (bi, i, 0))
    return pl.pallas_call(
        kern,
        grid=(b, s // tm, n // tn),
        in_specs=[
            pl.BlockSpec((None, tm, D_MODEL), lambda bi, i, j: (bi, i, 0)),
            pl.BlockSpec((1, D_MODEL), lambda bi, i, j: (0, 0)),
            pl.BlockSpec((D_MODEL, tn), lambda bi, i, j: (0, col0 + j)),
            tab, tab, tab,
        ],
        out_specs=pl.BlockSpec((None, d, tm // d, tn), lambda bi, i, j: (bi, 0, i, j)),
        out_shape=jax.ShapeDtypeStruct((b, d, s // d, n), BF16),
        scratch_shapes=[pltpu.VMEM((tm, D_MODEL), BF16), pltpu.VMEM((tn // LANES, tm, LANES), F32)],
        compiler_params=_params("parallel", "parallel", "arbitrary"),
        name=f"dil_qkv_d{d}",
    )(x, gain, w_in, cd, sa, sb)


def _dil_attn_kernel(q_ref, kc_ref, kp_ref, vc_ref, vp_ref, o_ref, lse_ref, *, scale):
    jb = pl.program_id(2)
    qb = DIL_BLOCK
    qi = lax.broadcasted_iota(jnp.int32, (qb, qb), 0)
    ki = lax.broadcasted_iota(jnp.int32, (qb, qb), 1)
    lane = lax.broadcasted_iota(jnp.int32, (qb, LANES), 1)
    mask_c = ki <= qi
    mask_p = ki >= qi + jnp.where(jb > 0, 0, qb)
    lse_all = jnp.zeros((qb, LANES), F32)
    for h in range(DIL_HEADS):
        sl = slice(h * DIL_HEAD_DIM, (h + 1) * DIL_HEAD_DIM)
        q = q_ref[:, sl]
        sc = jnp.where(mask_c, _dot_nt(q, kc_ref[:, sl]) * scale, -jnp.inf)
        sp = jnp.where(mask_p, _dot_nt(q, kp_ref[:, sl]) * scale, -jnp.inf)
        m = jnp.maximum(jnp.max(sc, axis=1, keepdims=True), jnp.max(sp, axis=1, keepdims=True))
        ec = jnp.exp(sc - m)
        ep = jnp.exp(sp - m)
        l = jnp.sum(ec, axis=1, keepdims=True) + jnp.sum(ep, axis=1, keepdims=True)
        inv = 1.0 / l
        o = _dot((ec * inv).astype(BF16), vc_ref[:, sl]) + _dot((ep * inv).astype(BF16), vp_ref[:, sl])
        o_ref[:, sl] = o
        lse_all = jnp.where(lane == h, m + jnp.log(l), lse_all)
    lse_ref[...] = lse_all


def _dil_attention(qkv_g, d):
    b, _, l, _ = qkv_g.shape
    hw = DIL_HEADS * DIL_HEAD_DIM
    qb = DIL_BLOCK
    kern = functools.partial(_dil_attn_kernel, scale=DIL_HEAD_DIM ** -0.5)

    def cur(c):
        return pl.BlockSpec((None, None, qb, hw), lambda bi, r, jb: (bi, r, jb, c))

    def prev(c):
        return pl.BlockSpec((None, None, qb, hw), lambda bi, r, jb: (bi, r, jnp.maximum(jb - 1, 0), c))

    return pl.pallas_call(
        kern,
        grid=(b, d, l // qb),
        in_specs=[cur(0), cur(1), prev(1), cur(2), prev(2)],
        out_specs=[
            pl.BlockSpec((None, None, qb, hw), lambda bi, r, jb: (bi, r, jb, 0)),
            pl.BlockSpec((None, None, qb, LANES), lambda bi, r, jb: (bi, r, jb, 0)),
        ],
        out_shape=[
            jax.ShapeDtypeStruct((b, d, l, hw), F32),
            jax.ShapeDtypeStruct((b, d, l, LANES), F32),
        ],
        compiler_params=_params("parallel", "parallel", "arbitrary"),
        name=f"dil_attention_d{d}",
    )(qkv_g, qkv_g, qkv_g, qkv_g, qkv_g)


def _dil_merge_kernel(o0_ref, l0_ref, o1_ref, l1_ref, o2_ref, l2_ref, out_ref,
                      n1_ref, n2_ref, nl1_ref, nl2_ref, *, tm):
    d1 = DIL_GROUPS[1][1]
    d2 = DIL_GROUPS[2][1]
    for d, o_ref, l_ref, n_ref, nl_ref in ((d1, o1_ref, l1_ref, n1_ref, nl1_ref),
                                           (d2, o2_ref, l2_ref, n2_ref, nl2_ref)):
        for r in range(d):
            rows = pl.ds(r, tm // d, stride=d)
            nl_ref[rows, :] = l_ref[r]
            for h in range(DIL_HEADS):
                n_ref[h, rows, :] = o_ref[r, :, h * DIL_HEAD_DIM:(h + 1) * DIL_HEAD_DIM]
    for h in range(DIL_HEADS):
        sl = slice(h * DIL_HEAD_DIM, (h + 1) * DIL_HEAD_DIM)
        a0 = l0_ref[:, h:h + 1]
        a1 = nl1_ref[:, h:h + 1]
        a2 = nl2_ref[:, h:h + 1]
        mx = jnp.maximum(jnp.maximum(a0, a1), a2)
        w0 = jnp.exp(a0 - mx)
        w1 = jnp.exp(a1 - mx)
        w2 = jnp.exp(a2 - mx)
        den = w0 + w1 + w2
        o = (w0 / den) * o0_ref[:, sl] + (w1 / den) * n1_ref[h] + (w2 / den) * n2_ref[h]
        out_ref[:, sl] = o.astype(BF16)


def _dil_merge(outs, lses, b, s):
    hw = DIL_HEADS * DIL_HEAD_DIM
    tm = min(s, TM_MERGE)
    d1 = DIL_GROUPS[1][1]
    d2 = DIL_GROUPS[2][1]
    kern = functools.partial(_dil_merge_kernel, tm=tm)

    def grp(d, w):
        return pl.BlockSpec((None, d, tm // d, w), lambda bi, i: (bi, 0, i, 0))

    return pl.pallas_call(
        kern,
        grid=(b, s // tm),
        in_specs=[
            pl.BlockSpec((None, None, tm, hw), lambda bi, i: (bi, 0, i, 0)),
            pl.BlockSpec((None, None, tm, LANES), lambda bi, i: (bi, 0, i, 0)),
            grp(d1, hw), grp(d1, LANES), grp(d2, hw), grp(d2, LANES),
        ],
        out_specs=pl.BlockSpec((None, tm, hw), lambda bi, i: (bi, i, 0)),
        out_shape=jax.ShapeDtypeStruct((b, s, hw), BF16),
        scratch_shapes=[
            pltpu.VMEM((DIL_HEADS, tm, DIL_HEAD_DIM), F32), pltpu.VMEM((DIL_HEADS, tm, DIL_HEAD_DIM), F32),
            pltpu.VMEM((tm, LANES), F32), pltpu.VMEM((tm, LANES), F32),
        ],
        compiler_params=_params("parallel", "parallel"),
        name="dil_merge",
    )(outs[0], lses[0], outs[1], lses[1], outs[2], lses[2])


def _ffn_kernel(x_ref, g_ref, wg_ref, wv_ref, cwg_ref, cwv_ref, cbg_ref, cbv_ref, wd_ref, o_ref,
                h_ref, pg_ref, pv_ref, *, tm):
    i = pl.program_id(1)
    f = pl.program_id(2)

    @pl.when(f == 0)
    def _():
        x = x_ref[...]
        h_ref[...] = _rms(x, g_ref[...]).astype(BF16)
        o_ref[...] = x

    h = h_ref[...]
    row = lax.broadcasted_iota(jnp.int32, (tm, 1), 0)
    first = i == 0

    def conv(w_ref, cw_ref, cb_ref, prev_ref):
        u = _dot(h, w_ref[...])

        @pl.when(first)
        def _():
            prev_ref[f] = jnp.zeros(prev_ref.shape[1:], F32)

        prev = prev_ref[f]
        prev_ref[f] = u[tm - CARRY_ROWS:, :]
        p1 = prev[CARRY_ROWS - 1:CARRY_ROWS, :]
        p2 = prev[CARRY_ROWS - 2:CARRY_ROWS - 1, :]
        u1 = jnp.where(row == 0, p1, pltpu.roll(u, 1, 0))
        u2 = jnp.where(row == 0, p2, jnp.where(row == 1, p1, pltpu.roll(u, 2, 0)))
        cw = cw_ref[...]
        c = cb_ref[...] + u2 * cw[0:1, :]
        c = c + u1 * cw[1:2, :]
        return c + u * cw[2:3, :]

    gate = conv(wg_ref, cwg_ref, cbg_ref, pg_ref)
    val = conv(wv_ref, cwv_ref, cbv_ref, pv_ref)
    act = (gate * (1.0 / (1.0 + jnp.exp(-gate)))) * val
    o_ref[...] += _dot(act.astype(BF16), wd_ref[...])


def _ffn(x, gain, w_up, conv_w, conv_b, w_down):
    b, s, _ = x.shape
    tm = min(s, TM_FFN)
    tf = TF_FFN
    nf = FFN_HIDDEN // tf
    kern = functools.partial(_ffn_kernel, tm=tm)
    return pl.pallas_call(
        kern,
        grid=(b, s // tm, nf),
        in_specs=[
            pl.BlockSpec((None, tm, D_MODEL), lambda bi, i, f: (bi, i, 0)),
            pl.BlockSpec((1, D_MODEL), lambda bi, i, f: (0, 0)),
            pl.BlockSpec((D_MODEL, tf), lambda bi, i, f: (0, f)),
            pl.BlockSpec((D_MODEL, tf), lambda bi, i, f: (0, nf + f)),
            pl.BlockSpec((CONV_WIDTH, tf), lambda bi, i, f: (0, f)),
            pl.BlockSpec((CONV_WIDTH, tf), lambda bi, i, f: (0, nf + f)),
            pl.BlockSpec((1, tf), lambda bi, i, f: (0, f)),
            pl.BlockSpec((1, tf), lambda bi, i, f: (0, nf + f)),
            pl.BlockSpec((tf, D_MODEL), lambda bi, i, f: (f, 0)),
        ],
        out_specs=pl.BlockSpec((None, tm, D_MODEL), lambda bi, i, f: (bi, i, 0)),
        out_shape=jax.ShapeDtypeStruct(x.shape, F32),
        scratch_shapes=[
            pltpu.VMEM((tm, D_MODEL), BF16),
            pltpu.VMEM((nf, CARRY_ROWS, tf), F32),
            pltpu.VMEM((nf, CARRY_ROWS, tf), F32),
        ],
        compiler_params=_params("arbitrary", "arbitrary", "arbitrary"),
        name="conv_ffn",
    )(x, gain, w_up, w_up, conv_w, conv_w, conv_b, conv_b, w_down)


def _rmsnorm_kernel(x_ref, g_ref, o_ref):
    o_ref[...] = _rms(x_ref[...], g_ref[...])


def _rmsnorm(x2, gain):
    m, dm = x2.shape
    tm = min(m, TM_PROJ)
    return pl.pallas_call(
        _rmsnorm_kernel,
        grid=(m // tm,),
        in_specs=[pl.BlockSpec((tm, dm), lambda i: (i, 0)), pl.BlockSpec((1, dm), lambda i: (0, 0))],
        out_specs=pl.BlockSpec((tm, dm), lambda i: (i, 0)),
        out_shape=jax.ShapeDtypeStruct((m, dm), F32),
        compiler_params=_params("parallel"),
        name="final_norm",
    )(x2, gain)


def _rotate_half_cols(w):
    half = w.shape[-1] // 2
    return jnp.concatenate([-w[..., half:], w[..., :half]], axis=-1)


def _mla_weights(wq_a, wq_b, wkv_a, wkv_b):
    w_pe = wkv_a[:, MLA_KV_RANK:]
    w1 = jnp.concatenate([wq_a, wkv_a[:, :MLA_KV_RANK], w_pe, _rotate_half_cols(w_pe)], axis=1)
    qb = wq_b.reshape(MLA_Q_RANK, MLA_HEADS, MLA_NOPE + MLA_ROPE)
    nope = qb[:, :, :MLA_NOPE].reshape(MLA_Q_RANK, MLA_PAIRS, 2 * MLA_NOPE)
    pe = qb[:, :, MLA_NOPE:]
    rot = _rotate_half_cols(pe).reshape(MLA_Q_RANK, MLA_PAIRS, LANES)
    pe = pe.reshape(MLA_Q_RANK, MLA_PAIRS, LANES)
    wq = jnp.concatenate([nope, pe, rot], axis=2).reshape(MLA_Q_RANK, MLA_PAIRS * Q_PAIR_IN)
    kvb = wkv_b.reshape(MLA_KV_RANK, MLA_HEADS, MLA_NOPE + MLA_V)
    wkv = jnp.concatenate([kvb[:, :, :MLA_NOPE].reshape(MLA_KV_RANK, -1),
                           kvb[:, :, MLA_NOPE:].reshape(MLA_KV_RANK, -1)], axis=1)
    return w1.astype(BF16), wq.astype(BF16), wkv.astype(BF16)


def _mla_layer(x, gain, tabs, wq_a, q_norm, wq_b, wkv_a, kv_norm, wkv_b, wo):
    b, s, dm = x.shape
    cm, sm, tk = tabs
    w1, wq, wkv = _mla_weights(wq_a, wq_b, wkv_a, wkv_b)
    x2 = x.reshape(b * s, dm)
    q_lat, ckv, kpe = _mla_down(x2, gain.reshape(1, dm), w1, q_norm.reshape(1, -1), kv_norm.reshape(1, -1), tk)
    q = _mla_q_up(q_lat, wq, cm, sm)
    kv = _mla_kv_up(ckv, wkv)
    o = _mla_attention(q.reshape(b, s, -1), kv.reshape(b, s, -1), kpe.reshape(b, s, -1), b, s)
    return _matmul_resid(o.reshape(b * s, -1), wo.astype(BF16), x2).reshape(b, s, dm)


def _dil_layer(x, gain, tabs, w_in, wo):
    b, s, dm = x.shape
    cd, sa, sb = (t.reshape(b, s, LANES) for t in tabs)
    w_in = w_in.astype(BF16)
    outs, lses = [], []
    for g, (_, d) in enumerate(DIL_GROUPS):
        qkv_g = _dil_qkv(x, gain.reshape(1, dm), w_in, g, cd, sa, sb, d)
        o, lse = _dil_attention(qkv_g, d)
        outs.append(o)
        lses.append(lse)
    merged = _dil_merge(outs, lses, b, s)
    return _matmul_resid(merged.reshape(b * s, -1), wo.astype(BF16), x.reshape(b * s, dm)).reshape(b, s, dm)


def kernel(x, positions, attn_norm, ffn_norm, final_norm, mla_wq_a, mla_q_norm, mla_wq_b, mla_wkv_a,
           mla_kv_norm, mla_wkv_b, mla_wo, dil_w_in, dil_wo, ffn_w_up, ffn_conv_w, ffn_conv_b, ffn_w_down):
    b, s, dm = x.shape
    cm, sm, tk, cd, sa, sb = _rope_tables(positions)
    for i in range(DEPTH):
        j = i // N_MIXERS
        if i % N_MIXERS == 0:
            x = _mla_layer(x, attn_norm[i], (cm, sm, tk), mla_wq_a[j], mla_q_norm[j], mla_wq_b[j],
                           mla_wkv_a[j], mla_kv_norm[j], mla_wkv_b[j], mla_wo[j])
        else:
            x = _dil_layer(x, attn_norm[i], (cd, sa, sb), dil_w_in[j], dil_wo[j])
        x = _ffn(x, ffn_norm[i].reshape(1, dm), ffn_w_up[i].astype(BF16), ffn_conv_w[i],
                 ffn_conv_b[i].reshape(1, -1), ffn_w_down[i].astype(BF16))
    return _rmsnorm(x.reshape(b * s, dm), final_norm.reshape(1, dm)).reshape(b, s, dm)
```

```python
import functools

import jax
import jax.numpy as jnp
from jax import lax
from jax.experimental import pallas as pl
from jax.experimental.pallas import tpu as pltpu

F32 = jnp.float32
BF16 = jnp.bfloat16

D_MODEL = 2048
DEPTH = 4
N_MIXERS = 2
ROPE_THETA = 500000.0
NORM_EPS = 1e-6
LOG2_E = 1.4426950408889634

MLA_HEADS = 16
MLA_Q_RANK = 512
MLA_KV_RANK = 512
MLA_NOPE = 128
MLA_ROPE = 64
MLA_V = 128

DIL_GROUPS = ((128, 1), (512, 4), (2048, 16))
DIL_HEADS = 16
DIL_HEAD_DIM = 128
DIL_ROT = DIL_HEAD_DIM // 4
DIL_BLOCK = 128

FFN_HIDDEN = 5632
CONV_WIDTH = 3

LANES = 128
SUBLANES = 8
VMEM_BYTES_V7X = 64 * 1024 * 1024
VMEM_LIMIT = VMEM_BYTES_V7X - 8 * 1024 * 1024

TM_PROJ = 1024
TN_PROJ = 1024
TM_SMALL = 512
TM_MERGE = 256
TQ_MLA = 512
TM_FFN = 512
TF_FFN = 512
CARRY_ROWS = SUBLANES


def _params(*sem):
    return pltpu.CompilerParams(dimension_semantics=sem, vmem_limit_bytes=VMEM_LIMIT)


def _rms(x, g):
    ms = jnp.mean(x * x, axis=-1, keepdims=True)
    return (x * lax.rsqrt(ms + NORM_EPS)) * g


def _dot(a, b):
    return jnp.dot(a, b, preferred_element_type=F32)


def _dot_nt(a, b):
    return lax.dot_general(a, b, (((1,), (1,)), ((), ())), preferred_element_type=F32)


def _rope_tables_kernel(pos_ref, fm_ref, fd_ref, cm_ref, sm_ref, tk_ref, cd_ref, sa_ref, sb_ref):
    pos = pos_ref[...]
    lane = lax.broadcasted_iota(jnp.int32, (1, LANES), 1)
    am = pos * fm_ref[...]
    cm = jnp.cos(am)
    sm = jnp.sin(am)
    cm_ref[...] = cm
    sm_ref[...] = sm
    tk_ref[...] = jnp.where(lane < MLA_ROPE, cm, sm)
    ad = pos * fd_ref[...]
    cd = jnp.cos(ad)
    sd = jnp.sin(ad)
    half = DIL_ROT // 2
    cd_ref[...] = jnp.where(lane < DIL_ROT, cd, 1.0)
    sa_ref[...] = jnp.where(lane < half, 0.0, jnp.where(lane < DIL_ROT, sd, 0.0))
    sb_ref[...] = jnp.where(lane < half, -sd, 0.0)


def _rope_tables(positions):
    m = positions.size
    pos = positions.reshape(m, 1).astype(F32)
    inv_m = ROPE_THETA ** (-jnp.arange(0, MLA_ROPE, 2, dtype=F32) / MLA_ROPE)
    inv_d = ROPE_THETA ** (-jnp.arange(0, DIL_ROT, 2, dtype=F32) / DIL_ROT)
    fm = jnp.tile(inv_m, LANES // inv_m.size).reshape(1, LANES)
    fd = jnp.tile(inv_d, LANES // inv_d.size).reshape(1, LANES)
    tm = min(m, 2048)
    row = pl.BlockSpec((tm, LANES), lambda i: (i, 0))
    vec = pl.BlockSpec((1, LANES), lambda i: (0, 0))
    tab = jax.ShapeDtypeStruct((m, LANES), F32)
    return pl.pallas_call(
        _rope_tables_kernel,
        grid=(m // tm,),
        in_specs=[pl.BlockSpec((tm, 1), lambda i: (i, 0)), vec, vec],
        out_specs=[row] * 6,
        out_shape=[tab] * 6,
        compiler_params=_params("parallel"),
        name="rope_tables",
    )(pos, fm, fd)


def _mla_down_kernel(x_ref, g_ref, w_ref, qn_ref, kvn_ref, tk_ref, ql_ref, ckv_ref, kpe_ref):
    h = _rms(x_ref[...], g_ref[...]).astype(BF16)
    acc = _dot(h, w_ref[...])
    ql_ref[...] = _rms(acc[:, :MLA_Q_RANK], qn_ref[...]).astype(BF16)
    ckv_ref[...] = _rms(acc[:, MLA_Q_RANK:MLA_Q_RANK + MLA_KV_RANK], kvn_ref[...]).astype(BF16)
    y = acc[:, MLA_Q_RANK + MLA_KV_RANK:] * tk_ref[...]
    z = y + pltpu.roll(y, MLA_ROPE, 1)
    lane = lax.broadcasted_iota(jnp.int32, z.shape, 1)
    kpe_ref[:, :LANES] = jnp.where(lane < MLA_ROPE, z, 0.0).astype(BF16)
    kpe_ref[:, LANES:] = jnp.where(lane < MLA_ROPE, 0.0, z).astype(BF16)


def _mla_down(x2, gain, w1, q_norm, kv_norm, tk):
    m = x2.shape[0]
    tm = min(m, TM_SMALL)
    n1 = w1.shape[1]
    return pl.pallas_call(
        _mla_down_kernel,
        grid=(m // tm,),
        in_specs=[
            pl.BlockSpec((tm, D_MODEL), lambda i: (i, 0)),
            pl.BlockSpec((1, D_MODEL), lambda i: (0, 0)),
            pl.BlockSpec((D_MODEL, n1), lambda i: (0, 0)),
            pl.BlockSpec((1, MLA_Q_RANK), lambda i: (0, 0)),
            pl.BlockSpec((1, MLA_KV_RANK), lambda i: (0, 0)),
            pl.BlockSpec((tm, LANES), lambda i: (i, 0)),
        ],
        out_specs=[
            pl.BlockSpec((tm, MLA_Q_RANK), lambda i: (i, 0)),
            pl.BlockSpec((tm, MLA_KV_RANK), lambda i: (i, 0)),
            pl.BlockSpec((tm, 2 * LANES), lambda i: (i, 0)),
        ],
        out_shape=[
            jax.ShapeDtypeStruct((m, MLA_Q_RANK), BF16),
            jax.ShapeDtypeStruct((m, MLA_KV_RANK), BF16),
            jax.ShapeDtypeStruct((m, 2 * LANES), BF16),
        ],
        compiler_params=_params("parallel"),
        name="mla_down",
    )(x2, gain, w1, q_norm, kv_norm, tk)


MLA_PAIRS = MLA_HEADS // 2
Q_PAIR_IN = 2 * MLA_NOPE + 2 * LANES
Q_PAIR_OUT = 2 * MLA_NOPE + LANES


def _mla_q_up_kernel(a_ref, w_ref, c_ref, s_ref, o_ref):
    a = a_ref[...]
    c = c_ref[...]
    s = s_ref[...]
    for p in range(MLA_PAIRS):
        acc = _dot(a, w_ref[:, p * Q_PAIR_IN:(p + 1) * Q_PAIR_IN])
        o_ref[:, p * Q_PAIR_OUT:p * Q_PAIR_OUT + 2 * MLA_NOPE] = acc[:, :2 * MLA_NOPE].astype(BF16)
        pe = acc[:, 2 * MLA_NOPE:2 * MLA_NOPE + LANES] * c + acc[:, 2 * MLA_NOPE + LANES:] * s
        o_ref[:, p * Q_PAIR_OUT + 2 * MLA_NOPE:(p + 1) * Q_PAIR_OUT] = pe.astype(BF16)


def _mla_q_up(q_lat, wq, cm, sm):
    m = q_lat.shape[0]
    tm = min(m, TM_SMALL)
    n_out = MLA_PAIRS * Q_PAIR_OUT
    return pl.pallas_call(
        _mla_q_up_kernel,
        grid=(m // tm,),
        in_specs=[
            pl.BlockSpec((tm, MLA_Q_RANK), lambda i: (i, 0)),
            pl.BlockSpec(wq.shape, lambda i: (0, 0)),
            pl.BlockSpec((tm, LANES), lambda i: (i, 0)),
            pl.BlockSpec((tm, LANES), lambda i: (i, 0)),
        ],
        out_specs=pl.BlockSpec((tm, n_out), lambda i: (i, 0)),
        out_shape=jax.ShapeDtypeStruct((m, n_out), BF16),
        compiler_params=_params("parallel"),
        name="mla_q_up",
    )(q_lat, wq, cm, sm)


def _matmul_bf16_kernel(a_ref, w_ref, o_ref):
    o_ref[...] = _dot(a_ref[...], w_ref[...]).astype(BF16)


def _mla_kv_up(ckv, wkv):
    m = ckv.shape[0]
    tm = min(m, TM_SMALL)
    n = wkv.shape[1]
    return pl.pallas_call(
        _matmul_bf16_kernel,
        grid=(m // tm,),
        in_specs=[
            pl.BlockSpec((tm, MLA_KV_RANK), lambda i: (i, 0)),
            pl.BlockSpec(wkv.shape, lambda i: (0, 0)),
        ],
        out_specs=pl.BlockSpec((tm, n), lambda i: (i, 0)),
        out_shape=jax.ShapeDtypeStruct((m, n), BF16),
        compiler_params=_params("parallel"),
        name="mla_kv_up",
    )(ckv, wkv)


def _mla_attn_kernel(q_ref, kn_ref, kpe_ref, v_ref, o_ref, m_ref, l_ref, acc_ref, *, tq, scale):
    qi = pl.program_id(2)
    scale_log2e = scale * LOG2_E
    q_pe = q_ref[:, 2 * MLA_NOPE:]
    row = lax.broadcasted_iota(jnp.int32, (tq, tq), 0)
    col = lax.broadcasted_iota(jnp.int32, (tq, tq), 1)
    causal = col <= row
    for hh in range(2):
        lanes = slice(hh * LANES, (hh + 1) * LANES)
        q = jnp.concatenate([q_ref[:, lanes], q_pe], axis=1)
        m_ref[...] = jnp.full(m_ref.shape, -jnp.inf, F32)
        l_ref[...] = jnp.zeros(l_ref.shape, F32)
        acc_ref[...] = jnp.zeros(acc_ref.shape, F32)

        def step(kb, masked):
            ks = pl.multiple_of(kb * tq, tq)
            k = jnp.concatenate([kn_ref[pl.ds(ks, tq), lanes], kpe_ref[pl.ds(ks, tq), lanes]], axis=1)
            s = _dot_nt(q, k)
            if masked:
                s = jnp.where(causal, s, -jnp.inf)
            m_prev = m_ref[...]
            m_next = jnp.maximum(m_prev, jnp.max(s, axis=1, keepdims=True))
            p = jnp.exp2((s - pltpu.repeat(m_next, tq // LANES, 1)) * scale_log2e)
            alpha = jnp.exp2((m_prev - m_next) * scale_log2e)
            l_ref[...] = alpha * l_ref[...] + jnp.sum(p, axis=1, keepdims=True)
            m_ref[...] = m_next
            acc_ref[...] = alpha * acc_ref[...] + _dot(p.astype(BF16), v_ref[pl.ds(ks, tq), lanes])

        def pair(k2, carry):
            step(2 * k2, False)
            step(2 * k2 + 1, False)
            return carry

        lax.fori_loop(0, qi // 2, pair, 0)

        @pl.when(qi % 2 == 1)
        def _():
            step(qi - 1, False)
            step(qi, True)

        @pl.when(qi % 2 == 0)
        def _():
            step(qi, True)

        o_ref[:, lanes] = (acc_ref[...] / l_ref[...]).astype(BF16)


def _mla_attention(q, kv, kpe, b, s):
    tq = min(s, TQ_MLA)
    scale = (MLA_NOPE + MLA_ROPE) ** -0.5
    kern = functools.partial(_mla_attn_kernel, tq=tq, scale=scale)
    pair_w = 2 * LANES
    return pl.pallas_call(
        kern,
        grid=(b, MLA_PAIRS, s // tq),
        in_specs=[
            pl.BlockSpec((None, tq, Q_PAIR_OUT), lambda bi, p, qi: (bi, qi, p)),
            pl.BlockSpec((None, s, pair_w), lambda bi, p, qi: (bi, 0, p)),
            pl.BlockSpec((None, s, pair_w), lambda bi, p, qi: (bi, 0, 0)),
            pl.BlockSpec((None, s, pair_w), lambda bi, p, qi: (bi, 0, MLA_PAIRS + p)),
        ],
        out_specs=pl.BlockSpec((None, tq, pair_w), lambda bi, p, qi: (bi, qi, p)),
        out_shape=jax.ShapeDtypeStruct((b, s, MLA_HEADS * MLA_V), BF16),
        scratch_shapes=[
            pltpu.VMEM((tq, LANES), F32),
            pltpu.VMEM((tq, LANES), F32),
            pltpu.VMEM((tq, LANES), F32),
        ],
        compiler_params=_params("parallel", "parallel", "arbitrary"),
        name="mla_attention",
    )(q, kv, kpe, kv)


def _matmul_resid_kernel(a_ref, w_ref, r_ref, o_ref):
    o_ref[...] = r_ref[...] + _dot(a_ref[...], w_ref[...])


def _matmul_resid(a, w, resid):
    m, k = a.shape
    n = w.shape[1]
    tm = min(m, TM_SMALL)
    return pl.pallas_call(
        _matmul_resid_kernel,
        grid=(m // tm,),
        in_specs=[
            pl.BlockSpec((tm, k), lambda i: (i, 0)),
            pl.BlockSpec((k, n), lambda i: (0, 0)),
            pl.BlockSpec((tm, n), lambda i: (i, 0)),
        ],
        out_specs=pl.BlockSpec((tm, n), lambda i: (i, 0)),
        out_shape=jax.ShapeDtypeStruct((m, n), F32),
        compiler_params=_params("parallel"),
        name="out_proj_resid",
    )(a, w, resid)


def _dil_rope(xc, c, sa, sb):
    half = DIL_ROT // 2
    return xc * c + pltpu.roll(xc, half, 1) * sa + pltpu.roll(xc, LANES - half, 1) * sb


def _dil_qkv_kernel(x_ref, g_ref, w_ref, c_ref, sa_ref, sb_ref, o_ref, h_ref, acc_ref, *, d, tm, n_rope):
    j = pl.program_id(2)

    @pl.when(j == 0)
    def _():
        h_ref[...] = _rms(x_ref[...], g_ref[...]).astype(BF16)

    acc = _dot(h_ref[...], w_ref[...])
    heads = acc.shape[1] // LANES
    for t in range(heads):
        acc_ref[t] = acc[:, t * LANES:(t + 1) * LANES]
    rows = tm // d

    def emit(rope):
        for r in range(d):
            sl = pl.ds(r, rows, stride=d) if d > 1 else slice(None)
            if rope:
                c, sa, sb = c_ref[sl, :], sa_ref[sl, :], sb_ref[sl, :]
            for t in range(heads):
                a = acc_ref[t, sl, :]
                if rope:
                    a = _dil_rope(a, c, sa, sb)
                o_ref[r, :, t * LANES:(t + 1) * LANES] = a.astype(BF16)

    @pl.when(j < n_rope)
    def _():
        emit(True)

    @pl.when(j >= n_rope)
    def _():
        emit(False)


def _dil_qkv(x, gain, w_in, g, cd, sa, sb, d):
    b, s, _ = x.shape
    n = 3 * DIL_HEADS * DIL_HEAD_DIM
    tm = min(s, TM_PROJ)
    tn = TN_PROJ
    col0 = g * (n // tn)
    n_rope = 2 * DIL_HEADS * DIL_HEAD_DIM // tn
    kern = functools.partial(_dil_qkv_kernel, d=d, tm=tm, n_rope=n_rope)
    tab = pl.BlockSpec((None, tm, LANES), lambda bi, i, j: (bi, i, 0))
    return pl.pallas_call(
        kern,
        grid=(b, s // tm, n // tn),
        in_specs=[
            pl.BlockSpec((None, tm, D_MODEL), lambda bi, i, j: (bi, i, 0)),
            pl.BlockSpec((1, D_MODEL), lambda bi, i, j: (0, 0)),
            pl.BlockSpec((D_MODEL, tn), lambda bi, i, j: (0, col0 + j)),
            tab, tab, tab,
        ],
        out_specs=pl.BlockSpec((None, d, tm // d, tn), lambda bi, i, j: (bi, 0, i, j)),
        out_shape=jax.ShapeDtypeStruct((b, d, s // d, n), BF16),
        scratch_shapes=[pltpu.VMEM((tm, D_MODEL), BF16), pltpu.VMEM((tn // LANES, tm, LANES), F32)],
        compiler_params=_params("parallel", "parallel", "arbitrary"),
        name=f"dil_qkv_d{d}",
    )(x, gain, w_in, cd, sa, sb)


def _dil_attn_kernel(q_ref, kc_ref, kp_ref, vc_ref, vp_ref, o_ref, lse_ref, s_ref, p_ref, *, scale):
    jb = pl.program_id(2)
    qb = DIL_BLOCK
    qi = lax.broadcasted_iota(jnp.int32, (qb, 2 * qb), 0)
    ki = lax.broadcasted_iota(jnp.int32, (qb, 2 * qb), 1)
    first_key = jnp.where(jb > 0, qi, qb)
    band = jnp.logical_and(ki >= first_key, ki <= qi + qb)
    for h in range(DIL_HEADS):
        sl = slice(h * DIL_HEAD_DIM, (h + 1) * DIL_HEAD_DIM)
        k2 = jnp.concatenate([kp_ref[:, sl], kc_ref[:, sl]], axis=0)
        s_ref[h] = _dot_nt(q_ref[:, sl], k2)
    s = jnp.where(band[None], s_ref[...] * scale, -jnp.inf)
    m = jnp.max(s, axis=-1, keepdims=True)
    e = jnp.exp(s - m)
    l = jnp.sum(e, axis=-1, keepdims=True)
    p_ref[...] = (e * (1.0 / l)).astype(BF16)
    lse = m + jnp.log(l)
    lane = lax.broadcasted_iota(jnp.int32, (qb, LANES), 1)
    lse_all = jnp.zeros((qb, LANES), F32)
    for h in range(DIL_HEADS):
        sl = slice(h * DIL_HEAD_DIM, (h + 1) * DIL_HEAD_DIM)
        v2 = jnp.concatenate([vp_ref[:, sl], vc_ref[:, sl]], axis=0)
        o_ref[:, sl] = _dot(p_ref[h], v2)
        lse_all = jnp.where(lane == h, lse[h], lse_all)
    lse_ref[...] = lse_all


def _dil_attention(qkv_g, d):
    b, _, l, _ = qkv_g.shape
    hw = DIL_HEADS * DIL_HEAD_DIM
    qb = DIL_BLOCK
    kern = functools.partial(_dil_attn_kernel, scale=DIL_HEAD_DIM ** -0.5)

    def cur(c):
        return pl.BlockSpec((None, None, qb, hw), lambda bi, r, jb: (bi, r, jb, c))

    def prev(c):
        return pl.BlockSpec((None, None, qb, hw), lambda bi, r, jb: (bi, r, jnp.maximum(jb - 1, 0), c))

    return pl.pallas_call(
        kern,
        grid=(b, d, l // qb),
        in_specs=[cur(0), cur(1), prev(1), cur(2), prev(2)],
        out_specs=[
            pl.BlockSpec((None, None, qb, hw), lambda bi, r, jb: (bi, r, jb, 0)),
            pl.BlockSpec((None, None, qb, LANES), lambda bi, r, jb: (bi, r, jb, 0)),
        ],
        out_shape=[
            jax.ShapeDtypeStruct((b, d, l, hw), F32),
            jax.ShapeDtypeStruct((b, d, l, LANES), F32),
        ],
        scratch_shapes=[
            pltpu.VMEM((DIL_HEADS, qb, 2 * qb), F32),
            pltpu.VMEM((DIL_HEADS, qb, 2 * qb), BF16),
        ],
        compiler_params=_params("parallel", "parallel", "arbitrary"),
        name=f"dil_attention_d{d}",
    )(qkv_g, qkv_g, qkv_g, qkv_g, qkv_g)


def _dil_merge_kernel(o0_ref, l0_ref, o1_ref, l1_ref, o2_ref, l2_ref, out_ref,
                      n1_ref, n2_ref, nl1_ref, nl2_ref, *, tm):
    d1 = DIL_GROUPS[1][1]
    d2 = DIL_GROUPS[2][1]
    for d, o_ref, l_ref, n_ref, nl_ref in ((d1, o1_ref, l1_ref, n1_ref, nl1_ref),
                                           (d2, o2_ref, l2_ref, n2_ref, nl2_ref)):
        for r in range(d):
            rows = pl.ds(r, tm // d, stride=d)
            nl_ref[rows, :] = l_ref[r]
            for h in range(DIL_HEADS):
                n_ref[h, rows, :] = o_ref[r, :, h * DIL_HEAD_DIM:(h + 1) * DIL_HEAD_DIM]
    for h in range(DIL_HEADS):
        sl = slice(h * DIL_HEAD_DIM, (h + 1) * DIL_HEAD_DIM)
        a0 = l0_ref[:, h:h + 1]
        a1 = nl1_ref[:, h:h + 1]
        a2 = nl2_ref[:, h:h + 1]
        mx = jnp.maximum(jnp.maximum(a0, a1), a2)
        w0 = jnp.exp(a0 - mx)
        w1 = jnp.exp(a1 - mx)
        w2 = jnp.exp(a2 - mx)
        den = w0 + w1 + w2
        o = (w0 / den) * o0_ref[:, sl] + (w1 / den) * n1_ref[h] + (w2 / den) * n2_ref[h]
        out_ref[:, sl] = o.astype(BF16)


def _dil_merge(outs, lses, b, s):
    hw = DIL_HEADS * DIL_HEAD_DIM
    tm = min(s, TM_MERGE)
    d1 = DIL_GROUPS[1][1]
    d2 = DIL_GROUPS[2][1]
    kern = functools.partial(_dil_merge_kernel, tm=tm)

    def grp(d, w):
        return pl.BlockSpec((None, d, tm // d, w), lambda bi, i: (bi, 0, i, 0))

    return pl.pallas_call(
        kern,
        grid=(b, s // tm),
        in_specs=[
            pl.BlockSpec((None, None, tm, hw), lambda bi, i: (bi, 0, i, 0)),
            pl.BlockSpec((None, None, tm, LANES), lambda bi, i: (bi, 0, i, 0)),
            grp(d1, hw), grp(d1, LANES), grp(d2, hw), grp(d2, LANES),
        ],
        out_specs=pl.BlockSpec((None, tm, hw), lambda bi, i: (bi, i, 0)),
        out_shape=jax.ShapeDtypeStruct((b, s, hw), BF16),
        scratch_shapes=[
            pltpu.VMEM((DIL_HEADS, tm, DIL_HEAD_DIM), F32), pltpu.VMEM((DIL_HEADS, tm, DIL_HEAD_DIM), F32),
            pltpu.VMEM((tm, LANES), F32), pltpu.VMEM((tm, LANES), F32),
        ],
        compiler_params=_params("parallel", "parallel"),
        name="dil_merge",
    )(outs[0], lses[0], outs[1], lses[1], outs[2], lses[2])


def _ffn_kernel(x_ref, g_ref, wg_ref, wv_ref, cwg_ref, cwv_ref, cbg_ref, cbv_ref, wd_ref, o_ref,
                h_ref, pg_ref, pv_ref, *, tm):
    i = pl.program_id(1)
    f = pl.program_id(2)

    @pl.when(f == 0)
    def _():
        x = x_ref[...]
        h_ref[...] = _rms(x, g_ref[...]).astype(BF16)
        o_ref[...] = x

    h = h_ref[...]
    row = lax.broadcasted_iota(jnp.int32, (tm, 1), 0)
    first = i == 0

    def conv(w_ref, cw_ref, cb_ref, prev_ref):
        u = _dot(h, w_ref[...])

        @pl.when(first)
        def _():
            prev_ref[f] = jnp.zeros(prev_ref.shape[1:], F32)

        prev = prev_ref[f]
        prev_ref[f] = u[tm - CARRY_ROWS:, :]
        p1 = prev[CARRY_ROWS - 1:CARRY_ROWS, :]
        p2 = prev[CARRY_ROWS - 2:CARRY_ROWS - 1, :]
        u1 = jnp.where(row == 0, p1, pltpu.roll(u, 1, 0))
        u2 = jnp.where(row == 0, p2, jnp.where(row == 1, p1, pltpu.roll(u, 2, 0)))
        cw = cw_ref[...]
        c = cb_ref[...] + u2 * cw[0:1, :]
        c = c + u1 * cw[1:2, :]
        return c + u * cw[2:3, :]

    gate = conv(wg_ref, cwg_ref, cbg_ref, pg_ref)
    val = conv(wv_ref, cwv_ref, cbv_ref, pv_ref)
    act = (gate * (1.0 / (1.0 + jnp.exp(-gate)))) * val
    o_ref[...] += _dot(act.astype(BF16), wd_ref[...])


def _ffn(x, gain, w_up, conv_w, conv_b, w_down):
    b, s, _ = x.shape
    tm = min(s, TM_FFN)
    tf = TF_FFN
    nf = FFN_HIDDEN // tf
    kern = functools.partial(_ffn_kernel, tm=tm)
    return pl.pallas_call(
        kern,
        grid=(b, s // tm, nf),
        in_specs=[
            pl.BlockSpec((None, tm, D_MODEL), lambda bi, i, f: (bi, i, 0)),
            pl.BlockSpec((1, D_MODEL), lambda bi, i, f: (0, 0)),
            pl.BlockSpec((D_MODEL, tf), lambda bi, i, f: (0, f)),
            pl.BlockSpec((D_MODEL, tf), lambda bi, i, f: (0, nf + f)),
            pl.BlockSpec((CONV_WIDTH, tf), lambda bi, i, f: (0, f)),
            pl.BlockSpec((CONV_WIDTH, tf), lambda bi, i, f: (0, nf + f)),
            pl.BlockSpec((1, tf), lambda bi, i, f: (0, f)),
            pl.BlockSpec((1, tf), lambda bi, i, f: (0, nf + f)),
            pl.BlockSpec((tf, D_MODEL), lambda bi, i, f: (f, 0)),
        ],
        out_specs=pl.BlockSpec((None, tm, D_MODEL), lambda bi, i, f: (bi, i, 0)),
        out_shape=jax.ShapeDtypeStruct(x.shape, F32),
        scratch_shapes=[
            pltpu.VMEM((tm, D_MODEL), BF16),
            pltpu.VMEM((nf, CARRY_ROWS, tf), F32),
            pltpu.VMEM((nf, CARRY_ROWS, tf), F32),
        ],
        compiler_params=_params("arbitrary", "arbitrary", "arbitrary"),
        name="conv_ffn",
    )(x, gain, w_up, w_up, conv_w, conv_w, conv_b, conv_b, w_down)


def _rmsnorm_kernel(x_ref, g_ref, o_ref):
    o_ref[...] = _rms(x_ref[...], g_ref[...])


def _rmsnorm(x2, gain):
    m, dm = x2.shape
    tm = min(m, TM_PROJ)
    return pl.pallas_call(
        _rmsnorm_kernel,
        grid=(m // tm,),
        in_specs=[pl.BlockSpec((tm, dm), lambda i: (i, 0)), pl.BlockSpec((1, dm), lambda i: (0, 0))],
        out_specs=pl.BlockSpec((tm, dm), lambda i: (i, 0)),
        out_shape=jax.ShapeDtypeStruct((m, dm), F32),
        compiler_params=_params("parallel"),
        name="final_norm",
    )(x2, gain)


def _rotate_half_cols(w):
    half = w.shape[-1] // 2
    return jnp.concatenate([-w[..., half:], w[..., :half]], axis=-1)


def _mla_weights(wq_a, wq_b, wkv_a, wkv_b):
    w_pe = wkv_a[:, MLA_KV_RANK:]
    w1 = jnp.concatenate([wq_a, wkv_a[:, :MLA_KV_RANK], w_pe, _rotate_half_cols(w_pe)], axis=1)
    qb = wq_b.reshape(MLA_Q_RANK, MLA_HEADS, MLA_NOPE + MLA_ROPE)
    nope = qb[:, :, :MLA_NOPE].reshape(MLA_Q_RANK, MLA_PAIRS, 2 * MLA_NOPE)
    pe = qb[:, :, MLA_NOPE:]
    rot = _rotate_half_cols(pe).reshape(MLA_Q_RANK, MLA_PAIRS, LANES)
    pe = pe.reshape(MLA_Q_RANK, MLA_PAIRS, LANES)
    wq = jnp.concatenate([nope, pe, rot], axis=2).reshape(MLA_Q_RANK, MLA_PAIRS * Q_PAIR_IN)
    kvb = wkv_b.reshape(MLA_KV_RANK, MLA_HEADS, MLA_NOPE + MLA_V)
    wkv = jnp.concatenate([kvb[:, :, :MLA_NOPE].reshape(MLA_KV_RANK, -1),
                           kvb[:, :, MLA_NOPE:].reshape(MLA_KV_RANK, -1)], axis=1)
    return w1.astype(BF16), wq.astype(BF16), wkv.astype(BF16)


def _mla_layer(x, gain, tabs, wq_a, q_norm, wq_b, wkv_a, kv_norm, wkv_b, wo):
    b, s, dm = x.shape
    cm, sm, tk = tabs
    w1, wq, wkv = _mla_weights(wq_a, wq_b, wkv_a, wkv_b)
    x2 = x.reshape(b * s, dm)
    q_lat, ckv, kpe = _mla_down(x2, gain.reshape(1, dm), w1, q_norm.reshape(1, -1), kv_norm.reshape(1, -1), tk)
    q = _mla_q_up(q_lat, wq, cm, sm)
    kv = _mla_kv_up(ckv, wkv)
    o = _mla_attention(q.reshape(b, s, -1), kv.reshape(b, s, -1), kpe.reshape(b, s, -1), b, s)
    return _matmul_resid(o.reshape(b * s, -1), wo.astype(BF16), x2).reshape(b, s, dm)


def _dil_layer(x, gain, tabs, w_in, wo):
    b, s, dm = x.shape
    cd, sa, sb = (t.reshape(b, s, LANES) for t in tabs)
    w_in = w_in.astype(BF16)
    outs, lses = [], []
    for g, (_, d) in enumerate(DIL_GROUPS):
        qkv_g = _dil_qkv(x, gain.reshape(1, dm), w_in, g, cd, sa, sb, d)
        o, lse = _dil_attention(qkv_g, d)
        outs.append(o)
        lses.append(lse)
    merged = _dil_merge(outs, lses, b, s)
    return _matmul_resid(merged.reshape(b * s, -1), wo.astype(BF16), x.reshape(b * s, dm)).reshape(b, s, dm)


def kernel(x, positions, attn_norm, ffn_norm, final_norm, mla_wq_a, mla_q_norm, mla_wq_b, mla_wkv_a,
           mla_kv_norm, mla_wkv_b, mla_wo, dil_w_in, dil_wo, ffn_w_up, ffn_conv_w, ffn_conv_b, ffn_w_down):
    b, s, dm = x.shape
    cm, sm, tk, cd, sa, sb = _rope_tables(positions)
    for i in range(DEPTH):
        j = i // N_MIXERS
        if i % N_MIXERS == 0:
            x = _mla_layer(x, attn_norm[i], (cm, sm, tk), mla_wq_a[j], mla_q_norm[j], mla_wq_b[j],
                           mla_wkv_a[j], mla_kv_norm[j], mla_wkv_b[j], mla_wo[j])
        else:
            x = _dil_layer(x, attn_norm[i], (cd, sa, sb), dil_w_in[j], dil_wo[j])
        x = _ffn(x, ffn_norm[i].reshape(1, dm), ffn_w_up[i].astype(BF16), ffn_conv_w[i],
                 ffn_conv_b[i].reshape(1, -1), ffn_w_down[i].astype(BF16))
    return _rmsnorm(x.reshape(b * s, dm), final_norm.reshape(1, dm)).reshape(b, s, dm)
```

```python
import functools

import jax
import jax.numpy as jnp
from jax import lax
from jax.experimental import pallas as pl
from jax.experimental.pallas import tpu as pltpu

F32 = jnp.float32
BF16 = jnp.bfloat16

D_MODEL = 2048
DEPTH = 4
N_MIXERS = 2
ROPE_THETA = 500000.0
NORM_EPS = 1e-6
LOG2_E = 1.4426950408889634

MLA_HEADS = 16
MLA_Q_RANK = 512
MLA_KV_RANK = 512
MLA_NOPE = 128
MLA_ROPE = 64
MLA_V = 128

DIL_GROUPS = ((128, 1), (512, 4), (2048, 16))
DIL_HEADS = 16
DIL_HEAD_DIM = 128
DIL_ROT = DIL_HEAD_DIM // 4
DIL_BLOCK = 128

FFN_HIDDEN = 5632
CONV_WIDTH = 3

LANES = 128
SUBLANES = 8
VMEM_BYTES_V7X = 64 * 1024 * 1024
VMEM_LIMIT = VMEM_BYTES_V7X - 8 * 1024 * 1024

TM_PROJ = 1024
TN_PROJ = 1024
TM_SMALL = 512
TM_MERGE = 256
TQ_MLA = 512
TM_FFN = 512
TF_FFN = 512
CARRY_ROWS = SUBLANES


def _params(*sem):
    return pltpu.CompilerParams(dimension_semantics=sem, vmem_limit_bytes=VMEM_LIMIT)


def _rms(x, g):
    ms = jnp.mean(x * x, axis=-1, keepdims=True)
    return (x * lax.rsqrt(ms + NORM_EPS)) * g


def _dot(a, b):
    return jnp.dot(a, b, preferred_element_type=F32)


def _dot_nt(a, b):
    return lax.dot_general(a, b, (((1,), (1,)), ((), ())), preferred_element_type=F32)


def _rope_tables_kernel(pos_ref, fm_ref, fd_ref, cm_ref, sm_ref, tk_ref, cd_ref, sa_ref, sb_ref):
    pos = pos_ref[...]
    lane = lax.broadcasted_iota(jnp.int32, (1, LANES), 1)
    am = pos * fm_ref[...]
    cm = jnp.cos(am)
    sm = jnp.sin(am)
    cm_ref[...] = cm
    sm_ref[...] = sm
    tk_ref[...] = jnp.where(lane < MLA_ROPE, cm, sm)
    ad = pos * fd_ref[...]
    cd = jnp.cos(ad)
    sd = jnp.sin(ad)
    half = DIL_ROT // 2
    cd_ref[...] = jnp.where(lane < DIL_ROT, cd, 1.0)
    sa_ref[...] = jnp.where(lane < half, 0.0, jnp.where(lane < DIL_ROT, sd, 0.0))
    sb_ref[...] = jnp.where(lane < half, -sd, 0.0)


def _rope_tables(positions):
    m = positions.size
    pos = positions.reshape(m, 1).astype(F32)
    inv_m = ROPE_THETA ** (-jnp.arange(0, MLA_ROPE, 2, dtype=F32) / MLA_ROPE)
    inv_d = ROPE_THETA ** (-jnp.arange(0, DIL_ROT, 2, dtype=F32) / DIL_ROT)
    fm = jnp.tile(inv_m, LANES // inv_m.size).reshape(1, LANES)
    fd = jnp.tile(inv_d, LANES // inv_d.size).reshape(1, LANES)
    tm = min(m, 2048)
    row = pl.BlockSpec((tm, LANES), lambda i: (i, 0))
    vec = pl.BlockSpec((1, LANES), lambda i: (0, 0))
    tab = jax.ShapeDtypeStruct((m, LANES), F32)
    return pl.pallas_call(
        _rope_tables_kernel,
        grid=(m // tm,),
        in_specs=[pl.BlockSpec((tm, 1), lambda i: (i, 0)), vec, vec],
        out_specs=[row] * 6,
        out_shape=[tab] * 6,
        compiler_params=_params("parallel"),
        name="rope_tables",
    )(pos, fm, fd)


def _mla_down_kernel(x_ref, g_ref, w_ref, qn_ref, kvn_ref, tk_ref, ql_ref, ckv_ref, kpe_ref):
    h = _rms(x_ref[...], g_ref[...]).astype(BF16)
    acc = _dot(h, w_ref[...])
    ql_ref[...] = _rms(acc[:, :MLA_Q_RANK], qn_ref[...]).astype(BF16)
    ckv_ref[...] = _rms(acc[:, MLA_Q_RANK:MLA_Q_RANK + MLA_KV_RANK], kvn_ref[...]).astype(BF16)
    y = acc[:, MLA_Q_RANK + MLA_KV_RANK:] * tk_ref[...]
    z = y + pltpu.roll(y, MLA_ROPE, 1)
    lane = lax.broadcasted_iota(jnp.int32, z.shape, 1)
    kpe_ref[:, :LANES] = jnp.where(lane < MLA_ROPE, z, 0.0).astype(BF16)
    kpe_ref[:, LANES:] = jnp.where(lane < MLA_ROPE, 0.0, z).astype(BF16)


def _mla_down(x2, gain, w1, q_norm, kv_norm, tk):
    m = x2.shape[0]
    tm = min(m, TM_SMALL)
    n1 = w1.shape[1]
    return pl.pallas_call(
        _mla_down_kernel,
        grid=(m // tm,),
        in_specs=[
            pl.BlockSpec((tm, D_MODEL), lambda i: (i, 0)),
            pl.BlockSpec((1, D_MODEL), lambda i: (0, 0)),
            pl.BlockSpec((D_MODEL, n1), lambda i: (0, 0)),
            pl.BlockSpec((1, MLA_Q_RANK), lambda i: (0, 0)),
            pl.BlockSpec((1, MLA_KV_RANK), lambda i: (0, 0)),
            pl.BlockSpec((tm, LANES), lambda i: (i, 0)),
        ],
        out_specs=[
            pl.BlockSpec((tm, MLA_Q_RANK), lambda i: (i, 0)),
            pl.BlockSpec((tm, MLA_KV_RANK), lambda i: (i, 0)),
            pl.BlockSpec((tm, 2 * LANES), lambda i: (i, 0)),
        ],
        out_shape=[
            jax.ShapeDtypeStruct((m, MLA_Q_RANK), BF16),
            jax.ShapeDtypeStruct((m, MLA_KV_RANK), BF16),
            jax.ShapeDtypeStruct((m, 2 * LANES), BF16),
        ],
        compiler_params=_params("parallel"),
        name="mla_down",
    )(x2, gain, w1, q_norm, kv_norm, tk)


MLA_PAIRS = MLA_HEADS // 2
Q_PAIR_IN = 2 * MLA_NOPE + 2 * LANES
Q_PAIR_OUT = 2 * MLA_NOPE + LANES


def _mla_q_up_kernel(a_ref, w_ref, c_ref, s_ref, o_ref):
    a = a_ref[...]
    c = c_ref[...]
    s = s_ref[...]
    for p in range(MLA_PAIRS):
        acc = _dot(a, w_ref[:, p * Q_PAIR_IN:(p + 1) * Q_PAIR_IN])
        o_ref[:, p * Q_PAIR_OUT:p * Q_PAIR_OUT + 2 * MLA_NOPE] = acc[:, :2 * MLA_NOPE].astype(BF16)
        pe = acc[:, 2 * MLA_NOPE:2 * MLA_NOPE + LANES] * c + acc[:, 2 * MLA_NOPE + LANES:] * s
        o_ref[:, p * Q_PAIR_OUT + 2 * MLA_NOPE:(p + 1) * Q_PAIR_OUT] = pe.astype(BF16)


def _mla_q_up(q_lat, wq, cm, sm):
    m = q_lat.shape[0]
    tm = min(m, TM_SMALL)
    n_out = MLA_PAIRS * Q_PAIR_OUT
    return pl.pallas_call(
        _mla_q_up_kernel,
        grid=(m // tm,),
        in_specs=[
            pl.BlockSpec((tm, MLA_Q_RANK), lambda i: (i, 0)),
            pl.BlockSpec(wq.shape, lambda i: (0, 0)),
            pl.BlockSpec((tm, LANES), lambda i: (i, 0)),
            pl.BlockSpec((tm, LANES), lambda i: (i, 0)),
        ],
        out_specs=pl.BlockSpec((tm, n_out), lambda i: (i, 0)),
        out_shape=jax.ShapeDtypeStruct((m, n_out), BF16),
        compiler_params=_params("parallel"),
        name="mla_q_up",
    )(q_lat, wq, cm, sm)


def _matmul_bf16_kernel(a_ref, w_ref, o_ref):
    o_ref[...] = _dot(a_ref[...], w_ref[...]).astype(BF16)


def _mla_kv_up(ckv, wkv):
    m = ckv.shape[0]
    tm = min(m, TM_SMALL)
    n = wkv.shape[1]
    return pl.pallas_call(
        _matmul_bf16_kernel,
        grid=(m // tm,),
        in_specs=[
            pl.BlockSpec((tm, MLA_KV_RANK), lambda i: (i, 0)),
            pl.BlockSpec(wkv.shape, lambda i: (0, 0)),
        ],
        out_specs=pl.BlockSpec((tm, n), lambda i: (i, 0)),
        out_shape=jax.ShapeDtypeStruct((m, n), BF16),
        compiler_params=_params("parallel"),
        name="mla_kv_up",
    )(ckv, wkv)


def _mla_attn_kernel(q_ref, kn_ref, kpe_ref, v_ref, o_ref, m_ref, l_ref, acc_ref, *, tq, scale):
    qi = pl.program_id(2)
    scale_log2e = scale * LOG2_E
    q_pe = q_ref[:, 2 * MLA_NOPE:]
    row = lax.broadcasted_iota(jnp.int32, (tq, tq), 0)
    col = lax.broadcasted_iota(jnp.int32, (tq, tq), 1)
    causal = col <= row
    for hh in range(2):
        lanes = slice(hh * LANES, (hh + 1) * LANES)
        q = jnp.concatenate([q_ref[:, lanes], q_pe], axis=1)
        m_ref[...] = jnp.full(m_ref.shape, -jnp.inf, F32)
        l_ref[...] = jnp.zeros(l_ref.shape, F32)
        acc_ref[...] = jnp.zeros(acc_ref.shape, F32)

        def step(kb, masked):
            ks = pl.multiple_of(kb * tq, tq)
            k = jnp.concatenate([kn_ref[pl.ds(ks, tq), lanes], kpe_ref[pl.ds(ks, tq), lanes]], axis=1)
            s = _dot_nt(q, k)
            if masked:
                s = jnp.where(causal, s, -jnp.inf)
            m_prev = m_ref[...]
            m_next = jnp.maximum(m_prev, jnp.max(s, axis=1, keepdims=True))
            m_wide = jnp.concatenate([m_next] * (tq // LANES), axis=1)
            p = jnp.exp2((s - m_wide) * scale_log2e)
            alpha = jnp.exp2((m_prev - m_next) * scale_log2e)
            l_ref[...] = alpha * l_ref[...] + jnp.sum(p, axis=1, keepdims=True)
            m_ref[...] = m_next
            acc_ref[...] = alpha * acc_ref[...] + _dot(p.astype(BF16), v_ref[pl.ds(ks, tq), lanes])

        def pair(k2, carry):
            step(2 * k2, False)
            step(2 * k2 + 1, False)
            return carry

        lax.fori_loop(0, qi // 2, pair, 0)

        @pl.when(qi % 2 == 1)
        def _():
            step(qi - 1, False)
            step(qi, True)

        @pl.when(qi % 2 == 0)
        def _():
            step(qi, True)

        o_ref[:, lanes] = (acc_ref[...] / l_ref[...]).astype(BF16)


def _mla_attention(q, kv, kpe, b, s):
    tq = min(s, TQ_MLA)
    scale = (MLA_NOPE + MLA_ROPE) ** -0.5
    kern = functools.partial(_mla_attn_kernel, tq=tq, scale=scale)
    pair_w = 2 * LANES
    return pl.pallas_call(
        kern,
        grid=(b, MLA_PAIRS, s // tq),
        in_specs=[
            pl.BlockSpec((None, tq, Q_PAIR_OUT), lambda bi, p, qi: (bi, qi, p)),
            pl.BlockSpec((None, s, pair_w), lambda bi, p, qi: (bi, 0, p)),
            pl.BlockSpec((None, s, pair_w), lambda bi, p, qi: (bi, 0, 0)),
            pl.BlockSpec((None, s, pair_w), lambda bi, p, qi: (bi, 0, MLA_PAIRS + p)),
        ],
        out_specs=pl.BlockSpec((None, tq, pair_w), lambda bi, p, qi: (bi, qi, p)),
        out_shape=jax.ShapeDtypeStruct((b, s, MLA_HEADS * MLA_V), BF16),
        scratch_shapes=[
            pltpu.VMEM((tq, LANES), F32),
            pltpu.VMEM((tq, LANES), F32),
            pltpu.VMEM((tq, LANES), F32),
        ],
        compiler_params=_params("parallel", "parallel", "arbitrary"),
        name="mla_attention",
    )(q, kv, kpe, kv)


def _matmul_resid_kernel(a_ref, w_ref, r_ref, o_ref):
    o_ref[...] = r_ref[...] + _dot(a_ref[...], w_ref[...])


def _matmul_resid(a, w, resid):
    m, k = a.shape
    n = w.shape[1]
    tm = min(m, TM_SMALL)
    return pl.pallas_call(
        _matmul_resid_kernel,
        grid=(m // tm,),
        in_specs=[
            pl.BlockSpec((tm, k), lambda i: (i, 0)),
            pl.BlockSpec((k, n), lambda i: (0, 0)),
            pl.BlockSpec((tm, n), lambda i: (i, 0)),
        ],
        out_specs=pl.BlockSpec((tm, n), lambda i: (i, 0)),
        out_shape=jax.ShapeDtypeStruct((m, n), F32),
        compiler_params=_params("parallel"),
        name="out_proj_resid",
    )(a, w, resid)


def _dil_rope(xc, c, sa, sb):
    half = DIL_ROT // 2
    return xc * c + pltpu.roll(xc, half, 1) * sa + pltpu.roll(xc, LANES - half, 1) * sb


def _dil_qkv_kernel(x_ref, g_ref, w_ref, c_ref, sa_ref, sb_ref, o_ref, h_ref, acc_ref, *,
                    d, tm, n_rope, nj):
    j = pl.program_id(2)
    heads = acc_ref.shape[0]
    rows = tm // d

    def matmul():
        acc = _dot(h_ref[...], w_ref[...])
        for t in range(heads):
            acc_ref[t] = acc[:, t * LANES:(t + 1) * LANES]

    def emit(rope):
        for r in range(d):
            sl = pl.ds(r, rows, stride=d) if d > 1 else slice(None)
            if rope:
                c, sa, sb = c_ref[sl, :], sa_ref[sl, :], sb_ref[sl, :]
            for t in range(heads):
                a = acc_ref[t, sl, :]
                if rope:
                    a = _dil_rope(a, c, sa, sb)
                o_ref[r, :, t * LANES:(t + 1) * LANES] = a.astype(BF16)

    @pl.when(j == 0)
    def _():
        h_ref[...] = _rms(x_ref[...], g_ref[...]).astype(BF16)
        matmul()

    @pl.when(jnp.logical_and(j >= 1, j <= n_rope))
    def _():
        emit(True)
        matmul()

    @pl.when(jnp.logical_and(j > n_rope, j < nj))
    def _():
        emit(False)
        matmul()

    @pl.when(j == nj)
    def _():
        emit(False)


def _dil_qkv(x, gain, w_in, g, cd, sa, sb, d):
    b, s, _ = x.shape
    n = 3 * DIL_HEADS * DIL_HEAD_DIM
    tm = min(s, TM_PROJ)
    tn = TN_PROJ
    col0 = g * (n // tn)
    n_rope = 2 * DIL_HEADS * DIL_HEAD_DIM // tn
    nj = n // tn
    assert n_rope < nj
    kern = functools.partial(_dil_qkv_kernel, d=d, tm=tm, n_rope=n_rope, nj=nj)
    tab = pl.BlockSpec((None, tm, LANES), lambda bi, i, j: (bi, i, 0))
    return pl.pallas_call(
        kern,
        grid=(b, s // tm, nj + 1),
        in_specs=[
            pl.BlockSpec((None, tm, D_MODEL), lambda bi, i, j: (bi, i, 0)),
            pl.BlockSpec((1, D_MODEL), lambda bi, i, j: (0, 0)),
            pl.BlockSpec((D_MODEL, tn), lambda bi, i, j: (0, col0 + jnp.minimum(j, nj - 1))),
            tab, tab, tab,
        ],
        out_specs=pl.BlockSpec((None, d, tm // d, tn), lambda bi, i, j: (bi, 0, i, jnp.maximum(j - 1, 0))),
        out_shape=jax.ShapeDtypeStruct((b, d, s // d, n), BF16),
        scratch_shapes=[pltpu.VMEM((tm, D_MODEL), BF16), pltpu.VMEM((tn // LANES, tm, LANES), F32)],
        compiler_params=_params("parallel", "parallel", "arbitrary"),
        name=f"dil_qkv_d{d}",
    )(x, gain, w_in, cd, sa, sb)


def _dil_attn_kernel(q_ref, kc_ref, kp_ref, vc_ref, vp_ref, o_ref, lse_ref, s_ref, p_ref, *, scale):
    jb = pl.program_id(2)
    qb = DIL_BLOCK
    qi = lax.broadcasted_iota(jnp.int32, (qb, 2 * qb), 0)
    ki = lax.broadcasted_iota(jnp.int32, (qb, 2 * qb), 1)
    first_key = jnp.where(jb > 0, qi, qb)
    band = jnp.logical_and(ki >= first_key, ki <= qi + qb)
    for h in range(DIL_HEADS):
        sl = slice(h * DIL_HEAD_DIM, (h + 1) * DIL_HEAD_DIM)
        k2 = jnp.concatenate([kp_ref[:, sl], kc_ref[:, sl]], axis=0)
        s_ref[h] = _dot_nt(q_ref[:, sl], k2)
    s = jnp.where(band[None], s_ref[...] * scale, -jnp.inf)
    m = jnp.max(s, axis=-1, keepdims=True)
    e = jnp.exp(s - m)
    l = jnp.sum(e, axis=-1, keepdims=True)
    p_ref[...] = (e * (1.0 / l)).astype(BF16)
    lse = m + jnp.log(l)
    lane = lax.broadcasted_iota(jnp.int32, (qb, LANES), 1)
    lse_all = jnp.zeros((qb, LANES), F32)
    for h in range(DIL_HEADS):
        sl = slice(h * DIL_HEAD_DIM, (h + 1) * DIL_HEAD_DIM)
        v2 = jnp.concatenate([vp_ref[:, sl], vc_ref[:, sl]], axis=0)
        o_ref[:, sl] = _dot(p_ref[h], v2)
        lse_all = jnp.where(lane == h, lse[h], lse_all)
    lse_ref[...] = lse_all


def _dil_attention(qkv_g, d):
    b, _, l, _ = qkv_g.shape
    hw = DIL_HEADS * DIL_HEAD_DIM
    qb = DIL_BLOCK
    kern = functools.partial(_dil_attn_kernel, scale=DIL_HEAD_DIM ** -0.5)

    def cur(c):
        return pl.BlockSpec((None, None, qb, hw), lambda bi, r, jb: (bi, r, jb, c))

    def prev(c):
        return pl.BlockSpec((None, None, qb, hw), lambda bi, r, jb: (bi, r, jnp.maximum(jb - 1, 0), c))

    return pl.pallas_call(
        kern,
        grid=(b, d, l // qb),
        in_specs=[cur(0), cur(1), prev(1), cur(2), prev(2)],
        out_specs=[
            pl.BlockSpec((None, None, qb, hw), lambda bi, r, jb: (bi, r, jb, 0)),
            pl.BlockSpec((None, None, qb, LANES), lambda bi, r, jb: (bi, r, jb, 0)),
        ],
        out_shape=[
            jax.ShapeDtypeStruct((b, d, l, hw), F32),
            jax.ShapeDtypeStruct((b, d, l, LANES), F32),
        ],
        scratch_shapes=[
            pltpu.VMEM((DIL_HEADS, qb, 2 * qb), F32),
            pltpu.VMEM((DIL_HEADS, qb, 2 * qb), BF16),
        ],
        compiler_params=_params("parallel", "parallel", "arbitrary"),
        name=f"dil_attention_d{d}",
    )(qkv_g, qkv_g, qkv_g, qkv_g, qkv_g)


def _dil_merge_kernel(o0_ref, l0_ref, o1_ref, l1_ref, o2_ref, l2_ref, out_ref,
                      n1_ref, n2_ref, nl1_ref, nl2_ref, *, tm):
    d1 = DIL_GROUPS[1][1]
    d2 = DIL_GROUPS[2][1]
    for d, o_ref, l_ref, n_ref, nl_ref in ((d1, o1_ref, l1_ref, n1_ref, nl1_ref),
                                           (d2, o2_ref, l2_ref, n2_ref, nl2_ref)):
        for r in range(d):
            rows = pl.ds(r, tm // d, stride=d)
            nl_ref[rows, :] = l_ref[r]
            for h in range(DIL_HEADS):
                n_ref[h, rows, :] = o_ref[r, :, h * DIL_HEAD_DIM:(h + 1) * DIL_HEAD_DIM]
    for h in range(DIL_HEADS):
        sl = slice(h * DIL_HEAD_DIM, (h + 1) * DIL_HEAD_DIM)
        a0 = l0_ref[:, h:h + 1]
        a1 = nl1_ref[:, h:h + 1]
        a2 = nl2_ref[:, h:h + 1]
        mx = jnp.maximum(jnp.maximum(a0, a1), a2)
        w0 = jnp.exp(a0 - mx)
        w1 = jnp.exp(a1 - mx)
        w2 = jnp.exp(a2 - mx)
        den = w0 + w1 + w2
        o = (w0 / den) * o0_ref[:, sl] + (w1 / den) * n1_ref[h] + (w2 / den) * n2_ref[h]
        out_ref[:, sl] = o.astype(BF16)


def _dil_merge(outs, lses, b, s):
    hw = DIL_HEADS * DIL_HEAD_DIM
    tm = min(s, TM_MERGE)
    d1 = DIL_GROUPS[1][1]
    d2 = DIL_GROUPS[2][1]
    kern = functools.partial(_dil_merge_kernel, tm=tm)

    def grp(d, w):
        return pl.BlockSpec((None, d, tm // d, w), lambda bi, i: (bi, 0, i, 0))

    return pl.pallas_call(
        kern,
        grid=(b, s // tm),
        in_specs=[
            pl.BlockSpec((None, None, tm, hw), lambda bi, i: (bi, 0, i, 0)),
            pl.BlockSpec((None, None, tm, LANES), lambda bi, i: (bi, 0, i, 0)),
            grp(d1, hw), grp(d1, LANES), grp(d2, hw), grp(d2, LANES),
        ],
        out_specs=pl.BlockSpec((None, tm, hw), lambda bi, i: (bi, i, 0)),
        out_shape=jax.ShapeDtypeStruct((b, s, hw), BF16),
        scratch_shapes=[
            pltpu.VMEM((DIL_HEADS, tm, DIL_HEAD_DIM), F32), pltpu.VMEM((DIL_HEADS, tm, DIL_HEAD_DIM), F32),
            pltpu.VMEM((tm, LANES), F32), pltpu.VMEM((tm, LANES), F32),
        ],
        compiler_params=_params("parallel", "parallel"),
        name="dil_merge",
    )(outs[0], lses[0], outs[1], lses[1], outs[2], lses[2])


def _ffn_kernel(x_ref, g_ref, wg_ref, wv_ref, cwg_ref, cwv_ref, cbg_ref, cbv_ref, wd_ref, o_ref,
                h_ref, ug_ref, uv_ref, pg_ref, pv_ref, *, tm, nf):
    i = pl.program_id(1)
    t = pl.program_id(2)
    row = lax.broadcasted_iota(jnp.int32, (CARRY_ROWS, 1), 0)

    def up():
        h = h_ref[...]
        return _dot(h, wg_ref[...]), _dot(h, wv_ref[...])

    def conv(u, cw_ref, cb_ref, prev_ref, c):
        prev = prev_ref[c]
        prev_ref[c] = u[tm - CARRY_ROWS:, :]
        p1 = prev[CARRY_ROWS - 1:CARRY_ROWS, :]
        p2 = prev[CARRY_ROWS - 2:CARRY_ROWS - 1, :]
        r1 = pltpu.roll(u, 1, 0)
        r2 = pltpu.roll(u, 2, 0)
        u1 = jnp.concatenate([jnp.where(row == 0, p1, r1[:CARRY_ROWS]), r1[CARRY_ROWS:]], axis=0)
        u2 = jnp.concatenate(
            [jnp.where(row == 0, p2, jnp.where(row == 1, p1, r2[:CARRY_ROWS])), r2[CARRY_ROWS:]], axis=0)
        cw = cw_ref[...]
        acc = cb_ref[...] + u2 * cw[0:1, :]
        acc = acc + u1 * cw[1:2, :]
        return acc + u * cw[2:3, :]

    def down(ug, uv):
        c = t - 1
        gate = conv(ug, cwg_ref, cbg_ref, pg_ref, c)
        val = conv(uv, cwv_ref, cbv_ref, pv_ref, c)
        act = (gate * (1.0 / (1.0 + jnp.exp(-gate)))) * val
        o_ref[...] += _dot(act.astype(BF16), wd_ref[...])

    @pl.when(t == 0)
    def _():
        x = x_ref[...]
        h_ref[...] = _rms(x, g_ref[...]).astype(BF16)
        o_ref[...] = x
        ug_ref[...], uv_ref[...] = up()

    @pl.when(jnp.logical_and(t == 0, i == 0))
    def _():
        pg_ref[...] = jnp.zeros(pg_ref.shape, F32)
        pv_ref[...] = jnp.zeros(pv_ref.shape, F32)

    @pl.when(jnp.logical_and(t > 0, t < nf))
    def _():
        ug = ug_ref[...]
        uv = uv_ref[...]
        ug_ref[...], uv_ref[...] = up()
        down(ug, uv)

    @pl.when(t == nf)
    def _():
        down(ug_ref[...], uv_ref[...])


def _ffn(x, gain, w_up, conv_w, conv_b, w_down):
    b, s, _ = x.shape
    tm = min(s, TM_FFN)
    tf = TF_FFN
    nf = FFN_HIDDEN // tf
    kern = functools.partial(_ffn_kernel, tm=tm, nf=nf)

    def up_chunk(t):
        return jnp.minimum(t, nf - 1)

    def down_chunk(t):
        return jnp.maximum(t - 1, 0)

    return pl.pallas_call(
        kern,
        grid=(b, s // tm, nf + 1),
        in_specs=[
            pl.BlockSpec((None, tm, D_MODEL), lambda bi, i, t: (bi, i, 0)),
            pl.BlockSpec((1, D_MODEL), lambda bi, i, t: (0, 0)),
            pl.BlockSpec((D_MODEL, tf), lambda bi, i, t: (0, up_chunk(t))),
            pl.BlockSpec((D_MODEL, tf), lambda bi, i, t: (0, nf + up_chunk(t))),
            pl.BlockSpec((CONV_WIDTH, tf), lambda bi, i, t: (0, down_chunk(t))),
            pl.BlockSpec((CONV_WIDTH, tf), lambda bi, i, t: (0, nf + down_chunk(t))),
            pl.BlockSpec((1, tf), lambda bi, i, t: (0, down_chunk(t))),
            pl.BlockSpec((1, tf), lambda bi, i, t: (0, nf + down_chunk(t))),
            pl.BlockSpec((tf, D_MODEL), lambda bi, i, t: (down_chunk(t), 0)),
        ],
        out_specs=pl.BlockSpec((None, tm, D_MODEL), lambda bi, i, t: (bi, i, 0)),
        out_shape=jax.ShapeDtypeStruct(x.shape, F32),
        scratch_shapes=[
            pltpu.VMEM((tm, D_MODEL), BF16),
            pltpu.VMEM((tm, tf), F32),
            pltpu.VMEM((tm, tf), F32),
            pltpu.VMEM((nf, CARRY_ROWS, tf), F32),
            pltpu.VMEM((nf, CARRY_ROWS, tf), F32),
        ],
        compiler_params=_params("arbitrary", "arbitrary", "arbitrary"),
        name="conv_ffn",
    )(x, gain, w_up, w_up, conv_w, conv_w, conv_b, conv_b, w_down)


def _rmsnorm_kernel(x_ref, g_ref, o_ref):
    o_ref[...] = _rms(x_ref[...], g_ref[...])


def _rmsnorm(x2, gain):
    m, dm = x2.shape
    tm = min(m, TM_PROJ)
    return pl.pallas_call(
        _rmsnorm_kernel,
        grid=(m // tm,),
        in_specs=[pl.BlockSpec((tm, dm), lambda i: (i, 0)), pl.BlockSpec((1, dm), lambda i: (0, 0))],
        out_specs=pl.BlockSpec((tm, dm), lambda i: (i, 0)),
        out_shape=jax.ShapeDtypeStruct((m, dm), F32),
        compiler_params=_params("parallel"),
        name="final_norm",
    )(x2, gain)


def _rotate_half_cols(w):
    half = w.shape[-1] // 2
    return jnp.concatenate([-w[..., half:], w[..., :half]], axis=-1)


def _mla_weights(wq_a, wq_b, wkv_a, wkv_b):
    w_pe = wkv_a[:, MLA_KV_RANK:]
    w1 = jnp.concatenate([wq_a, wkv_a[:, :MLA_KV_RANK], w_pe, _rotate_half_cols(w_pe)], axis=1)
    qb = wq_b.reshape(MLA_Q_RANK, MLA_HEADS, MLA_NOPE + MLA_ROPE)
    nope = qb[:, :, :MLA_NOPE].reshape(MLA_Q_RANK, MLA_PAIRS, 2 * MLA_NOPE)
    pe = qb[:, :, MLA_NOPE:]
    rot = _rotate_half_cols(pe).reshape(MLA_Q_RANK, MLA_PAIRS, LANES)
    pe = pe.reshape(MLA_Q_RANK, MLA_PAIRS, LANES)
    wq = jnp.concatenate([nope, pe, rot], axis=2).reshape(MLA_Q_RANK, MLA_PAIRS * Q_PAIR_IN)
    kvb = wkv_b.reshape(MLA_KV_RANK, MLA_HEADS, MLA_NOPE + MLA_V)
    wkv = jnp.concatenate([kvb[:, :, :MLA_NOPE].reshape(MLA_KV_RANK, -1),
                           kvb[:, :, MLA_NOPE:].reshape(MLA_KV_RANK, -1)], axis=1)
    return w1.astype(BF16), wq.astype(BF16), wkv.astype(BF16)


def _mla_layer(x, gain, tabs, wq_a, q_norm, wq_b, wkv_a, kv_norm, wkv_b, wo):
    b, s, dm = x.shape
    cm, sm, tk = tabs
    w1, wq, wkv = _mla_weights(wq_a, wq_b, wkv_a, wkv_b)
    x2 = x.reshape(b * s, dm)
    q_lat, ckv, kpe = _mla_down(x2, gain.reshape(1, dm), w1, q_norm.reshape(1, -1), kv_norm.reshape(1, -1), tk)
    q = _mla_q_up(q_lat, wq, cm, sm)
    kv = _mla_kv_up(ckv, wkv)
    o = _mla_attention(q.reshape(b, s, -1), kv.reshape(b, s, -1), kpe.reshape(b, s, -1), b, s)
    return _matmul_resid(o.reshape(b * s, -1), wo.astype(BF16), x2).reshape(b, s, dm)


def _dil_layer(x, gain, tabs, w_in, wo):
    b, s, dm = x.shape
    cd, sa, sb = (t.reshape(b, s, LANES) for t in tabs)
    w_in = w_in.astype(BF16)
    outs, lses = [], []
    for g, (_, d) in enumerate(DIL_GROUPS):
        qkv_g = _dil_qkv(x, gain.reshape(1, dm), w_in, g, cd, sa, sb, d)
        o, lse = _dil_attention(qkv_g, d)
        outs.append(o)
        lses.append(lse)
    merged = _dil_merge(outs, lses, b, s)
    return _matmul_resid(merged.reshape(b * s, -1), wo.astype(BF16), x.reshape(b * s, dm)).reshape(b, s, dm)


def kernel(x, positions, attn_norm, ffn_norm, final_norm, mla_wq_a, mla_q_norm, mla_wq_b, mla_wkv_a,
           mla_kv_norm, mla_wkv_b, mla_wo, dil_w_in, dil_wo, ffn_w_up, ffn_conv_w, ffn_conv_b, ffn_w_down):
    b, s, dm = x.shape
    cm, sm, tk, cd, sa, sb = _rope_tables(positions)
    for i in range(DEPTH):
        j = i // N_MIXERS
        if i % N_MIXERS == 0:
            x = _mla_layer(x, attn_norm[i], (cm, sm, tk), mla_wq_a[j], mla_q_norm[j], mla_wq_b[j],
                           mla_wkv_a[j], mla_kv_norm[j], mla_wkv_b[j], mla_wo[j])
        else:
            x = _dil_layer(x, attn_norm[i], (cd, sa, sb), dil_w_in[j], dil_wo[j])
        x = _ffn(x, ffn_norm[i].reshape(1, dm), ffn_w_up[i].astype(BF16), ffn_conv_w[i],
                 ffn_conv_b[i].reshape(1, -1), ffn_w_down[i].astype(BF16))
    return _rmsnorm(x.reshape(b * s, dm), final_norm.reshape(1, dm)).reshape(b, s, dm)
```

```python
import functools

import jax
import jax.numpy as jnp
from jax import lax
from jax.experimental import pallas as pl
from jax.experimental.pallas import tpu as pltpu

F32 = jnp.float32
BF16 = jnp.bfloat16

D_MODEL = 2048
DEPTH = 4
N_MIXERS = 2
ROPE_THETA = 500000.0
NORM_EPS = 1e-6
LOG2_E = 1.4426950408889634

MLA_HEADS = 16
MLA_Q_RANK = 512
MLA_KV_RANK = 512
MLA_NOPE = 128
MLA_ROPE = 64
MLA_V = 128

DIL_GROUPS = ((128, 1), (512, 4), (2048, 16))
DIL_HEADS = 16
DIL_HEAD_DIM = 128
DIL_ROT = DIL_HEAD_DIM // 4
DIL_BLOCK = 128
DIL_BLOCKS_PER_STEP = 2

FFN_HIDDEN = 5632
CONV_WIDTH = 3

LANES = 128
SUBLANES = 8
VMEM_BYTES_V7X = 64 * 1024 * 1024
VMEM_LIMIT = VMEM_BYTES_V7X - 8 * 1024 * 1024

TM_PROJ = 1024
TN_PROJ = 1024
TM_SMALL = 512
TM_MERGE = 256
TQ_MLA = 512
TM_FFN = 512
TF_FFN = 512
CARRY_ROWS = SUBLANES
PERM_SLOTS = 4


def _params(*sem):
    return pltpu.CompilerParams(dimension_semantics=sem, vmem_limit_bytes=VMEM_LIMIT)


def _rms(x, g):
    ms = jnp.mean(x * x, axis=-1, keepdims=True)
    return (x * lax.rsqrt(ms + NORM_EPS)) * g


def _dot(a, b):
    return jnp.dot(a, b, preferred_element_type=F32)


def _dot_nt(a, b):
    return lax.dot_general(a, b, (((1,), (1,)), ((), ())), preferred_element_type=F32)


def _rope_tables_kernel(pos_ref, fm_ref, fd_ref, cm_ref, sm_ref, tk_ref, cd_ref, sa_ref, sb_ref):
    pos = pos_ref[...]
    lane = lax.broadcasted_iota(jnp.int32, (1, LANES), 1)
    am = pos * fm_ref[...]
    cm = jnp.cos(am)
    sm = jnp.sin(am)
    cm_ref[...] = cm
    sm_ref[...] = sm
    tk_ref[...] = jnp.where(lane < MLA_ROPE, cm, sm)
    ad = pos * fd_ref[...]
    cd = jnp.cos(ad)
    sd = jnp.sin(ad)
    half = DIL_ROT // 2
    cd_ref[...] = jnp.where(lane < DIL_ROT, cd, 1.0)
    sa_ref[...] = jnp.where(lane < half, 0.0, jnp.where(lane < DIL_ROT, sd, 0.0))
    sb_ref[...] = jnp.where(lane < half, -sd, 0.0)


def _rope_tables(positions):
    m = positions.size
    pos = positions.reshape(m, 1).astype(F32)
    inv_m = ROPE_THETA ** (-jnp.arange(0, MLA_ROPE, 2, dtype=F32) / MLA_ROPE)
    inv_d = ROPE_THETA ** (-jnp.arange(0, DIL_ROT, 2, dtype=F32) / DIL_ROT)
    fm = jnp.tile(inv_m, LANES // inv_m.size).reshape(1, LANES)
    fd = jnp.tile(inv_d, LANES // inv_d.size).reshape(1, LANES)
    tm = min(m, 2048)
    row = pl.BlockSpec((tm, LANES), lambda i: (i, 0))
    vec = pl.BlockSpec((1, LANES), lambda i: (0, 0))
    tab = jax.ShapeDtypeStruct((m, LANES), F32)
    return pl.pallas_call(
        _rope_tables_kernel,
        grid=(m // tm,),
        in_specs=[pl.BlockSpec((tm, 1), lambda i: (i, 0)), vec, vec],
        out_specs=[row] * 6,
        out_shape=[tab] * 6,
        compiler_params=_params("parallel"),
        name="rope_tables",
    )(pos, fm, fd)


def _mla_down_kernel(x_ref, g_ref, w_ref, qn_ref, kvn_ref, tk_ref, ql_ref, ckv_ref, kpe_ref):
    h = _rms(x_ref[...], g_ref[...]).astype(BF16)
    acc = _dot(h, w_ref[...])
    ql_ref[...] = _rms(acc[:, :MLA_Q_RANK], qn_ref[...]).astype(BF16)
    ckv_ref[...] = _rms(acc[:, MLA_Q_RANK:MLA_Q_RANK + MLA_KV_RANK], kvn_ref[...]).astype(BF16)
    y = acc[:, MLA_Q_RANK + MLA_KV_RANK:] * tk_ref[...]
    z = y + pltpu.roll(y, MLA_ROPE, 1)
    lane = lax.broadcasted_iota(jnp.int32, z.shape, 1)
    kpe_ref[:, :LANES] = jnp.where(lane < MLA_ROPE, z, 0.0).astype(BF16)
    kpe_ref[:, LANES:] = jnp.where(lane < MLA_ROPE, 0.0, z).astype(BF16)


def _mla_down(x2, gain, w1, q_norm, kv_norm, tk):
    m = x2.shape[0]
    tm = min(m, TM_SMALL)
    n1 = w1.shape[1]
    return pl.pallas_call(
        _mla_down_kernel,
        grid=(m // tm,),
        in_specs=[
            pl.BlockSpec((tm, D_MODEL), lambda i: (i, 0)),
            pl.BlockSpec((1, D_MODEL), lambda i: (0, 0)),
            pl.BlockSpec((D_MODEL, n1), lambda i: (0, 0)),
            pl.BlockSpec((1, MLA_Q_RANK), lambda i: (0, 0)),
            pl.BlockSpec((1, MLA_KV_RANK), lambda i: (0, 0)),
            pl.BlockSpec((tm, LANES), lambda i: (i, 0)),
        ],
        out_specs=[
            pl.BlockSpec((tm, MLA_Q_RANK), lambda i: (i, 0)),
            pl.BlockSpec((tm, MLA_KV_RANK), lambda i: (i, 0)),
            pl.BlockSpec((tm, 2 * LANES), lambda i: (i, 0)),
        ],
        out_shape=[
            jax.ShapeDtypeStruct((m, MLA_Q_RANK), BF16),
            jax.ShapeDtypeStruct((m, MLA_KV_RANK), BF16),
            jax.ShapeDtypeStruct((m, 2 * LANES), BF16),
        ],
        compiler_params=_params("parallel"),
        name="mla_down",
    )(x2, gain, w1, q_norm, kv_norm, tk)


MLA_PAIRS = MLA_HEADS // 2
Q_PAIR_IN = 2 * MLA_NOPE + 2 * LANES
Q_PAIR_OUT = 2 * MLA_NOPE + LANES


def _mla_q_up_kernel(a_ref, w_ref, c_ref, s_ref, o_ref):
    a = a_ref[...]
    c = c_ref[...]
    s = s_ref[...]
    for p in range(MLA_PAIRS):
        acc = _dot(a, w_ref[:, p * Q_PAIR_IN:(p + 1) * Q_PAIR_IN])
        o_ref[:, p * Q_PAIR_OUT:p * Q_PAIR_OUT + 2 * MLA_NOPE] = acc[:, :2 * MLA_NOPE].astype(BF16)
        pe = acc[:, 2 * MLA_NOPE:2 * MLA_NOPE + LANES] * c + acc[:, 2 * MLA_NOPE + LANES:] * s
        o_ref[:, p * Q_PAIR_OUT + 2 * MLA_NOPE:(p + 1) * Q_PAIR_OUT] = pe.astype(BF16)


def _mla_q_up(q_lat, wq, cm, sm):
    m = q_lat.shape[0]
    tm = min(m, TM_SMALL)
    n_out = MLA_PAIRS * Q_PAIR_OUT
    return pl.pallas_call(
        _mla_q_up_kernel,
        grid=(m // tm,),
        in_specs=[
            pl.BlockSpec((tm, MLA_Q_RANK), lambda i: (i, 0)),
            pl.BlockSpec(wq.shape, lambda i: (0, 0)),
            pl.BlockSpec((tm, LANES), lambda i: (i, 0)),
            pl.BlockSpec((tm, LANES), lambda i: (i, 0)),
        ],
        out_specs=pl.BlockSpec((tm, n_out), lambda i: (i, 0)),
        out_shape=jax.ShapeDtypeStruct((m, n_out), BF16),
        compiler_params=_params("parallel"),
        name="mla_q_up",
    )(q_lat, wq, cm, sm)


def _matmul_bf16_kernel(a_ref, w_ref, o_ref):
    o_ref[...] = _dot(a_ref[...], w_ref[...]).astype(BF16)


def _mla_kv_up(ckv, wkv):
    m = ckv.shape[0]
    tm = min(m, TM_SMALL)
    n = wkv.shape[1]
    return pl.pallas_call(
        _matmul_bf16_kernel,
        grid=(m // tm,),
        in_specs=[
            pl.BlockSpec((tm, MLA_KV_RANK), lambda i: (i, 0)),
            pl.BlockSpec(wkv.shape, lambda i: (0, 0)),
        ],
        out_specs=pl.BlockSpec((tm, n), lambda i: (i, 0)),
        out_shape=jax.ShapeDtypeStruct((m, n), BF16),
        compiler_params=_params("parallel"),
        name="mla_kv_up",
    )(ckv, wkv)


def _mla_attn_kernel(q_ref, kn_ref, kpe_ref, v_ref, o_ref, m_ref, l_ref, acc_ref, *, tq, scale):
    qi = pl.program_id(2)
    scale_log2e = scale * LOG2_E
    q_pe = q_ref[:, 2 * MLA_NOPE:]
    row = lax.broadcasted_iota(jnp.int32, (tq, tq), 0)
    col = lax.broadcasted_iota(jnp.int32, (tq, tq), 1)
    causal = col <= row
    for hh in range(2):
        lanes = slice(hh * LANES, (hh + 1) * LANES)
        q = jnp.concatenate([q_ref[:, lanes], q_pe], axis=1)
        m_ref[...] = jnp.full(m_ref.shape, -jnp.inf, F32)
        l_ref[...] = jnp.zeros(l_ref.shape, F32)
        acc_ref[...] = jnp.zeros(acc_ref.shape, F32)

        def step(kb, masked):
            ks = pl.multiple_of(kb * tq, tq)
            k = jnp.concatenate([kn_ref[pl.ds(ks, tq), lanes], kpe_ref[pl.ds(ks, tq), lanes]], axis=1)
            s = _dot_nt(q, k)
            if masked:
                s = jnp.where(causal, s, -jnp.inf)
            m_prev = m_ref[...]
            m_next = jnp.maximum(m_prev, jnp.max(s, axis=1, keepdims=True))
            m_wide = jnp.concatenate([m_next] * (tq // LANES), axis=1)
            p = jnp.exp2((s - m_wide) * scale_log2e)
            alpha = jnp.exp2((m_prev - m_next) * scale_log2e)
            l_ref[...] = alpha * l_ref[...] + jnp.sum(p, axis=1, keepdims=True)
            m_ref[...] = m_next
            acc_ref[...] = alpha * acc_ref[...] + _dot(p.astype(BF16), v_ref[pl.ds(ks, tq), lanes])

        def pair(k2, carry):
            step(2 * k2, False)
            step(2 * k2 + 1, False)
            return carry

        lax.fori_loop(0, qi // 2, pair, 0)

        @pl.when(qi % 2 == 1)
        def _():
            step(qi - 1, False)
            step(qi, True)

        @pl.when(qi % 2 == 0)
        def _():
            step(qi, True)

        o_ref[:, lanes] = (acc_ref[...] / l_ref[...]).astype(BF16)


def _mla_attention(q, kv, kpe, b, s):
    tq = min(s, TQ_MLA)
    scale = (MLA_NOPE + MLA_ROPE) ** -0.5
    kern = functools.partial(_mla_attn_kernel, tq=tq, scale=scale)
    pair_w = 2 * LANES
    return pl.pallas_call(
        kern,
        grid=(b, MLA_PAIRS, s // tq),
        in_specs=[
            pl.BlockSpec((None, tq, Q_PAIR_OUT), lambda bi, p, qi: (bi, qi, p)),
            pl.BlockSpec((None, s, pair_w), lambda bi, p, qi: (bi, 0, p)),
            pl.BlockSpec((None, s, pair_w), lambda bi, p, qi: (bi, 0, 0)),
            pl.BlockSpec((None, s, pair_w), lambda bi, p, qi: (bi, 0, MLA_PAIRS + p)),
        ],
        out_specs=pl.BlockSpec((None, tq, pair_w), lambda bi, p, qi: (bi, qi, p)),
        out_shape=jax.ShapeDtypeStruct((b, s, MLA_HEADS * MLA_V), BF16),
        scratch_shapes=[
            pltpu.VMEM((tq, LANES), F32),
            pltpu.VMEM((tq, LANES), F32),
            pltpu.VMEM((tq, LANES), F32),
        ],
        compiler_params=_params("parallel", "parallel", "arbitrary"),
        name="mla_attention",
    )(q, kv, kpe, kv)


def _matmul_resid_kernel(a_ref, w_ref, r_ref, o_ref):
    o_ref[...] = r_ref[...] + _dot(a_ref[...], w_ref[...])


def _matmul_resid(a, w, layer, resid):
    m, k = a.shape
    n = w.shape[2]
    tm = min(m, TM_SMALL)
    return pl.pallas_call(
        _matmul_resid_kernel,
        grid=(m // tm,),
        in_specs=[
            pl.BlockSpec((tm, k), lambda i: (i, 0)),
            pl.BlockSpec((None, k, n), lambda i: (layer, 0, 0)),
            pl.BlockSpec((tm, n), lambda i: (i, 0)),
        ],
        out_specs=pl.BlockSpec((tm, n), lambda i: (i, 0)),
        out_shape=jax.ShapeDtypeStruct((m, n), F32),
        compiler_params=_params("parallel"),
        name="out_proj_resid",
    )(a, w, resid)


def _dil_rope(xc, c, sa, sb):
    half = DIL_ROT // 2
    return xc * c + pltpu.roll(xc, half, 1) * sa + pltpu.roll(xc, LANES - half, 1) * sb


def _dil_qkv_kernel(x_ref, g_ref, w_ref, c_ref, sa_ref, sb_ref, o_ref, h_ref, acc_ref, *perm_refs,
                    d, tm, n_rope, nj):
    j = pl.program_id(2)
    rows = tm // d
    if d > 1:
        xs_ref, cs_ref, sas_ref, sbs_ref = perm_refs
    else:
        cs_ref, sas_ref, sbs_ref = c_ref, sa_ref, sb_ref

    def norm_input():
        if d == 1:
            h_ref[...] = _rms(x_ref[...], g_ref[...]).astype(BF16)
            return
        x = x_ref[...]
        rinv = lax.rsqrt(jnp.mean(x * x, axis=-1, keepdims=True) + NORM_EPS)
        for t in range(x_ref.shape[1] // LANES):
            lanes = slice(t * LANES, (t + 1) * LANES)
            slot = t % xs_ref.shape[0]
            xs_ref[slot] = (x_ref[:, lanes] * rinv) * g_ref[:, lanes]
            for r in range(d):
                h_ref[r * rows:(r + 1) * rows, lanes] = xs_ref[slot, pl.ds(r, rows, stride=d), :].astype(BF16)
        for r in range(d):
            sl = pl.ds(r, rows, stride=d)
            dst = slice(r * rows, (r + 1) * rows)
            cs_ref[dst, :] = c_ref[sl, :]
            sas_ref[dst, :] = sa_ref[sl, :]
            sbs_ref[dst, :] = sb_ref[sl, :]

    def matmul():
        acc_ref[...] = _dot(h_ref[...], w_ref[...])

    def emit(rope):
        for r in range(d):
            src = slice(r * rows, (r + 1) * rows)
            if rope:
                c, sa, sb = cs_ref[src, :], sas_ref[src, :], sbs_ref[src, :]
            for t in range(acc_ref.shape[1] // LANES):
                a = acc_ref[src, t * LANES:(t + 1) * LANES]
                if rope:
                    a = _dil_rope(a, c, sa, sb)
                o_ref[r, :, t * LANES:(t + 1) * LANES] = a.astype(BF16)

    @pl.when(j == 0)
    def _():
        norm_input()
        matmul()

    @pl.when(jnp.logical_and(j >= 1, j <= n_rope))
    def _():
        emit(True)
        matmul()

    @pl.when(jnp.logical_and(j > n_rope, j < nj))
    def _():
        emit(False)
        matmul()

    @pl.when(j == nj)
    def _():
        emit(False)


def _dil_qkv(x, gain, w_in, layer, g, cd, sa, sb, d):
    b, s, _ = x.shape
    n = 3 * DIL_HEADS * DIL_HEAD_DIM
    tm = min(s, TM_PROJ)
    tn = TN_PROJ
    col0 = g * (n // tn)
    n_rope = 2 * DIL_HEADS * DIL_HEAD_DIM // tn
    nj = n // tn
    assert n_rope < nj
    kern = functools.partial(_dil_qkv_kernel, d=d, tm=tm, n_rope=n_rope, nj=nj)
    tab = pl.BlockSpec((None, tm, LANES), lambda bi, i, j: (bi, i, 0))
    perm_scratch = []
    if d > 1:
        perm_scratch = [pltpu.VMEM((PERM_SLOTS, tm, LANES), F32)] + [pltpu.VMEM((tm, LANES), F32)] * 3
    return pl.pallas_call(
        kern,
        grid=(b, s // tm, nj + 1),
        in_specs=[
            pl.BlockSpec((None, tm, D_MODEL), lambda bi, i, j: (bi, i, 0)),
            pl.BlockSpec((1, D_MODEL), lambda bi, i, j: (0, 0)),
            pl.BlockSpec((None, D_MODEL, tn), lambda bi, i, j: (layer, 0, col0 + jnp.minimum(j, nj - 1))),
            tab, tab, tab,
        ],
        out_specs=pl.BlockSpec((None, d, tm // d, tn), lambda bi, i, j: (bi, 0, i, jnp.maximum(j - 1, 0))),
        out_shape=jax.ShapeDtypeStruct((b, d, s // d, n), BF16),
        scratch_shapes=[pltpu.VMEM((tm, D_MODEL), BF16), pltpu.VMEM((tm, tn), F32)] + perm_scratch,
        compiler_params=_params("parallel", "parallel", "arbitrary"),
        name=f"dil_qkv_d{d}",
    )(x, gain, w_in, cd, sa, sb)


def _dil_attn_kernel(q_ref, kc_ref, kp_ref, vc_ref, vp_ref, o_ref, lse_ref, s_ref, p_ref, *, scale, nblk):
    jb = pl.program_id(2)
    qb = DIL_BLOCK
    qi = lax.broadcasted_iota(jnp.int32, (qb, 2 * qb), 0)
    ki = lax.broadcasted_iota(jnp.int32, (qb, 2 * qb), 1)
    lane = lax.broadcasted_iota(jnp.int32, (qb, LANES), 1)

    def keys(cur_ref, prev_ref, blk, sl):
        own = cur_ref[blk * qb:(blk + 1) * qb, sl]
        before = prev_ref[:, sl] if blk == 0 else cur_ref[(blk - 1) * qb:blk * qb, sl]
        return jnp.concatenate([before, own], axis=0)

    for blk in range(nblk):
        for h in range(DIL_HEADS):
            sl = slice(h * DIL_HEAD_DIM, (h + 1) * DIL_HEAD_DIM)
            s_ref[blk, h] = _dot_nt(q_ref[blk * qb:(blk + 1) * qb, sl], keys(kc_ref, kp_ref, blk, sl))
    for blk in range(nblk):
        first_key = jnp.where(jb > 0, qi, qb) if blk == 0 else qi
        band = jnp.logical_and(ki >= first_key, ki <= qi + qb)
        s = jnp.where(band[None], s_ref[blk] * scale, -jnp.inf)
        m = jnp.max(s, axis=-1, keepdims=True)
        e = jnp.exp(s - m)
        l = jnp.sum(e, axis=-1, keepdims=True)
        p_ref[blk] = (e * (1.0 / l)).astype(BF16)
        lse = m + jnp.log(l)
        lse_all = jnp.zeros((qb, LANES), F32)
        for h in range(DIL_HEADS):
            lse_all = jnp.where(lane == h, lse[h], lse_all)
        lse_ref[blk * qb:(blk + 1) * qb, :] = lse_all
    for blk in range(nblk):
        for h in range(DIL_HEADS):
            sl = slice(h * DIL_HEAD_DIM, (h + 1) * DIL_HEAD_DIM)
            o_ref[blk * qb:(blk + 1) * qb, sl] = _dot(p_ref[blk, h], keys(vc_ref, vp_ref, blk, sl))


def _dil_attention(qkv_g, d):
    b, _, l, _ = qkv_g.shape
    hw = DIL_HEADS * DIL_HEAD_DIM
    qb = DIL_BLOCK
    nblk = DIL_BLOCKS_PER_STEP
    rows = nblk * qb
    kern = functools.partial(_dil_attn_kernel, scale=DIL_HEAD_DIM ** -0.5, nblk=nblk)

    def cur(c):
        return pl.BlockSpec((None, None, rows, hw), lambda bi, r, jb: (bi, r, jb, c))

    def prev(c):
        return pl.BlockSpec((None, None, qb, hw), lambda bi, r, jb: (bi, r, jnp.maximum(jb * nblk - 1, 0), c))

    return pl.pallas_call(
        kern,
        grid=(b, d, l // rows),
        in_specs=[cur(0), cur(1), prev(1), cur(2), prev(2)],
        out_specs=[
            pl.BlockSpec((None, None, rows, hw), lambda bi, r, jb: (bi, r, jb, 0)),
            pl.BlockSpec((None, None, rows, LANES), lambda bi, r, jb: (bi, r, jb, 0)),
        ],
        out_shape=[
            jax.ShapeDtypeStruct((b, d, l, hw), F32),
            jax.ShapeDtypeStruct((b, d, l, LANES), F32),
        ],
        scratch_shapes=[
            pltpu.VMEM((nblk, DIL_HEADS, qb, 2 * qb), F32),
            pltpu.VMEM((nblk, DIL_HEADS, qb, 2 * qb), BF16),
        ],
        compiler_params=_params("parallel", "parallel", "arbitrary"),
        name=f"dil_attention_d{d}",
    )(qkv_g, qkv_g, qkv_g, qkv_g, qkv_g)


def _dil_merge_kernel(o0_ref, l0_ref, o1_ref, l1_ref, o2_ref, l2_ref, out_ref,
                      n1_ref, n2_ref, nl1_ref, nl2_ref, *, tm):
    d1 = DIL_GROUPS[1][1]
    d2 = DIL_GROUPS[2][1]
    for d, o_ref, l_ref, n_ref, nl_ref in ((d1, o1_ref, l1_ref, n1_ref, nl1_ref),
                                           (d2, o2_ref, l2_ref, n2_ref, nl2_ref)):
        for r in range(d):
            rows = pl.ds(r, tm // d, stride=d)
            nl_ref[rows, :] = l_ref[r]
            for h in range(DIL_HEADS):
                n_ref[h, rows, :] = o_ref[r, :, h * DIL_HEAD_DIM:(h + 1) * DIL_HEAD_DIM]
    for h in range(DIL_HEADS):
        sl = slice(h * DIL_HEAD_DIM, (h + 1) * DIL_HEAD_DIM)
        a0 = l0_ref[:, h:h + 1]
        a1 = nl1_ref[:, h:h + 1]
        a2 = nl2_ref[:, h:h + 1]
        mx = jnp.maximum(jnp.maximum(a0, a1), a2)
        w0 = jnp.exp(a0 - mx)
        w1 = jnp.exp(a1 - mx)
        w2 = jnp.exp(a2 - mx)
        den = w0 + w1 + w2
        o = (w0 / den) * o0_ref[:, sl] + (w1 / den) * n1_ref[h] + (w2 / den) * n2_ref[h]
        out_ref[:, sl] = o.astype(BF16)


def _dil_merge(outs, lses, b, s):
    hw = DIL_HEADS * DIL_HEAD_DIM
    tm = min(s, TM_MERGE)
    d1 = DIL_GROUPS[1][1]
    d2 = DIL_GROUPS[2][1]
    kern = functools.partial(_dil_merge_kernel, tm=tm)

    def grp(d, w):
        return pl.BlockSpec((None, d, tm // d, w), lambda bi, i: (bi, 0, i, 0))

    return pl.pallas_call(
        kern,
        grid=(b, s // tm),
        in_specs=[
            pl.BlockSpec((None, None, tm, hw), lambda bi, i: (bi, 0, i, 0)),
            pl.BlockSpec((None, None, tm, LANES), lambda bi, i: (bi, 0, i, 0)),
            grp(d1, hw), grp(d1, LANES), grp(d2, hw), grp(d2, LANES),
        ],
        out_specs=pl.BlockSpec((None, tm, hw), lambda bi, i: (bi, i, 0)),
        out_shape=jax.ShapeDtypeStruct((b, s, hw), BF16),
        scratch_shapes=[
            pltpu.VMEM((DIL_HEADS, tm, DIL_HEAD_DIM), F32), pltpu.VMEM((DIL_HEADS, tm, DIL_HEAD_DIM), F32),
            pltpu.VMEM((tm, LANES), F32), pltpu.VMEM((tm, LANES), F32),
        ],
        compiler_params=_params("parallel", "parallel"),
        name="dil_merge",
    )(outs[0], lses[0], outs[1], lses[1], outs[2], lses[2])


def _ffn_kernel(x_ref, g_ref, wu_ref, cp_ref, wd_ref, o_ref, h_ref, u_ref, prev_ref, *, tm, tf, nf):
    i = pl.program_id(1)
    t = pl.program_id(2)
    row = lax.broadcasted_iota(jnp.int32, (CARRY_ROWS, 1), 0)

    def up():
        return _dot(h_ref[...], wu_ref[...])

    def conv(u, c):
        prev = prev_ref[c]
        prev_ref[c] = u[tm - CARRY_ROWS:, :]
        p1 = prev[CARRY_ROWS - 1:CARRY_ROWS, :]
        p2 = prev[CARRY_ROWS - 2:CARRY_ROWS - 1, :]
        r1 = pltpu.roll(u, 1, 0)
        r2 = pltpu.roll(u, 2, 0)
        u1 = jnp.concatenate([jnp.where(row == 0, p1, r1[:CARRY_ROWS]), r1[CARRY_ROWS:]], axis=0)
        u2 = jnp.concatenate(
            [jnp.where(row == 0, p2, jnp.where(row == 1, p1, r2[:CARRY_ROWS])), r2[CARRY_ROWS:]], axis=0)
        cp = cp_ref[...]
        acc = cp[CONV_WIDTH:CONV_WIDTH + 1, :] + u2 * cp[0:1, :]
        acc = acc + u1 * cp[1:2, :]
        return acc + u * cp[2:3, :]

    def down(u):
        cv = conv(u, t - 1)
        gate = cv[:, :tf]
        val = cv[:, tf:]
        act = (gate * (1.0 / (1.0 + jnp.exp(-gate)))) * val
        o_ref[...] += _dot(act.astype(BF16), wd_ref[...])

    @pl.when(t == 0)
    def _():
        x = x_ref[...]
        h_ref[...] = _rms(x, g_ref[...]).astype(BF16)
        o_ref[...] = x
        u_ref[...] = up()

    @pl.when(jnp.logical_and(t == 0, i == 0))
    def _():
        prev_ref[...] = jnp.zeros(prev_ref.shape, F32)

    @pl.when(jnp.logical_and(t > 0, t < nf))
    def _():
        u = u_ref[...]
        u_ref[...] = up()
        down(u)

    @pl.when(t == nf)
    def _():
        down(u_ref[...])


def _ffn_chunk_order(a, tf):
    lead = a.shape[:-1]
    return a.reshape(*lead, 2, FFN_HIDDEN // tf, tf).swapaxes(-3, -2).reshape(*lead, 2 * FFN_HIDDEN)


def _ffn_weights(w_up, conv_w, conv_b, w_down):
    tf = TF_FFN
    pad = jnp.zeros((conv_w.shape[0], SUBLANES - CONV_WIDTH - 1, conv_w.shape[2]), F32)
    conv_p = jnp.concatenate([conv_w, conv_b[:, None, :], pad], axis=1)
    return (_ffn_chunk_order(w_up, tf).astype(BF16), _ffn_chunk_order(conv_p, tf), w_down.astype(BF16))


def _ffn(x, gain, w_up, conv_p, w_down, layer):
    b, s, _ = x.shape
    tm = min(s, TM_FFN)
    tf = TF_FFN
    nf = FFN_HIDDEN // tf
    kern = functools.partial(_ffn_kernel, tm=tm, tf=tf, nf=nf)

    def up_chunk(t):
        return jnp.minimum(t, nf - 1)

    def down_chunk(t):
        return jnp.maximum(t - 1, 0)

    return pl.pallas_call(
        kern,
        grid=(b, s // tm, nf + 1),
        in_specs=[
            pl.BlockSpec((None, tm, D_MODEL), lambda bi, i, t: (bi, i, 0)),
            pl.BlockSpec((1, D_MODEL), lambda bi, i, t: (0, 0)),
            pl.BlockSpec((None, D_MODEL, 2 * tf), lambda bi, i, t: (layer, 0, up_chunk(t))),
            pl.BlockSpec((None, SUBLANES, 2 * tf), lambda bi, i, t: (layer, 0, down_chunk(t))),
            pl.BlockSpec((None, tf, D_MODEL), lambda bi, i, t: (layer, down_chunk(t), 0)),
        ],
        out_specs=pl.BlockSpec((None, tm, D_MODEL), lambda bi, i, t: (bi, i, 0)),
        out_shape=jax.ShapeDtypeStruct(x.shape, F32),
        scratch_shapes=[
            pltpu.VMEM((tm, D_MODEL), BF16),
            pltpu.VMEM((tm, 2 * tf), F32),
            pltpu.VMEM((nf, CARRY_ROWS, 2 * tf), F32),
        ],
        compiler_params=_params("arbitrary", "arbitrary", "arbitrary"),
        name="conv_ffn",
    )(x, gain, w_up, conv_p, w_down)


def _rmsnorm_kernel(x_ref, g_ref, o_ref):
    o_ref[...] = _rms(x_ref[...], g_ref[...])


def _rmsnorm(x2, gain):
    m, dm = x2.shape
    tm = min(m, TM_PROJ)
    return pl.pallas_call(
        _rmsnorm_kernel,
        grid=(m // tm,),
        in_specs=[pl.BlockSpec((tm, dm), lambda i: (i, 0)), pl.BlockSpec((1, dm), lambda i: (0, 0))],
        out_specs=pl.BlockSpec((tm, dm), lambda i: (i, 0)),
        out_shape=jax.ShapeDtypeStruct((m, dm), F32),
        compiler_params=_params("parallel"),
        name="final_norm",
    )(x2, gain)


def _rotate_half_cols(w):
    half = w.shape[-1] // 2
    return jnp.concatenate([-w[..., half:], w[..., :half]], axis=-1)


def _mla_weights(wq_a, wq_b, wkv_a, wkv_b):
    w_pe = wkv_a[:, MLA_KV_RANK:]
    w1 = jnp.concatenate([wq_a, wkv_a[:, :MLA_KV_RANK], w_pe, _rotate_half_cols(w_pe)], axis=1)
    qb = wq_b.reshape(MLA_Q_RANK, MLA_HEADS, MLA_NOPE + MLA_ROPE)
    nope = qb[:, :, :MLA_NOPE].reshape(MLA_Q_RANK, MLA_PAIRS, 2 * MLA_NOPE)
    pe = qb[:, :, MLA_NOPE:]
    rot = _rotate_half_cols(pe).reshape(MLA_Q_RANK, MLA_PAIRS, LANES)
    pe = pe.reshape(MLA_Q_RANK, MLA_PAIRS, LANES)
    wq = jnp.concatenate([nope, pe, rot], axis=2).reshape(MLA_Q_RANK, MLA_PAIRS * Q_PAIR_IN)
    kvb = wkv_b.reshape(MLA_KV_RANK, MLA_HEADS, MLA_NOPE + MLA_V)
    wkv = jnp.concatenate([kvb[:, :, :MLA_NOPE].reshape(MLA_KV_RANK, -1),
                           kvb[:, :, MLA_NOPE:].reshape(MLA_KV_RANK, -1)], axis=1)
    return w1.astype(BF16), wq.astype(BF16), wkv.astype(BF16)


def _mla_layer(x, gain, tabs, wq_a, q_norm, wq_b, wkv_a, kv_norm, wkv_b, wo, layer):
    b, s, dm = x.shape
    cm, sm, tk = tabs
    w1, wq, wkv = _mla_weights(wq_a, wq_b, wkv_a, wkv_b)
    x2 = x.reshape(b * s, dm)
    q_lat, ckv, kpe = _mla_down(x2, gain.reshape(1, dm), w1, q_norm.reshape(1, -1), kv_norm.reshape(1, -1), tk)
    q = _mla_q_up(q_lat, wq, cm, sm)
    kv = _mla_kv_up(ckv, wkv)
    o = _mla_attention(q.reshape(b, s, -1), kv.reshape(b, s, -1), kpe.reshape(b, s, -1), b, s)
    return _matmul_resid(o.reshape(b * s, -1), wo, layer, x2).reshape(b, s, dm)


def _dil_layer(x, gain, tabs, w_in, wo, layer):
    b, s, dm = x.shape
    cd, sa, sb = (t.reshape(b, s, LANES) for t in tabs)
    outs, lses = [], []
    for g, (_, d) in enumerate(DIL_GROUPS):
        qkv_g = _dil_qkv(x, gain.reshape(1, dm), w_in, layer, g, cd, sa, sb, d)
        o, lse = _dil_attention(qkv_g, d)
        outs.append(o)
        lses.append(lse)
    merged = _dil_merge(outs, lses, b, s)
    return _matmul_resid(merged.reshape(b * s, -1), wo, layer, x.reshape(b * s, dm)).reshape(b, s, dm)


def kernel(x, positions, attn_norm, ffn_norm, final_norm, mla_wq_a, mla_q_norm, mla_wq_b, mla_wkv_a,
           mla_kv_norm, mla_wkv_b, mla_wo, dil_w_in, dil_wo, ffn_w_up, ffn_conv_w, ffn_conv_b, ffn_w_down):
    b, s, dm = x.shape
    cm, sm, tk, cd, sa, sb = _rope_tables(positions)
    mla_wo_b = mla_wo.astype(BF16)
    dil_w_in_b = dil_w_in.astype(BF16)
    dil_wo_b = dil_wo.astype(BF16)
    ffn_up_b, ffn_conv_p, ffn_down_b = _ffn_weights(ffn_w_up, ffn_conv_w, ffn_conv_b, ffn_w_down)
    for i in range(DEPTH):
        j = i // N_MIXERS
        if i % N_MIXERS == 0:
            x = _mla_layer(x, attn_norm[i], (cm, sm, tk), mla_wq_a[j], mla_q_norm[j], mla_wq_b[j],
                           mla_wkv_a[j], mla_kv_norm[j], mla_wkv_b[j], mla_wo_b, j)
        else:
            x = _dil_layer(x, attn_norm[i], (cd, sa, sb), dil_w_in_b, dil_wo_b, j)
        x = _ffn(x, ffn_norm[i].reshape(1, dm), ffn_up_b, ffn_conv_p, ffn_down_b, i)
    return _rmsnorm(x.reshape(b * s, dm), final_norm.reshape(1, dm)).reshape(b, s, dm)
```

```python
import functools

import jax
import jax.numpy as jnp
from jax import lax
from jax.experimental import pallas as pl
from jax.experimental.pallas import tpu as pltpu

F32 = jnp.float32
BF16 = jnp.bfloat16

D_MODEL = 2048
DEPTH = 4
N_MIXERS = 2
ROPE_THETA = 500000.0
NORM_EPS = 1e-6
LOG2_E = 1.4426950408889634

MLA_HEADS = 16
MLA_Q_RANK = 512
MLA_KV_RANK = 512
MLA_NOPE = 128
MLA_ROPE = 64
MLA_V = 128

DIL_GROUPS = ((128, 1), (512, 4), (2048, 16))
DIL_HEADS = 16
DIL_HEAD_DIM = 128
DIL_ROT = DIL_HEAD_DIM // 4
DIL_BLOCK = 128
DIL_BLOCKS_PER_STEP = 2

FFN_HIDDEN = 5632
CONV_WIDTH = 3

LANES = 128
SUBLANES = 8
VMEM_BYTES_V7X = 64 * 1024 * 1024
VMEM_LIMIT = VMEM_BYTES_V7X - 8 * 1024 * 1024

TM_PROJ = 1024
TN_PROJ = 1024
TM_SMALL = 512
TM_MLA_PROJ = 1024
TM_MERGE = 256
TQ_MLA = 512
TM_FFN = 512
TF_FFN = 512
CARRY_ROWS = SUBLANES
PERM_SLOTS = 4


def _params(*sem):
    return pltpu.CompilerParams(dimension_semantics=sem, vmem_limit_bytes=VMEM_LIMIT)


def _rms(x, g):
    ms = jnp.mean(x * x, axis=-1, keepdims=True)
    return (x * lax.rsqrt(ms + NORM_EPS)) * g


def _dot(a, b):
    return jnp.dot(a, b, preferred_element_type=F32)


def _dot_nt(a, b):
    return lax.dot_general(a, b, (((1,), (1,)), ((), ())), preferred_element_type=F32)


def _rope_tables_kernel(pos_ref, fm_ref, fd_ref, cm_ref, sm_ref, tk_ref, cd_ref, sa_ref, sb_ref):
    pos = pos_ref[...]
    lane = lax.broadcasted_iota(jnp.int32, (1, LANES), 1)
    am = pos * fm_ref[...]
    cm = jnp.cos(am)
    sm = jnp.sin(am)
    cm_ref[...] = cm
    sm_ref[...] = sm
    tk_ref[...] = jnp.where(lane < MLA_ROPE, cm, sm)
    ad = pos * fd_ref[...]
    cd = jnp.cos(ad)
    sd = jnp.sin(ad)
    half = DIL_ROT // 2
    cd_ref[...] = jnp.where(lane < DIL_ROT, cd, 1.0)
    sa_ref[...] = jnp.where(lane < half, 0.0, jnp.where(lane < DIL_ROT, sd, 0.0))
    sb_ref[...] = jnp.where(lane < half, -sd, 0.0)


def _rope_tables(positions):
    m = positions.size
    pos = positions.reshape(m, 1).astype(F32)
    inv_m = ROPE_THETA ** (-jnp.arange(0, MLA_ROPE, 2, dtype=F32) / MLA_ROPE)
    inv_d = ROPE_THETA ** (-jnp.arange(0, DIL_ROT, 2, dtype=F32) / DIL_ROT)
    fm = jnp.tile(inv_m, LANES // inv_m.size).reshape(1, LANES)
    fd = jnp.tile(inv_d, LANES // inv_d.size).reshape(1, LANES)
    tm = min(m, 2048)
    row = pl.BlockSpec((tm, LANES), lambda i: (i, 0))
    vec = pl.BlockSpec((1, LANES), lambda i: (0, 0))
    tab = jax.ShapeDtypeStruct((m, LANES), F32)
    return pl.pallas_call(
        _rope_tables_kernel,
        grid=(m // tm,),
        in_specs=[pl.BlockSpec((tm, 1), lambda i: (i, 0)), vec, vec],
        out_specs=[row] * 6,
        out_shape=[tab] * 6,
        compiler_params=_params("parallel"),
        name="rope_tables",
    )(pos, fm, fd)


def _mla_down_kernel(x_ref, g_ref, w_ref, qn_ref, kvn_ref, tk_ref, ql_ref, ckv_ref, kpe_ref):
    h = _rms(x_ref[...], g_ref[...]).astype(BF16)
    acc = _dot(h, w_ref[...])
    ql_ref[...] = _rms(acc[:, :MLA_Q_RANK], qn_ref[...]).astype(BF16)
    ckv_ref[...] = _rms(acc[:, MLA_Q_RANK:MLA_Q_RANK + MLA_KV_RANK], kvn_ref[...]).astype(BF16)
    y = acc[:, MLA_Q_RANK + MLA_KV_RANK:] * tk_ref[...]
    z = y + pltpu.roll(y, MLA_ROPE, 1)
    lane = lax.broadcasted_iota(jnp.int32, z.shape, 1)
    kpe_ref[:, :LANES] = jnp.where(lane < MLA_ROPE, z, 0.0).astype(BF16)
    kpe_ref[:, LANES:] = jnp.where(lane < MLA_ROPE, 0.0, z).astype(BF16)


def _mla_down(x2, gain, w1, q_norm, kv_norm, tk):
    m = x2.shape[0]
    tm = min(m, TM_MLA_PROJ)
    n1 = w1.shape[1]
    return pl.pallas_call(
        _mla_down_kernel,
        grid=(m // tm,),
        in_specs=[
            pl.BlockSpec((tm, D_MODEL), lambda i: (i, 0)),
            pl.BlockSpec((1, D_MODEL), lambda i: (0, 0)),
            pl.BlockSpec((D_MODEL, n1), lambda i: (0, 0)),
            pl.BlockSpec((1, MLA_Q_RANK), lambda i: (0, 0)),
            pl.BlockSpec((1, MLA_KV_RANK), lambda i: (0, 0)),
            pl.BlockSpec((tm, LANES), lambda i: (i, 0)),
        ],
        out_specs=[
            pl.BlockSpec((tm, MLA_Q_RANK), lambda i: (i, 0)),
            pl.BlockSpec((tm, MLA_KV_RANK), lambda i: (i, 0)),
            pl.BlockSpec((tm, 2 * LANES), lambda i: (i, 0)),
        ],
        out_shape=[
            jax.ShapeDtypeStruct((m, MLA_Q_RANK), BF16),
            jax.ShapeDtypeStruct((m, MLA_KV_RANK), BF16),
            jax.ShapeDtypeStruct((m, 2 * LANES), BF16),
        ],
        compiler_params=_params("parallel"),
        name="mla_down",
    )(x2, gain, w1, q_norm, kv_norm, tk)


MLA_PAIRS = MLA_HEADS // 2
Q_PAIR_IN = 2 * MLA_NOPE + 2 * LANES
Q_PAIR_OUT = 2 * MLA_NOPE + LANES


def _mla_q_up_kernel(a_ref, w_ref, c_ref, s_ref, o_ref):
    a = a_ref[...]
    c = c_ref[...]
    s = s_ref[...]
    for p in range(MLA_PAIRS):
        acc = _dot(a, w_ref[:, p * Q_PAIR_IN:(p + 1) * Q_PAIR_IN])
        o_ref[:, p * Q_PAIR_OUT:p * Q_PAIR_OUT + 2 * MLA_NOPE] = acc[:, :2 * MLA_NOPE].astype(BF16)
        pe = acc[:, 2 * MLA_NOPE:2 * MLA_NOPE + LANES] * c + acc[:, 2 * MLA_NOPE + LANES:] * s
        o_ref[:, p * Q_PAIR_OUT + 2 * MLA_NOPE:(p + 1) * Q_PAIR_OUT] = pe.astype(BF16)


def _mla_q_up(q_lat, wq, cm, sm):
    m = q_lat.shape[0]
    tm = min(m, TM_MLA_PROJ)
    n_out = MLA_PAIRS * Q_PAIR_OUT
    return pl.pallas_call(
        _mla_q_up_kernel,
        grid=(m // tm,),
        in_specs=[
            pl.BlockSpec((tm, MLA_Q_RANK), lambda i: (i, 0)),
            pl.BlockSpec(wq.shape, lambda i: (0, 0)),
            pl.BlockSpec((tm, LANES), lambda i: (i, 0)),
            pl.BlockSpec((tm, LANES), lambda i: (i, 0)),
        ],
        out_specs=pl.BlockSpec((tm, n_out), lambda i: (i, 0)),
        out_shape=jax.ShapeDtypeStruct((m, n_out), BF16),
        compiler_params=_params("parallel"),
        name="mla_q_up",
    )(q_lat, wq, cm, sm)


def _matmul_bf16_kernel(a_ref, w_ref, o_ref):
    o_ref[...] = _dot(a_ref[...], w_ref[...]).astype(BF16)


def _mla_kv_up(ckv, wkv):
    m = ckv.shape[0]
    tm = min(m, TM_MLA_PROJ)
    n = wkv.shape[1]
    return pl.pallas_call(
        _matmul_bf16_kernel,
        grid=(m // tm,),
        in_specs=[
            pl.BlockSpec((tm, MLA_KV_RANK), lambda i: (i, 0)),
            pl.BlockSpec(wkv.shape, lambda i: (0, 0)),
        ],
        out_specs=pl.BlockSpec((tm, n), lambda i: (i, 0)),
        out_shape=jax.ShapeDtypeStruct((m, n), BF16),
        compiler_params=_params("parallel"),
        name="mla_kv_up",
    )(ckv, wkv)


def _mla_attn_kernel(q_ref, kn_ref, kpe_ref, v_ref, o_ref, m_ref, l_ref, acc_ref, *, tq, scale):
    g = pl.program_id(2)
    scale_log2e = scale * LOG2_E
    row = lax.broadcasted_iota(jnp.int32, (tq, tq), 0)
    col = lax.broadcasted_iota(jnp.int32, (tq, tq), 1)
    causal = col <= row
    heads = (0, 1)
    for sub in range(2):
        qi = 2 * g + sub
        rows = slice(sub * tq, (sub + 1) * tq)
        q_pe = q_ref[rows, 2 * MLA_NOPE:]
        qs = [jnp.concatenate([q_ref[rows, hh * LANES:(hh + 1) * LANES], q_pe], axis=1) for hh in heads]
        m_ref[...] = jnp.full(m_ref.shape, -jnp.inf, F32)
        l_ref[...] = jnp.zeros(l_ref.shape, F32)
        acc_ref[...] = jnp.zeros(acc_ref.shape, F32)

        def step(hh, kb, masked):
            lanes = slice(hh * LANES, (hh + 1) * LANES)
            ks = pl.multiple_of(kb * tq, tq)
            k = jnp.concatenate([kn_ref[pl.ds(ks, tq), lanes], kpe_ref[pl.ds(ks, tq), lanes]], axis=1)
            s = _dot_nt(qs[hh], k)
            if masked:
                s = jnp.where(causal, s, -jnp.inf)
            m_prev = m_ref[hh]
            m_next = jnp.maximum(m_prev, jnp.max(s, axis=1, keepdims=True))
            m_wide = jnp.concatenate([m_next] * (tq // LANES), axis=1)
            p = jnp.exp2((s - m_wide) * scale_log2e)
            alpha = jnp.exp2((m_prev - m_next) * scale_log2e)
            l_ref[hh] = alpha * l_ref[hh] + jnp.sum(p, axis=1, keepdims=True)
            m_ref[hh] = m_next
            acc_ref[hh] = alpha * acc_ref[hh] + _dot(p.astype(BF16), v_ref[pl.ds(ks, tq), lanes])

        def trip(k2, carry):
            for kb in (2 * k2, 2 * k2 + 1):
                for hh in heads:
                    step(hh, kb, False)
            return carry

        lax.fori_loop(0, g, trip, 0)
        if sub == 1:
            for hh in heads:
                step(hh, qi - 1, False)
        for hh in heads:
            step(hh, qi, True)
        for hh in heads:
            o_ref[rows, hh * LANES:(hh + 1) * LANES] = (acc_ref[hh] / l_ref[hh]).astype(BF16)


def _mla_attention(q, kv, kpe, b, s):
    tq = min(s // 2, TQ_MLA)
    scale = (MLA_NOPE + MLA_ROPE) ** -0.5
    kern = functools.partial(_mla_attn_kernel, tq=tq, scale=scale)
    pair_w = 2 * LANES
    return pl.pallas_call(
        kern,
        grid=(b, MLA_PAIRS, s // (2 * tq)),
        in_specs=[
            pl.BlockSpec((None, 2 * tq, Q_PAIR_OUT), lambda bi, p, g: (bi, g, p)),
            pl.BlockSpec((None, s, pair_w), lambda bi, p, g: (bi, 0, p)),
            pl.BlockSpec((None, s, pair_w), lambda bi, p, g: (bi, 0, 0)),
            pl.BlockSpec((None, s, pair_w), lambda bi, p, g: (bi, 0, MLA_PAIRS + p)),
        ],
        out_specs=pl.BlockSpec((None, 2 * tq, pair_w), lambda bi, p, g: (bi, g, p)),
        out_shape=jax.ShapeDtypeStruct((b, s, MLA_HEADS * MLA_V), BF16),
        scratch_shapes=[
            pltpu.VMEM((2, tq, LANES), F32),
            pltpu.VMEM((2, tq, LANES), F32),
            pltpu.VMEM((2, tq, LANES), F32),
        ],
        compiler_params=_params("parallel", "parallel", "arbitrary"),
        name="mla_attention",
    )(q, kv, kpe, kv)


def _matmul_resid_kernel(a_ref, w_ref, r_ref, o_ref):
    o_ref[...] = r_ref[...] + _dot(a_ref[...], w_ref[...])


def _matmul_resid(a, w, layer, resid):
    m, k = a.shape
    n = w.shape[2]
    tm = min(m, TM_SMALL)
    return pl.pallas_call(
        _matmul_resid_kernel,
        grid=(m // tm,),
        in_specs=[
            pl.BlockSpec((tm, k), lambda i: (i, 0)),
            pl.BlockSpec((None, k, n), lambda i: (layer, 0, 0)),
            pl.BlockSpec((tm, n), lambda i: (i, 0)),
        ],
        out_specs=pl.BlockSpec((tm, n), lambda i: (i, 0)),
        out_shape=jax.ShapeDtypeStruct((m, n), F32),
        compiler_params=_params("parallel"),
        name="out_proj_resid",
    )(a, w, resid)


def _dil_rope(xc, c, sa, sb):
    half = DIL_ROT // 2
    return xc * c + pltpu.roll(xc, half, 1) * sa + pltpu.roll(xc, LANES - half, 1) * sb


def _dil_qkv_kernel(x_ref, g_ref, w_ref, c_ref, sa_ref, sb_ref, o_ref, h_ref, acc_ref, *perm_refs,
                    d, tm, n_rope, nj):
    j = pl.program_id(2)
    rows = tm // d
    if d > 1:
        xs_ref, cs_ref, sas_ref, sbs_ref = perm_refs
    else:
        cs_ref, sas_ref, sbs_ref = c_ref, sa_ref, sb_ref

    def norm_input():
        if d == 1:
            h_ref[...] = _rms(x_ref[...], g_ref[...]).astype(BF16)
            return
        x = x_ref[...]
        rinv = lax.rsqrt(jnp.mean(x * x, axis=-1, keepdims=True) + NORM_EPS)
        for t in range(x_ref.shape[1] // LANES):
            lanes = slice(t * LANES, (t + 1) * LANES)
            slot = t % xs_ref.shape[0]
            xs_ref[slot] = (x_ref[:, lanes] * rinv) * g_ref[:, lanes]
            for r in range(d):
                h_ref[r * rows:(r + 1) * rows, lanes] = xs_ref[slot, pl.ds(r, rows, stride=d), :].astype(BF16)
        for r in range(d):
            sl = pl.ds(r, rows, stride=d)
            dst = slice(r * rows, (r + 1) * rows)
            cs_ref[dst, :] = c_ref[sl, :]
            sas_ref[dst, :] = sa_ref[sl, :]
            sbs_ref[dst, :] = sb_ref[sl, :]

    def matmul():
        acc_ref[...] = _dot(h_ref[...], w_ref[...])

    def emit(rope):
        for r in range(d):
            src = slice(r * rows, (r + 1) * rows)
            if rope:
                c, sa, sb = cs_ref[src, :], sas_ref[src, :], sbs_ref[src, :]
            for t in range(acc_ref.shape[1] // LANES):
                a = acc_ref[src, t * LANES:(t + 1) * LANES]
                if rope:
                    a = _dil_rope(a, c, sa, sb)
                o_ref[r, :, t * LANES:(t + 1) * LANES] = a.astype(BF16)

    @pl.when(j == 0)
    def _():
        norm_input()
        matmul()

    @pl.when(jnp.logical_and(j >= 1, j <= n_rope))
    def _():
        emit(True)
        matmul()

    @pl.when(jnp.logical_and(j > n_rope, j < nj))
    def _():
        emit(False)
        matmul()

    @pl.when(j == nj)
    def _():
        emit(False)


def _dil_qkv(x, gain, w_in, layer, g, cd, sa, sb, d):
    b, s, _ = x.shape
    n = 3 * DIL_HEADS * DIL_HEAD_DIM
    tm = min(s, TM_PROJ)
    tn = TN_PROJ
    col0 = g * (n // tn)
    n_rope = 2 * DIL_HEADS * DIL_HEAD_DIM // tn
    nj = n // tn
    assert n_rope < nj
    kern = functools.partial(_dil_qkv_kernel, d=d, tm=tm, n_rope=n_rope, nj=nj)
    tab = pl.BlockSpec((None, tm, LANES), lambda bi, i, j: (bi, i, 0))
    perm_scratch = []
    if d > 1:
        perm_scratch = [pltpu.VMEM((PERM_SLOTS, tm, LANES), F32)] + [pltpu.VMEM((tm, LANES), F32)] * 3
    return pl.pallas_call(
        kern,
        grid=(b, s // tm, nj + 1),
        in_specs=[
            pl.BlockSpec((None, tm, D_MODEL), lambda bi, i, j: (bi, i, 0)),
            pl.BlockSpec((1, D_MODEL), lambda bi, i, j: (0, 0)),
            pl.BlockSpec((None, D_MODEL, tn), lambda bi, i, j: (layer, 0, col0 + jnp.minimum(j, nj - 1))),
            tab, tab, tab,
        ],
        out_specs=pl.BlockSpec((None, d, tm // d, tn), lambda bi, i, j: (bi, 0, i, jnp.maximum(j - 1, 0))),
        out_shape=jax.ShapeDtypeStruct((b, d, s // d, n), BF16),
        scratch_shapes=[pltpu.VMEM((tm, D_MODEL), BF16), pltpu.VMEM((tm, tn), F32)] + perm_scratch,
        compiler_params=_params("parallel", "parallel", "arbitrary"),
        name=f"dil_qkv_d{d}",
    )(x, gain, w_in, cd, sa, sb)


def _dil_attn_kernel(q_ref, kc_ref, kp_ref, vc_ref, vp_ref, o_ref, lse_ref, s_ref, p_ref, *, scale, nblk):
    jb = pl.program_id(2)
    qb = DIL_BLOCK
    qi = lax.broadcasted_iota(jnp.int32, (qb, 2 * qb), 0)
    ki = lax.broadcasted_iota(jnp.int32, (qb, 2 * qb), 1)
    lane = lax.broadcasted_iota(jnp.int32, (qb, LANES), 1)

    def keys(cur_ref, prev_ref, blk, sl):
        own = cur_ref[blk * qb:(blk + 1) * qb, sl]
        before = prev_ref[:, sl] if blk == 0 else cur_ref[(blk - 1) * qb:blk * qb, sl]
        return jnp.concatenate([before, own], axis=0)

    for blk in range(nblk):
        for h in range(DIL_HEADS):
            sl = slice(h * DIL_HEAD_DIM, (h + 1) * DIL_HEAD_DIM)
            s_ref[blk, h] = _dot_nt(q_ref[blk * qb:(blk + 1) * qb, sl], keys(kc_ref, kp_ref, blk, sl))
    for blk in range(nblk):
        first_key = jnp.where(jb > 0, qi, qb) if blk == 0 else qi
        band = jnp.logical_and(ki >= first_key, ki <= qi + qb)
        s = jnp.where(band[None], s_ref[blk] * scale, -jnp.inf)
        m = jnp.max(s, axis=-1, keepdims=True)
        e = jnp.exp(s - m)
        l = jnp.sum(e, axis=-1, keepdims=True)
        p_ref[blk] = (e * (1.0 / l)).astype(BF16)
        lse = m + jnp.log(l)
        lse_all = jnp.zeros((qb, LANES), F32)
        for h in range(DIL_HEADS):
            lse_all = jnp.where(lane == h, lse[h], lse_all)
        lse_ref[blk * qb:(blk + 1) * qb, :] = lse_all
    for blk in range(nblk):
        for h in range(DIL_HEADS):
            sl = slice(h * DIL_HEAD_DIM, (h + 1) * DIL_HEAD_DIM)
            o_ref[blk * qb:(blk + 1) * qb, sl] = _dot(p_ref[blk, h], keys(vc_ref, vp_ref, blk, sl))


def _dil_attention(qkv_g, d):
    b, _, l, _ = qkv_g.shape
    hw = DIL_HEADS * DIL_HEAD_DIM
    qb = DIL_BLOCK
    nblk = DIL_BLOCKS_PER_STEP
    rows = nblk * qb
    kern = functools.partial(_dil_attn_kernel, scale=DIL_HEAD_DIM ** -0.5, nblk=nblk)

    def cur(c):
        return pl.BlockSpec((None, None, rows, hw), lambda bi, r, jb: (bi, r, jb, c))

    def prev(c):
        return pl.BlockSpec((None, None, qb, hw), lambda bi, r, jb: (bi, r, jnp.maximum(jb * nblk - 1, 0), c))

    return pl.pallas_call(
        kern,
        grid=(b, d, l // rows),
        in_specs=[cur(0), cur(1), prev(1), cur(2), prev(2)],
        out_specs=[
            pl.BlockSpec((None, None, rows, hw), lambda bi, r, jb: (bi, r, jb, 0)),
            pl.BlockSpec((None, None, rows, LANES), lambda bi, r, jb: (bi, r, jb, 0)),
        ],
        out_shape=[
            jax.ShapeDtypeStruct((b, d, l, hw), F32),
            jax.ShapeDtypeStruct((b, d, l, LANES), F32),
        ],
        scratch_shapes=[
            pltpu.VMEM((nblk, DIL_HEADS, qb, 2 * qb), F32),
            pltpu.VMEM((nblk, DIL_HEADS, qb, 2 * qb), BF16),
        ],
        compiler_params=_params("parallel", "parallel", "arbitrary"),
        name=f"dil_attention_d{d}",
    )(qkv_g, qkv_g, qkv_g, qkv_g, qkv_g)


def _dil_merge_kernel(o0_ref, l0_ref, o1_ref, l1_ref, o2_ref, l2_ref, out_ref,
                      n1_ref, n2_ref, nl1_ref, nl2_ref, *, tm):
    d1 = DIL_GROUPS[1][1]
    d2 = DIL_GROUPS[2][1]
    for d, o_ref, l_ref, n_ref, nl_ref in ((d1, o1_ref, l1_ref, n1_ref, nl1_ref),
                                           (d2, o2_ref, l2_ref, n2_ref, nl2_ref)):
        for r in range(d):
            rows = pl.ds(r, tm // d, stride=d)
            nl_ref[rows, :] = l_ref[r]
            for h in range(DIL_HEADS):
                n_ref[h, rows, :] = o_ref[r, :, h * DIL_HEAD_DIM:(h + 1) * DIL_HEAD_DIM]
    for h in range(DIL_HEADS):
        sl = slice(h * DIL_HEAD_DIM, (h + 1) * DIL_HEAD_DIM)
        a0 = l0_ref[:, h:h + 1]
        a1 = nl1_ref[:, h:h + 1]
        a2 = nl2_ref[:, h:h + 1]
        mx = jnp.maximum(jnp.maximum(a0, a1), a2)
        w0 = jnp.exp(a0 - mx)
        w1 = jnp.exp(a1 - mx)
        w2 = jnp.exp(a2 - mx)
        den = w0 + w1 + w2
        o = (w0 / den) * o0_ref[:, sl] + (w1 / den) * n1_ref[h] + (w2 / den) * n2_ref[h]
        out_ref[:, sl] = o.astype(BF16)


def _dil_merge(outs, lses, b, s):
    hw = DIL_HEADS * DIL_HEAD_DIM
    tm = min(s, TM_MERGE)
    d1 = DIL_GROUPS[1][1]
    d2 = DIL_GROUPS[2][1]
    kern = functools.partial(_dil_merge_kernel, tm=tm)

    def grp(d, w):
        return pl.BlockSpec((None, d, tm // d, w), lambda bi, i: (bi, 0, i, 0))

    return pl.pallas_call(
        kern,
        grid=(b, s // tm),
        in_specs=[
            pl.BlockSpec((None, None, tm, hw), lambda bi, i: (bi, 0, i, 0)),
            pl.BlockSpec((None, None, tm, LANES), lambda bi, i: (bi, 0, i, 0)),
            grp(d1, hw), grp(d1, LANES), grp(d2, hw), grp(d2, LANES),
        ],
        out_specs=pl.BlockSpec((None, tm, hw), lambda bi, i: (bi, i, 0)),
        out_shape=jax.ShapeDtypeStruct((b, s, hw), BF16),
        scratch_shapes=[
            pltpu.VMEM((DIL_HEADS, tm, DIL_HEAD_DIM), F32), pltpu.VMEM((DIL_HEADS, tm, DIL_HEAD_DIM), F32),
            pltpu.VMEM((tm, LANES), F32), pltpu.VMEM((tm, LANES), F32),
        ],
        compiler_params=_params("parallel", "parallel"),
        name="dil_merge",
    )(outs[0], lses[0], outs[1], lses[1], outs[2], lses[2])


def _ffn_kernel(x_ref, g_ref, wg_ref, wv_ref, cp_ref, wd_ref, o_ref, h_ref, u_ref, prev_ref, *, tm, tf, nf):
    i = pl.program_id(1)
    t = pl.program_id(2)
    row = lax.broadcasted_iota(jnp.int32, (CARRY_ROWS, 1), 0)

    def up():
        h = h_ref[...]
        u_ref[:, :tf] = _dot(h, wg_ref[...])
        u_ref[:, tf:] = _dot(h, wv_ref[...])

    def conv(u, c):
        prev = prev_ref[c]
        prev_ref[c] = u[tm - CARRY_ROWS:, :]
        p1 = prev[CARRY_ROWS - 1:CARRY_ROWS, :]
        p2 = prev[CARRY_ROWS - 2:CARRY_ROWS - 1, :]
        r1 = pltpu.roll(u, 1, 0)
        r2 = pltpu.roll(u, 2, 0)
        u1 = jnp.concatenate([jnp.where(row == 0, p1, r1[:CARRY_ROWS]), r1[CARRY_ROWS:]], axis=0)
        u2 = jnp.concatenate(
            [jnp.where(row == 0, p2, jnp.where(row == 1, p1, r2[:CARRY_ROWS])), r2[CARRY_ROWS:]], axis=0)
        cp = cp_ref[...]
        acc = cp[CONV_WIDTH:CONV_WIDTH + 1, :] + u2 * cp[0:1, :]
        acc = acc + u1 * cp[1:2, :]
        return acc + u * cp[2:3, :]

    def down(u):
        cv = conv(u, t - 1)
        gate = cv[:, :tf]
        val = cv[:, tf:]
        act = (gate * (1.0 / (1.0 + jnp.exp(-gate)))) * val
        o_ref[...] += _dot(act.astype(BF16), wd_ref[...])

    @pl.when(t == 0)
    def _():
        x = x_ref[...]
        h_ref[...] = _rms(x, g_ref[...]).astype(BF16)
        o_ref[...] = x
        up()

    @pl.when(jnp.logical_and(t == 0, i == 0))
    def _():
        prev_ref[...] = jnp.zeros(prev_ref.shape, F32)

    @pl.when(jnp.logical_and(t > 0, t < nf))
    def _():
        u = u_ref[...]
        up()
        down(u)

    @pl.when(t == nf)
    def _():
        down(u_ref[...])


def _ffn_chunk_order(a, tf):
    lead = a.shape[:-1]
    return a.reshape(*lead, 2, FFN_HIDDEN // tf, tf).swapaxes(-3, -2).reshape(*lead, 2 * FFN_HIDDEN)


def _ffn_weights(w_up, conv_w, conv_b, w_down):
    tf = TF_FFN
    pad = jnp.zeros((conv_w.shape[0], SUBLANES - CONV_WIDTH - 1, conv_w.shape[2]), F32)
    conv_p = jnp.concatenate([conv_w, conv_b[:, None, :], pad], axis=1)
    return w_up.astype(BF16), _ffn_chunk_order(conv_p, tf), w_down.astype(BF16)


def _ffn(x, gain, w_up, conv_p, w_down, layer):
    b, s, _ = x.shape
    tm = min(s, TM_FFN)
    tf = TF_FFN
    nf = FFN_HIDDEN // tf
    kern = functools.partial(_ffn_kernel, tm=tm, tf=tf, nf=nf)

    def up_chunk(t):
        return jnp.minimum(t, nf - 1)

    def down_chunk(t):
        return jnp.maximum(t - 1, 0)

    return pl.pallas_call(
        kern,
        grid=(b, s // tm, nf + 1),
        in_specs=[
            pl.BlockSpec((None, tm, D_MODEL), lambda bi, i, t: (bi, i, 0)),
            pl.BlockSpec((1, D_MODEL), lambda bi, i, t: (0, 0)),
            pl.BlockSpec((None, D_MODEL, tf), lambda bi, i, t: (layer, 0, up_chunk(t))),
            pl.BlockSpec((None, D_MODEL, tf), lambda bi, i, t: (layer, 0, nf + up_chunk(t))),
            pl.BlockSpec((None, SUBLANES, 2 * tf), lambda bi, i, t: (layer, 0, down_chunk(t))),
            pl.BlockSpec((None, tf, D_MODEL), lambda bi, i, t: (layer, down_chunk(t), 0)),
        ],
        out_specs=pl.BlockSpec((None, tm, D_MODEL), lambda bi, i, t: (bi, i, 0)),
        out_shape=jax.ShapeDtypeStruct(x.shape, F32),
        scratch_shapes=[
            pltpu.VMEM((tm, D_MODEL), BF16),
            pltpu.VMEM((tm, 2 * tf), F32),
            pltpu.VMEM((nf, CARRY_ROWS, 2 * tf), F32),
        ],
        compiler_params=_params("arbitrary", "arbitrary", "arbitrary"),
        name="conv_ffn",
    )(x, gain, w_up, w_up, conv_p, w_down)


def _rmsnorm_kernel(x_ref, g_ref, o_ref):
    o_ref[...] = _rms(x_ref[...], g_ref[...])


def _rmsnorm(x2, gain):
    m, dm = x2.shape
    tm = min(m, TM_PROJ)
    return pl.pallas_call(
        _rmsnorm_kernel,
        grid=(m // tm,),
        in_specs=[pl.BlockSpec((tm, dm), lambda i: (i, 0)), pl.BlockSpec((1, dm), lambda i: (0, 0))],
        out_specs=pl.BlockSpec((tm, dm), lambda i: (i, 0)),
        out_shape=jax.ShapeDtypeStruct((m, dm), F32),
        compiler_params=_params("parallel"),
        name="final_norm",
    )(x2, gain)


def _rotate_half_cols(w):
    half = w.shape[-1] // 2
    return jnp.concatenate([-w[..., half:], w[..., :half]], axis=-1)


def _mla_weights(wq_a, wq_b, wkv_a, wkv_b):
    w_pe = wkv_a[:, MLA_KV_RANK:]
    w1 = jnp.concatenate([wq_a, wkv_a[:, :MLA_KV_RANK], w_pe, _rotate_half_cols(w_pe)], axis=1)
    qb = wq_b.reshape(MLA_Q_RANK, MLA_HEADS, MLA_NOPE + MLA_ROPE)
    nope = qb[:, :, :MLA_NOPE].reshape(MLA_Q_RANK, MLA_PAIRS, 2 * MLA_NOPE)
    pe = qb[:, :, MLA_NOPE:]
    rot = _rotate_half_cols(pe).reshape(MLA_Q_RANK, MLA_PAIRS, LANES)
    pe = pe.reshape(MLA_Q_RANK, MLA_PAIRS, LANES)
    wq = jnp.concatenate([nope, pe, rot], axis=2).reshape(MLA_Q_RANK, MLA_PAIRS * Q_PAIR_IN)
    kvb = wkv_b.reshape(MLA_KV_RANK, MLA_HEADS, MLA_NOPE + MLA_V)
    wkv = jnp.concatenate([kvb[:, :, :MLA_NOPE].reshape(MLA_KV_RANK, -1),
                           kvb[:, :, MLA_NOPE:].reshape(MLA_KV_RANK, -1)], axis=1)
    return w1.astype(BF16), wq.astype(BF16), wkv.astype(BF16)


def _mla_layer(x, gain, tabs, wq_a, q_norm, wq_b, wkv_a, kv_norm, wkv_b, wo, layer):
    b, s, dm = x.shape
    cm, sm, tk = tabs
    w1, wq, wkv = _mla_weights(wq_a, wq_b, wkv_a, wkv_b)
    x2 = x.reshape(b * s, dm)
    q_lat, ckv, kpe = _mla_down(x2, gain.reshape(1, dm), w1, q_norm.reshape(1, -1), kv_norm.reshape(1, -1), tk)
    q = _mla_q_up(q_lat, wq, cm, sm)
    kv = _mla_kv_up(ckv, wkv)
    o = _mla_attention(q.reshape(b, s, -1), kv.reshape(b, s, -1), kpe.reshape(b, s, -1), b, s)
    return _matmul_resid(o.reshape(b * s, -1), wo, layer, x2).reshape(b, s, dm)


def _dil_layer(x, gain, tabs, w_in, wo, layer):
    b, s, dm = x.shape
    cd, sa, sb = (t.reshape(b, s, LANES) for t in tabs)
    outs, lses = [], []
    for g, (_, d) in enumerate(DIL_GROUPS):
        qkv_g = _dil_qkv(x, gain.reshape(1, dm), w_in, layer, g, cd, sa, sb, d)
        o, lse = _dil_attention(qkv_g, d)
        outs.append(o)
        lses.append(lse)
    merged = _dil_merge(outs, lses, b, s)
    return _matmul_resid(merged.reshape(b * s, -1), wo, layer, x.reshape(b * s, dm)).reshape(b, s, dm)


def kernel(x, positions, attn_norm, ffn_norm, final_norm, mla_wq_a, mla_q_norm, mla_wq_b, mla_wkv_a,
           mla_kv_norm, mla_wkv_b, mla_wo, dil_w_in, dil_wo, ffn_w_up, ffn_conv_w, ffn_conv_b, ffn_w_down):
    b, s, dm = x.shape
    cm, sm, tk, cd, sa, sb = _rope_tables(positions)
    mla_wo_b = mla_wo.astype(BF16)
    dil_w_in_b = dil_w_in.astype(BF16)
    dil_wo_b = dil_wo.astype(BF16)
    ffn_up_b, ffn_conv_p, ffn_down_b = _ffn_weights(ffn_w_up, ffn_conv_w, ffn_conv_b, ffn_w_down)
    for i in range(DEPTH):
        j = i // N_MIXERS
        if i % N_MIXERS == 0:
            x = _mla_layer(x, attn_norm[i], (cm, sm, tk), mla_wq_a[j], mla_q_norm[j], mla_wq_b[j],
                           mla_wkv_a[j], mla_kv_norm[j], mla_wkv_b[j], mla_wo_b, j)
        else:
            x = _dil_layer(x, attn_norm[i], (cd, sa, sb), dil_w_in_b, dil_wo_b, j)
        x = _ffn(x, ffn_norm[i].reshape(1, dm), ffn_up_b, ffn_conv_p, ffn_down_b, i)
    return _rmsnorm(x.reshape(b * s, dm), final_norm.reshape(1, dm)).reshape(b, s, dm)
```

```python
import functools

import jax
import jax.numpy as jnp
from jax import lax
from jax.experimental import pallas as pl
from jax.experimental.pallas import tpu as pltpu

F32 = jnp.float32
BF16 = jnp.bfloat16

D_MODEL = 2048
DEPTH = 4
N_MIXERS = 2
ROPE_THETA = 500000.0
NORM_EPS = 1e-6
LOG2_E = 1.4426950408889634

MLA_HEADS = 16
MLA_Q_RANK = 512
MLA_KV_RANK = 512
MLA_NOPE = 128
MLA_ROPE = 64
MLA_V = 128

DIL_GROUPS = ((128, 1), (512, 4), (2048, 16))
DIL_HEADS = 16
DIL_HEAD_DIM = 128
DIL_ROT = DIL_HEAD_DIM // 4
DIL_BLOCK = 128
DIL_BLOCKS_PER_STEP = 4

FFN_HIDDEN = 5632
CONV_WIDTH = 3

LANES = 128
SUBLANES = 8
VMEM_BYTES_V7X = 64 * 1024 * 1024
VMEM_LIMIT = VMEM_BYTES_V7X - 8 * 1024 * 1024

TM_PROJ = 1024
TN_PROJ = 1024
TM_SMALL = 512
TM_MLA_PROJ = 1024
TM_MERGE = 256
TQ_MLA = 512
TM_FFN = 512
TF_FFN = 512
CARRY_ROWS = SUBLANES
PERM_SLOTS = 4


def _params(*sem):
    return pltpu.CompilerParams(dimension_semantics=sem, vmem_limit_bytes=VMEM_LIMIT)


def _rms(x, g):
    ms = jnp.mean(x * x, axis=-1, keepdims=True)
    return (x * lax.rsqrt(ms + NORM_EPS)) * g


def _dot(a, b):
    return jnp.dot(a, b, preferred_element_type=F32)


def _dot_nt(a, b):
    return lax.dot_general(a, b, (((1,), (1,)), ((), ())), preferred_element_type=F32)


def _rope_tables_kernel(pos_ref, fm_ref, fd_ref, cm_ref, sm_ref, tk_ref, cd_ref, sa_ref, sb_ref):
    pos = pos_ref[...]
    lane = lax.broadcasted_iota(jnp.int32, (1, LANES), 1)
    am = pos * fm_ref[...]
    cm = jnp.cos(am)
    sm = jnp.sin(am)
    cm_ref[...] = cm
    sm_ref[...] = sm
    tk_ref[...] = jnp.where(lane < MLA_ROPE, cm, sm)
    ad = pos * fd_ref[...]
    cd = jnp.cos(ad)
    sd = jnp.sin(ad)
    half = DIL_ROT // 2
    cd_ref[...] = jnp.where(lane < DIL_ROT, cd, 1.0)
    sa_ref[...] = jnp.where(lane < half, 0.0, jnp.where(lane < DIL_ROT, sd, 0.0))
    sb_ref[...] = jnp.where(lane < half, -sd, 0.0)


def _rope_tables(positions):
    m = positions.size
    pos = positions.reshape(m, 1).astype(F32)
    inv_m = ROPE_THETA ** (-jnp.arange(0, MLA_ROPE, 2, dtype=F32) / MLA_ROPE)
    inv_d = ROPE_THETA ** (-jnp.arange(0, DIL_ROT, 2, dtype=F32) / DIL_ROT)
    fm = jnp.tile(inv_m, LANES // inv_m.size).reshape(1, LANES)
    fd = jnp.tile(inv_d, LANES // inv_d.size).reshape(1, LANES)
    tm = min(m, 2048)
    row = pl.BlockSpec((tm, LANES), lambda i: (i, 0))
    vec = pl.BlockSpec((1, LANES), lambda i: (0, 0))
    tab = jax.ShapeDtypeStruct((m, LANES), F32)
    return pl.pallas_call(
        _rope_tables_kernel,
        grid=(m // tm,),
        in_specs=[pl.BlockSpec((tm, 1), lambda i: (i, 0)), vec, vec],
        out_specs=[row] * 6,
        out_shape=[tab] * 6,
        compiler_params=_params("parallel"),
        name="rope_tables",
    )(pos, fm, fd)


def _mla_down_kernel(x_ref, g_ref, w_ref, qn_ref, kvn_ref, tk_ref, ql_ref, ckv_ref, kpe_ref):
    h = _rms(x_ref[...], g_ref[...]).astype(BF16)
    acc = _dot(h, w_ref[...])
    ql_ref[...] = _rms(acc[:, :MLA_Q_RANK], qn_ref[...]).astype(BF16)
    ckv_ref[...] = _rms(acc[:, MLA_Q_RANK:MLA_Q_RANK + MLA_KV_RANK], kvn_ref[...]).astype(BF16)
    y = acc[:, MLA_Q_RANK + MLA_KV_RANK:] * tk_ref[...]
    z = y + pltpu.roll(y, MLA_ROPE, 1)
    lane = lax.broadcasted_iota(jnp.int32, z.shape, 1)
    kpe_ref[:, :LANES] = jnp.where(lane < MLA_ROPE, z, 0.0).astype(BF16)
    kpe_ref[:, LANES:] = jnp.where(lane < MLA_ROPE, 0.0, z).astype(BF16)


def _mla_down(x2, gain, w1, q_norm, kv_norm, tk):
    m = x2.shape[0]
    tm = min(m, TM_MLA_PROJ)
    n1 = w1.shape[1]
    return pl.pallas_call(
        _mla_down_kernel,
        grid=(m // tm,),
        in_specs=[
            pl.BlockSpec((tm, D_MODEL), lambda i: (i, 0)),
            pl.BlockSpec((1, D_MODEL), lambda i: (0, 0)),
            pl.BlockSpec((D_MODEL, n1), lambda i: (0, 0)),
            pl.BlockSpec((1, MLA_Q_RANK), lambda i: (0, 0)),
            pl.BlockSpec((1, MLA_KV_RANK), lambda i: (0, 0)),
            pl.BlockSpec((tm, LANES), lambda i: (i, 0)),
        ],
        out_specs=[
            pl.BlockSpec((tm, MLA_Q_RANK), lambda i: (i, 0)),
            pl.BlockSpec((tm, MLA_KV_RANK), lambda i: (i, 0)),
            pl.BlockSpec((tm, 2 * LANES), lambda i: (i, 0)),
        ],
        out_shape=[
            jax.ShapeDtypeStruct((m, MLA_Q_RANK), BF16),
            jax.ShapeDtypeStruct((m, MLA_KV_RANK), BF16),
            jax.ShapeDtypeStruct((m, 2 * LANES), BF16),
        ],
        compiler_params=_params("parallel"),
        name="mla_down",
    )(x2, gain, w1, q_norm, kv_norm, tk)


MLA_PAIRS = MLA_HEADS // 2
Q_PAIR_IN = 2 * MLA_NOPE + 2 * LANES
Q_PAIR_OUT = 2 * MLA_NOPE + LANES


def _mla_q_up_kernel(a_ref, w_ref, c_ref, s_ref, o_ref):
    a = a_ref[...]
    c = c_ref[...]
    s = s_ref[...]
    for p in range(MLA_PAIRS):
        acc = _dot(a, w_ref[:, p * Q_PAIR_IN:(p + 1) * Q_PAIR_IN])
        o_ref[:, p * Q_PAIR_OUT:p * Q_PAIR_OUT + 2 * MLA_NOPE] = acc[:, :2 * MLA_NOPE].astype(BF16)
        pe = acc[:, 2 * MLA_NOPE:2 * MLA_NOPE + LANES] * c + acc[:, 2 * MLA_NOPE + LANES:] * s
        o_ref[:, p * Q_PAIR_OUT + 2 * MLA_NOPE:(p + 1) * Q_PAIR_OUT] = pe.astype(BF16)


def _mla_q_up(q_lat, wq, cm, sm):
    m = q_lat.shape[0]
    tm = min(m, TM_MLA_PROJ)
    n_out = MLA_PAIRS * Q_PAIR_OUT
    return pl.pallas_call(
        _mla_q_up_kernel,
        grid=(m // tm,),
        in_specs=[
            pl.BlockSpec((tm, MLA_Q_RANK), lambda i: (i, 0)),
            pl.BlockSpec(wq.shape, lambda i: (0, 0)),
            pl.BlockSpec((tm, LANES), lambda i: (i, 0)),
            pl.BlockSpec((tm, LANES), lambda i: (i, 0)),
        ],
        out_specs=pl.BlockSpec((tm, n_out), lambda i: (i, 0)),
        out_shape=jax.ShapeDtypeStruct((m, n_out), BF16),
        compiler_params=_params("parallel"),
        name="mla_q_up",
    )(q_lat, wq, cm, sm)


def _matmul_bf16_kernel(a_ref, w_ref, o_ref):
    o_ref[...] = _dot(a_ref[...], w_ref[...]).astype(BF16)


def _mla_kv_up(ckv, wkv):
    m = ckv.shape[0]
    tm = min(m, TM_MLA_PROJ)
    n = wkv.shape[1]
    return pl.pallas_call(
        _matmul_bf16_kernel,
        grid=(m // tm,),
        in_specs=[
            pl.BlockSpec((tm, MLA_KV_RANK), lambda i: (i, 0)),
            pl.BlockSpec(wkv.shape, lambda i: (0, 0)),
        ],
        out_specs=pl.BlockSpec((tm, n), lambda i: (i, 0)),
        out_shape=jax.ShapeDtypeStruct((m, n), BF16),
        compiler_params=_params("parallel"),
        name="mla_kv_up",
    )(ckv, wkv)


def _mla_attn_kernel(q_ref, kn_ref, kpe_ref, v_ref, o_ref, m_ref, l_ref, acc_ref, *, tq, scale):
    g = pl.program_id(2)
    scale_log2e = scale * LOG2_E
    row = lax.broadcasted_iota(jnp.int32, (tq, tq), 0)
    col = lax.broadcasted_iota(jnp.int32, (tq, tq), 1)
    causal = col <= row
    heads = (0, 1)
    for sub in range(2):
        qi = 2 * g + sub
        rows = slice(sub * tq, (sub + 1) * tq)
        q_pe = q_ref[rows, 2 * MLA_NOPE:]
        qs = [jnp.concatenate([q_ref[rows, hh * LANES:(hh + 1) * LANES], q_pe], axis=1) for hh in heads]
        m_ref[...] = jnp.full(m_ref.shape, -jnp.inf, F32)
        l_ref[...] = jnp.zeros(l_ref.shape, F32)
        acc_ref[...] = jnp.zeros(acc_ref.shape, F32)

        def step(hh, kb, masked):
            lanes = slice(hh * LANES, (hh + 1) * LANES)
            ks = pl.multiple_of(kb * tq, tq)
            k = jnp.concatenate([kn_ref[pl.ds(ks, tq), lanes], kpe_ref[pl.ds(ks, tq), lanes]], axis=1)
            s = _dot_nt(qs[hh], k)
            if masked:
                s = jnp.where(causal, s, -jnp.inf)
            m_prev = m_ref[hh]
            m_next = jnp.maximum(m_prev, jnp.max(s, axis=1, keepdims=True))
            m_wide = jnp.concatenate([m_next] * (tq // LANES), axis=1)
            p = jnp.exp2((s - m_wide) * scale_log2e)
            alpha = jnp.exp2((m_prev - m_next) * scale_log2e)
            l_ref[hh] = alpha * l_ref[hh] + jnp.sum(p, axis=1, keepdims=True)
            m_ref[hh] = m_next
            acc_ref[hh] = alpha * acc_ref[hh] + _dot(p.astype(BF16), v_ref[pl.ds(ks, tq), lanes])

        def trip(k2, carry):
            for kb in (2 * k2, 2 * k2 + 1):
                for hh in heads:
                    step(hh, kb, False)
            return carry

        lax.fori_loop(0, g, trip, 0)
        if sub == 1:
            for hh in heads:
                step(hh, qi - 1, False)
        for hh in heads:
            step(hh, qi, True)
        for hh in heads:
            o_ref[rows, hh * LANES:(hh + 1) * LANES] = (acc_ref[hh] / l_ref[hh]).astype(BF16)


def _mla_attention(q, kv, kpe, b, s):
    tq = min(s // 2, TQ_MLA)
    scale = (MLA_NOPE + MLA_ROPE) ** -0.5
    kern = functools.partial(_mla_attn_kernel, tq=tq, scale=scale)
    pair_w = 2 * LANES
    return pl.pallas_call(
        kern,
        grid=(b, MLA_PAIRS, s // (2 * tq)),
        in_specs=[
            pl.BlockSpec((None, 2 * tq, Q_PAIR_OUT), lambda bi, p, g: (bi, g, p)),
            pl.BlockSpec((None, s, pair_w), lambda bi, p, g: (bi, 0, p)),
            pl.BlockSpec((None, s, pair_w), lambda bi, p, g: (bi, 0, 0)),
            pl.BlockSpec((None, s, pair_w), lambda bi, p, g: (bi, 0, MLA_PAIRS + p)),
        ],
        out_specs=pl.BlockSpec((None, 2 * tq, pair_w), lambda bi, p, g: (bi, g, p)),
        out_shape=jax.ShapeDtypeStruct((b, s, MLA_HEADS * MLA_V), BF16),
        scratch_shapes=[
            pltpu.VMEM((2, tq, LANES), F32),
            pltpu.VMEM((2, tq, LANES), F32),
            pltpu.VMEM((2, tq, LANES), F32),
        ],
        compiler_params=_params("parallel", "parallel", "arbitrary"),
        name="mla_attention",
    )(q, kv, kpe, kv)


def _matmul_resid_kernel(a_ref, w_ref, r_ref, o_ref):
    o_ref[...] = r_ref[...] + _dot(a_ref[...], w_ref[...])


def _matmul_resid(a, w, layer, resid):
    m, k = a.shape
    n = w.shape[2]
    tm = min(m, TM_SMALL)
    return pl.pallas_call(
        _matmul_resid_kernel,
        grid=(m // tm,),
        in_specs=[
            pl.BlockSpec((tm, k), lambda i: (i, 0)),
            pl.BlockSpec((None, k, n), lambda i: (layer, 0, 0)),
            pl.BlockSpec((tm, n), lambda i: (i, 0)),
        ],
        out_specs=pl.BlockSpec((tm, n), lambda i: (i, 0)),
        out_shape=jax.ShapeDtypeStruct((m, n), F32),
        compiler_params=_params("parallel"),
        name="out_proj_resid",
    )(a, w, resid)


def _dil_rope(xc, c, sa, sb):
    half = DIL_ROT // 2
    return xc * c + pltpu.roll(xc, half, 1) * sa + pltpu.roll(xc, LANES - half, 1) * sb


def _dil_qkv_kernel(x_ref, g_ref, w_ref, c_ref, sa_ref, sb_ref, o_ref, h_ref, acc_ref, *perm_refs,
                    d, tm, n_rope, nj):
    j = pl.program_id(2)
    rows = tm // d
    if d > 1:
        xs_ref, cs_ref, sas_ref, sbs_ref = perm_refs
    else:
        cs_ref, sas_ref, sbs_ref = c_ref, sa_ref, sb_ref

    def norm_input():
        if d == 1:
            h_ref[...] = _rms(x_ref[...], g_ref[...]).astype(BF16)
            return
        x = x_ref[...]
        rinv = lax.rsqrt(jnp.mean(x * x, axis=-1, keepdims=True) + NORM_EPS)
        for t in range(x_ref.shape[1] // LANES):
            lanes = slice(t * LANES, (t + 1) * LANES)
            slot = t % xs_ref.shape[0]
            xs_ref[slot] = (x_ref[:, lanes] * rinv) * g_ref[:, lanes]
            for r in range(d):
                h_ref[r * rows:(r + 1) * rows, lanes] = xs_ref[slot, pl.ds(r, rows, stride=d), :].astype(BF16)
        for r in range(d):
            sl = pl.ds(r, rows, stride=d)
            dst = slice(r * rows, (r + 1) * rows)
            cs_ref[dst, :] = c_ref[sl, :]
            sas_ref[dst, :] = sa_ref[sl, :]
            sbs_ref[dst, :] = sb_ref[sl, :]

    def matmul():
        acc_ref[...] = _dot(h_ref[...], w_ref[...])

    def emit(rope):
        for r in range(d):
            src = slice(r * rows, (r + 1) * rows)
            if rope:
                c, sa, sb = cs_ref[src, :], sas_ref[src, :], sbs_ref[src, :]
            for t in range(acc_ref.shape[1] // LANES):
                a = acc_ref[src, t * LANES:(t + 1) * LANES]
                if rope:
                    a = _dil_rope(a, c, sa, sb)
                o_ref[r, :, t * LANES:(t + 1) * LANES] = a.astype(BF16)

    @pl.when(j == 0)
    def _():
        norm_input()
        matmul()

    @pl.when(jnp.logical_and(j >= 1, j <= n_rope))
    def _():
        emit(True)
        matmul()

    @pl.when(jnp.logical_and(j > n_rope, j < nj))
    def _():
        emit(False)
        matmul()

    @pl.when(j == nj)
    def _():
        emit(False)


def _dil_qkv(x, gain, w_in, layer, g, cd, sa, sb, d):
    b, s, _ = x.shape
    n = 3 * DIL_HEADS * DIL_HEAD_DIM
    tm = min(s, TM_PROJ)
    tn = TN_PROJ
    col0 = g * (n // tn)
    n_rope = 2 * DIL_HEADS * DIL_HEAD_DIM // tn
    nj = n // tn
    assert n_rope < nj
    kern = functools.partial(_dil_qkv_kernel, d=d, tm=tm, n_rope=n_rope, nj=nj)
    tab = pl.BlockSpec((None, tm, LANES), lambda bi, i, j: (bi, i, 0))
    perm_scratch = []
    if d > 1:
        perm_scratch = [pltpu.VMEM((PERM_SLOTS, tm, LANES), F32)] + [pltpu.VMEM((tm, LANES), F32)] * 3
    return pl.pallas_call(
        kern,
        grid=(b, s // tm, nj + 1),
        in_specs=[
            pl.BlockSpec((None, tm, D_MODEL), lambda bi, i, j: (bi, i, 0)),
            pl.BlockSpec((1, D_MODEL), lambda bi, i, j: (0, 0)),
            pl.BlockSpec((None, D_MODEL, tn), lambda bi, i, j: (layer, 0, col0 + jnp.minimum(j, nj - 1))),
            tab, tab, tab,
        ],
        out_specs=pl.BlockSpec((None, d, tm // d, tn), lambda bi, i, j: (bi, 0, i, jnp.maximum(j - 1, 0))),
        out_shape=jax.ShapeDtypeStruct((b, d, s // d, n), BF16),
        scratch_shapes=[pltpu.VMEM((tm, D_MODEL), BF16), pltpu.VMEM((tm, tn), F32)] + perm_scratch,
        compiler_params=_params("parallel", "parallel", "arbitrary"),
        name=f"dil_qkv_d{d}",
    )(x, gain, w_in, cd, sa, sb)


def _dil_attn_kernel(q_ref, kc_ref, kp_ref, vc_ref, vp_ref, o_ref, lse_ref, s_ref, p_ref, *, scale, nblk):
    jb = pl.program_id(2)
    qb = DIL_BLOCK
    qi = lax.broadcasted_iota(jnp.int32, (qb, 2 * qb), 0)
    ki = lax.broadcasted_iota(jnp.int32, (qb, 2 * qb), 1)
    lane = lax.broadcasted_iota(jnp.int32, (qb, LANES), 1)

    def keys(cur_ref, prev_ref, blk, sl):
        own = cur_ref[blk * qb:(blk + 1) * qb, sl]
        before = prev_ref[:, sl] if blk == 0 else cur_ref[(blk - 1) * qb:blk * qb, sl]
        return jnp.concatenate([before, own], axis=0)

    for blk in range(nblk):
        for h in range(DIL_HEADS):
            sl = slice(h * DIL_HEAD_DIM, (h + 1) * DIL_HEAD_DIM)
            s_ref[blk, h] = _dot_nt(q_ref[blk * qb:(blk + 1) * qb, sl], keys(kc_ref, kp_ref, blk, sl))
    for blk in range(nblk):
        first_key = jnp.where(jb > 0, qi, qb) if blk == 0 else qi
        band = jnp.logical_and(ki >= first_key, ki <= qi + qb)
        s = jnp.where(band[None], s_ref[blk], -jnp.inf)
        m = jnp.max(s, axis=-1, keepdims=True)
        e = jnp.exp2((s - m) * (scale * LOG2_E))
        l = jnp.sum(e, axis=-1, keepdims=True)
        p_ref[blk] = e.astype(BF16)
        inv = 1.0 / l
        lse = scale * m + jnp.log(l)
        lse_all = jnp.zeros((qb, LANES), F32)
        for h in range(DIL_HEADS):
            lse_all = jnp.where(lane == h, lse[h], lse_all)
        lse_ref[blk * qb:(blk + 1) * qb, :] = lse_all
        for h in range(DIL_HEADS):
            sl = slice(h * DIL_HEAD_DIM, (h + 1) * DIL_HEAD_DIM)
            pv = _dot(p_ref[blk, h], keys(vc_ref, vp_ref, blk, sl))
            o_ref[blk * qb:(blk + 1) * qb, sl] = pv * inv[h]


def _dil_attention(qkv_g, d):
    b, _, l, _ = qkv_g.shape
    hw = DIL_HEADS * DIL_HEAD_DIM
    qb = DIL_BLOCK
    nblk = min(DIL_BLOCKS_PER_STEP, l // qb)
    rows = nblk * qb
    kern = functools.partial(_dil_attn_kernel, scale=DIL_HEAD_DIM ** -0.5, nblk=nblk)

    def cur(c):
        return pl.BlockSpec((None, None, rows, hw), lambda bi, r, jb: (bi, r, jb, c))

    def prev(c):
        return pl.BlockSpec((None, None, qb, hw), lambda bi, r, jb: (bi, r, jnp.maximum(jb * nblk - 1, 0), c))

    return pl.pallas_call(
        kern,
        grid=(b, d, l // rows),
        in_specs=[cur(0), cur(1), prev(1), cur(2), prev(2)],
        out_specs=[
            pl.BlockSpec((None, None, rows, hw), lambda bi, r, jb: (bi, r, jb, 0)),
            pl.BlockSpec((None, None, rows, LANES), lambda bi, r, jb: (bi, r, jb, 0)),
        ],
        out_shape=[
            jax.ShapeDtypeStruct((b, d, l, hw), F32),
            jax.ShapeDtypeStruct((b, d, l, LANES), F32),
        ],
        scratch_shapes=[
            pltpu.VMEM((nblk, DIL_HEADS, qb, 2 * qb), F32),
            pltpu.VMEM((nblk, DIL_HEADS, qb, 2 * qb), BF16),
        ],
        compiler_params=_params("parallel", "parallel", "arbitrary"),
        name=f"dil_attention_d{d}",
    )(qkv_g, qkv_g, qkv_g, qkv_g, qkv_g)


def _dil_merge_kernel(o0_ref, l0_ref, o1_ref, l1_ref, o2_ref, l2_ref, out_ref,
                      n1_ref, n2_ref, nl1_ref, nl2_ref, *, tm):
    d1 = DIL_GROUPS[1][1]
    d2 = DIL_GROUPS[2][1]
    for d, o_ref, l_ref, n_ref, nl_ref in ((d1, o1_ref, l1_ref, n1_ref, nl1_ref),
                                           (d2, o2_ref, l2_ref, n2_ref, nl2_ref)):
        for r in range(d):
            rows = pl.ds(r, tm // d, stride=d)
            nl_ref[rows, :] = l_ref[r]
            for h in range(DIL_HEADS):
                n_ref[h, rows, :] = o_ref[r, :, h * DIL_HEAD_DIM:(h + 1) * DIL_HEAD_DIM]
    for h in range(DIL_HEADS):
        sl = slice(h * DIL_HEAD_DIM, (h + 1) * DIL_HEAD_DIM)
        a0 = l0_ref[:, h:h + 1]
        a1 = nl1_ref[:, h:h + 1]
        a2 = nl2_ref[:, h:h + 1]
        mx = jnp.maximum(jnp.maximum(a0, a1), a2)
        w0 = jnp.exp(a0 - mx)
        w1 = jnp.exp(a1 - mx)
        w2 = jnp.exp(a2 - mx)
        den = w0 + w1 + w2
        o = (w0 / den) * o0_ref[:, sl] + (w1 / den) * n1_ref[h] + (w2 / den) * n2_ref[h]
        out_ref[:, sl] = o.astype(BF16)


def _dil_merge(outs, lses, b, s):
    hw = DIL_HEADS * DIL_HEAD_DIM
    tm = min(s, TM_MERGE)
    d1 = DIL_GROUPS[1][1]
    d2 = DIL_GROUPS[2][1]
    kern = functools.partial(_dil_merge_kernel, tm=tm)

    def grp(d, w):
        return pl.BlockSpec((None, d, tm // d, w), lambda bi, i: (bi, 0, i, 0))

    return pl.pallas_call(
        kern,
        grid=(b, s // tm),
        in_specs=[
            pl.BlockSpec((None, None, tm, hw), lambda bi, i: (bi, 0, i, 0)),
            pl.BlockSpec((None, None, tm, LANES), lambda bi, i: (bi, 0, i, 0)),
            grp(d1, hw), grp(d1, LANES), grp(d2, hw), grp(d2, LANES),
        ],
        out_specs=pl.BlockSpec((None, tm, hw), lambda bi, i: (bi, i, 0)),
        out_shape=jax.ShapeDtypeStruct((b, s, hw), BF16),
        scratch_shapes=[
            pltpu.VMEM((DIL_HEADS, tm, DIL_HEAD_DIM), F32), pltpu.VMEM((DIL_HEADS, tm, DIL_HEAD_DIM), F32),
            pltpu.VMEM((tm, LANES), F32), pltpu.VMEM((tm, LANES), F32),
        ],
        compiler_params=_params("parallel", "parallel"),
        name="dil_merge",
    )(outs[0], lses[0], outs[1], lses[1], outs[2], lses[2])


def _ffn_kernel(x_ref, g_ref, og_ref, wg_ref, wv_ref, cp_ref, wd_ref, o_ref, h_ref, u_ref, prev_ref, *,
                tm, tf, nf, out_norm):
    i = pl.program_id(1)
    t = pl.program_id(2)
    row = lax.broadcasted_iota(jnp.int32, (CARRY_ROWS, 1), 0)

    def up():
        h = h_ref[...]
        u_ref[:, :tf] = _dot(h, wg_ref[...])
        u_ref[:, tf:] = _dot(h, wv_ref[...])

    def conv(u, c):
        prev = prev_ref[c]
        prev_ref[c] = u[tm - CARRY_ROWS:, :]
        p1 = prev[CARRY_ROWS - 1:CARRY_ROWS, :]
        p2 = prev[CARRY_ROWS - 2:CARRY_ROWS - 1, :]
        r1 = pltpu.roll(u, 1, 0)
        r2 = pltpu.roll(u, 2, 0)
        u1 = jnp.concatenate([jnp.where(row == 0, p1, r1[:CARRY_ROWS]), r1[CARRY_ROWS:]], axis=0)
        u2 = jnp.concatenate(
            [jnp.where(row == 0, p2, jnp.where(row == 1, p1, r2[:CARRY_ROWS])), r2[CARRY_ROWS:]], axis=0)
        cp = cp_ref[...]
        acc = cp[CONV_WIDTH:CONV_WIDTH + 1, :] + u2 * cp[0:1, :]
        acc = acc + u1 * cp[1:2, :]
        return acc + u * cp[2:3, :]

    def down(u):
        cv = conv(u, t - 1)
        gate = cv[:, :tf]
        val = cv[:, tf:]
        act = (gate * (1.0 / (1.0 + jnp.exp(-gate)))) * val
        o_ref[...] += _dot(act.astype(BF16), wd_ref[...])

    @pl.when(t == 0)
    def _():
        x = x_ref[...]
        h_ref[...] = _rms(x, g_ref[...]).astype(BF16)
        o_ref[...] = x
        up()

    @pl.when(jnp.logical_and(t == 0, i == 0))
    def _():
        prev_ref[...] = jnp.zeros(prev_ref.shape, F32)

    @pl.when(jnp.logical_and(t > 0, t < nf))
    def _():
        u = u_ref[...]
        up()
        down(u)

    @pl.when(t == nf)
    def _():
        down(u_ref[...])
        if out_norm:
            o_ref[...] = _rms(o_ref[...], og_ref[...])


def _ffn_chunk_order(a, tf):
    lead = a.shape[:-1]
    return a.reshape(*lead, 2, FFN_HIDDEN // tf, tf).swapaxes(-3, -2).reshape(*lead, 2 * FFN_HIDDEN)


def _ffn_weights(w_up, conv_w, conv_b, w_down):
    tf = TF_FFN
    pad = jnp.zeros((conv_w.shape[0], SUBLANES - CONV_WIDTH - 1, conv_w.shape[2]), F32)
    conv_p = jnp.concatenate([conv_w, conv_b[:, None, :], pad], axis=1)
    return w_up.astype(BF16), _ffn_chunk_order(conv_p, tf), w_down.astype(BF16)


def _ffn(x, gain, out_gain, w_up, conv_p, w_down, layer, out_norm):
    b, s, _ = x.shape
    tm = min(s, TM_FFN)
    tf = TF_FFN
    nf = FFN_HIDDEN // tf
    kern = functools.partial(_ffn_kernel, tm=tm, tf=tf, nf=nf, out_norm=out_norm)

    def up_chunk(t):
        return jnp.minimum(t, nf - 1)

    def down_chunk(t):
        return jnp.maximum(t - 1, 0)

    return pl.pallas_call(
        kern,
        grid=(b, s // tm, nf + 1),
        in_specs=[
            pl.BlockSpec((None, tm, D_MODEL), lambda bi, i, t: (bi, i, 0)),
            pl.BlockSpec((1, D_MODEL), lambda bi, i, t: (0, 0)),
            pl.BlockSpec((1, D_MODEL), lambda bi, i, t: (0, 0)),
            pl.BlockSpec((None, D_MODEL, tf), lambda bi, i, t: (layer, 0, up_chunk(t))),
            pl.BlockSpec((None, D_MODEL, tf), lambda bi, i, t: (layer, 0, nf + up_chunk(t))),
            pl.BlockSpec((None, SUBLANES, 2 * tf), lambda bi, i, t: (layer, 0, down_chunk(t))),
            pl.BlockSpec((None, tf, D_MODEL), lambda bi, i, t: (layer, down_chunk(t), 0)),
        ],
        out_specs=pl.BlockSpec((None, tm, D_MODEL), lambda bi, i, t: (bi, i, 0)),
        out_shape=jax.ShapeDtypeStruct(x.shape, F32),
        scratch_shapes=[
            pltpu.VMEM((tm, D_MODEL), BF16),
            pltpu.VMEM((tm, 2 * tf), F32),
            pltpu.VMEM((nf, CARRY_ROWS, 2 * tf), F32),
        ],
        compiler_params=_params("arbitrary", "arbitrary", "arbitrary"),
        name="conv_ffn",
    )(x, gain, out_gain, w_up, w_up, conv_p, w_down)


def _rotate_half_cols(w):
    half = w.shape[-1] // 2
    return jnp.concatenate([-w[..., half:], w[..., :half]], axis=-1)


def _mla_weights(wq_a, wq_b, wkv_a, wkv_b):
    w_pe = wkv_a[:, MLA_KV_RANK:]
    w1 = jnp.concatenate([wq_a, wkv_a[:, :MLA_KV_RANK], w_pe, _rotate_half_cols(w_pe)], axis=1)
    qb = wq_b.reshape(MLA_Q_RANK, MLA_HEADS, MLA_NOPE + MLA_ROPE)
    nope = qb[:, :, :MLA_NOPE].reshape(MLA_Q_RANK, MLA_PAIRS, 2 * MLA_NOPE)
    pe = qb[:, :, MLA_NOPE:]
    rot = _rotate_half_cols(pe).reshape(MLA_Q_RANK, MLA_PAIRS, LANES)
    pe = pe.reshape(MLA_Q_RANK, MLA_PAIRS, LANES)
    wq = jnp.concatenate([nope, pe, rot], axis=2).reshape(MLA_Q_RANK, MLA_PAIRS * Q_PAIR_IN)
    kvb = wkv_b.reshape(MLA_KV_RANK, MLA_HEADS, MLA_NOPE + MLA_V)
    wkv = jnp.concatenate([kvb[:, :, :MLA_NOPE].reshape(MLA_KV_RANK, -1),
                           kvb[:, :, MLA_NOPE:].reshape(MLA_KV_RANK, -1)], axis=1)
    return w1.astype(BF16), wq.astype(BF16), wkv.astype(BF16)


def _mla_layer(x, gain, tabs, wq_a, q_norm, wq_b, wkv_a, kv_norm, wkv_b, wo, layer):
    b, s, dm = x.shape
    cm, sm, tk = tabs
    w1, wq, wkv = _mla_weights(wq_a, wq_b, wkv_a, wkv_b)
    x2 = x.reshape(b * s, dm)
    q_lat, ckv, kpe = _mla_down(x2, gain.reshape(1, dm), w1, q_norm.reshape(1, -1), kv_norm.reshape(1, -1), tk)
    q = _mla_q_up(q_lat, wq, cm, sm)
    kv = _mla_kv_up(ckv, wkv)
    o = _mla_attention(q.reshape(b, s, -1), kv.reshape(b, s, -1), kpe.reshape(b, s, -1), b, s)
    return _matmul_resid(o.reshape(b * s, -1), wo, layer, x2).reshape(b, s, dm)


def _dil_layer(x, gain, tabs, w_in, wo, layer):
    b, s, dm = x.shape
    cd, sa, sb = (t.reshape(b, s, LANES) for t in tabs)
    outs, lses = [], []
    for g, (_, d) in enumerate(DIL_GROUPS):
        qkv_g = _dil_qkv(x, gain.reshape(1, dm), w_in, layer, g, cd, sa, sb, d)
        o, lse = _dil_attention(qkv_g, d)
        outs.append(o)
        lses.append(lse)
    merged = _dil_merge(outs, lses, b, s)
    return _matmul_resid(merged.reshape(b * s, -1), wo, layer, x.reshape(b * s, dm)).reshape(b, s, dm)


def kernel(x, positions, attn_norm, ffn_norm, final_norm, mla_wq_a, mla_q_norm, mla_wq_b, mla_wkv_a,
           mla_kv_norm, mla_wkv_b, mla_wo, dil_w_in, dil_wo, ffn_w_up, ffn_conv_w, ffn_conv_b, ffn_w_down):
    b, s, dm = x.shape
    cm, sm, tk, cd, sa, sb = _rope_tables(positions)
    mla_wo_b = mla_wo.astype(BF16)
    dil_w_in_b = dil_w_in.astype(BF16)
    dil_wo_b = dil_wo.astype(BF16)
    ffn_up_b, ffn_conv_p, ffn_down_b = _ffn_weights(ffn_w_up, ffn_conv_w, ffn_conv_b, ffn_w_down)
    for i in range(DEPTH):
        j = i // N_MIXERS
        if i % N_MIXERS == 0:
            x = _mla_layer(x, attn_norm[i], (cm, sm, tk), mla_wq_a[j], mla_q_norm[j], mla_wq_b[j],
                           mla_wkv_a[j], mla_kv_norm[j], mla_wkv_b[j], mla_wo_b, j)
        else:
            x = _dil_layer(x, attn_norm[i], (cd, sa, sb), dil_w_in_b, dil_wo_b, j)
        x = _ffn(x, ffn_norm[i].reshape(1, dm), final_norm.reshape(1, dm), ffn_up_b, ffn_conv_p, ffn_down_b,
                 i, i == DEPTH - 1)
    return x
```

```python
import functools

import jax
import jax.numpy as jnp
from jax import lax
from jax.experimental import pallas as pl
from jax.experimental.pallas import tpu as pltpu

F32 = jnp.float32
BF16 = jnp.bfloat16

D_MODEL = 2048
DEPTH = 4
N_MIXERS = 2
ROPE_THETA = 500000.0
NORM_EPS = 1e-6
LOG2_E = 1.4426950408889634

MLA_HEADS = 16
MLA_Q_RANK = 512
MLA_KV_RANK = 512
MLA_NOPE = 128
MLA_ROPE = 64
MLA_V = 128

DIL_GROUPS = ((128, 1), (512, 4), (2048, 16))
DIL_HEADS = 16
DIL_HEAD_DIM = 128
DIL_ROT = DIL_HEAD_DIM // 4
DIL_BLOCK = 128
DIL_BLOCKS_PER_STEP = 4

FFN_HIDDEN = 5632
CONV_WIDTH = 3

LANES = 128
SUBLANES = 8
VMEM_BYTES_V7X = 64 * 1024 * 1024
VMEM_LIMIT = VMEM_BYTES_V7X - 8 * 1024 * 1024

TM_PROJ = 1024
TN_PROJ = 1024
TM_SMALL = 512
TM_MLA_PROJ = 1024
TM_MERGE = 256
TQ_MLA = 512
TM_FFN = 1024
TF_FFN = 512
CARRY_ROWS = SUBLANES
PERM_SLOTS = 4


def _params(*sem):
    return pltpu.CompilerParams(dimension_semantics=sem, vmem_limit_bytes=VMEM_LIMIT)


def _rms(x, g):
    ms = jnp.mean(x * x, axis=-1, keepdims=True)
    return (x * lax.rsqrt(ms + NORM_EPS)) * g


def _dot(a, b):
    return jnp.dot(a, b, preferred_element_type=F32)


def _dot_nt(a, b):
    return lax.dot_general(a, b, (((1,), (1,)), ((), ())), preferred_element_type=F32)


def _rope_tables_kernel(pos_ref, fm_ref, fd_ref, cm_ref, sm_ref, tk_ref, cd_ref, sa_ref, sb_ref):
    pos = pos_ref[...]
    lane = lax.broadcasted_iota(jnp.int32, (1, LANES), 1)
    am = pos * fm_ref[...]
    cm = jnp.cos(am)
    sm = jnp.sin(am)
    cm_ref[...] = cm
    sm_ref[...] = sm
    tk_ref[...] = jnp.where(lane < MLA_ROPE, cm, sm)
    ad = pos * fd_ref[...]
    cd = jnp.cos(ad)
    sd = jnp.sin(ad)
    half = DIL_ROT // 2
    cd_ref[...] = jnp.where(lane < DIL_ROT, cd, 1.0)
    sa_ref[...] = jnp.where(lane < half, 0.0, jnp.where(lane < DIL_ROT, sd, 0.0))
    sb_ref[...] = jnp.where(lane < half, -sd, 0.0)


def _rope_tables(positions):
    m = positions.size
    pos = positions.reshape(m, 1).astype(F32)
    inv_m = ROPE_THETA ** (-jnp.arange(0, MLA_ROPE, 2, dtype=F32) / MLA_ROPE)
    inv_d = ROPE_THETA ** (-jnp.arange(0, DIL_ROT, 2, dtype=F32) / DIL_ROT)
    fm = jnp.tile(inv_m, LANES // inv_m.size).reshape(1, LANES)
    fd = jnp.tile(inv_d, LANES // inv_d.size).reshape(1, LANES)
    tm = min(m, 2048)
    row = pl.BlockSpec((tm, LANES), lambda i: (i, 0))
    vec = pl.BlockSpec((1, LANES), lambda i: (0, 0))
    tab = jax.ShapeDtypeStruct((m, LANES), F32)
    return pl.pallas_call(
        _rope_tables_kernel,
        grid=(m // tm,),
        in_specs=[pl.BlockSpec((tm, 1), lambda i: (i, 0)), vec, vec],
        out_specs=[row] * 6,
        out_shape=[tab] * 6,
        compiler_params=_params("parallel"),
        name="rope_tables",
    )(pos, fm, fd)


def _mla_down_kernel(x_ref, g_ref, w_ref, qn_ref, kvn_ref, tk_ref, ql_ref, ckv_ref, kpe_ref):
    h = _rms(x_ref[...], g_ref[...]).astype(BF16)
    acc = _dot(h, w_ref[...])
    ql_ref[...] = _rms(acc[:, :MLA_Q_RANK], qn_ref[...]).astype(BF16)
    ckv_ref[...] = _rms(acc[:, MLA_Q_RANK:MLA_Q_RANK + MLA_KV_RANK], kvn_ref[...]).astype(BF16)
    y = acc[:, MLA_Q_RANK + MLA_KV_RANK:] * tk_ref[...]
    z = y + pltpu.roll(y, MLA_ROPE, 1)
    lane = lax.broadcasted_iota(jnp.int32, z.shape, 1)
    kpe_ref[:, :LANES] = jnp.where(lane < MLA_ROPE, z, 0.0).astype(BF16)
    kpe_ref[:, LANES:] = jnp.where(lane < MLA_ROPE, 0.0, z).astype(BF16)


def _mla_down(x2, gain, w1, q_norm, kv_norm, tk):
    m = x2.shape[0]
    tm = min(m, TM_MLA_PROJ)
    n1 = w1.shape[1]
    return pl.pallas_call(
        _mla_down_kernel,
        grid=(m // tm,),
        in_specs=[
            pl.BlockSpec((tm, D_MODEL), lambda i: (i, 0)),
            pl.BlockSpec((1, D_MODEL), lambda i: (0, 0)),
            pl.BlockSpec((D_MODEL, n1), lambda i: (0, 0)),
            pl.BlockSpec((1, MLA_Q_RANK), lambda i: (0, 0)),
            pl.BlockSpec((1, MLA_KV_RANK), lambda i: (0, 0)),
            pl.BlockSpec((tm, LANES), lambda i: (i, 0)),
        ],
        out_specs=[
            pl.BlockSpec((tm, MLA_Q_RANK), lambda i: (i, 0)),
            pl.BlockSpec((tm, MLA_KV_RANK), lambda i: (i, 0)),
            pl.BlockSpec((tm, 2 * LANES), lambda i: (i, 0)),
        ],
        out_shape=[
            jax.ShapeDtypeStruct((m, MLA_Q_RANK), BF16),
            jax.ShapeDtypeStruct((m, MLA_KV_RANK), BF16),
            jax.ShapeDtypeStruct((m, 2 * LANES), BF16),
        ],
        compiler_params=_params("parallel"),
        name="mla_down",
    )(x2, gain, w1, q_norm, kv_norm, tk)


MLA_PAIRS = MLA_HEADS // 2
Q_PAIR_IN = 2 * MLA_NOPE + 2 * LANES
Q_PAIR_OUT = 2 * MLA_NOPE + LANES


def _mla_q_up_kernel(a_ref, w_ref, c_ref, s_ref, o_ref):
    a = a_ref[...]
    c = c_ref[...]
    s = s_ref[...]
    for p in range(MLA_PAIRS):
        acc = _dot(a, w_ref[:, p * Q_PAIR_IN:(p + 1) * Q_PAIR_IN])
        o_ref[:, p * Q_PAIR_OUT:p * Q_PAIR_OUT + 2 * MLA_NOPE] = acc[:, :2 * MLA_NOPE].astype(BF16)
        pe = acc[:, 2 * MLA_NOPE:2 * MLA_NOPE + LANES] * c + acc[:, 2 * MLA_NOPE + LANES:] * s
        o_ref[:, p * Q_PAIR_OUT + 2 * MLA_NOPE:(p + 1) * Q_PAIR_OUT] = pe.astype(BF16)


def _mla_q_up(q_lat, wq, cm, sm):
    m = q_lat.shape[0]
    tm = min(m, TM_MLA_PROJ)
    n_out = MLA_PAIRS * Q_PAIR_OUT
    return pl.pallas_call(
        _mla_q_up_kernel,
        grid=(m // tm,),
        in_specs=[
            pl.BlockSpec((tm, MLA_Q_RANK), lambda i: (i, 0)),
            pl.BlockSpec(wq.shape, lambda i: (0, 0)),
            pl.BlockSpec((tm, LANES), lambda i: (i, 0)),
            pl.BlockSpec((tm, LANES), lambda i: (i, 0)),
        ],
        out_specs=pl.BlockSpec((tm, n_out), lambda i: (i, 0)),
        out_shape=jax.ShapeDtypeStruct((m, n_out), BF16),
        compiler_params=_params("parallel"),
        name="mla_q_up",
    )(q_lat, wq, cm, sm)


def _matmul_bf16_kernel(a_ref, w_ref, o_ref):
    o_ref[...] = _dot(a_ref[...], w_ref[...]).astype(BF16)


def _mla_kv_up(ckv, wkv):
    m = ckv.shape[0]
    tm = min(m, TM_MLA_PROJ)
    n = wkv.shape[1]
    return pl.pallas_call(
        _matmul_bf16_kernel,
        grid=(m // tm,),
        in_specs=[
            pl.BlockSpec((tm, MLA_KV_RANK), lambda i: (i, 0)),
            pl.BlockSpec(wkv.shape, lambda i: (0, 0)),
        ],
        out_specs=pl.BlockSpec((tm, n), lambda i: (i, 0)),
        out_shape=jax.ShapeDtypeStruct((m, n), BF16),
        compiler_params=_params("parallel"),
        name="mla_kv_up",
    )(ckv, wkv)


def _mla_attn_kernel(q_ref, kn_ref, kpe_ref, v_ref, o_ref, m_ref, l_ref, acc_ref, *, tq, scale):
    g = pl.program_id(2)
    scale_log2e = scale * LOG2_E
    row = lax.broadcasted_iota(jnp.int32, (tq, tq), 0)
    col = lax.broadcasted_iota(jnp.int32, (tq, tq), 1)
    causal = col <= row
    heads = (0, 1)
    for sub in range(2):
        qi = 2 * g + sub
        rows = slice(sub * tq, (sub + 1) * tq)
        q_pe = q_ref[rows, 2 * MLA_NOPE:]
        qs = [jnp.concatenate([q_ref[rows, hh * LANES:(hh + 1) * LANES], q_pe], axis=1) for hh in heads]
        m_ref[...] = jnp.full(m_ref.shape, -jnp.inf, F32)
        l_ref[...] = jnp.zeros(l_ref.shape, F32)
        acc_ref[...] = jnp.zeros(acc_ref.shape, F32)

        def step(hh, kb, masked):
            lanes = slice(hh * LANES, (hh + 1) * LANES)
            ks = pl.multiple_of(kb * tq, tq)
            k = jnp.concatenate([kn_ref[pl.ds(ks, tq), lanes], kpe_ref[pl.ds(ks, tq), lanes]], axis=1)
            s = _dot_nt(qs[hh], k)
            if masked:
                s = jnp.where(causal, s, -jnp.inf)
            m_prev = m_ref[hh]
            m_next = jnp.maximum(m_prev, jnp.max(s, axis=1, keepdims=True))
            m_wide = jnp.concatenate([m_next] * (tq // LANES), axis=1)
            p = jnp.exp2((s - m_wide) * scale_log2e)
            alpha = jnp.exp2((m_prev - m_next) * scale_log2e)
            l_ref[hh] = alpha * l_ref[hh] + jnp.sum(p, axis=1, keepdims=True)
            m_ref[hh] = m_next
            acc_ref[hh] = alpha * acc_ref[hh] + _dot(p.astype(BF16), v_ref[pl.ds(ks, tq), lanes])

        def trip(k2, carry):
            for kb in (2 * k2, 2 * k2 + 1):
                for hh in heads:
                    step(hh, kb, False)
            return carry

        lax.fori_loop(0, g, trip, 0)
        if sub == 1:
            for hh in heads:
                step(hh, qi - 1, False)
        for hh in heads:
            step(hh, qi, True)
        for hh in heads:
            o_ref[rows, hh * LANES:(hh + 1) * LANES] = (acc_ref[hh] / l_ref[hh]).astype(BF16)


def _mla_attention(q, kv, kpe, b, s):
    tq = min(s // 2, TQ_MLA)
    scale = (MLA_NOPE + MLA_ROPE) ** -0.5
    kern = functools.partial(_mla_attn_kernel, tq=tq, scale=scale)
    pair_w = 2 * LANES
    return pl.pallas_call(
        kern,
        grid=(b, MLA_PAIRS, s // (2 * tq)),
        in_specs=[
            pl.BlockSpec((None, 2 * tq, Q_PAIR_OUT), lambda bi, p, g: (bi, g, p)),
            pl.BlockSpec((None, s, pair_w), lambda bi, p, g: (bi, 0, p)),
            pl.BlockSpec((None, s, pair_w), lambda bi, p, g: (bi, 0, 0)),
            pl.BlockSpec((None, s, pair_w), lambda bi, p, g: (bi, 0, MLA_PAIRS + p)),
        ],
        out_specs=pl.BlockSpec((None, 2 * tq, pair_w), lambda bi, p, g: (bi, g, p)),
        out_shape=jax.ShapeDtypeStruct((b, s, MLA_HEADS * MLA_V), BF16),
        scratch_shapes=[
            pltpu.VMEM((2, tq, LANES), F32),
            pltpu.VMEM((2, tq, LANES), F32),
            pltpu.VMEM((2, tq, LANES), F32),
        ],
        compiler_params=_params("parallel", "parallel", "arbitrary"),
        name="mla_attention",
    )(q, kv, kpe, kv)


def _matmul_resid_kernel(a_ref, w_ref, r_ref, o_ref):
    o_ref[...] = r_ref[...] + _dot(a_ref[...], w_ref[...])


def _matmul_resid(a, w, layer, resid):
    m, k = a.shape
    n = w.shape[2]
    tm = min(m, TM_SMALL)
    return pl.pallas_call(
        _matmul_resid_kernel,
        grid=(m // tm,),
        in_specs=[
            pl.BlockSpec((tm, k), lambda i: (i, 0)),
            pl.BlockSpec((None, k, n), lambda i: (layer, 0, 0)),
            pl.BlockSpec((tm, n), lambda i: (i, 0)),
        ],
        out_specs=pl.BlockSpec((tm, n), lambda i: (i, 0)),
        out_shape=jax.ShapeDtypeStruct((m, n), F32),
        compiler_params=_params("parallel"),
        name="out_proj_resid",
    )(a, w, resid)


def _dil_rope(xc, c, sa, sb):
    half = DIL_ROT // 2
    return xc * c + pltpu.roll(xc, half, 1) * sa + pltpu.roll(xc, LANES - half, 1) * sb


def _dil_qkv_kernel(x_ref, g_ref, w_ref, c_ref, sa_ref, sb_ref, o_ref, h_ref, acc_ref, *perm_refs,
                    d, tm, n_rope, nj):
    j = pl.program_id(2)
    rows = tm // d
    if d > 1:
        xs_ref, cs_ref, sas_ref, sbs_ref = perm_refs
    else:
        cs_ref, sas_ref, sbs_ref = c_ref, sa_ref, sb_ref

    def norm_input():
        if d == 1:
            h_ref[...] = _rms(x_ref[...], g_ref[...]).astype(BF16)
            return
        x = x_ref[...]
        rinv = lax.rsqrt(jnp.mean(x * x, axis=-1, keepdims=True) + NORM_EPS)
        for t in range(x_ref.shape[1] // LANES):
            lanes = slice(t * LANES, (t + 1) * LANES)
            slot = t % xs_ref.shape[0]
            xs_ref[slot] = (x_ref[:, lanes] * rinv) * g_ref[:, lanes]
            for r in range(d):
                h_ref[r * rows:(r + 1) * rows, lanes] = xs_ref[slot, pl.ds(r, rows, stride=d), :].astype(BF16)
        for r in range(d):
            sl = pl.ds(r, rows, stride=d)
            dst = slice(r * rows, (r + 1) * rows)
            cs_ref[dst, :] = c_ref[sl, :]
            sas_ref[dst, :] = sa_ref[sl, :]
            sbs_ref[dst, :] = sb_ref[sl, :]

    def matmul():
        acc_ref[...] = _dot(h_ref[...], w_ref[...])

    def emit(rope):
        for r in range(d):
            src = slice(r * rows, (r + 1) * rows)
            if rope:
                c, sa, sb = cs_ref[src, :], sas_ref[src, :], sbs_ref[src, :]
            for t in range(acc_ref.shape[1] // LANES):
                a = acc_ref[src, t * LANES:(t + 1) * LANES]
                if rope:
                    a = _dil_rope(a, c, sa, sb)
                o_ref[r, :, t * LANES:(t + 1) * LANES] = a.astype(BF16)

    @pl.when(j == 0)
    def _():
        norm_input()
        matmul()

    @pl.when(jnp.logical_and(j >= 1, j <= n_rope))
    def _():
        emit(True)
        matmul()

    @pl.when(jnp.logical_and(j > n_rope, j < nj))
    def _():
        emit(False)
        matmul()

    @pl.when(j == nj)
    def _():
        emit(False)


def _dil_qkv(x, gain, w_in, layer, g, cd, sa, sb, d):
    b, s, _ = x.shape
    n = 3 * DIL_HEADS * DIL_HEAD_DIM
    tm = min(s, TM_PROJ)
    tn = TN_PROJ
    col0 = g * (n // tn)
    n_rope = 2 * DIL_HEADS * DIL_HEAD_DIM // tn
    nj = n // tn
    assert n_rope < nj
    kern = functools.partial(_dil_qkv_kernel, d=d, tm=tm, n_rope=n_rope, nj=nj)
    tab = pl.BlockSpec((None, tm, LANES), lambda bi, i, j: (bi, i, 0))
    perm_scratch = []
    if d > 1:
        perm_scratch = [pltpu.VMEM((PERM_SLOTS, tm, LANES), F32)] + [pltpu.VMEM((tm, LANES), F32)] * 3
    return pl.pallas_call(
        kern,
        grid=(b, s // tm, nj + 1),
        in_specs=[
            pl.BlockSpec((None, tm, D_MODEL), lambda bi, i, j: (bi, i, 0)),
            pl.BlockSpec((1, D_MODEL), lambda bi, i, j: (0, 0)),
            pl.BlockSpec((None, D_MODEL, tn), lambda bi, i, j: (layer, 0, col0 + jnp.minimum(j, nj - 1))),
            tab, tab, tab,
        ],
        out_specs=pl.BlockSpec((None, d, tm // d, tn), lambda bi, i, j: (bi, 0, i, jnp.maximum(j - 1, 0))),
        out_shape=jax.ShapeDtypeStruct((b, d, s // d, n), BF16),
        scratch_shapes=[pltpu.VMEM((tm, D_MODEL), BF16), pltpu.VMEM((tm, tn), F32)] + perm_scratch,
        compiler_params=_params("parallel", "parallel", "arbitrary"),
        name=f"dil_qkv_d{d}",
    )(x, gain, w_in, cd, sa, sb)


def _dil_attn_kernel(q_ref, kc_ref, kp_ref, vc_ref, vp_ref, o_ref, lse_ref, s_ref, p_ref, *, scale, nblk):
    jb = pl.program_id(2)
    qb = DIL_BLOCK
    qi = lax.broadcasted_iota(jnp.int32, (qb, 2 * qb), 0)
    ki = lax.broadcasted_iota(jnp.int32, (qb, 2 * qb), 1)
    lane = lax.broadcasted_iota(jnp.int32, (qb, LANES), 1)

    def keys(cur_ref, prev_ref, blk, sl):
        own = cur_ref[blk * qb:(blk + 1) * qb, sl]
        before = prev_ref[:, sl] if blk == 0 else cur_ref[(blk - 1) * qb:blk * qb, sl]
        return jnp.concatenate([before, own], axis=0)

    for blk in range(nblk):
        for h in range(DIL_HEADS):
            sl = slice(h * DIL_HEAD_DIM, (h + 1) * DIL_HEAD_DIM)
            s_ref[blk, h] = _dot_nt(q_ref[blk * qb:(blk + 1) * qb, sl], keys(kc_ref, kp_ref, blk, sl))
    for blk in range(nblk):
        first_key = jnp.where(jb > 0, qi, qb) if blk == 0 else qi
        band = jnp.logical_and(ki >= first_key, ki <= qi + qb)
        s = jnp.where(band[None], s_ref[blk], -jnp.inf)
        m = jnp.max(s, axis=-1, keepdims=True)
        e = jnp.exp2((s - m) * (scale * LOG2_E))
        l = jnp.sum(e, axis=-1, keepdims=True)
        p_ref[blk] = e.astype(BF16)
        inv = 1.0 / l
        lse = scale * m + jnp.log(l)
        lse_all = jnp.zeros((qb, LANES), F32)
        for h in range(DIL_HEADS):
            lse_all = jnp.where(lane == h, lse[h], lse_all)
        lse_ref[blk * qb:(blk + 1) * qb, :] = lse_all
        for h in range(DIL_HEADS):
            sl = slice(h * DIL_HEAD_DIM, (h + 1) * DIL_HEAD_DIM)
            pv = _dot(p_ref[blk, h], keys(vc_ref, vp_ref, blk, sl))
            o_ref[blk * qb:(blk + 1) * qb, sl] = pv * inv[h]


def _dil_attention(qkv_g, d):
    b, _, l, _ = qkv_g.shape
    hw = DIL_HEADS * DIL_HEAD_DIM
    qb = DIL_BLOCK
    nblk = min(DIL_BLOCKS_PER_STEP, l // qb)
    rows = nblk * qb
    kern = functools.partial(_dil_attn_kernel, scale=DIL_HEAD_DIM ** -0.5, nblk=nblk)

    def cur(c):
        return pl.BlockSpec((None, None, rows, hw), lambda bi, r, jb: (bi, r, jb, c))

    def prev(c):
        return pl.BlockSpec((None, None, qb, hw), lambda bi, r, jb: (bi, r, jnp.maximum(jb * nblk - 1, 0), c))

    return pl.pallas_call(
        kern,
        grid=(b, d, l // rows),
        in_specs=[cur(0), cur(1), prev(1), cur(2), prev(2)],
        out_specs=[
            pl.BlockSpec((None, None, rows, hw), lambda bi, r, jb: (bi, r, jb, 0)),
            pl.BlockSpec((None, None, rows, LANES), lambda bi, r, jb: (bi, r, jb, 0)),
        ],
        out_shape=[
            jax.ShapeDtypeStruct((b, d, l, hw), F32),
            jax.ShapeDtypeStruct((b, d, l, LANES), F32),
        ],
        scratch_shapes=[
            pltpu.VMEM((nblk, DIL_HEADS, qb, 2 * qb), F32),
            pltpu.VMEM((nblk, DIL_HEADS, qb, 2 * qb), BF16),
        ],
        compiler_params=_params("parallel", "parallel", "arbitrary"),
        name=f"dil_attention_d{d}",
    )(qkv_g, qkv_g, qkv_g, qkv_g, qkv_g)


def _dil_merge_kernel(o0_ref, l0_ref, o1_ref, l1_ref, o2_ref, l2_ref, out_ref,
                      n1_ref, n2_ref, nl1_ref, nl2_ref, *, tm):
    d1 = DIL_GROUPS[1][1]
    d2 = DIL_GROUPS[2][1]
    for d, o_ref, l_ref, n_ref, nl_ref in ((d1, o1_ref, l1_ref, n1_ref, nl1_ref),
                                           (d2, o2_ref, l2_ref, n2_ref, nl2_ref)):
        for r in range(d):
            rows = pl.ds(r, tm // d, stride=d)
            nl_ref[rows, :] = l_ref[r]
            for h in range(DIL_HEADS):
                n_ref[h, rows, :] = o_ref[r, :, h * DIL_HEAD_DIM:(h + 1) * DIL_HEAD_DIM]
    for h in range(DIL_HEADS):
        sl = slice(h * DIL_HEAD_DIM, (h + 1) * DIL_HEAD_DIM)
        a0 = l0_ref[:, h:h + 1]
        a1 = nl1_ref[:, h:h + 1]
        a2 = nl2_ref[:, h:h + 1]
        mx = jnp.maximum(jnp.maximum(a0, a1), a2)
        w0 = jnp.exp(a0 - mx)
        w1 = jnp.exp(a1 - mx)
        w2 = jnp.exp(a2 - mx)
        den = w0 + w1 + w2
        o = (w0 / den) * o0_ref[:, sl] + (w1 / den) * n1_ref[h] + (w2 / den) * n2_ref[h]
        out_ref[:, sl] = o.astype(BF16)


def _dil_merge(outs, lses, b, s):
    hw = DIL_HEADS * DIL_HEAD_DIM
    tm = min(s, TM_MERGE)
    d1 = DIL_GROUPS[1][1]
    d2 = DIL_GROUPS[2][1]
    kern = functools.partial(_dil_merge_kernel, tm=tm)

    def grp(d, w):
        return pl.BlockSpec((None, d, tm // d, w), lambda bi, i: (bi, 0, i, 0))

    return pl.pallas_call(
        kern,
        grid=(b, s // tm),
        in_specs=[
            pl.BlockSpec((None, None, tm, hw), lambda bi, i: (bi, 0, i, 0)),
            pl.BlockSpec((None, None, tm, LANES), lambda bi, i: (bi, 0, i, 0)),
            grp(d1, hw), grp(d1, LANES), grp(d2, hw), grp(d2, LANES),
        ],
        out_specs=pl.BlockSpec((None, tm, hw), lambda bi, i: (bi, i, 0)),
        out_shape=jax.ShapeDtypeStruct((b, s, hw), BF16),
        scratch_shapes=[
            pltpu.VMEM((DIL_HEADS, tm, DIL_HEAD_DIM), F32), pltpu.VMEM((DIL_HEADS, tm, DIL_HEAD_DIM), F32),
            pltpu.VMEM((tm, LANES), F32), pltpu.VMEM((tm, LANES), F32),
        ],
        compiler_params=_params("parallel", "parallel"),
        name="dil_merge",
    )(outs[0], lses[0], outs[1], lses[1], outs[2], lses[2])


def _ffn_kernel(x_ref, g_ref, og_ref, wg_ref, wv_ref, cp_ref, wd_ref, o_ref, h_ref, u_ref, prev_ref, *,
                tm, tf, nf, out_norm):
    i = pl.program_id(1)
    t = pl.program_id(2)
    row = lax.broadcasted_iota(jnp.int32, (CARRY_ROWS, 1), 0)

    def up():
        h = h_ref[...]
        u_ref[:, :tf] = _dot(h, wg_ref[...])
        u_ref[:, tf:] = _dot(h, wv_ref[...])

    def conv(u, c):
        prev = prev_ref[c]
        prev_ref[c] = u[tm - CARRY_ROWS:, :]
        p1 = prev[CARRY_ROWS - 1:CARRY_ROWS, :]
        p2 = prev[CARRY_ROWS - 2:CARRY_ROWS - 1, :]
        r1 = pltpu.roll(u, 1, 0)
        r2 = pltpu.roll(u, 2, 0)
        u1 = jnp.concatenate([jnp.where(row == 0, p1, r1[:CARRY_ROWS]), r1[CARRY_ROWS:]], axis=0)
        u2 = jnp.concatenate(
            [jnp.where(row == 0, p2, jnp.where(row == 1, p1, r2[:CARRY_ROWS])), r2[CARRY_ROWS:]], axis=0)
        cp = cp_ref[...]
        acc = cp[CONV_WIDTH:CONV_WIDTH + 1, :] + u2 * cp[0:1, :]
        acc = acc + u1 * cp[1:2, :]
        return acc + u * cp[2:3, :]

    def down(u):
        cv = conv(u, t - 1)
        gate = cv[:, :tf]
        val = cv[:, tf:]
        act = (gate * (1.0 / (1.0 + jnp.exp(-gate)))) * val
        o_ref[...] += _dot(act.astype(BF16), wd_ref[...])

    @pl.when(t == 0)
    def _():
        x = x_ref[...]
        h_ref[...] = _rms(x, g_ref[...]).astype(BF16)
        o_ref[...] = x
        up()

    @pl.when(jnp.logical_and(t == 0, i == 0))
    def _():
        prev_ref[...] = jnp.zeros(prev_ref.shape, F32)

    @pl.when(jnp.logical_and(t > 0, t < nf))
    def _():
        u = u_ref[...]
        up()
        down(u)

    @pl.when(t == nf)
    def _():
        down(u_ref[...])
        if out_norm:
            o_ref[...] = _rms(o_ref[...], og_ref[...])


def _ffn_chunk_order(a, tf):
    lead = a.shape[:-1]
    return a.reshape(*lead, 2, FFN_HIDDEN // tf, tf).swapaxes(-3, -2).reshape(*lead, 2 * FFN_HIDDEN)


def _ffn_weights(w_up, conv_w, conv_b, w_down):
    tf = TF_FFN
    pad = jnp.zeros((conv_w.shape[0], SUBLANES - CONV_WIDTH - 1, conv_w.shape[2]), F32)
    conv_p = jnp.concatenate([conv_w, conv_b[:, None, :], pad], axis=1)
    return w_up.astype(BF16), _ffn_chunk_order(conv_p, tf), w_down.astype(BF16)


def _ffn(x, gain, out_gain, w_up, conv_p, w_down, layer, out_norm):
    b, s, _ = x.shape
    tm = min(s, TM_FFN)
    tf = TF_FFN
    nf = FFN_HIDDEN // tf
    kern = functools.partial(_ffn_kernel, tm=tm, tf=tf, nf=nf, out_norm=out_norm)

    def up_chunk(t):
        return jnp.minimum(t, nf - 1)

    def down_chunk(t):
        return jnp.maximum(t - 1, 0)

    return pl.pallas_call(
        kern,
        grid=(b, s // tm, nf + 1),
        in_specs=[
            pl.BlockSpec((None, tm, D_MODEL), lambda bi, i, t: (bi, i, 0)),
            pl.BlockSpec((1, D_MODEL), lambda bi, i, t: (0, 0)),
            pl.BlockSpec((1, D_MODEL), lambda bi, i, t: (0, 0)),
            pl.BlockSpec((None, D_MODEL, tf), lambda bi, i, t: (layer, 0, up_chunk(t))),
            pl.BlockSpec((None, D_MODEL, tf), lambda bi, i, t: (layer, 0, nf + up_chunk(t))),
            pl.BlockSpec((None, SUBLANES, 2 * tf), lambda bi, i, t: (layer, 0, down_chunk(t))),
            pl.BlockSpec((None, tf, D_MODEL), lambda bi, i, t: (layer, down_chunk(t), 0)),
        ],
        out_specs=pl.BlockSpec((None, tm, D_MODEL), lambda bi, i, t: (bi, i, 0), pipeline_mode=pl.Buffered(1)),
        out_shape=jax.ShapeDtypeStruct(x.shape, F32),
        scratch_shapes=[
            pltpu.VMEM((tm, D_MODEL), BF16),
            pltpu.VMEM((tm, 2 * tf), F32),
            pltpu.VMEM((nf, CARRY_ROWS, 2 * tf), F32),
        ],
        compiler_params=_params("arbitrary", "arbitrary", "arbitrary"),
        name="conv_ffn",
    )(x, gain, out_gain, w_up, w_up, conv_p, w_down)


def _rotate_half_cols(w):
    half = w.shape[-1] // 2
    return jnp.concatenate([-w[..., half:], w[..., :half]], axis=-1)


def _mla_weights(wq_a, wq_b, wkv_a, wkv_b):
    w_pe = wkv_a[:, MLA_KV_RANK:]
    w1 = jnp.concatenate([wq_a, wkv_a[:, :MLA_KV_RANK], w_pe, _rotate_half_cols(w_pe)], axis=1)
    qb = wq_b.reshape(MLA_Q_RANK, MLA_HEADS, MLA_NOPE + MLA_ROPE)
    nope = qb[:, :, :MLA_NOPE].reshape(MLA_Q_RANK, MLA_PAIRS, 2 * MLA_NOPE)
    pe = qb[:, :, MLA_NOPE:]
    rot = _rotate_half_cols(pe).reshape(MLA_Q_RANK, MLA_PAIRS, LANES)
    pe = pe.reshape(MLA_Q_RANK, MLA_PAIRS, LANES)
    wq = jnp.concatenate([nope, pe, rot], axis=2).reshape(MLA_Q_RANK, MLA_PAIRS * Q_PAIR_IN)
    kvb = wkv_b.reshape(MLA_KV_RANK, MLA_HEADS, MLA_NOPE + MLA_V)
    wkv = jnp.concatenate([kvb[:, :, :MLA_NOPE].reshape(MLA_KV_RANK, -1),
                           kvb[:, :, MLA_NOPE:].reshape(MLA_KV_RANK, -1)], axis=1)
    return w1.astype(BF16), wq.astype(BF16), wkv.astype(BF16)


def _mla_layer(x, gain, tabs, wq_a, q_norm, wq_b, wkv_a, kv_norm, wkv_b, wo, layer):
    b, s, dm = x.shape
    cm, sm, tk = tabs
    w1, wq, wkv = _mla_weights(wq_a, wq_b, wkv_a, wkv_b)
    x2 = x.reshape(b * s, dm)
    q_lat, ckv, kpe = _mla_down(x2, gain.reshape(1, dm), w1, q_norm.reshape(1, -1), kv_norm.reshape(1, -1), tk)
    q = _mla_q_up(q_lat, wq, cm, sm)
    kv = _mla_kv_up(ckv, wkv)
    o = _mla_attention(q.reshape(b, s, -1), kv.reshape(b, s, -1), kpe.reshape(b, s, -1), b, s)
    return _matmul_resid(o.reshape(b * s, -1), wo, layer, x2).reshape(b, s, dm)


def _dil_layer(x, gain, tabs, w_in, wo, layer):
    b, s, dm = x.shape
    cd, sa, sb = (t.reshape(b, s, LANES) for t in tabs)
    outs, lses = [], []
    for g, (_, d) in enumerate(DIL_GROUPS):
        qkv_g = _dil_qkv(x, gain.reshape(1, dm), w_in, layer, g, cd, sa, sb, d)
        o, lse = _dil_attention(qkv_g, d)
        outs.append(o)
        lses.append(lse)
    merged = _dil_merge(outs, lses, b, s)
    return _matmul_resid(merged.reshape(b * s, -1), wo, layer, x.reshape(b * s, dm)).reshape(b, s, dm)


def kernel(x, positions, attn_norm, ffn_norm, final_norm, mla_wq_a, mla_q_norm, mla_wq_b, mla_wkv_a,
           mla_kv_norm, mla_wkv_b, mla_wo, dil_w_in, dil_wo, ffn_w_up, ffn_conv_w, ffn_conv_b, ffn_w_down):
    b, s, dm = x.shape
    cm, sm, tk, cd, sa, sb = _rope_tables(positions)
    mla_wo_b = mla_wo.astype(BF16)
    dil_w_in_b = dil_w_in.astype(BF16)
    dil_wo_b = dil_wo.astype(BF16)
    ffn_up_b, ffn_conv_p, ffn_down_b = _ffn_weights(ffn_w_up, ffn_conv_w, ffn_conv_b, ffn_w_down)
    for i in range(DEPTH):
        j = i // N_MIXERS
        if i % N_MIXERS == 0:
            x = _mla_layer(x, attn_norm[i], (cm, sm, tk), mla_wq_a[j], mla_q_norm[j], mla_wq_b[j],
                           mla_wkv_a[j], mla_kv_norm[j], mla_wkv_b[j], mla_wo_b, j)
        else:
            x = _dil_layer(x, attn_norm[i], (cd, sa, sb), dil_w_in_b, dil_wo_b, j)
        x = _ffn(x, ffn_norm[i].reshape(1, dm), final_norm.reshape(1, dm), ffn_up_b, ffn_conv_p, ffn_down_b,
                 i, i == DEPTH - 1)
    return x
```

```python
import functools

import jax
import jax.numpy as jnp
from jax import lax
from jax.experimental import pallas as pl
from jax.experimental.pallas import tpu as pltpu

F32 = jnp.float32
BF16 = jnp.bfloat16

D_MODEL = 2048
DEPTH = 4
N_MIXERS = 2
ROPE_THETA = 500000.0
NORM_EPS = 1e-6
LOG2_E = 1.4426950408889634

MLA_HEADS = 16
MLA_Q_RANK = 512
MLA_KV_RANK = 512
MLA_NOPE = 128
MLA_ROPE = 64
MLA_V = 128

DIL_GROUPS = ((128, 1), (512, 4), (2048, 16))
DIL_HEADS = 16
DIL_HEAD_DIM = 128
DIL_ROT = DIL_HEAD_DIM // 4
DIL_BLOCK = 128
DIL_BLOCKS_PER_STEP = 4

FFN_HIDDEN = 5632
CONV_WIDTH = 3

LANES = 128
SUBLANES = 8
VMEM_BYTES_V7X = 64 * 1024 * 1024
VMEM_LIMIT = VMEM_BYTES_V7X - 8 * 1024 * 1024

TM_PROJ = 1024
TN_PROJ = 1024
TM_SMALL = 512
TM_MLA_PROJ = 1024
TM_MERGE = 256
TQ_MLA = 512
TM_FFN = 512
TF_FFN = 512
CARRY_ROWS = SUBLANES
PERM_SLOTS = 4


def _params(*sem):
    return pltpu.CompilerParams(dimension_semantics=sem, vmem_limit_bytes=VMEM_LIMIT)


def _rms(x, g):
    ms = jnp.mean(x * x, axis=-1, keepdims=True)
    return (x * lax.rsqrt(ms + NORM_EPS)) * g


def _dot(a, b):
    return jnp.dot(a, b, preferred_element_type=F32)


def _dot_nt(a, b):
    return lax.dot_general(a, b, (((1,), (1,)), ((), ())), preferred_element_type=F32)


def _rope_tables_kernel(pos_ref, fm_ref, fd_ref, cm_ref, sm_ref, tk_ref, cd_ref, sa_ref, sb_ref):
    pos = pos_ref[...]
    lane = lax.broadcasted_iota(jnp.int32, (1, LANES), 1)
    am = pos * fm_ref[...]
    cm = jnp.cos(am)
    sm = jnp.sin(am)
    cm_ref[...] = cm
    sm_ref[...] = sm
    tk_ref[...] = jnp.where(lane < MLA_ROPE, cm, sm)
    ad = pos * fd_ref[...]
    cd = jnp.cos(ad)
    sd = jnp.sin(ad)
    half = DIL_ROT // 2
    cd_ref[...] = jnp.where(lane < DIL_ROT, cd, 1.0)
    sa_ref[...] = jnp.where(lane < half, 0.0, jnp.where(lane < DIL_ROT, sd, 0.0))
    sb_ref[...] = jnp.where(lane < half, -sd, 0.0)


def _rope_tables(positions):
    m = positions.size
    pos = positions.reshape(m, 1).astype(F32)
    inv_m = ROPE_THETA ** (-jnp.arange(0, MLA_ROPE, 2, dtype=F32) / MLA_ROPE)
    inv_d = ROPE_THETA ** (-jnp.arange(0, DIL_ROT, 2, dtype=F32) / DIL_ROT)
    fm = jnp.tile(inv_m, LANES // inv_m.size).reshape(1, LANES)
    fd = jnp.tile(inv_d, LANES // inv_d.size).reshape(1, LANES)
    tm = min(m, 2048)
    row = pl.BlockSpec((tm, LANES), lambda i: (i, 0))
    vec = pl.BlockSpec((1, LANES), lambda i: (0, 0))
    tab = jax.ShapeDtypeStruct((m, LANES), F32)
    return pl.pallas_call(
        _rope_tables_kernel,
        grid=(m // tm,),
        in_specs=[pl.BlockSpec((tm, 1), lambda i: (i, 0)), vec, vec],
        out_specs=[row] * 6,
        out_shape=[tab] * 6,
        compiler_params=_params("parallel"),
        name="rope_tables",
    )(pos, fm, fd)


def _mla_down_kernel(x_ref, g_ref, w_ref, qn_ref, kvn_ref, tk_ref, ql_ref, ckv_ref, kpe_ref):
    h = _rms(x_ref[...], g_ref[...]).astype(BF16)
    acc = _dot(h, w_ref[...])
    ql_ref[...] = _rms(acc[:, :MLA_Q_RANK], qn_ref[...]).astype(BF16)
    ckv_ref[...] = _rms(acc[:, MLA_Q_RANK:MLA_Q_RANK + MLA_KV_RANK], kvn_ref[...]).astype(BF16)
    y = acc[:, MLA_Q_RANK + MLA_KV_RANK:] * tk_ref[...]
    z = y + pltpu.roll(y, MLA_ROPE, 1)
    lane = lax.broadcasted_iota(jnp.int32, z.shape, 1)
    kpe_ref[:, :LANES] = jnp.where(lane < MLA_ROPE, z, 0.0).astype(BF16)
    kpe_ref[:, LANES:] = jnp.where(lane < MLA_ROPE, 0.0, z).astype(BF16)


def _mla_down(x2, gain, w1, q_norm, kv_norm, tk):
    m = x2.shape[0]
    tm = min(m, TM_MLA_PROJ)
    n1 = w1.shape[1]
    return pl.pallas_call(
        _mla_down_kernel,
        grid=(m // tm,),
        in_specs=[
            pl.BlockSpec((tm, D_MODEL), lambda i: (i, 0)),
            pl.BlockSpec((1, D_MODEL), lambda i: (0, 0)),
            pl.BlockSpec((D_MODEL, n1), lambda i: (0, 0)),
            pl.BlockSpec((1, MLA_Q_RANK), lambda i: (0, 0)),
            pl.BlockSpec((1, MLA_KV_RANK), lambda i: (0, 0)),
            pl.BlockSpec((tm, LANES), lambda i: (i, 0)),
        ],
        out_specs=[
            pl.BlockSpec((tm, MLA_Q_RANK), lambda i: (i, 0)),
            pl.BlockSpec((tm, MLA_KV_RANK), lambda i: (i, 0)),
            pl.BlockSpec((tm, 2 * LANES), lambda i: (i, 0)),
        ],
        out_shape=[
            jax.ShapeDtypeStruct((m, MLA_Q_RANK), BF16),
            jax.ShapeDtypeStruct((m, MLA_KV_RANK), BF16),
            jax.ShapeDtypeStruct((m, 2 * LANES), BF16),
        ],
        compiler_params=_params("parallel"),
        name="mla_down",
    )(x2, gain, w1, q_norm, kv_norm, tk)


MLA_PAIRS = MLA_HEADS // 2
Q_PAIR_IN = 2 * MLA_NOPE + 2 * LANES
Q_PAIR_OUT = 2 * MLA_NOPE + LANES


def _mla_q_up_kernel(a_ref, w_ref, c_ref, s_ref, o_ref):
    a = a_ref[...]
    c = c_ref[...]
    s = s_ref[...]
    for p in range(MLA_PAIRS):
        acc = _dot(a, w_ref[:, p * Q_PAIR_IN:(p + 1) * Q_PAIR_IN])
        o_ref[:, p * Q_PAIR_OUT:p * Q_PAIR_OUT + 2 * MLA_NOPE] = acc[:, :2 * MLA_NOPE].astype(BF16)
        pe = acc[:, 2 * MLA_NOPE:2 * MLA_NOPE + LANES] * c + acc[:, 2 * MLA_NOPE + LANES:] * s
        o_ref[:, p * Q_PAIR_OUT + 2 * MLA_NOPE:(p + 1) * Q_PAIR_OUT] = pe.astype(BF16)


def _mla_q_up(q_lat, wq, cm, sm):
    m = q_lat.shape[0]
    tm = min(m, TM_MLA_PROJ)
    n_out = MLA_PAIRS * Q_PAIR_OUT
    return pl.pallas_call(
        _mla_q_up_kernel,
        grid=(m // tm,),
        in_specs=[
            pl.BlockSpec((tm, MLA_Q_RANK), lambda i: (i, 0)),
            pl.BlockSpec(wq.shape, lambda i: (0, 0)),
            pl.BlockSpec((tm, LANES), lambda i: (i, 0)),
            pl.BlockSpec((tm, LANES), lambda i: (i, 0)),
        ],
        out_specs=pl.BlockSpec((tm, n_out), lambda i: (i, 0)),
        out_shape=jax.ShapeDtypeStruct((m, n_out), BF16),
        compiler_params=_params("parallel"),
        name="mla_q_up",
    )(q_lat, wq, cm, sm)


def _matmul_bf16_kernel(a_ref, w_ref, o_ref):
    o_ref[...] = _dot(a_ref[...], w_ref[...]).astype(BF16)


def _mla_kv_up(ckv, wkv):
    m = ckv.shape[0]
    tm = min(m, TM_MLA_PROJ)
    n = wkv.shape[1]
    return pl.pallas_call(
        _matmul_bf16_kernel,
        grid=(m // tm,),
        in_specs=[
            pl.BlockSpec((tm, MLA_KV_RANK), lambda i: (i, 0)),
            pl.BlockSpec(wkv.shape, lambda i: (0, 0)),
        ],
        out_specs=pl.BlockSpec((tm, n), lambda i: (i, 0)),
        out_shape=jax.ShapeDtypeStruct((m, n), BF16),
        compiler_params=_params("parallel"),
        name="mla_kv_up",
    )(ckv, wkv)


def _mla_attn_kernel(q_ref, kn_ref, kpe_ref, v_ref, o_ref, m_ref, l_ref, acc_ref, *, tq, scale):
    g = pl.program_id(2)
    scale_log2e = scale * LOG2_E
    row = lax.broadcasted_iota(jnp.int32, (tq, tq), 0)
    col = lax.broadcasted_iota(jnp.int32, (tq, tq), 1)
    causal = col <= row
    heads = (0, 1)
    for sub in range(2):
        qi = 2 * g + sub
        rows = slice(sub * tq, (sub + 1) * tq)
        q_pe = q_ref[rows, 2 * MLA_NOPE:]
        qs = [jnp.concatenate([q_ref[rows, hh * LANES:(hh + 1) * LANES], q_pe], axis=1) for hh in heads]
        m_ref[...] = jnp.full(m_ref.shape, -jnp.inf, F32)
        l_ref[...] = jnp.zeros(l_ref.shape, F32)
        acc_ref[...] = jnp.zeros(acc_ref.shape, F32)

        def step(hh, kb, masked):
            lanes = slice(hh * LANES, (hh + 1) * LANES)
            ks = pl.multiple_of(kb * tq, tq)
            k = jnp.concatenate([kn_ref[pl.ds(ks, tq), lanes], kpe_ref[pl.ds(ks, tq), lanes]], axis=1)
            s = _dot_nt(qs[hh], k)
            if masked:
                s = jnp.where(causal, s, -jnp.inf)
            m_prev = m_ref[hh]
            m_next = jnp.maximum(m_prev, jnp.max(s, axis=1, keepdims=True))
            m_wide = jnp.concatenate([m_next] * (tq // LANES), axis=1)
            p = jnp.exp2((s - m_wide) * scale_log2e)
            alpha = jnp.exp2((m_prev - m_next) * scale_log2e)
            l_ref[hh] = alpha * l_ref[hh] + jnp.sum(p, axis=1, keepdims=True)
            m_ref[hh] = m_next
            acc_ref[hh] = alpha * acc_ref[hh] + _dot(p.astype(BF16), v_ref[pl.ds(ks, tq), lanes])

        def trip(k2, carry):
            for kb in (2 * k2, 2 * k2 + 1):
                for hh in heads:
                    step(hh, kb, False)
            return carry

        lax.fori_loop(0, g, trip, 0)
        if sub == 1:
            for hh in heads:
                step(hh, qi - 1, False)
        for hh in heads:
            step(hh, qi, True)
        for hh in heads:
            o_ref[rows, hh * LANES:(hh + 1) * LANES] = (acc_ref[hh] / l_ref[hh]).astype(BF16)


def _mla_attention(q, kv, kpe, b, s):
    tq = min(s // 2, TQ_MLA)
    scale = (MLA_NOPE + MLA_ROPE) ** -0.5
    kern = functools.partial(_mla_attn_kernel, tq=tq, scale=scale)
    pair_w = 2 * LANES
    return pl.pallas_call(
        kern,
        grid=(b, MLA_PAIRS, s // (2 * tq)),
        in_specs=[
            pl.BlockSpec((None, 2 * tq, Q_PAIR_OUT), lambda bi, p, g: (bi, g, p)),
            pl.BlockSpec((None, s, pair_w), lambda bi, p, g: (bi, 0, p)),
            pl.BlockSpec((None, s, pair_w), lambda bi, p, g: (bi, 0, 0)),
            pl.BlockSpec((None, s, pair_w), lambda bi, p, g: (bi, 0, MLA_PAIRS + p)),
        ],
        out_specs=pl.BlockSpec((None, 2 * tq, pair_w), lambda bi, p, g: (bi, g, p)),
        out_shape=jax.ShapeDtypeStruct((b, s, MLA_HEADS * MLA_V), BF16),
        scratch_shapes=[
            pltpu.VMEM((2, tq, LANES), F32),
            pltpu.VMEM((2, tq, LANES), F32),
            pltpu.VMEM((2, tq, LANES), F32),
        ],
        compiler_params=_params("parallel", "parallel", "arbitrary"),
        name="mla_attention",
    )(q, kv, kpe, kv)


def _matmul_resid_kernel(a_ref, w_ref, r_ref, o_ref):
    o_ref[...] = r_ref[...] + _dot(a_ref[...], w_ref[...])


def _matmul_resid(a, w, layer, resid):
    m, k = a.shape
    n = w.shape[2]
    tm = min(m, TM_SMALL)
    return pl.pallas_call(
        _matmul_resid_kernel,
        grid=(m // tm,),
        in_specs=[
            pl.BlockSpec((tm, k), lambda i: (i, 0)),
            pl.BlockSpec((None, k, n), lambda i: (layer, 0, 0)),
            pl.BlockSpec((tm, n), lambda i: (i, 0)),
        ],
        out_specs=pl.BlockSpec((tm, n), lambda i: (i, 0)),
        out_shape=jax.ShapeDtypeStruct((m, n), F32),
        compiler_params=_params("parallel"),
        name="out_proj_resid",
    )(a, w, resid)


def _dil_rope(xc, c, sa, sb):
    half = DIL_ROT // 2
    return xc * c + pltpu.roll(xc, half, 1) * sa + pltpu.roll(xc, LANES - half, 1) * sb


def _dil_qkv_kernel(x_ref, g_ref, w_ref, c_ref, sa_ref, sb_ref, o_ref, h_ref, acc_ref, *perm_refs,
                    d, tm, n_rope, nj):
    j = pl.program_id(2)
    rows = tm // d
    if d > 1:
        xs_ref, cs_ref, sas_ref, sbs_ref = perm_refs
    else:
        cs_ref, sas_ref, sbs_ref = c_ref, sa_ref, sb_ref

    def norm_input():
        if d == 1:
            h_ref[...] = _rms(x_ref[...], g_ref[...]).astype(BF16)
            return
        x = x_ref[...]
        rinv = lax.rsqrt(jnp.mean(x * x, axis=-1, keepdims=True) + NORM_EPS)
        for t in range(x_ref.shape[1] // LANES):
            lanes = slice(t * LANES, (t + 1) * LANES)
            slot = t % xs_ref.shape[0]
            xs_ref[slot] = (x_ref[:, lanes] * rinv) * g_ref[:, lanes]
            for r in range(d):
                h_ref[r * rows:(r + 1) * rows, lanes] = xs_ref[slot, pl.ds(r, rows, stride=d), :].astype(BF16)
        for r in range(d):
            sl = pl.ds(r, rows, stride=d)
            dst = slice(r * rows, (r + 1) * rows)
            cs_ref[dst, :] = c_ref[sl, :]
            sas_ref[dst, :] = sa_ref[sl, :]
            sbs_ref[dst, :] = sb_ref[sl, :]

    def matmul():
        acc_ref[...] = _dot(h_ref[...], w_ref[...])

    def emit(rope):
        for r in range(d):
            src = slice(r * rows, (r + 1) * rows)
            if rope:
                c, sa, sb = cs_ref[src, :], sas_ref[src, :], sbs_ref[src, :]
            for t in range(acc_ref.shape[1] // LANES):
                a = acc_ref[src, t * LANES:(t + 1) * LANES]
                if rope:
                    a = _dil_rope(a, c, sa, sb)
                o_ref[r, :, t * LANES:(t + 1) * LANES] = a.astype(BF16)

    @pl.when(j == 0)
    def _():
        norm_input()
        matmul()

    @pl.when(jnp.logical_and(j >= 1, j <= n_rope))
    def _():
        emit(True)
        matmul()

    @pl.when(jnp.logical_and(j > n_rope, j < nj))
    def _():
        emit(False)
        matmul()

    @pl.when(j == nj)
    def _():
        emit(False)


def _dil_qkv(x, gain, w_in, layer, g, cd, sa, sb, d):
    b, s, _ = x.shape
    n = 3 * DIL_HEADS * DIL_HEAD_DIM
    tm = min(s, TM_PROJ)
    tn = TN_PROJ
    col0 = g * (n // tn)
    n_rope = 2 * DIL_HEADS * DIL_HEAD_DIM // tn
    nj = n // tn
    assert n_rope < nj
    kern = functools.partial(_dil_qkv_kernel, d=d, tm=tm, n_rope=n_rope, nj=nj)
    tab = pl.BlockSpec((None, tm, LANES), lambda bi, i, j: (bi, i, 0))
    perm_scratch = []
    if d > 1:
        perm_scratch = [pltpu.VMEM((PERM_SLOTS, tm, LANES), F32)] + [pltpu.VMEM((tm, LANES), F32)] * 3
    return pl.pallas_call(
        kern,
        grid=(b, s // tm, nj + 1),
        in_specs=[
            pl.BlockSpec((None, tm, D_MODEL), lambda bi, i, j: (bi, i, 0)),
            pl.BlockSpec((1, D_MODEL), lambda bi, i, j: (0, 0)),
            pl.BlockSpec((None, D_MODEL, tn), lambda bi, i, j: (layer, 0, col0 + jnp.minimum(j, nj - 1))),
            tab, tab, tab,
        ],
        out_specs=pl.BlockSpec((None, d, tm // d, tn), lambda bi, i, j: (bi, 0, i, jnp.maximum(j - 1, 0))),
        out_shape=jax.ShapeDtypeStruct((b, d, s // d, n), BF16),
        scratch_shapes=[pltpu.VMEM((tm, D_MODEL), BF16), pltpu.VMEM((tm, tn), F32)] + perm_scratch,
        compiler_params=_params("parallel", "parallel", "arbitrary"),
        name=f"dil_qkv_d{d}",
    )(x, gain, w_in, cd, sa, sb)


def _dil_attn_kernel(q_ref, kc_ref, kp_ref, vc_ref, vp_ref, o_ref, lse_ref, s_ref, p_ref, *, scale, nblk):
    jb = pl.program_id(2)
    qb = DIL_BLOCK
    qi = lax.broadcasted_iota(jnp.int32, (qb, 2 * qb), 0)
    ki = lax.broadcasted_iota(jnp.int32, (qb, 2 * qb), 1)
    lane = lax.broadcasted_iota(jnp.int32, (qb, LANES), 1)

    def keys(cur_ref, prev_ref, blk, sl):
        own = cur_ref[blk * qb:(blk + 1) * qb, sl]
        before = prev_ref[:, sl] if blk == 0 else cur_ref[(blk - 1) * qb:blk * qb, sl]
        return jnp.concatenate([before, own], axis=0)

    for blk in range(nblk):
        for h in range(DIL_HEADS):
            sl = slice(h * DIL_HEAD_DIM, (h + 1) * DIL_HEAD_DIM)
            s_ref[blk, h] = _dot_nt(q_ref[blk * qb:(blk + 1) * qb, sl], keys(kc_ref, kp_ref, blk, sl))
    for blk in range(nblk):
        first_key = jnp.where(jb > 0, qi, qb) if blk == 0 else qi
        band = jnp.logical_and(ki >= first_key, ki <= qi + qb)
        s = jnp.where(band[None], s_ref[blk], -jnp.inf)
        m = jnp.max(s, axis=-1, keepdims=True)
        e = jnp.exp2((s - m) * (scale * LOG2_E))
        l = jnp.sum(e, axis=-1, keepdims=True)
        p_ref[blk] = e.astype(BF16)
        inv = 1.0 / l
        lse = scale * m + jnp.log(l)
        lse_all = jnp.zeros((qb, LANES), F32)
        for h in range(DIL_HEADS):
            lse_all = jnp.where(lane == h, lse[h], lse_all)
        lse_ref[blk * qb:(blk + 1) * qb, :] = lse_all
        for h in range(DIL_HEADS):
            sl = slice(h * DIL_HEAD_DIM, (h + 1) * DIL_HEAD_DIM)
            pv = _dot(p_ref[blk, h], keys(vc_ref, vp_ref, blk, sl))
            o_ref[blk * qb:(blk + 1) * qb, sl] = pv * inv[h]


def _dil_attention(qkv_g, d):
    b, _, l, _ = qkv_g.shape
    hw = DIL_HEADS * DIL_HEAD_DIM
    qb = DIL_BLOCK
    nblk = min(DIL_BLOCKS_PER_STEP, l // qb)
    rows = nblk * qb
    kern = functools.partial(_dil_attn_kernel, scale=DIL_HEAD_DIM ** -0.5, nblk=nblk)

    def cur(c):
        return pl.BlockSpec((None, None, rows, hw), lambda bi, r, jb: (bi, r, jb, c))

    def prev(c):
        return pl.BlockSpec((None, None, qb, hw), lambda bi, r, jb: (bi, r, jnp.maximum(jb * nblk - 1, 0), c))

    return pl.pallas_call(
        kern,
        grid=(b, d, l // rows),
        in_specs=[cur(0), cur(1), prev(1), cur(2), prev(2)],
        out_specs=[
            pl.BlockSpec((None, None, rows, hw), lambda bi, r, jb: (bi, r, jb, 0)),
            pl.BlockSpec((None, None, rows, LANES), lambda bi, r, jb: (bi, r, jb, 0)),
        ],
        out_shape=[
            jax.ShapeDtypeStruct((b, d, l, hw), F32),
            jax.ShapeDtypeStruct((b, d, l, LANES), F32),
        ],
        scratch_shapes=[
            pltpu.VMEM((nblk, DIL_HEADS, qb, 2 * qb), F32),
            pltpu.VMEM((nblk, DIL_HEADS, qb, 2 * qb), BF16),
        ],
        compiler_params=_params("parallel", "parallel", "arbitrary"),
        name=f"dil_attention_d{d}",
    )(qkv_g, qkv_g, qkv_g, qkv_g, qkv_g)


def _dil_merge_kernel(o0_ref, l0_ref, o1_ref, l1_ref, o2_ref, l2_ref, w_ref, x_ref, out_ref,
                      n1_ref, n2_ref, nl1_ref, nl2_ref, a_ref, *, tm, n):
    i = pl.program_id(1)
    d1 = DIL_GROUPS[1][1]
    d2 = DIL_GROUPS[2][1]

    def merge():
        for d, o_ref, l_ref, n_ref, nl_ref in ((d1, o1_ref, l1_ref, n1_ref, nl1_ref),
                                               (d2, o2_ref, l2_ref, n2_ref, nl2_ref)):
            for r in range(d):
                rows = pl.ds(r, tm // d, stride=d)
                nl_ref[rows, :] = l_ref[r]
                for h in range(DIL_HEADS):
                    n_ref[h, rows, :] = o_ref[r, :, h * DIL_HEAD_DIM:(h + 1) * DIL_HEAD_DIM]
        for h in range(DIL_HEADS):
            sl = slice(h * DIL_HEAD_DIM, (h + 1) * DIL_HEAD_DIM)
            a0 = l0_ref[:, h:h + 1]
            a1 = nl1_ref[:, h:h + 1]
            a2 = nl2_ref[:, h:h + 1]
            mx = jnp.maximum(jnp.maximum(a0, a1), a2)
            w0 = jnp.exp(a0 - mx)
            w1 = jnp.exp(a1 - mx)
            w2 = jnp.exp(a2 - mx)
            den = w0 + w1 + w2
            o = (w0 / den) * o0_ref[:, sl] + (w1 / den) * n1_ref[h] + (w2 / den) * n2_ref[h]
            a_ref[:, sl] = o.astype(BF16)

    def project(a):
        out_ref[...] = x_ref[...] + _dot(a, w_ref[...])

    @pl.when(i == 0)
    def _():
        merge()

    @pl.when(jnp.logical_and(i > 0, i < n))
    def _():
        a = a_ref[...]
        merge()
        project(a)

    @pl.when(i == n)
    def _():
        project(a_ref[...])


def _dil_merge_out(outs, lses, wo, layer, x):
    b, s, dm = x.shape
    hw = DIL_HEADS * DIL_HEAD_DIM
    tm = min(s, TM_MERGE)
    d1 = DIL_GROUPS[1][1]
    d2 = DIL_GROUPS[2][1]
    n = s // tm
    kern = functools.partial(_dil_merge_kernel, tm=tm, n=n)

    def merged(i):
        return jnp.minimum(i, n - 1)

    def projected(i):
        return jnp.maximum(i - 1, 0)

    def grp(d, w):
        return pl.BlockSpec((None, d, tm // d, w), lambda bi, i: (bi, 0, merged(i), 0))

    return pl.pallas_call(
        kern,
        grid=(b, n + 1),
        in_specs=[
            pl.BlockSpec((None, None, tm, hw), lambda bi, i: (bi, 0, merged(i), 0)),
            pl.BlockSpec((None, None, tm, LANES), lambda bi, i: (bi, 0, merged(i), 0)),
            grp(d1, hw), grp(d1, LANES), grp(d2, hw), grp(d2, LANES),
            pl.BlockSpec((None, hw, dm), lambda bi, i: (layer, 0, 0)),
            pl.BlockSpec((None, tm, dm), lambda bi, i: (bi, projected(i), 0)),
        ],
        out_specs=pl.BlockSpec((None, tm, dm), lambda bi, i: (bi, projected(i), 0)),
        out_shape=jax.ShapeDtypeStruct((b, s, dm), F32),
        scratch_shapes=[
            pltpu.VMEM((DIL_HEADS, tm, DIL_HEAD_DIM), F32), pltpu.VMEM((DIL_HEADS, tm, DIL_HEAD_DIM), F32),
            pltpu.VMEM((tm, LANES), F32), pltpu.VMEM((tm, LANES), F32),
            pltpu.VMEM((tm, hw), BF16),
        ],
        compiler_params=_params("parallel", "arbitrary"),
        name="dil_merge_out",
    )(outs[0], lses[0], outs[1], lses[1], outs[2], lses[2], wo, x)


def _ffn_kernel(x_ref, g_ref, og_ref, wg_ref, wv_ref, cp_ref, wd_ref, o_ref, h_ref, u_ref, prev_ref, *,
                tm, tf, nf, out_norm):
    i = pl.program_id(1)
    t = pl.program_id(2)
    row = lax.broadcasted_iota(jnp.int32, (CARRY_ROWS, 1), 0)

    def up():
        h = h_ref[...]
        u_ref[:, :tf] = _dot(h, wg_ref[...])
        u_ref[:, tf:] = _dot(h, wv_ref[...])

    def conv(u, c):
        prev = prev_ref[c]
        prev_ref[c] = u[tm - CARRY_ROWS:, :]
        p1 = prev[CARRY_ROWS - 1:CARRY_ROWS, :]
        p2 = prev[CARRY_ROWS - 2:CARRY_ROWS - 1, :]
        r1 = pltpu.roll(u, 1, 0)
        r2 = pltpu.roll(u, 2, 0)
        u1 = jnp.concatenate([jnp.where(row == 0, p1, r1[:CARRY_ROWS]), r1[CARRY_ROWS:]], axis=0)
        u2 = jnp.concatenate(
            [jnp.where(row == 0, p2, jnp.where(row == 1, p1, r2[:CARRY_ROWS])), r2[CARRY_ROWS:]], axis=0)
        cp = cp_ref[...]
        acc = cp[CONV_WIDTH:CONV_WIDTH + 1, :] + u2 * cp[0:1, :]
        acc = acc + u1 * cp[1:2, :]
        return acc + u * cp[2:3, :]

    def down(u):
        cv = conv(u, t - 1)
        gate = cv[:, :tf]
        val = cv[:, tf:]
        act = (gate * (1.0 / (1.0 + jnp.exp(-gate)))) * val
        o_ref[...] += _dot(act.astype(BF16), wd_ref[...])

    @pl.when(t == 0)
    def _():
        x = x_ref[...]
        h_ref[...] = _rms(x, g_ref[...]).astype(BF16)
        o_ref[...] = x
        up()

    @pl.when(jnp.logical_and(t == 0, i == 0))
    def _():
        prev_ref[...] = jnp.zeros(prev_ref.shape, F32)

    @pl.when(jnp.logical_and(t > 0, t < nf))
    def _():
        u = u_ref[...]
        up()
        down(u)

    @pl.when(t == nf)
    def _():
        down(u_ref[...])
        if out_norm:
            o_ref[...] = _rms(o_ref[...], og_ref[...])


def _ffn_chunk_order(a, tf):
    lead = a.shape[:-1]
    return a.reshape(*lead, 2, FFN_HIDDEN // tf, tf).swapaxes(-3, -2).reshape(*lead, 2 * FFN_HIDDEN)


def _ffn_weights(w_up, conv_w, conv_b, w_down):
    tf = TF_FFN
    pad = jnp.zeros((conv_w.shape[0], SUBLANES - CONV_WIDTH - 1, conv_w.shape[2]), F32)
    conv_p = jnp.concatenate([conv_w, conv_b[:, None, :], pad], axis=1)
    return w_up.astype(BF16), _ffn_chunk_order(conv_p, tf), w_down.astype(BF16)


def _ffn(x, gain, out_gain, w_up, conv_p, w_down, layer, out_norm):
    b, s, _ = x.shape
    tm = min(s, TM_FFN)
    tf = TF_FFN
    nf = FFN_HIDDEN // tf
    kern = functools.partial(_ffn_kernel, tm=tm, tf=tf, nf=nf, out_norm=out_norm)

    def up_chunk(t):
        return jnp.minimum(t, nf - 1)

    def down_chunk(t):
        return jnp.maximum(t - 1, 0)

    return pl.pallas_call(
        kern,
        grid=(b, s // tm, nf + 1),
        in_specs=[
            pl.BlockSpec((None, tm, D_MODEL), lambda bi, i, t: (bi, i, 0)),
            pl.BlockSpec((1, D_MODEL), lambda bi, i, t: (0, 0)),
            pl.BlockSpec((1, D_MODEL), lambda bi, i, t: (0, 0)),
            pl.BlockSpec((None, D_MODEL, tf), lambda bi, i, t: (layer, 0, up_chunk(t))),
            pl.BlockSpec((None, D_MODEL, tf), lambda bi, i, t: (layer, 0, nf + up_chunk(t))),
            pl.BlockSpec((None, SUBLANES, 2 * tf), lambda bi, i, t: (layer, 0, down_chunk(t))),
            pl.BlockSpec((None, tf, D_MODEL), lambda bi, i, t: (layer, down_chunk(t), 0)),
        ],
        out_specs=pl.BlockSpec((None, tm, D_MODEL), lambda bi, i, t: (bi, i, 0)),
        out_shape=jax.ShapeDtypeStruct(x.shape, F32),
        scratch_shapes=[
            pltpu.VMEM((tm, D_MODEL), BF16),
            pltpu.VMEM((tm, 2 * tf), F32),
            pltpu.VMEM((nf, CARRY_ROWS, 2 * tf), F32),
        ],
        compiler_params=_params("arbitrary", "arbitrary", "arbitrary"),
        name="conv_ffn",
    )(x, gain, out_gain, w_up, w_up, conv_p, w_down)


def _rotate_half_cols(w):
    half = w.shape[-1] // 2
    return jnp.concatenate([-w[..., half:], w[..., :half]], axis=-1)


def _mla_weights(wq_a, wq_b, wkv_a, wkv_b):
    w_pe = wkv_a[:, MLA_KV_RANK:]
    w1 = jnp.concatenate([wq_a, wkv_a[:, :MLA_KV_RANK], w_pe, _rotate_half_cols(w_pe)], axis=1)
    qb = wq_b.reshape(MLA_Q_RANK, MLA_HEADS, MLA_NOPE + MLA_ROPE)
    nope = qb[:, :, :MLA_NOPE].reshape(MLA_Q_RANK, MLA_PAIRS, 2 * MLA_NOPE)
    pe = qb[:, :, MLA_NOPE:]
    rot = _rotate_half_cols(pe).reshape(MLA_Q_RANK, MLA_PAIRS, LANES)
    pe = pe.reshape(MLA_Q_RANK, MLA_PAIRS, LANES)
    wq = jnp.concatenate([nope, pe, rot], axis=2).reshape(MLA_Q_RANK, MLA_PAIRS * Q_PAIR_IN)
    kvb = wkv_b.reshape(MLA_KV_RANK, MLA_HEADS, MLA_NOPE + MLA_V)
    wkv = jnp.concatenate([kvb[:, :, :MLA_NOPE].reshape(MLA_KV_RANK, -1),
                           kvb[:, :, MLA_NOPE:].reshape(MLA_KV_RANK, -1)], axis=1)
    return w1.astype(BF16), wq.astype(BF16), wkv.astype(BF16)


def _mla_layer(x, gain, tabs, wq_a, q_norm, wq_b, wkv_a, kv_norm, wkv_b, wo, layer):
    b, s, dm = x.shape
    cm, sm, tk = tabs
    w1, wq, wkv = _mla_weights(wq_a, wq_b, wkv_a, wkv_b)
    x2 = x.reshape(b * s, dm)
    q_lat, ckv, kpe = _mla_down(x2, gain.reshape(1, dm), w1, q_norm.reshape(1, -1), kv_norm.reshape(1, -1), tk)
    q = _mla_q_up(q_lat, wq, cm, sm)
    kv = _mla_kv_up(ckv, wkv)
    o = _mla_attention(q.reshape(b, s, -1), kv.reshape(b, s, -1), kpe.reshape(b, s, -1), b, s)
    return _matmul_resid(o.reshape(b * s, -1), wo, layer, x2).reshape(b, s, dm)


def _dil_layer(x, gain, tabs, w_in, wo, layer):
    b, s, dm = x.shape
    cd, sa, sb = (t.reshape(b, s, LANES) for t in tabs)
    outs, lses = [], []
    for g, (_, d) in enumerate(DIL_GROUPS):
        qkv_g = _dil_qkv(x, gain.reshape(1, dm), w_in, layer, g, cd, sa, sb, d)
        o, lse = _dil_attention(qkv_g, d)
        outs.append(o)
        lses.append(lse)
    return _dil_merge_out(outs, lses, wo, layer, x)


def kernel(x, positions, attn_norm, ffn_norm, final_norm, mla_wq_a, mla_q_norm, mla_wq_b, mla_wkv_a,
           mla_kv_norm, mla_wkv_b, mla_wo, dil_w_in, dil_wo, ffn_w_up, ffn_conv_w, ffn_conv_b, ffn_w_down):
    b, s, dm = x.shape
    cm, sm, tk, cd, sa, sb = _rope_tables(positions)
    mla_wo_b = mla_wo.astype(BF16)
    dil_w_in_b = dil_w_in.astype(BF16)
    dil_wo_b = dil_wo.astype(BF16)
    ffn_up_b, ffn_conv_p, ffn_down_b = _ffn_weights(ffn_w_up, ffn_conv_w, ffn_conv_b, ffn_w_down)
    for i in range(DEPTH):
        j = i // N_MIXERS
        if i % N_MIXERS == 0:
            x = _mla_layer(x, attn_norm[i], (cm, sm, tk), mla_wq_a[j], mla_q_norm[j], mla_wq_b[j],
                           mla_wkv_a[j], mla_kv_norm[j], mla_wkv_b[j], mla_wo_b, j)
        else:
            x = _dil_layer(x, attn_norm[i], (cd, sa, sb), dil_w_in_b, dil_wo_b, j)
        x = _ffn(x, ffn_norm[i].reshape(1, dm), final_norm.reshape(1, dm), ffn_up_b, ffn_conv_p, ffn_down_b,
                 i, i == DEPTH - 1)
    return x
```

```python
import functools

import jax
import jax.numpy as jnp
from jax import lax
from jax.experimental import pallas as pl
from jax.experimental.pallas import tpu as pltpu

F32 = jnp.float32
BF16 = jnp.bfloat16

D_MODEL = 2048
DEPTH = 4
N_MIXERS = 2
ROPE_THETA = 500000.0
NORM_EPS = 1e-6
LOG2_E = 1.4426950408889634

MLA_HEADS = 16
MLA_Q_RANK = 512
MLA_KV_RANK = 512
MLA_NOPE = 128
MLA_ROPE = 64
MLA_V = 128

DIL_GROUPS = ((128, 1), (512, 4), (2048, 16))
DIL_HEADS = 16
DIL_HEAD_DIM = 128
DIL_ROT = DIL_HEAD_DIM // 4
DIL_BLOCK = 128
DIL_BLOCKS_PER_STEP = 4

FFN_HIDDEN = 5632
CONV_WIDTH = 3

LANES = 128
SUBLANES = 8
VMEM_BYTES_V7X = 64 * 1024 * 1024
VMEM_LIMIT = VMEM_BYTES_V7X - 8 * 1024 * 1024

TM_PROJ = 1024
TN_PROJ = 1024
TM_SMALL = 512
TM_MLA_PROJ = 1024
TM_MERGE = 256
TQ_MLA = 512
TM_FFN = 512
TF_FFN = 512
CARRY_ROWS = SUBLANES
PERM_SLOTS = 4


def _params(*sem):
    return pltpu.CompilerParams(dimension_semantics=sem, vmem_limit_bytes=VMEM_LIMIT)


def _rms(x, g):
    ms = jnp.mean(x * x, axis=-1, keepdims=True)
    return (x * lax.rsqrt(ms + NORM_EPS)) * g


def _dot(a, b):
    return jnp.dot(a, b, preferred_element_type=F32)


def _dot_nt(a, b):
    return lax.dot_general(a, b, (((1,), (1,)), ((), ())), preferred_element_type=F32)


def _rope_tables_kernel(pos_ref, fm_ref, fd_ref, cm_ref, sm_ref, tk_ref, cd_ref, sa_ref, sb_ref):
    pos = pos_ref[...]
    lane = lax.broadcasted_iota(jnp.int32, (1, LANES), 1)
    am = pos * fm_ref[...]
    cm = jnp.cos(am)
    sm = jnp.sin(am)
    cm_ref[...] = cm
    sm_ref[...] = sm
    tk_ref[...] = jnp.where(lane < MLA_ROPE, cm, sm)
    ad = pos * fd_ref[...]
    cd = jnp.cos(ad)
    sd = jnp.sin(ad)
    half = DIL_ROT // 2
    cd_ref[...] = jnp.where(lane < DIL_ROT, cd, 1.0)
    sa_ref[...] = jnp.where(lane < half, 0.0, jnp.where(lane < DIL_ROT, sd, 0.0))
    sb_ref[...] = jnp.where(lane < half, -sd, 0.0)


def _rope_tables(positions):
    m = positions.size
    pos = positions.reshape(m, 1).astype(F32)
    inv_m = ROPE_THETA ** (-jnp.arange(0, MLA_ROPE, 2, dtype=F32) / MLA_ROPE)
    inv_d = ROPE_THETA ** (-jnp.arange(0, DIL_ROT, 2, dtype=F32) / DIL_ROT)
    fm = jnp.tile(inv_m, LANES // inv_m.size).reshape(1, LANES)
    fd = jnp.tile(inv_d, LANES // inv_d.size).reshape(1, LANES)
    tm = min(m, 2048)
    row = pl.BlockSpec((tm, LANES), lambda i: (i, 0))
    vec = pl.BlockSpec((1, LANES), lambda i: (0, 0))
    tab = jax.ShapeDtypeStruct((m, LANES), F32)
    return pl.pallas_call(
        _rope_tables_kernel,
        grid=(m // tm,),
        in_specs=[pl.BlockSpec((tm, 1), lambda i: (i, 0)), vec, vec],
        out_specs=[row] * 6,
        out_shape=[tab] * 6,
        compiler_params=_params("parallel"),
        name="rope_tables",
    )(pos, fm, fd)


def _mla_down_kernel(x_ref, g_ref, w_ref, qn_ref, kvn_ref, tk_ref, ql_ref, ckv_ref, kpe_ref):
    half = x_ref.shape[0] // 2
    for rows in (slice(0, half), slice(half, 2 * half)):
        h = _rms(x_ref[rows, :], g_ref[...]).astype(BF16)
        acc = _dot(h, w_ref[...])
        ql_ref[rows, :] = _rms(acc[:, :MLA_Q_RANK], qn_ref[...]).astype(BF16)
        ckv_ref[rows, :] = _rms(acc[:, MLA_Q_RANK:MLA_Q_RANK + MLA_KV_RANK], kvn_ref[...]).astype(BF16)
        y = acc[:, MLA_Q_RANK + MLA_KV_RANK:] * tk_ref[rows, :]
        z = y + pltpu.roll(y, MLA_ROPE, 1)
        lane = lax.broadcasted_iota(jnp.int32, z.shape, 1)
        kpe_ref[rows, :LANES] = jnp.where(lane < MLA_ROPE, z, 0.0).astype(BF16)
        kpe_ref[rows, LANES:] = jnp.where(lane < MLA_ROPE, 0.0, z).astype(BF16)


def _mla_down(x2, gain, w1, q_norm, kv_norm, tk):
    m = x2.shape[0]
    tm = min(m, TM_MLA_PROJ)
    n1 = w1.shape[1]
    return pl.pallas_call(
        _mla_down_kernel,
        grid=(m // tm,),
        in_specs=[
            pl.BlockSpec((tm, D_MODEL), lambda i: (i, 0)),
            pl.BlockSpec((1, D_MODEL), lambda i: (0, 0)),
            pl.BlockSpec((D_MODEL, n1), lambda i: (0, 0)),
            pl.BlockSpec((1, MLA_Q_RANK), lambda i: (0, 0)),
            pl.BlockSpec((1, MLA_KV_RANK), lambda i: (0, 0)),
            pl.BlockSpec((tm, LANES), lambda i: (i, 0)),
        ],
        out_specs=[
            pl.BlockSpec((tm, MLA_Q_RANK), lambda i: (i, 0)),
            pl.BlockSpec((tm, MLA_KV_RANK), lambda i: (i, 0)),
            pl.BlockSpec((tm, 2 * LANES), lambda i: (i, 0)),
        ],
        out_shape=[
            jax.ShapeDtypeStruct((m, MLA_Q_RANK), BF16),
            jax.ShapeDtypeStruct((m, MLA_KV_RANK), BF16),
            jax.ShapeDtypeStruct((m, 2 * LANES), BF16),
        ],
        compiler_params=_params("parallel"),
        name="mla_down",
    )(x2, gain, w1, q_norm, kv_norm, tk)


MLA_PAIRS = MLA_HEADS // 2
Q_PAIR_IN = 2 * MLA_NOPE + 2 * LANES
Q_PAIR_OUT = 2 * MLA_NOPE + LANES


def _mla_q_up_kernel(a_ref, w_ref, c_ref, s_ref, o_ref):
    a = a_ref[...]
    c = c_ref[...]
    s = s_ref[...]
    for p in range(MLA_PAIRS):
        acc = _dot(a, w_ref[:, p * Q_PAIR_IN:(p + 1) * Q_PAIR_IN])
        o_ref[:, p * Q_PAIR_OUT:p * Q_PAIR_OUT + 2 * MLA_NOPE] = acc[:, :2 * MLA_NOPE].astype(BF16)
        pe = acc[:, 2 * MLA_NOPE:2 * MLA_NOPE + LANES] * c + acc[:, 2 * MLA_NOPE + LANES:] * s
        o_ref[:, p * Q_PAIR_OUT + 2 * MLA_NOPE:(p + 1) * Q_PAIR_OUT] = pe.astype(BF16)


def _mla_q_up(q_lat, wq, cm, sm):
    m = q_lat.shape[0]
    tm = min(m, TM_MLA_PROJ)
    n_out = MLA_PAIRS * Q_PAIR_OUT
    return pl.pallas_call(
        _mla_q_up_kernel,
        grid=(m // tm,),
        in_specs=[
            pl.BlockSpec((tm, MLA_Q_RANK), lambda i: (i, 0)),
            pl.BlockSpec(wq.shape, lambda i: (0, 0)),
            pl.BlockSpec((tm, LANES), lambda i: (i, 0)),
            pl.BlockSpec((tm, LANES), lambda i: (i, 0)),
        ],
        out_specs=pl.BlockSpec((tm, n_out), lambda i: (i, 0)),
        out_shape=jax.ShapeDtypeStruct((m, n_out), BF16),
        compiler_params=_params("parallel"),
        name="mla_q_up",
    )(q_lat, wq, cm, sm)


def _matmul_bf16_kernel(a_ref, w_ref, o_ref):
    o_ref[...] = _dot(a_ref[...], w_ref[...]).astype(BF16)


def _mla_kv_up(ckv, wkv):
    m = ckv.shape[0]
    tm = min(m, TM_MLA_PROJ)
    n = wkv.shape[1]
    return pl.pallas_call(
        _matmul_bf16_kernel,
        grid=(m // tm,),
        in_specs=[
            pl.BlockSpec((tm, MLA_KV_RANK), lambda i: (i, 0)),
            pl.BlockSpec(wkv.shape, lambda i: (0, 0)),
        ],
        out_specs=pl.BlockSpec((tm, n), lambda i: (i, 0)),
        out_shape=jax.ShapeDtypeStruct((m, n), BF16),
        compiler_params=_params("parallel"),
        name="mla_kv_up",
    )(ckv, wkv)


def _mla_attn_kernel(q_ref, kn_ref, kpe_ref, v_ref, o_ref, m_ref, l_ref, acc_ref, *, tq, scale):
    g = pl.program_id(2)
    scale_log2e = scale * LOG2_E
    row = lax.broadcasted_iota(jnp.int32, (tq, tq), 0)
    col = lax.broadcasted_iota(jnp.int32, (tq, tq), 1)
    causal = col <= row
    heads = (0, 1)
    for sub in range(2):
        qi = 2 * g + sub
        rows = slice(sub * tq, (sub + 1) * tq)
        q_pe = q_ref[rows, 2 * MLA_NOPE:]
        qs = [jnp.concatenate([q_ref[rows, hh * LANES:(hh + 1) * LANES], q_pe], axis=1) for hh in heads]
        m_ref[...] = jnp.full(m_ref.shape, -jnp.inf, F32)
        l_ref[...] = jnp.zeros(l_ref.shape, F32)
        acc_ref[...] = jnp.zeros(acc_ref.shape, F32)

        def step(hh, kb, masked, nkb=1):
            lanes = slice(hh * LANES, (hh + 1) * LANES)
            tk = nkb * tq
            ks = pl.multiple_of(kb * tq, tq)
            k = jnp.concatenate([kn_ref[pl.ds(ks, tk), lanes], kpe_ref[pl.ds(ks, tk), lanes]], axis=1)
            s = _dot_nt(qs[hh], k)
            if masked:
                s = jnp.where(causal, s, -jnp.inf)
            m_prev = m_ref[hh]
            m_next = jnp.maximum(m_prev, jnp.max(s, axis=1, keepdims=True))
            m_wide = jnp.concatenate([m_next] * (tk // LANES), axis=1)
            p = jnp.exp2((s - m_wide) * scale_log2e)
            alpha = jnp.exp2((m_prev - m_next) * scale_log2e)
            l_ref[hh] = alpha * l_ref[hh] + jnp.sum(p, axis=1, keepdims=True)
            m_ref[hh] = m_next
            acc_ref[hh] = alpha * acc_ref[hh] + _dot(p.astype(BF16), v_ref[pl.ds(ks, tk), lanes])

        def trip(k2, carry):
            for hh in heads:
                step(hh, 2 * k2, False, nkb=2)
            return carry

        lax.fori_loop(0, g, trip, 0)
        if sub == 1:
            for hh in heads:
                step(hh, qi - 1, False)
        for hh in heads:
            step(hh, qi, True)
        for hh in heads:
            o_ref[rows, hh * LANES:(hh + 1) * LANES] = (acc_ref[hh] / l_ref[hh]).astype(BF16)


def _mla_attention(q, kv, kpe, b, s):
    tq = min(s // 2, TQ_MLA)
    scale = (MLA_NOPE + MLA_ROPE) ** -0.5
    kern = functools.partial(_mla_attn_kernel, tq=tq, scale=scale)
    pair_w = 2 * LANES
    return pl.pallas_call(
        kern,
        grid=(b, MLA_PAIRS, s // (2 * tq)),
        in_specs=[
            pl.BlockSpec((None, 2 * tq, Q_PAIR_OUT), lambda bi, p, g: (bi, g, p)),
            pl.BlockSpec((None, s, pair_w), lambda bi, p, g: (bi, 0, p)),
            pl.BlockSpec((None, s, pair_w), lambda bi, p, g: (bi, 0, 0)),
            pl.BlockSpec((None, s, pair_w), lambda bi, p, g: (bi, 0, MLA_PAIRS + p)),
        ],
        out_specs=pl.BlockSpec((None, 2 * tq, pair_w), lambda bi, p, g: (bi, g, p)),
        out_shape=jax.ShapeDtypeStruct((b, s, MLA_HEADS * MLA_V), BF16),
        scratch_shapes=[
            pltpu.VMEM((2, tq, LANES), F32),
            pltpu.VMEM((2, tq, LANES), F32),
            pltpu.VMEM((2, tq, LANES), F32),
        ],
        compiler_params=_params("parallel", "parallel", "arbitrary"),
        name="mla_attention",
    )(q, kv, kpe, kv)


def _matmul_resid_kernel(a_ref, w_ref, r_ref, o_ref):
    o_ref[...] = r_ref[...] + _dot(a_ref[...], w_ref[...])


def _matmul_resid(a, w, layer, resid):
    m, k = a.shape
    n = w.shape[2]
    tm = min(m, TM_SMALL)
    return pl.pallas_call(
        _matmul_resid_kernel,
        grid=(m // tm,),
        in_specs=[
            pl.BlockSpec((tm, k), lambda i: (i, 0)),
            pl.BlockSpec((None, k, n), lambda i: (layer, 0, 0)),
            pl.BlockSpec((tm, n), lambda i: (i, 0)),
        ],
        out_specs=pl.BlockSpec((tm, n), lambda i: (i, 0)),
        out_shape=jax.ShapeDtypeStruct((m, n), F32),
        compiler_params=_params("parallel"),
        name="out_proj_resid",
    )(a, w, resid)


def _dil_rope(xc, c, sa, sb):
    half = DIL_ROT // 2
    return xc * c + pltpu.roll(xc, half, 1) * sa + pltpu.roll(xc, LANES - half, 1) * sb


def _dil_qkv_kernel(x_ref, g_ref, w_ref, c_ref, sa_ref, sb_ref, o_ref, h_ref, acc_ref, *perm_refs,
                    d, tm, n_rope, nj):
    j = pl.program_id(2)
    rows = tm // d
    if d > 1:
        xs_ref, cs_ref, sas_ref, sbs_ref = perm_refs
    else:
        cs_ref, sas_ref, sbs_ref = c_ref, sa_ref, sb_ref

    def norm_input():
        if d == 1:
            h_ref[...] = _rms(x_ref[...], g_ref[...]).astype(BF16)
            return
        x = x_ref[...]
        rinv = lax.rsqrt(jnp.mean(x * x, axis=-1, keepdims=True) + NORM_EPS)
        for t in range(x_ref.shape[1] // LANES):
            lanes = slice(t * LANES, (t + 1) * LANES)
            slot = t % xs_ref.shape[0]
            xs_ref[slot] = (x_ref[:, lanes] * rinv) * g_ref[:, lanes]
            for r in range(d):
                h_ref[r * rows:(r + 1) * rows, lanes] = xs_ref[slot, pl.ds(r, rows, stride=d), :].astype(BF16)
        for r in range(d):
            sl = pl.ds(r, rows, stride=d)
            dst = slice(r * rows, (r + 1) * rows)
            cs_ref[dst, :] = c_ref[sl, :]
            sas_ref[dst, :] = sa_ref[sl, :]
            sbs_ref[dst, :] = sb_ref[sl, :]

    def matmul():
        acc_ref[...] = _dot(h_ref[...], w_ref[...])

    def emit(rope):
        for r in range(d):
            src = slice(r * rows, (r + 1) * rows)
            if rope:
                c, sa, sb = cs_ref[src, :], sas_ref[src, :], sbs_ref[src, :]
            for t in range(acc_ref.shape[1] // LANES):
                a = acc_ref[src, t * LANES:(t + 1) * LANES]
                if rope:
                    a = _dil_rope(a, c, sa, sb)
                o_ref[r, :, t * LANES:(t + 1) * LANES] = a.astype(BF16)

    @pl.when(j == 0)
    def _():
        norm_input()
        matmul()

    @pl.when(jnp.logical_and(j >= 1, j <= n_rope))
    def _():
        emit(True)
        matmul()

    @pl.when(jnp.logical_and(j > n_rope, j < nj))
    def _():
        emit(False)
        matmul()

    @pl.when(j == nj)
    def _():
        emit(False)


def _dil_qkv(x, gain, w_in, layer, g, cd, sa, sb, d):
    b, s, _ = x.shape
    n = 3 * DIL_HEADS * DIL_HEAD_DIM
    tm = min(s, TM_PROJ)
    tn = TN_PROJ
    col0 = g * (n // tn)
    n_rope = 2 * DIL_HEADS * DIL_HEAD_DIM // tn
    nj = n // tn
    assert n_rope < nj
    kern = functools.partial(_dil_qkv_kernel, d=d, tm=tm, n_rope=n_rope, nj=nj)
    tab = pl.BlockSpec((None, tm, LANES), lambda bi, i, j: (bi, i, 0))
    perm_scratch = []
    if d > 1:
        perm_scratch = [pltpu.VMEM((PERM_SLOTS, tm, LANES), F32)] + [pltpu.VMEM((tm, LANES), F32)] * 3
    return pl.pallas_call(
        kern,
        grid=(b, s // tm, nj + 1),
        in_specs=[
            pl.BlockSpec((None, tm, D_MODEL), lambda bi, i, j: (bi, i, 0)),
            pl.BlockSpec((1, D_MODEL), lambda bi, i, j: (0, 0)),
            pl.BlockSpec((None, D_MODEL, tn), lambda bi, i, j: (layer, 0, col0 + jnp.minimum(j, nj - 1))),
            tab, tab, tab,
        ],
        out_specs=pl.BlockSpec((None, d, tm // d, tn), lambda bi, i, j: (bi, 0, i, jnp.maximum(j - 1, 0))),
        out_shape=jax.ShapeDtypeStruct((b, d, s // d, n), BF16),
        scratch_shapes=[pltpu.VMEM((tm, D_MODEL), BF16), pltpu.VMEM((tm, tn), F32)] + perm_scratch,
        compiler_params=_params("parallel", "parallel", "arbitrary"),
        name=f"dil_qkv_d{d}",
    )(x, gain, w_in, cd, sa, sb)


def _dil_attn_kernel(q_ref, kc_ref, kp_ref, vc_ref, vp_ref, o_ref, lse_ref, s_ref, p_ref, *, scale, nblk):
    jb = pl.program_id(2)
    qb = DIL_BLOCK
    qi = lax.broadcasted_iota(jnp.int32, (qb, 2 * qb), 0)
    ki = lax.broadcasted_iota(jnp.int32, (qb, 2 * qb), 1)
    lane = lax.broadcasted_iota(jnp.int32, (qb, LANES), 1)

    def keys(cur_ref, prev_ref, blk, sl):
        own = cur_ref[blk * qb:(blk + 1) * qb, sl]
        before = prev_ref[:, sl] if blk == 0 else cur_ref[(blk - 1) * qb:blk * qb, sl]
        return jnp.concatenate([before, own], axis=0)

    for blk in range(nblk):
        for h in range(DIL_HEADS):
            sl = slice(h * DIL_HEAD_DIM, (h + 1) * DIL_HEAD_DIM)
            s_ref[blk, h] = _dot_nt(q_ref[blk * qb:(blk + 1) * qb, sl], keys(kc_ref, kp_ref, blk, sl))
    for blk in range(nblk):
        first_key = jnp.where(jb > 0, qi, qb) if blk == 0 else qi
        band = jnp.logical_and(ki >= first_key, ki <= qi + qb)
        s = jnp.where(band[None], s_ref[blk], -jnp.inf)
        m = jnp.max(s, axis=-1, keepdims=True)
        e = jnp.exp2((s - m) * (scale * LOG2_E))
        l = jnp.sum(e, axis=-1, keepdims=True)
        p_ref[blk] = e.astype(BF16)
        inv = 1.0 / l
        lse = scale * m + jnp.log(l)
        lse_all = jnp.zeros((qb, LANES), F32)
        for h in range(DIL_HEADS):
            lse_all = jnp.where(lane == h, lse[h], lse_all)
        lse_ref[blk * qb:(blk + 1) * qb, :] = lse_all
        for h in range(DIL_HEADS):
            sl = slice(h * DIL_HEAD_DIM, (h + 1) * DIL_HEAD_DIM)
            pv = _dot(p_ref[blk, h], keys(vc_ref, vp_ref, blk, sl))
            o_ref[blk * qb:(blk + 1) * qb, sl] = pv * inv[h]


def _dil_attention(qkv_g, d):
    b, _, l, _ = qkv_g.shape
    hw = DIL_HEADS * DIL_HEAD_DIM
    qb = DIL_BLOCK
    nblk = min(DIL_BLOCKS_PER_STEP, l // qb)
    rows = nblk * qb
    kern = functools.partial(_dil_attn_kernel, scale=DIL_HEAD_DIM ** -0.5, nblk=nblk)

    def cur(c):
        return pl.BlockSpec((None, None, rows, hw), lambda bi, r, jb: (bi, r, jb, c))

    def prev(c):
        return pl.BlockSpec((None, None, qb, hw), lambda bi, r, jb: (bi, r, jnp.maximum(jb * nblk - 1, 0), c))

    return pl.pallas_call(
        kern,
        grid=(b, d, l // rows),
        in_specs=[cur(0), cur(1), prev(1), cur(2), prev(2)],
        out_specs=[
            pl.BlockSpec((None, None, rows, hw), lambda bi, r, jb: (bi, r, jb, 0)),
            pl.BlockSpec((None, None, rows, LANES), lambda bi, r, jb: (bi, r, jb, 0)),
        ],
        out_shape=[
            jax.ShapeDtypeStruct((b, d, l, hw), F32),
            jax.ShapeDtypeStruct((b, d, l, LANES), F32),
        ],
        scratch_shapes=[
            pltpu.VMEM((nblk, DIL_HEADS, qb, 2 * qb), F32),
            pltpu.VMEM((nblk, DIL_HEADS, qb, 2 * qb), BF16),
        ],
        compiler_params=_params("parallel", "parallel", "arbitrary"),
        name=f"dil_attention_d{d}",
    )(qkv_g, qkv_g, qkv_g, qkv_g, qkv_g)


def _dil_merge_kernel(o0_ref, l0_ref, o1_ref, l1_ref, o2_ref, l2_ref, w_ref, x_ref, out_ref,
                      n1_ref, n2_ref, nl1_ref, nl2_ref, a_ref, *, tm, n):
    i = pl.program_id(1)
    d1 = DIL_GROUPS[1][1]
    d2 = DIL_GROUPS[2][1]

    def merge():
        for d, o_ref, l_ref, n_ref, nl_ref in ((d1, o1_ref, l1_ref, n1_ref, nl1_ref),
                                               (d2, o2_ref, l2_ref, n2_ref, nl2_ref)):
            for r in range(d):
                rows = pl.ds(r, tm // d, stride=d)
                nl_ref[rows, :] = l_ref[r]
                for h in range(DIL_HEADS):
                    n_ref[h, rows, :] = o_ref[r, :, h * DIL_HEAD_DIM:(h + 1) * DIL_HEAD_DIM]
        for h in range(DIL_HEADS):
            sl = slice(h * DIL_HEAD_DIM, (h + 1) * DIL_HEAD_DIM)
            a0 = l0_ref[:, h:h + 1]
            a1 = nl1_ref[:, h:h + 1]
            a2 = nl2_ref[:, h:h + 1]
            mx = jnp.maximum(jnp.maximum(a0, a1), a2)
            w0 = jnp.exp(a0 - mx)
            w1 = jnp.exp(a1 - mx)
            w2 = jnp.exp(a2 - mx)
            den = w0 + w1 + w2
            o = (w0 / den) * o0_ref[:, sl] + (w1 / den) * n1_ref[h] + (w2 / den) * n2_ref[h]
            a_ref[:, sl] = o.astype(BF16)

    def project(a):
        out_ref[...] = x_ref[...] + _dot(a, w_ref[...])

    @pl.when(i == 0)
    def _():
        merge()

    @pl.when(jnp.logical_and(i > 0, i < n))
    def _():
        a = a_ref[...]
        merge()
        project(a)

    @pl.when(i == n)
    def _():
        project(a_ref[...])


def _dil_merge_out(outs, lses, wo, layer, x):
    b, s, dm = x.shape
    hw = DIL_HEADS * DIL_HEAD_DIM
    tm = min(s, TM_MERGE)
    d1 = DIL_GROUPS[1][1]
    d2 = DIL_GROUPS[2][1]
    n = s // tm
    kern = functools.partial(_dil_merge_kernel, tm=tm, n=n)

    def merged(i):
        return jnp.minimum(i, n - 1)

    def projected(i):
        return jnp.maximum(i - 1, 0)

    def grp(d, w):
        return pl.BlockSpec((None, d, tm // d, w), lambda bi, i: (bi, 0, merged(i), 0))

    return pl.pallas_call(
        kern,
        grid=(b, n + 1),
        in_specs=[
            pl.BlockSpec((None, None, tm, hw), lambda bi, i: (bi, 0, merged(i), 0)),
            pl.BlockSpec((None, None, tm, LANES), lambda bi, i: (bi, 0, merged(i), 0)),
            grp(d1, hw), grp(d1, LANES), grp(d2, hw), grp(d2, LANES),
            pl.BlockSpec((None, hw, dm), lambda bi, i: (layer, 0, 0)),
            pl.BlockSpec((None, tm, dm), lambda bi, i: (bi, projected(i), 0)),
        ],
        out_specs=pl.BlockSpec((None, tm, dm), lambda bi, i: (bi, projected(i), 0)),
        out_shape=jax.ShapeDtypeStruct((b, s, dm), F32),
        scratch_shapes=[
            pltpu.VMEM((DIL_HEADS, tm, DIL_HEAD_DIM), F32), pltpu.VMEM((DIL_HEADS, tm, DIL_HEAD_DIM), F32),
            pltpu.VMEM((tm, LANES), F32), pltpu.VMEM((tm, LANES), F32),
            pltpu.VMEM((tm, hw), BF16),
        ],
        compiler_params=_params("parallel", "arbitrary"),
        name="dil_merge_out",
    )(outs[0], lses[0], outs[1], lses[1], outs[2], lses[2], wo, x)


def _ffn_kernel(x_ref, g_ref, og_ref, wg_ref, wv_ref, cp_ref, wd_ref, o_ref, h_ref, u_ref, prev_ref, *,
                tm, tf, nf, out_norm):
    i = pl.program_id(1)
    t = pl.program_id(2)
    row = lax.broadcasted_iota(jnp.int32, (CARRY_ROWS, 1), 0)

    def up():
        h = h_ref[...]
        u_ref[:, :tf] = _dot(h, wg_ref[...])
        u_ref[:, tf:] = _dot(h, wv_ref[...])

    def conv(u, c):
        prev = prev_ref[c]
        prev_ref[c] = u[tm - CARRY_ROWS:, :]
        p1 = prev[CARRY_ROWS - 1:CARRY_ROWS, :]
        p2 = prev[CARRY_ROWS - 2:CARRY_ROWS - 1, :]
        r1 = pltpu.roll(u, 1, 0)
        r2 = pltpu.roll(u, 2, 0)
        u1 = jnp.concatenate([jnp.where(row == 0, p1, r1[:CARRY_ROWS]), r1[CARRY_ROWS:]], axis=0)
        u2 = jnp.concatenate(
            [jnp.where(row == 0, p2, jnp.where(row == 1, p1, r2[:CARRY_ROWS])), r2[CARRY_ROWS:]], axis=0)
        cp = cp_ref[...]
        acc = cp[CONV_WIDTH:CONV_WIDTH + 1, :] + u2 * cp[0:1, :]
        acc = acc + u1 * cp[1:2, :]
        return acc + u * cp[2:3, :]

    def down(u):
        cv = conv(u, t - 1)
        gate = cv[:, :tf]
        val = cv[:, tf:]
        act = (gate * (1.0 / (1.0 + jnp.exp(-gate)))) * val
        o_ref[...] += _dot(act.astype(BF16), wd_ref[...])

    @pl.when(t == 0)
    def _():
        x = x_ref[...]
        h_ref[...] = _rms(x, g_ref[...]).astype(BF16)
        o_ref[...] = x
        up()

    @pl.when(jnp.logical_and(t == 0, i == 0))
    def _():
        prev_ref[...] = jnp.zeros(prev_ref.shape, F32)

    @pl.when(jnp.logical_and(t > 0, t < nf))
    def _():
        u = u_ref[...]
        up()
        down(u)

    @pl.when(t == nf)
    def _():
        down(u_ref[...])
        if out_norm:
            o_ref[...] = _rms(o_ref[...], og_ref[...])


def _ffn_chunk_order(a, tf):
    lead = a.shape[:-1]
    return a.reshape(*lead, 2, FFN_HIDDEN // tf, tf).swapaxes(-3, -2).reshape(*lead, 2 * FFN_HIDDEN)


def _ffn_weights(w_up, conv_w, conv_b, w_down):
    tf = TF_FFN
    pad = jnp.zeros((conv_w.shape[0], SUBLANES - CONV_WIDTH - 1, conv_w.shape[2]), F32)
    conv_p = jnp.concatenate([conv_w, conv_b[:, None, :], pad], axis=1)
    return w_up.astype(BF16), _ffn_chunk_order(conv_p, tf), w_down.astype(BF16)


def _ffn(x, gain, out_gain, w_up, conv_p, w_down, layer, out_norm):
    b, s, _ = x.shape
    tm = min(s, TM_FFN)
    tf = TF_FFN
    nf = FFN_HIDDEN // tf
    kern = functools.partial(_ffn_kernel, tm=tm, tf=tf, nf=nf, out_norm=out_norm)

    def up_chunk(t):
        return jnp.minimum(t, nf - 1)

    def down_chunk(t):
        return jnp.maximum(t - 1, 0)

    return pl.pallas_call(
        kern,
        grid=(b, s // tm, nf + 1),
        in_specs=[
            pl.BlockSpec((None, tm, D_MODEL), lambda bi, i, t: (bi, i, 0)),
            pl.BlockSpec((1, D_MODEL), lambda bi, i, t: (0, 0)),
            pl.BlockSpec((1, D_MODEL), lambda bi, i, t: (0, 0)),
            pl.BlockSpec((None, D_MODEL, tf), lambda bi, i, t: (layer, 0, up_chunk(t))),
            pl.BlockSpec((None, D_MODEL, tf), lambda bi, i, t: (layer, 0, nf + up_chunk(t))),
            pl.BlockSpec((None, SUBLANES, 2 * tf), lambda bi, i, t: (layer, 0, down_chunk(t))),
            pl.BlockSpec((None, tf, D_MODEL), lambda bi, i, t: (layer, down_chunk(t), 0)),
        ],
        out_specs=pl.BlockSpec((None, tm, D_MODEL), lambda bi, i, t: (bi, i, 0)),
        out_shape=jax.ShapeDtypeStruct(x.shape, F32),
        scratch_shapes=[
            pltpu.VMEM((tm, D_MODEL), BF16),
            pltpu.VMEM((tm, 2 * tf), F32),
            pltpu.VMEM((nf, CARRY_ROWS, 2 * tf), F32),
        ],
        compiler_params=_params("arbitrary", "arbitrary", "arbitrary"),
        name="conv_ffn",
    )(x, gain, out_gain, w_up, w_up, conv_p, w_down)


def _rotate_half_cols(w):
    half = w.shape[-1] // 2
    return jnp.concatenate([-w[..., half:], w[..., :half]], axis=-1)


def _mla_weights(wq_a, wq_b, wkv_a, wkv_b):
    w_pe = wkv_a[:, MLA_KV_RANK:]
    w1 = jnp.concatenate([wq_a, wkv_a[:, :MLA_KV_RANK], w_pe, _rotate_half_cols(w_pe)], axis=1)
    qb = wq_b.reshape(MLA_Q_RANK, MLA_HEADS, MLA_NOPE + MLA_ROPE)
    nope = qb[:, :, :MLA_NOPE].reshape(MLA_Q_RANK, MLA_PAIRS, 2 * MLA_NOPE)
    pe = qb[:, :, MLA_NOPE:]
    rot = _rotate_half_cols(pe).reshape(MLA_Q_RANK, MLA_PAIRS, LANES)
    pe = pe.reshape(MLA_Q_RANK, MLA_PAIRS, LANES)
    wq = jnp.concatenate([nope, pe, rot], axis=2).reshape(MLA_Q_RANK, MLA_PAIRS * Q_PAIR_IN)
    kvb = wkv_b.reshape(MLA_KV_RANK, MLA_HEADS, MLA_NOPE + MLA_V)
    wkv = jnp.concatenate([kvb[:, :, :MLA_NOPE].reshape(MLA_KV_RANK, -1),
                           kvb[:, :, MLA_NOPE:].reshape(MLA_KV_RANK, -1)], axis=1)
    return w1.astype(BF16), wq.astype(BF16), wkv.astype(BF16)


def _mla_layer(x, gain, tabs, wq_a, q_norm, wq_b, wkv_a, kv_norm, wkv_b, wo, layer):
    b, s, dm = x.shape
    cm, sm, tk = tabs
    w1, wq, wkv = _mla_weights(wq_a, wq_b, wkv_a, wkv_b)
    x2 = x.reshape(b * s, dm)
    q_lat, ckv, kpe = _mla_down(x2, gain.reshape(1, dm), w1, q_norm.reshape(1, -1), kv_norm.reshape(1, -1), tk)
    q = _mla_q_up(q_lat, wq, cm, sm)
    kv = _mla_kv_up(ckv, wkv)
    o = _mla_attention(q.reshape(b, s, -1), kv.reshape(b, s, -1), kpe.reshape(b, s, -1), b, s)
    return _matmul_resid(o.reshape(b * s, -1), wo, layer, x2).reshape(b, s, dm)


def _dil_layer(x, gain, tabs, w_in, wo, layer):
    b, s, dm = x.shape
    cd, sa, sb = (t.reshape(b, s, LANES) for t in tabs)
    outs, lses = [], []
    for g, (_, d) in enumerate(DIL_GROUPS):
        qkv_g = _dil_qkv(x, gain.reshape(1, dm), w_in, layer, g, cd, sa, sb, d)
        o, lse = _dil_attention(qkv_g, d)
        outs.append(o)
        lses.append(lse)
    return _dil_merge_out(outs, lses, wo, layer, x)


def kernel(x, positions, attn_norm, ffn_norm, final_norm, mla_wq_a, mla_q_norm, mla_wq_b, mla_wkv_a,
           mla_kv_norm, mla_wkv_b, mla_wo, dil_w_in, dil_wo, ffn_w_up, ffn_conv_w, ffn_conv_b, ffn_w_down):
    b, s, dm = x.shape
    cm, sm, tk, cd, sa, sb = _rope_tables(positions)
    mla_wo_b = mla_wo.astype(BF16)
    dil_w_in_b = dil_w_in.astype(BF16)
    dil_wo_b = dil_wo.astype(BF16)
    ffn_up_b, ffn_conv_p, ffn_down_b = _ffn_weights(ffn_w_up, ffn_conv_w, ffn_conv_b, ffn_w_down)
    for i in range(DEPTH):
        j = i // N_MIXERS
        if i % N_MIXERS == 0:
            x = _mla_layer(x, attn_norm[i], (cm, sm, tk), mla_wq_a[j], mla_q_norm[j], mla_wq_b[j],
                           mla_wkv_a[j], mla_kv_norm[j], mla_wkv_b[j], mla_wo_b, j)
        else:
            x = _dil_layer(x, attn_norm[i], (cd, sa, sb), dil_w_in_b, dil_wo_b, j)
        x = _ffn(x, ffn_norm[i].reshape(1, dm), final_norm.reshape(1, dm), ffn_up_b, ffn_conv_p, ffn_down_b,
                 i, i == DEPTH - 1)
    return x
```

```python
import functools

import jax
import jax.numpy as jnp
from jax import lax
from jax.experimental import pallas as pl
from jax.experimental.pallas import tpu as pltpu

F32 = jnp.float32
BF16 = jnp.bfloat16

D_MODEL = 2048
DEPTH = 4
N_MIXERS = 2
ROPE_THETA = 500000.0
NORM_EPS = 1e-6
LOG2_E = 1.4426950408889634

MLA_HEADS = 16
MLA_Q_RANK = 512
MLA_KV_RANK = 512
MLA_NOPE = 128
MLA_ROPE = 64
MLA_V = 128

DIL_GROUPS = ((128, 1), (512, 4), (2048, 16))
DIL_HEADS = 16
DIL_HEAD_DIM = 128
DIL_ROT = DIL_HEAD_DIM // 4
DIL_BLOCK = 128
DIL_BLOCKS_PER_STEP = 4

FFN_HIDDEN = 5632
CONV_WIDTH = 3

LANES = 128
SUBLANES = 8
VMEM_BYTES_V7X = 64 * 1024 * 1024
VMEM_LIMIT = VMEM_BYTES_V7X - 8 * 1024 * 1024

TM_PROJ = 1024
TN_PROJ = 1024
TM_SMALL = 512
TM_MLA_PROJ = 1024
TM_MERGE = 256
TQ_MLA = 512
TM_FFN = 512
TF_FFN = 512
CARRY_ROWS = SUBLANES
PERM_SLOTS = 4


def _params(*sem):
    return pltpu.CompilerParams(dimension_semantics=sem, vmem_limit_bytes=VMEM_LIMIT)


def _prefetch_tile(bi, i, last, nb, ni):
    nxt = i + last.astype(jnp.int32)
    wrap = nxt == ni
    return jnp.minimum(bi + wrap.astype(jnp.int32), nb - 1), jnp.where(wrap, 0, nxt)


def _rms(x, g):
    ms = jnp.mean(x * x, axis=-1, keepdims=True)
    return (x * lax.rsqrt(ms + NORM_EPS)) * g


def _dot(a, b):
    return jnp.dot(a, b, preferred_element_type=F32)


def _dot_nt(a, b):
    return lax.dot_general(a, b, (((1,), (1,)), ((), ())), preferred_element_type=F32)


def _rope_tables_kernel(pos_ref, fm_ref, fd_ref, cm_ref, sm_ref, tk_ref, cd_ref, sa_ref, sb_ref):
    pos = pos_ref[...]
    lane = lax.broadcasted_iota(jnp.int32, (1, LANES), 1)
    am = pos * fm_ref[...]
    cm = jnp.cos(am)
    sm = jnp.sin(am)
    cm_ref[...] = cm
    sm_ref[...] = sm
    tk_ref[...] = jnp.where(lane < MLA_ROPE, cm, sm)
    ad = pos * fd_ref[...]
    cd = jnp.cos(ad)
    sd = jnp.sin(ad)
    half = DIL_ROT // 2
    cd_ref[...] = jnp.where(lane < DIL_ROT, cd, 1.0)
    sa_ref[...] = jnp.where(lane < half, 0.0, jnp.where(lane < DIL_ROT, sd, 0.0))
    sb_ref[...] = jnp.where(lane < half, -sd, 0.0)


def _rope_tables(positions):
    m = positions.size
    pos = positions.reshape(m, 1).astype(F32)
    inv_m = ROPE_THETA ** (-jnp.arange(0, MLA_ROPE, 2, dtype=F32) / MLA_ROPE)
    inv_d = ROPE_THETA ** (-jnp.arange(0, DIL_ROT, 2, dtype=F32) / DIL_ROT)
    fm = jnp.tile(inv_m, LANES // inv_m.size).reshape(1, LANES)
    fd = jnp.tile(inv_d, LANES // inv_d.size).reshape(1, LANES)
    tm = min(m, 2048)
    row = pl.BlockSpec((tm, LANES), lambda i: (i, 0))
    vec = pl.BlockSpec((1, LANES), lambda i: (0, 0))
    tab = jax.ShapeDtypeStruct((m, LANES), F32)
    return pl.pallas_call(
        _rope_tables_kernel,
        grid=(m // tm,),
        in_specs=[pl.BlockSpec((tm, 1), lambda i: (i, 0)), vec, vec],
        out_specs=[row] * 6,
        out_shape=[tab] * 6,
        compiler_params=_params("parallel"),
        name="rope_tables",
    )(pos, fm, fd)


def _mla_down_kernel(x_ref, g_ref, w_ref, qn_ref, kvn_ref, tk_ref, ql_ref, ckv_ref, kpe_ref):
    half = x_ref.shape[0] // 2
    for rows in (slice(0, half), slice(half, 2 * half)):
        h = _rms(x_ref[rows, :], g_ref[...]).astype(BF16)
        acc = _dot(h, w_ref[...])
        ql_ref[rows, :] = _rms(acc[:, :MLA_Q_RANK], qn_ref[...]).astype(BF16)
        ckv_ref[rows, :] = _rms(acc[:, MLA_Q_RANK:MLA_Q_RANK + MLA_KV_RANK], kvn_ref[...]).astype(BF16)
        y = acc[:, MLA_Q_RANK + MLA_KV_RANK:] * tk_ref[rows, :]
        z = y + pltpu.roll(y, MLA_ROPE, 1)
        lane = lax.broadcasted_iota(jnp.int32, z.shape, 1)
        kpe_ref[rows, :LANES] = jnp.where(lane < MLA_ROPE, z, 0.0).astype(BF16)
        kpe_ref[rows, LANES:] = jnp.where(lane < MLA_ROPE, 0.0, z).astype(BF16)


def _mla_down(x2, gain, w1, q_norm, kv_norm, tk):
    m = x2.shape[0]
    tm = min(m, TM_MLA_PROJ)
    n1 = w1.shape[1]
    return pl.pallas_call(
        _mla_down_kernel,
        grid=(m // tm,),
        in_specs=[
            pl.BlockSpec((tm, D_MODEL), lambda i: (i, 0)),
            pl.BlockSpec((1, D_MODEL), lambda i: (0, 0)),
            pl.BlockSpec((D_MODEL, n1), lambda i: (0, 0)),
            pl.BlockSpec((1, MLA_Q_RANK), lambda i: (0, 0)),
            pl.BlockSpec((1, MLA_KV_RANK), lambda i: (0, 0)),
            pl.BlockSpec((tm, LANES), lambda i: (i, 0)),
        ],
        out_specs=[
            pl.BlockSpec((tm, MLA_Q_RANK), lambda i: (i, 0)),
            pl.BlockSpec((tm, MLA_KV_RANK), lambda i: (i, 0)),
            pl.BlockSpec((tm, 2 * LANES), lambda i: (i, 0)),
        ],
        out_shape=[
            jax.ShapeDtypeStruct((m, MLA_Q_RANK), BF16),
            jax.ShapeDtypeStruct((m, MLA_KV_RANK), BF16),
            jax.ShapeDtypeStruct((m, 2 * LANES), BF16),
        ],
        compiler_params=_params("parallel"),
        name="mla_down",
    )(x2, gain, w1, q_norm, kv_norm, tk)


MLA_PAIRS = MLA_HEADS // 2
Q_PAIR_IN = 2 * MLA_NOPE + 2 * LANES
Q_PAIR_OUT = 2 * MLA_NOPE + LANES


def _mla_q_up_kernel(a_ref, w_ref, c_ref, s_ref, o_ref):
    a = a_ref[...]
    c = c_ref[...]
    s = s_ref[...]
    for p in range(MLA_PAIRS):
        acc = _dot(a, w_ref[:, p * Q_PAIR_IN:(p + 1) * Q_PAIR_IN])
        o_ref[:, p * Q_PAIR_OUT:p * Q_PAIR_OUT + 2 * MLA_NOPE] = acc[:, :2 * MLA_NOPE].astype(BF16)
        pe = acc[:, 2 * MLA_NOPE:2 * MLA_NOPE + LANES] * c + acc[:, 2 * MLA_NOPE + LANES:] * s
        o_ref[:, p * Q_PAIR_OUT + 2 * MLA_NOPE:(p + 1) * Q_PAIR_OUT] = pe.astype(BF16)


def _mla_q_up(q_lat, wq, cm, sm):
    m = q_lat.shape[0]
    tm = min(m, TM_MLA_PROJ)
    n_out = MLA_PAIRS * Q_PAIR_OUT
    return pl.pallas_call(
        _mla_q_up_kernel,
        grid=(m // tm,),
        in_specs=[
            pl.BlockSpec((tm, MLA_Q_RANK), lambda i: (i, 0)),
            pl.BlockSpec(wq.shape, lambda i: (0, 0)),
            pl.BlockSpec((tm, LANES), lambda i: (i, 0)),
            pl.BlockSpec((tm, LANES), lambda i: (i, 0)),
        ],
        out_specs=pl.BlockSpec((tm, n_out), lambda i: (i, 0)),
        out_shape=jax.ShapeDtypeStruct((m, n_out), BF16),
        compiler_params=_params("parallel"),
        name="mla_q_up",
    )(q_lat, wq, cm, sm)


def _matmul_bf16_kernel(a_ref, w_ref, o_ref):
    o_ref[...] = _dot(a_ref[...], w_ref[...]).astype(BF16)


def _mla_kv_up(ckv, wkv):
    m = ckv.shape[0]
    tm = min(m, TM_MLA_PROJ)
    n = wkv.shape[1]
    return pl.pallas_call(
        _matmul_bf16_kernel,
        grid=(m // tm,),
        in_specs=[
            pl.BlockSpec((tm, MLA_KV_RANK), lambda i: (i, 0)),
            pl.BlockSpec(wkv.shape, lambda i: (0, 0)),
        ],
        out_specs=pl.BlockSpec((tm, n), lambda i: (i, 0)),
        out_shape=jax.ShapeDtypeStruct((m, n), BF16),
        compiler_params=_params("parallel"),
        name="mla_kv_up",
    )(ckv, wkv)


def _mla_attn_kernel(q_ref, kn_ref, kpe_ref, v_ref, o_ref, m_ref, l_ref, acc_ref, *, tq, scale):
    g = pl.program_id(2)
    scale_log2e = scale * LOG2_E
    row = lax.broadcasted_iota(jnp.int32, (tq, tq), 0)
    col = lax.broadcasted_iota(jnp.int32, (tq, tq), 1)
    causal = col <= row
    heads = (0, 1)
    for sub in range(2):
        qi = 2 * g + sub
        rows = slice(sub * tq, (sub + 1) * tq)
        q_pe = q_ref[rows, 2 * MLA_NOPE:]
        qs = [jnp.concatenate([q_ref[rows, hh * LANES:(hh + 1) * LANES], q_pe], axis=1) for hh in heads]
        m_ref[...] = jnp.full(m_ref.shape, -jnp.inf, F32)
        l_ref[...] = jnp.zeros(l_ref.shape, F32)
        acc_ref[...] = jnp.zeros(acc_ref.shape, F32)

        def step(hh, kb, masked, nkb=1):
            lanes = slice(hh * LANES, (hh + 1) * LANES)
            tk = nkb * tq
            ks = pl.multiple_of(kb * tq, tq)
            k = jnp.concatenate([kn_ref[pl.ds(ks, tk), lanes], kpe_ref[pl.ds(ks, tk), lanes]], axis=1)
            s = _dot_nt(qs[hh], k)
            if masked:
                s = jnp.where(causal, s, -jnp.inf)
            m_prev = m_ref[hh]
            m_next = jnp.maximum(m_prev, jnp.max(s, axis=1, keepdims=True))
            m_wide = jnp.concatenate([m_next] * (tk // LANES), axis=1)
            p = jnp.exp2((s - m_wide) * scale_log2e)
            alpha = jnp.exp2((m_prev - m_next) * scale_log2e)
            l_ref[hh] = alpha * l_ref[hh] + jnp.sum(p, axis=1, keepdims=True)
            m_ref[hh] = m_next
            acc_ref[hh] = alpha * acc_ref[hh] + _dot(p.astype(BF16), v_ref[pl.ds(ks, tk), lanes])

        def trip(k2, carry):
            for hh in heads:
                step(hh, 2 * k2, False, nkb=2)
            return carry

        lax.fori_loop(0, g, trip, 0)
        if sub == 1:
            for hh in heads:
                step(hh, qi - 1, False)
        for hh in heads:
            step(hh, qi, True)
        for hh in heads:
            o_ref[rows, hh * LANES:(hh + 1) * LANES] = (acc_ref[hh] / l_ref[hh]).astype(BF16)


def _mla_attention(q, kv, kpe, b, s):
    tq = min(s // 2, TQ_MLA)
    scale = (MLA_NOPE + MLA_ROPE) ** -0.5
    kern = functools.partial(_mla_attn_kernel, tq=tq, scale=scale)
    pair_w = 2 * LANES
    return pl.pallas_call(
        kern,
        grid=(b, MLA_PAIRS, s // (2 * tq)),
        in_specs=[
            pl.BlockSpec((None, 2 * tq, Q_PAIR_OUT), lambda bi, p, g: (bi, g, p)),
            pl.BlockSpec((None, s, pair_w), lambda bi, p, g: (bi, 0, p)),
            pl.BlockSpec((None, s, pair_w), lambda bi, p, g: (bi, 0, 0)),
            pl.BlockSpec((None, s, pair_w), lambda bi, p, g: (bi, 0, MLA_PAIRS + p)),
        ],
        out_specs=pl.BlockSpec((None, 2 * tq, pair_w), lambda bi, p, g: (bi, g, p)),
        out_shape=jax.ShapeDtypeStruct((b, s, MLA_HEADS * MLA_V), BF16),
        scratch_shapes=[
            pltpu.VMEM((2, tq, LANES), F32),
            pltpu.VMEM((2, tq, LANES), F32),
            pltpu.VMEM((2, tq, LANES), F32),
        ],
        compiler_params=_params("parallel", "parallel", "arbitrary"),
        name="mla_attention",
    )(q, kv, kpe, kv)


def _matmul_resid_kernel(a_ref, w_ref, r_ref, o_ref):
    o_ref[...] = r_ref[...] + _dot(a_ref[...], w_ref[...])


def _matmul_resid(a, w, layer, resid):
    m, k = a.shape
    n = w.shape[2]
    tm = min(m, TM_SMALL)
    return pl.pallas_call(
        _matmul_resid_kernel,
        grid=(m // tm,),
        in_specs=[
            pl.BlockSpec((tm, k), lambda i: (i, 0)),
            pl.BlockSpec((None, k, n), lambda i: (layer, 0, 0)),
            pl.BlockSpec((tm, n), lambda i: (i, 0)),
        ],
        out_specs=pl.BlockSpec((tm, n), lambda i: (i, 0)),
        out_shape=jax.ShapeDtypeStruct((m, n), F32),
        compiler_params=_params("parallel"),
        name="out_proj_resid",
    )(a, w, resid)


def _dil_rope(xc, c, sa, sb):
    half = DIL_ROT // 2
    return xc * c + pltpu.roll(xc, half, 1) * sa + pltpu.roll(xc, LANES - half, 1) * sb


def _dil_qkv_kernel(x_ref, g_ref, w_ref, c_ref, sa_ref, sb_ref, o_ref, h_ref, acc_ref, *perm_refs,
                    d, tm, n_rope, nj):
    j = pl.program_id(2)
    rows = tm // d
    if d > 1:
        xs_ref, cs_ref, sas_ref, sbs_ref = perm_refs
    else:
        cs_ref, sas_ref, sbs_ref = c_ref, sa_ref, sb_ref

    def norm_input():
        if d == 1:
            h_ref[...] = _rms(x_ref[...], g_ref[...]).astype(BF16)
            return
        x = x_ref[...]
        rinv = lax.rsqrt(jnp.mean(x * x, axis=-1, keepdims=True) + NORM_EPS)
        for t in range(x_ref.shape[1] // LANES):
            lanes = slice(t * LANES, (t + 1) * LANES)
            slot = t % xs_ref.shape[0]
            xs_ref[slot] = (x_ref[:, lanes] * rinv) * g_ref[:, lanes]
            for r in range(d):
                h_ref[r * rows:(r + 1) * rows, lanes] = xs_ref[slot, pl.ds(r, rows, stride=d), :].astype(BF16)
        for r in range(d):
            sl = pl.ds(r, rows, stride=d)
            dst = slice(r * rows, (r + 1) * rows)
            cs_ref[dst, :] = c_ref[sl, :]
            sas_ref[dst, :] = sa_ref[sl, :]
            sbs_ref[dst, :] = sb_ref[sl, :]

    def matmul():
        acc_ref[...] = _dot(h_ref[...], w_ref[...])

    def emit(rope):
        for r in range(d):
            src = slice(r * rows, (r + 1) * rows)
            if rope:
                c, sa, sb = cs_ref[src, :], sas_ref[src, :], sbs_ref[src, :]
            for t in range(acc_ref.shape[1] // LANES):
                a = acc_ref[src, t * LANES:(t + 1) * LANES]
                if rope:
                    a = _dil_rope(a, c, sa, sb)
                o_ref[r, :, t * LANES:(t + 1) * LANES] = a.astype(BF16)

    @pl.when(j == 0)
    def _():
        norm_input()
        matmul()

    @pl.when(jnp.logical_and(j >= 1, j <= n_rope))
    def _():
        emit(True)
        matmul()

    @pl.when(jnp.logical_and(j > n_rope, j < nj))
    def _():
        emit(False)
        matmul()

    @pl.when(j == nj)
    def _():
        emit(False)


def _dil_qkv(x, gain, w_in, layer, g, cd, sa, sb, d):
    b, s, _ = x.shape
    n = 3 * DIL_HEADS * DIL_HEAD_DIM
    tm = min(s, TM_PROJ)
    tn = TN_PROJ
    col0 = g * (n // tn)
    n_rope = 2 * DIL_HEADS * DIL_HEAD_DIM // tn
    nj = n // tn
    assert n_rope < nj
    kern = functools.partial(_dil_qkv_kernel, d=d, tm=tm, n_rope=n_rope, nj=nj)
    ni = s // tm

    def tile(bi, i, j):
        return _prefetch_tile(bi, i, j == nj, b, ni)

    tab = pl.BlockSpec((None, tm, LANES), lambda bi, i, j: (*tile(bi, i, j), 0))
    perm_scratch = []
    if d > 1:
        perm_scratch = [pltpu.VMEM((PERM_SLOTS, tm, LANES), F32)] + [pltpu.VMEM((tm, LANES), F32)] * 3
    return pl.pallas_call(
        kern,
        grid=(b, s // tm, nj + 1),
        in_specs=[
            pl.BlockSpec((None, tm, D_MODEL), lambda bi, i, j: (*tile(bi, i, j), 0)),
            pl.BlockSpec((1, D_MODEL), lambda bi, i, j: (0, 0)),
            pl.BlockSpec((None, D_MODEL, tn), lambda bi, i, j: (layer, 0, col0 + jnp.where(j == nj, 0, j))),
            tab, tab, tab,
        ],
        out_specs=pl.BlockSpec((None, d, tm // d, tn), lambda bi, i, j: (bi, 0, i, jnp.maximum(j - 1, 0))),
        out_shape=jax.ShapeDtypeStruct((b, d, s // d, n), BF16),
        scratch_shapes=[pltpu.VMEM((tm, D_MODEL), BF16), pltpu.VMEM((tm, tn), F32)] + perm_scratch,
        compiler_params=_params("parallel", "parallel", "arbitrary"),
        name=f"dil_qkv_d{d}",
    )(x, gain, w_in, cd, sa, sb)


def _dil_attn_kernel(q_ref, kc_ref, kp_ref, vc_ref, vp_ref, o_ref, lse_ref, s_ref, p_ref, *, scale, nblk):
    jb = pl.program_id(2)
    qb = DIL_BLOCK
    qi = lax.broadcasted_iota(jnp.int32, (qb, 2 * qb), 0)
    ki = lax.broadcasted_iota(jnp.int32, (qb, 2 * qb), 1)
    lane = lax.broadcasted_iota(jnp.int32, (qb, LANES), 1)

    def keys(cur_ref, prev_ref, blk, sl):
        own = cur_ref[blk * qb:(blk + 1) * qb, sl]
        before = prev_ref[:, sl] if blk == 0 else cur_ref[(blk - 1) * qb:blk * qb, sl]
        return jnp.concatenate([before, own], axis=0)

    for blk in range(nblk):
        for h in range(DIL_HEADS):
            sl = slice(h * DIL_HEAD_DIM, (h + 1) * DIL_HEAD_DIM)
            s_ref[blk, h] = _dot_nt(q_ref[blk * qb:(blk + 1) * qb, sl], keys(kc_ref, kp_ref, blk, sl))
    for blk in range(nblk):
        first_key = jnp.where(jb > 0, qi, qb) if blk == 0 else qi
        band = jnp.logical_and(ki >= first_key, ki <= qi + qb)
        s = jnp.where(band[None], s_ref[blk], -jnp.inf)
        m = jnp.max(s, axis=-1, keepdims=True)
        e = jnp.exp2((s - m) * (scale * LOG2_E))
        l = jnp.sum(e, axis=-1, keepdims=True)
        p_ref[blk] = e.astype(BF16)
        inv = 1.0 / l
        lse = scale * m + jnp.log(l)
        lse_all = jnp.zeros((qb, LANES), F32)
        for h in range(DIL_HEADS):
            lse_all = jnp.where(lane == h, lse[h], lse_all)
        lse_ref[blk * qb:(blk + 1) * qb, :] = lse_all
        for h in range(DIL_HEADS):
            sl = slice(h * DIL_HEAD_DIM, (h + 1) * DIL_HEAD_DIM)
            pv = _dot(p_ref[blk, h], keys(vc_ref, vp_ref, blk, sl))
            o_ref[blk * qb:(blk + 1) * qb, sl] = pv * inv[h]


def _dil_attention(qkv_g, d):
    b, _, l, _ = qkv_g.shape
    hw = DIL_HEADS * DIL_HEAD_DIM
    qb = DIL_BLOCK
    nblk = min(DIL_BLOCKS_PER_STEP, l // qb)
    rows = nblk * qb
    kern = functools.partial(_dil_attn_kernel, scale=DIL_HEAD_DIM ** -0.5, nblk=nblk)

    def cur(c):
        return pl.BlockSpec((None, None, rows, hw), lambda bi, r, jb: (bi, r, jb, c))

    def prev(c):
        return pl.BlockSpec((None, None, qb, hw), lambda bi, r, jb: (bi, r, jnp.maximum(jb * nblk - 1, 0), c))

    return pl.pallas_call(
        kern,
        grid=(b, d, l // rows),
        in_specs=[cur(0), cur(1), prev(1), cur(2), prev(2)],
        out_specs=[
            pl.BlockSpec((None, None, rows, hw), lambda bi, r, jb: (bi, r, jb, 0)),
            pl.BlockSpec((None, None, rows, LANES), lambda bi, r, jb: (bi, r, jb, 0)),
        ],
        out_shape=[
            jax.ShapeDtypeStruct((b, d, l, hw), F32),
            jax.ShapeDtypeStruct((b, d, l, LANES), F32),
        ],
        scratch_shapes=[
            pltpu.VMEM((nblk, DIL_HEADS, qb, 2 * qb), F32),
            pltpu.VMEM((nblk, DIL_HEADS, qb, 2 * qb), BF16),
        ],
        compiler_params=_params("parallel", "parallel", "arbitrary"),
        name=f"dil_attention_d{d}",
    )(qkv_g, qkv_g, qkv_g, qkv_g, qkv_g)


def _dil_merge_kernel(o0_ref, l0_ref, o1_ref, l1_ref, o2_ref, l2_ref, w_ref, x_ref, out_ref,
                      n1_ref, n2_ref, nl1_ref, nl2_ref, a_ref, *, tm, n):
    i = pl.program_id(1)
    d1 = DIL_GROUPS[1][1]
    d2 = DIL_GROUPS[2][1]

    def merge():
        for d, o_ref, l_ref, n_ref, nl_ref in ((d1, o1_ref, l1_ref, n1_ref, nl1_ref),
                                               (d2, o2_ref, l2_ref, n2_ref, nl2_ref)):
            for r in range(d):
                rows = pl.ds(r, tm // d, stride=d)
                nl_ref[rows, :] = l_ref[r]
                for h in range(DIL_HEADS):
                    n_ref[h, rows, :] = o_ref[r, :, h * DIL_HEAD_DIM:(h + 1) * DIL_HEAD_DIM]
        for h in range(DIL_HEADS):
            sl = slice(h * DIL_HEAD_DIM, (h + 1) * DIL_HEAD_DIM)
            a0 = l0_ref[:, h:h + 1]
            a1 = nl1_ref[:, h:h + 1]
            a2 = nl2_ref[:, h:h + 1]
            mx = jnp.maximum(jnp.maximum(a0, a1), a2)
            w0 = jnp.exp(a0 - mx)
            w1 = jnp.exp(a1 - mx)
            w2 = jnp.exp(a2 - mx)
            den = w0 + w1 + w2
            o = (w0 / den) * o0_ref[:, sl] + (w1 / den) * n1_ref[h] + (w2 / den) * n2_ref[h]
            a_ref[:, sl] = o.astype(BF16)

    def project(a):
        out_ref[...] = x_ref[...] + _dot(a, w_ref[...])

    @pl.when(i == 0)
    def _():
        merge()

    @pl.when(jnp.logical_and(i > 0, i < n))
    def _():
        a = a_ref[...]
        merge()
        project(a)

    @pl.when(i == n)
    def _():
        project(a_ref[...])


def _dil_merge_out(outs, lses, wo, layer, x):
    b, s, dm = x.shape
    hw = DIL_HEADS * DIL_HEAD_DIM
    tm = min(s, TM_MERGE)
    d1 = DIL_GROUPS[1][1]
    d2 = DIL_GROUPS[2][1]
    n = s // tm
    kern = functools.partial(_dil_merge_kernel, tm=tm, n=n)

    def merged(i):
        return jnp.minimum(i, n - 1)

    def projected(i):
        return jnp.maximum(i - 1, 0)

    def grp(d, w):
        return pl.BlockSpec((None, d, tm // d, w), lambda bi, i: (bi, 0, merged(i), 0))

    return pl.pallas_call(
        kern,
        grid=(b, n + 1),
        in_specs=[
            pl.BlockSpec((None, None, tm, hw), lambda bi, i: (bi, 0, merged(i), 0)),
            pl.BlockSpec((None, None, tm, LANES), lambda bi, i: (bi, 0, merged(i), 0)),
            grp(d1, hw), grp(d1, LANES), grp(d2, hw), grp(d2, LANES),
            pl.BlockSpec((None, hw, dm), lambda bi, i: (layer, 0, 0)),
            pl.BlockSpec((None, tm, dm), lambda bi, i: (bi, projected(i), 0)),
        ],
        out_specs=pl.BlockSpec((None, tm, dm), lambda bi, i: (bi, projected(i), 0)),
        out_shape=jax.ShapeDtypeStruct((b, s, dm), F32),
        scratch_shapes=[
            pltpu.VMEM((DIL_HEADS, tm, DIL_HEAD_DIM), F32), pltpu.VMEM((DIL_HEADS, tm, DIL_HEAD_DIM), F32),
            pltpu.VMEM((tm, LANES), F32), pltpu.VMEM((tm, LANES), F32),
            pltpu.VMEM((tm, hw), BF16),
        ],
        compiler_params=_params("parallel", "arbitrary"),
        name="dil_merge_out",
    )(outs[0], lses[0], outs[1], lses[1], outs[2], lses[2], wo, x)


def _ffn_kernel(x_ref, g_ref, og_ref, wg_ref, wv_ref, cp_ref, wd_ref, o_ref, h_ref, u_ref, prev_ref, *,
                tm, tf, nf, out_norm):
    i = pl.program_id(1)
    t = pl.program_id(2)
    row = lax.broadcasted_iota(jnp.int32, (CARRY_ROWS, 1), 0)

    def up():
        h = h_ref[...]
        u_ref[:, :tf] = _dot(h, wg_ref[...])
        u_ref[:, tf:] = _dot(h, wv_ref[...])

    def conv(u, c):
        prev = prev_ref[c]
        prev_ref[c] = u[tm - CARRY_ROWS:, :]
        p1 = prev[CARRY_ROWS - 1:CARRY_ROWS, :]
        p2 = prev[CARRY_ROWS - 2:CARRY_ROWS - 1, :]
        r1 = pltpu.roll(u, 1, 0)
        r2 = pltpu.roll(u, 2, 0)
        u1 = jnp.concatenate([jnp.where(row == 0, p1, r1[:CARRY_ROWS]), r1[CARRY_ROWS:]], axis=0)
        u2 = jnp.concatenate(
            [jnp.where(row == 0, p2, jnp.where(row == 1, p1, r2[:CARRY_ROWS])), r2[CARRY_ROWS:]], axis=0)
        cp = cp_ref[...]
        acc = cp[CONV_WIDTH:CONV_WIDTH + 1, :] + u2 * cp[0:1, :]
        acc = acc + u1 * cp[1:2, :]
        return acc + u * cp[2:3, :]

    def down(u):
        cv = conv(u, t - 1)
        gate = cv[:, :tf]
        val = cv[:, tf:]
        act = (gate * (1.0 / (1.0 + jnp.exp(-gate)))) * val
        o_ref[...] += _dot(act.astype(BF16), wd_ref[...])

    @pl.when(t == 0)
    def _():
        x = x_ref[...]
        h_ref[...] = _rms(x, g_ref[...]).astype(BF16)
        o_ref[...] = x
        up()

    @pl.when(jnp.logical_and(t == 0, i == 0))
    def _():
        prev_ref[...] = jnp.zeros(prev_ref.shape, F32)

    @pl.when(jnp.logical_and(t > 0, t < nf))
    def _():
        u = u_ref[...]
        up()
        down(u)

    @pl.when(t == nf)
    def _():
        down(u_ref[...])
        if out_norm:
            o_ref[...] = _rms(o_ref[...], og_ref[...])


def _ffn_chunk_order(a, tf):
    lead = a.shape[:-1]
    return a.reshape(*lead, 2, FFN_HIDDEN // tf, tf).swapaxes(-3, -2).reshape(*lead, 2 * FFN_HIDDEN)


def _ffn_weights(w_up, conv_w, conv_b, w_down):
    tf = TF_FFN
    pad = jnp.zeros((conv_w.shape[0], SUBLANES - CONV_WIDTH - 1, conv_w.shape[2]), F32)
    conv_p = jnp.concatenate([conv_w, conv_b[:, None, :], pad], axis=1)
    return w_up.astype(BF16), _ffn_chunk_order(conv_p, tf), w_down.astype(BF16)


def _ffn(x, gain, out_gain, w_up, conv_p, w_down, layer, out_norm):
    b, s, _ = x.shape
    tm = min(s, TM_FFN)
    tf = TF_FFN
    nf = FFN_HIDDEN // tf
    kern = functools.partial(_ffn_kernel, tm=tm, tf=tf, nf=nf, out_norm=out_norm)

    def up_chunk(t):
        return jnp.where(t == nf, 0, t)

    def tile(bi, i, t):
        return _prefetch_tile(bi, i, t == nf, b, s // tm)

    def down_chunk(t):
        return jnp.maximum(t - 1, 0)

    return pl.pallas_call(
        kern,
        grid=(b, s // tm, nf + 1),
        in_specs=[
            pl.BlockSpec((None, tm, D_MODEL), lambda bi, i, t: (*tile(bi, i, t), 0)),
            pl.BlockSpec((1, D_MODEL), lambda bi, i, t: (0, 0)),
            pl.BlockSpec((1, D_MODEL), lambda bi, i, t: (0, 0)),
            pl.BlockSpec((None, D_MODEL, tf), lambda bi, i, t: (layer, 0, up_chunk(t))),
            pl.BlockSpec((None, D_MODEL, tf), lambda bi, i, t: (layer, 0, nf + up_chunk(t))),
            pl.BlockSpec((None, SUBLANES, 2 * tf), lambda bi, i, t: (layer, 0, down_chunk(t))),
            pl.BlockSpec((None, tf, D_MODEL), lambda bi, i, t: (layer, down_chunk(t), 0)),
        ],
        out_specs=pl.BlockSpec((None, tm, D_MODEL), lambda bi, i, t: (bi, i, 0)),
        out_shape=jax.ShapeDtypeStruct(x.shape, F32),
        scratch_shapes=[
            pltpu.VMEM((tm, D_MODEL), BF16),
            pltpu.VMEM((tm, 2 * tf), F32),
            pltpu.VMEM((nf, CARRY_ROWS, 2 * tf), F32),
        ],
        compiler_params=_params("arbitrary", "arbitrary", "arbitrary"),
        name="conv_ffn",
    )(x, gain, out_gain, w_up, w_up, conv_p, w_down)


def _rotate_half_cols(w):
    half = w.shape[-1] // 2
    return jnp.concatenate([-w[..., half:], w[..., :half]], axis=-1)


def _mla_weights(wq_a, wq_b, wkv_a, wkv_b):
    w_pe = wkv_a[:, MLA_KV_RANK:]
    w1 = jnp.concatenate([wq_a, wkv_a[:, :MLA_KV_RANK], w_pe, _rotate_half_cols(w_pe)], axis=1)
    qb = wq_b.reshape(MLA_Q_RANK, MLA_HEADS, MLA_NOPE + MLA_ROPE)
    nope = qb[:, :, :MLA_NOPE].reshape(MLA_Q_RANK, MLA_PAIRS, 2 * MLA_NOPE)
    pe = qb[:, :, MLA_NOPE:]
    rot = _rotate_half_cols(pe).reshape(MLA_Q_RANK, MLA_PAIRS, LANES)
    pe = pe.reshape(MLA_Q_RANK, MLA_PAIRS, LANES)
    wq = jnp.concatenate([nope, pe, rot], axis=2).reshape(MLA_Q_RANK, MLA_PAIRS * Q_PAIR_IN)
    kvb = wkv_b.reshape(MLA_KV_RANK, MLA_HEADS, MLA_NOPE + MLA_V)
    wkv = jnp.concatenate([kvb[:, :, :MLA_NOPE].reshape(MLA_KV_RANK, -1),
                           kvb[:, :, MLA_NOPE:].reshape(MLA_KV_RANK, -1)], axis=1)
    return w1.astype(BF16), wq.astype(BF16), wkv.astype(BF16)


def _mla_layer(x, gain, tabs, wq_a, q_norm, wq_b, wkv_a, kv_norm, wkv_b, wo, layer):
    b, s, dm = x.shape
    cm, sm, tk = tabs
    w1, wq, wkv = _mla_weights(wq_a, wq_b, wkv_a, wkv_b)
    x2 = x.reshape(b * s, dm)
    q_lat, ckv, kpe = _mla_down(x2, gain.reshape(1, dm), w1, q_norm.reshape(1, -1), kv_norm.reshape(1, -1), tk)
    q = _mla_q_up(q_lat, wq, cm, sm)
    kv = _mla_kv_up(ckv, wkv)
    o = _mla_attention(q.reshape(b, s, -1), kv.reshape(b, s, -1), kpe.reshape(b, s, -1), b, s)
    return _matmul_resid(o.reshape(b * s, -1), wo, layer, x2).reshape(b, s, dm)


def _dil_layer(x, gain, tabs, w_in, wo, layer):
    b, s, dm = x.shape
    cd, sa, sb = (t.reshape(b, s, LANES) for t in tabs)
    outs, lses = [], []
    for g, (_, d) in enumerate(DIL_GROUPS):
        qkv_g = _dil_qkv(x, gain.reshape(1, dm), w_in, layer, g, cd, sa, sb, d)
        o, lse = _dil_attention(qkv_g, d)
        outs.append(o)
        lses.append(lse)
    return _dil_merge_out(outs, lses, wo, layer, x)


def kernel(x, positions, attn_norm, ffn_norm, final_norm, mla_wq_a, mla_q_norm, mla_wq_b, mla_wkv_a,
           mla_kv_norm, mla_wkv_b, mla_wo, dil_w_in, dil_wo, ffn_w_up, ffn_conv_w, ffn_conv_b, ffn_w_down):
    b, s, dm = x.shape
    cm, sm, tk, cd, sa, sb = _rope_tables(positions)
    mla_wo_b = mla_wo.astype(BF16)
    dil_w_in_b = dil_w_in.astype(BF16)
    dil_wo_b = dil_wo.astype(BF16)
    ffn_up_b, ffn_conv_p, ffn_down_b = _ffn_weights(ffn_w_up, ffn_conv_w, ffn_conv_b, ffn_w_down)
    for i in range(DEPTH):
        j = i // N_MIXERS
        if i % N_MIXERS == 0:
            x = _mla_layer(x, attn_norm[i], (cm, sm, tk), mla_wq_a[j], mla_q_norm[j], mla_wq_b[j],
                           mla_wkv_a[j], mla_kv_norm[j], mla_wkv_b[j], mla_wo_b, j)
        else:
            x = _dil_layer(x, attn_norm[i], (cd, sa, sb), dil_w_in_b, dil_wo_b, j)
        x = _ffn(x, ffn_norm[i].reshape(1, dm), final_norm.reshape(1, dm), ffn_up_b, ffn_conv_p, ffn_down_b,
                 i, i == DEPTH - 1)
    return x
```

```python
import functools

import jax
import jax.numpy as jnp
from jax import lax
from jax.experimental import pallas as pl
from jax.experimental.pallas import tpu as pltpu

F32 = jnp.float32
BF16 = jnp.bfloat16

D_MODEL = 2048
DEPTH = 4
N_MIXERS = 2
ROPE_THETA = 500000.0
NORM_EPS = 1e-6
LOG2_E = 1.4426950408889634

MLA_HEADS = 16
MLA_Q_RANK = 512
MLA_KV_RANK = 512
MLA_NOPE = 128
MLA_ROPE = 64
MLA_V = 128

DIL_GROUPS = ((128, 1), (512, 4), (2048, 16))
DIL_HEADS = 16
DIL_HEAD_DIM = 128
DIL_ROT = DIL_HEAD_DIM // 4
DIL_BLOCK = 128
DIL_BLOCKS_PER_STEP = 4

FFN_HIDDEN = 5632
CONV_WIDTH = 3

LANES = 128
SUBLANES = 8
VMEM_BYTES_V7X = 64 * 1024 * 1024
VMEM_LIMIT = VMEM_BYTES_V7X - 8 * 1024 * 1024

TM_PROJ = 1024
TN_PROJ = 1024
TM_SMALL = 512
TM_MLA_PROJ = 1024
TM_MERGE = 256
TQ_MLA = 512
TM_FFN = 512
TF_FFN = 512
CARRY_ROWS = SUBLANES
PERM_SLOTS = 4
CAST_ROWS = 16


def _params(*sem):
    return pltpu.CompilerParams(dimension_semantics=sem, vmem_limit_bytes=VMEM_LIMIT)


def _prefetch_tile(bi, i, last, nb, ni):
    nxt = i + last.astype(jnp.int32)
    wrap = nxt == ni
    return jnp.minimum(bi + wrap.astype(jnp.int32), nb - 1), jnp.where(wrap, 0, nxt)


def _rms(x, g):
    ms = jnp.mean(x * x, axis=-1, keepdims=True)
    return (x * lax.rsqrt(ms + NORM_EPS)) * g


def _dot(a, b):
    return jnp.dot(a, b, preferred_element_type=F32)


def _dot_nt(a, b):
    return lax.dot_general(a, b, (((1,), (1,)), ((), ())), preferred_element_type=F32)


def _rope_tables_kernel(pos_ref, fm_ref, fd_ref, cm_ref, sm_ref, tk_ref, cd_ref, sa_ref, sb_ref):
    pos = pos_ref[...]
    lane = lax.broadcasted_iota(jnp.int32, (1, LANES), 1)
    am = pos * fm_ref[...]
    cm = jnp.cos(am)
    sm = jnp.sin(am)
    cm_ref[...] = cm
    sm_ref[...] = sm
    tk_ref[...] = jnp.where(lane < MLA_ROPE, cm, sm)
    ad = pos * fd_ref[...]
    cd = jnp.cos(ad)
    sd = jnp.sin(ad)
    half = DIL_ROT // 2
    cd_ref[...] = jnp.where(lane < DIL_ROT, cd, 1.0)
    sa_ref[...] = jnp.where(lane < half, 0.0, jnp.where(lane < DIL_ROT, sd, 0.0))
    sb_ref[...] = jnp.where(lane < half, -sd, 0.0)


def _rope_tables(positions):
    m = positions.size
    pos = positions.reshape(m, 1).astype(F32)
    inv_m = ROPE_THETA ** (-jnp.arange(0, MLA_ROPE, 2, dtype=F32) / MLA_ROPE)
    inv_d = ROPE_THETA ** (-jnp.arange(0, DIL_ROT, 2, dtype=F32) / DIL_ROT)
    fm = jnp.tile(inv_m, LANES // inv_m.size).reshape(1, LANES)
    fd = jnp.tile(inv_d, LANES // inv_d.size).reshape(1, LANES)
    tm = min(m, 2048)
    row = pl.BlockSpec((tm, LANES), lambda i: (i, 0))
    vec = pl.BlockSpec((1, LANES), lambda i: (0, 0))
    tab = jax.ShapeDtypeStruct((m, LANES), F32)
    return pl.pallas_call(
        _rope_tables_kernel,
        grid=(m // tm,),
        in_specs=[pl.BlockSpec((tm, 1), lambda i: (i, 0)), vec, vec],
        out_specs=[row] * 6,
        out_shape=[tab] * 6,
        compiler_params=_params("parallel"),
        name="rope_tables",
    )(pos, fm, fd)


def _mla_down_kernel(x_ref, g_ref, w_ref, qn_ref, kvn_ref, tk_ref, ql_ref, ckv_ref, kpe_ref):
    half = x_ref.shape[0] // 2
    for rows in (slice(0, half), slice(half, 2 * half)):
        h = _rms(x_ref[rows, :], g_ref[...]).astype(BF16)
        acc = _dot(h, w_ref[...])
        ql_ref[rows, :] = _rms(acc[:, :MLA_Q_RANK], qn_ref[...]).astype(BF16)
        ckv_ref[rows, :] = _rms(acc[:, MLA_Q_RANK:MLA_Q_RANK + MLA_KV_RANK], kvn_ref[...]).astype(BF16)
        y = acc[:, MLA_Q_RANK + MLA_KV_RANK:] * tk_ref[rows, :]
        z = y + pltpu.roll(y, MLA_ROPE, 1)
        lane = lax.broadcasted_iota(jnp.int32, z.shape, 1)
        kpe_ref[rows, :LANES] = jnp.where(lane < MLA_ROPE, z, 0.0).astype(BF16)
        kpe_ref[rows, LANES:] = jnp.where(lane < MLA_ROPE, 0.0, z).astype(BF16)


def _mla_down(x2, gain, w1, q_norm, kv_norm, tk):
    m = x2.shape[0]
    tm = min(m, TM_MLA_PROJ)
    n1 = w1.shape[1]
    return pl.pallas_call(
        _mla_down_kernel,
        grid=(m // tm,),
        in_specs=[
            pl.BlockSpec((tm, D_MODEL), lambda i: (i, 0)),
            pl.BlockSpec((1, D_MODEL), lambda i: (0, 0)),
            pl.BlockSpec((D_MODEL, n1), lambda i: (0, 0)),
            pl.BlockSpec((1, MLA_Q_RANK), lambda i: (0, 0)),
            pl.BlockSpec((1, MLA_KV_RANK), lambda i: (0, 0)),
            pl.BlockSpec((tm, LANES), lambda i: (i, 0)),
        ],
        out_specs=[
            pl.BlockSpec((tm, MLA_Q_RANK), lambda i: (i, 0)),
            pl.BlockSpec((tm, MLA_KV_RANK), lambda i: (i, 0)),
            pl.BlockSpec((tm, 2 * LANES), lambda i: (i, 0)),
        ],
        out_shape=[
            jax.ShapeDtypeStruct((m, MLA_Q_RANK), BF16),
            jax.ShapeDtypeStruct((m, MLA_KV_RANK), BF16),
            jax.ShapeDtypeStruct((m, 2 * LANES), BF16),
        ],
        compiler_params=_params("parallel"),
        name="mla_down",
    )(x2, gain, w1, q_norm, kv_norm, tk)


MLA_PAIRS = MLA_HEADS // 2
Q_PAIR_IN = 2 * MLA_NOPE + 2 * LANES
Q_PAIR_OUT = 2 * MLA_NOPE + LANES


def _mla_q_up_kernel(a_ref, w_ref, c_ref, s_ref, o_ref):
    a = a_ref[...]
    c = c_ref[...]
    s = s_ref[...]
    for p in range(MLA_PAIRS):
        acc = _dot(a, w_ref[:, p * Q_PAIR_IN:(p + 1) * Q_PAIR_IN])
        o_ref[:, p * Q_PAIR_OUT:p * Q_PAIR_OUT + 2 * MLA_NOPE] = acc[:, :2 * MLA_NOPE].astype(BF16)
        pe = acc[:, 2 * MLA_NOPE:2 * MLA_NOPE + LANES] * c + acc[:, 2 * MLA_NOPE + LANES:] * s
        o_ref[:, p * Q_PAIR_OUT + 2 * MLA_NOPE:(p + 1) * Q_PAIR_OUT] = pe.astype(BF16)


def _mla_q_up(q_lat, wq, cm, sm):
    m = q_lat.shape[0]
    tm = min(m, TM_MLA_PROJ)
    n_out = MLA_PAIRS * Q_PAIR_OUT
    return pl.pallas_call(
        _mla_q_up_kernel,
        grid=(m // tm,),
        in_specs=[
            pl.BlockSpec((tm, MLA_Q_RANK), lambda i: (i, 0)),
            pl.BlockSpec(wq.shape, lambda i: (0, 0)),
            pl.BlockSpec((tm, LANES), lambda i: (i, 0)),
            pl.BlockSpec((tm, LANES), lambda i: (i, 0)),
        ],
        out_specs=pl.BlockSpec((tm, n_out), lambda i: (i, 0)),
        out_shape=jax.ShapeDtypeStruct((m, n_out), BF16),
        compiler_params=_params("parallel"),
        name="mla_q_up",
    )(q_lat, wq, cm, sm)


def _matmul_bf16_kernel(a_ref, w_ref, o_ref):
    o_ref[...] = _dot(a_ref[...], w_ref[...]).astype(BF16)


def _mla_kv_up(ckv, wkv):
    m = ckv.shape[0]
    tm = min(m, TM_MLA_PROJ)
    n = wkv.shape[1]
    return pl.pallas_call(
        _matmul_bf16_kernel,
        grid=(m // tm,),
        in_specs=[
            pl.BlockSpec((tm, MLA_KV_RANK), lambda i: (i, 0)),
            pl.BlockSpec(wkv.shape, lambda i: (0, 0)),
        ],
        out_specs=pl.BlockSpec((tm, n), lambda i: (i, 0)),
        out_shape=jax.ShapeDtypeStruct((m, n), BF16),
        compiler_params=_params("parallel"),
        name="mla_kv_up",
    )(ckv, wkv)


def _mla_attn_kernel(q_ref, kn_ref, kpe_ref, v_ref, o_ref, m_ref, l_ref, acc_ref, *, tq, scale):
    g = pl.program_id(2)
    scale_log2e = scale * LOG2_E
    row = lax.broadcasted_iota(jnp.int32, (tq, tq), 0)
    col = lax.broadcasted_iota(jnp.int32, (tq, tq), 1)
    causal = col <= row
    heads = (0, 1)
    for sub in range(2):
        qi = 2 * g + sub
        rows = slice(sub * tq, (sub + 1) * tq)
        q_pe = q_ref[rows, 2 * MLA_NOPE:]
        qs = [jnp.concatenate([q_ref[rows, hh * LANES:(hh + 1) * LANES], q_pe], axis=1) for hh in heads]
        m_ref[...] = jnp.full(m_ref.shape, -jnp.inf, F32)
        l_ref[...] = jnp.zeros(l_ref.shape, F32)
        acc_ref[...] = jnp.zeros(acc_ref.shape, F32)

        def step(hh, kb, masked, nkb=1):
            lanes = slice(hh * LANES, (hh + 1) * LANES)
            tk = nkb * tq
            ks = pl.multiple_of(kb * tq, tq)
            k = jnp.concatenate([kn_ref[pl.ds(ks, tk), lanes], kpe_ref[pl.ds(ks, tk), lanes]], axis=1)
            s = _dot_nt(qs[hh], k)
            if masked:
                s = jnp.where(causal, s, -jnp.inf)
            m_prev = m_ref[hh]
            m_next = jnp.maximum(m_prev, jnp.max(s, axis=1, keepdims=True))
            m_wide = jnp.concatenate([m_next] * (tk // LANES), axis=1)
            p = jnp.exp2((s - m_wide) * scale_log2e)
            alpha = jnp.exp2((m_prev - m_next) * scale_log2e)
            l_ref[hh] = alpha * l_ref[hh] + jnp.sum(p, axis=1, keepdims=True)
            m_ref[hh] = m_next
            acc_ref[hh] = alpha * acc_ref[hh] + _dot(p.astype(BF16), v_ref[pl.ds(ks, tk), lanes])

        def trip(k2, carry):
            for hh in heads:
                step(hh, 2 * k2, False, nkb=2)
            return carry

        lax.fori_loop(0, g, trip, 0)
        if sub == 1:
            for hh in heads:
                step(hh, qi - 1, False)
        for hh in heads:
            step(hh, qi, True)
        for hh in heads:
            o_ref[rows, hh * LANES:(hh + 1) * LANES] = (acc_ref[hh] / l_ref[hh]).astype(BF16)


def _mla_attention(q, kv, kpe, b, s):
    tq = min(s // 2, TQ_MLA)
    scale = (MLA_NOPE + MLA_ROPE) ** -0.5
    kern = functools.partial(_mla_attn_kernel, tq=tq, scale=scale)
    pair_w = 2 * LANES
    return pl.pallas_call(
        kern,
        grid=(b, MLA_PAIRS, s // (2 * tq)),
        in_specs=[
            pl.BlockSpec((None, 2 * tq, Q_PAIR_OUT), lambda bi, p, g: (bi, g, p)),
            pl.BlockSpec((None, s, pair_w), lambda bi, p, g: (bi, 0, p)),
            pl.BlockSpec((None, s, pair_w), lambda bi, p, g: (bi, 0, 0)),
            pl.BlockSpec((None, s, pair_w), lambda bi, p, g: (bi, 0, MLA_PAIRS + p)),
        ],
        out_specs=pl.BlockSpec((None, 2 * tq, pair_w), lambda bi, p, g: (bi, g, p)),
        out_shape=jax.ShapeDtypeStruct((b, s, MLA_HEADS * MLA_V), BF16),
        scratch_shapes=[
            pltpu.VMEM((2, tq, LANES), F32),
            pltpu.VMEM((2, tq, LANES), F32),
            pltpu.VMEM((2, tq, LANES), F32),
        ],
        compiler_params=_params("parallel", "parallel", "arbitrary"),
        name="mla_attention",
    )(q, kv, kpe, kv)


def _matmul_resid_kernel(a_ref, w_ref, r_ref, o_ref):
    o_ref[...] = r_ref[...] + _dot(a_ref[...], w_ref[...])


def _matmul_resid(a, w, layer, resid):
    m, k = a.shape
    n = w.shape[2]
    tm = min(m, TM_SMALL)
    return pl.pallas_call(
        _matmul_resid_kernel,
        grid=(m // tm,),
        in_specs=[
            pl.BlockSpec((tm, k), lambda i: (i, 0)),
            pl.BlockSpec((None, k, n), lambda i: (layer, 0, 0)),
            pl.BlockSpec((tm, n), lambda i: (i, 0)),
        ],
        out_specs=pl.BlockSpec((tm, n), lambda i: (i, 0)),
        out_shape=jax.ShapeDtypeStruct((m, n), F32),
        compiler_params=_params("parallel"),
        name="out_proj_resid",
    )(a, w, resid)


def _dil_rope(xc, c, sa, sb):
    half = DIL_ROT // 2
    return xc * c + pltpu.roll(xc, half, 1) * sa + pltpu.roll(xc, LANES - half, 1) * sb


def _dil_qkv_kernel(x_ref, g_ref, w_ref, c_ref, sa_ref, sb_ref, o_ref, h_ref, acc_ref, *perm_refs,
                    d, tm, n_rope, nj):
    j = pl.program_id(2)
    rows = tm // d
    if d > 1:
        xs_ref, cs_ref, sas_ref, sbs_ref = perm_refs
    else:
        cs_ref, sas_ref, sbs_ref = c_ref, sa_ref, sb_ref

    def norm_input():
        if d == 1:
            h_ref[...] = _rms(x_ref[...], g_ref[...]).astype(BF16)
            return
        x = x_ref[...]
        rinv = lax.rsqrt(jnp.mean(x * x, axis=-1, keepdims=True) + NORM_EPS)
        for t in range(x_ref.shape[1] // LANES):
            lanes = slice(t * LANES, (t + 1) * LANES)
            slot = t % xs_ref.shape[0]
            xs_ref[slot] = (x_ref[:, lanes] * rinv) * g_ref[:, lanes]
            for r in range(d):
                h_ref[r * rows:(r + 1) * rows, lanes] = xs_ref[slot, pl.ds(r, rows, stride=d), :].astype(BF16)
        for r in range(d):
            sl = pl.ds(r, rows, stride=d)
            dst = slice(r * rows, (r + 1) * rows)
            cs_ref[dst, :] = c_ref[sl, :]
            sas_ref[dst, :] = sa_ref[sl, :]
            sbs_ref[dst, :] = sb_ref[sl, :]

    def matmul():
        acc_ref[...] = _dot(h_ref[...], w_ref[...])

    def emit(rope):
        for r in range(d):
            src = slice(r * rows, (r + 1) * rows)
            if rope:
                c, sa, sb = cs_ref[src, :], sas_ref[src, :], sbs_ref[src, :]
            for t in range(acc_ref.shape[1] // LANES):
                a = acc_ref[src, t * LANES:(t + 1) * LANES]
                if rope:
                    a = _dil_rope(a, c, sa, sb)
                o_ref[r, :, t * LANES:(t + 1) * LANES] = a.astype(BF16)

    @pl.when(j == 0)
    def _():
        norm_input()
        matmul()

    @pl.when(jnp.logical_and(j >= 1, j <= n_rope))
    def _():
        emit(True)
        matmul()

    @pl.when(jnp.logical_and(j > n_rope, j < nj))
    def _():
        emit(False)
        matmul()

    @pl.when(j == nj)
    def _():
        emit(False)


def _dil_qkv(x, gain, w_in, layer, g, cd, sa, sb, d):
    b, s, _ = x.shape
    n = 3 * DIL_HEADS * DIL_HEAD_DIM
    tm = min(s, TM_PROJ)
    tn = TN_PROJ
    col0 = g * (n // tn)
    n_rope = 2 * DIL_HEADS * DIL_HEAD_DIM // tn
    nj = n // tn
    assert n_rope < nj
    kern = functools.partial(_dil_qkv_kernel, d=d, tm=tm, n_rope=n_rope, nj=nj)
    ni = s // tm

    def tile(bi, i, j):
        return _prefetch_tile(bi, i, j == nj, b, ni)

    tab = pl.BlockSpec((None, tm, LANES), lambda bi, i, j: (*tile(bi, i, j), 0))
    perm_scratch = []
    if d > 1:
        perm_scratch = [pltpu.VMEM((PERM_SLOTS, tm, LANES), F32)] + [pltpu.VMEM((tm, LANES), F32)] * 3
    return pl.pallas_call(
        kern,
        grid=(b, s // tm, nj + 1),
        in_specs=[
            pl.BlockSpec((None, tm, D_MODEL), lambda bi, i, j: (*tile(bi, i, j), 0)),
            pl.BlockSpec((1, D_MODEL), lambda bi, i, j: (0, 0)),
            pl.BlockSpec((None, D_MODEL, tn), lambda bi, i, j: (layer, 0, col0 + jnp.where(j == nj, 0, j))),
            tab, tab, tab,
        ],
        out_specs=pl.BlockSpec((None, d, tm // d, tn), lambda bi, i, j: (bi, 0, i, jnp.maximum(j - 1, 0))),
        out_shape=jax.ShapeDtypeStruct((b, d, s // d, n), BF16),
        scratch_shapes=[pltpu.VMEM((tm, D_MODEL), BF16), pltpu.VMEM((tm, tn), F32)] + perm_scratch,
        compiler_params=_params("parallel", "parallel", "arbitrary"),
        name=f"dil_qkv_d{d}",
    )(x, gain, w_in, cd, sa, sb)


def _dil_attn_kernel(q_ref, kc_ref, kp_ref, vc_ref, vp_ref, o_ref, lse_ref, s_ref, p_ref, *, scale, nblk):
    jb = pl.program_id(2)
    qb = DIL_BLOCK
    qi = lax.broadcasted_iota(jnp.int32, (qb, 2 * qb), 0)
    ki = lax.broadcasted_iota(jnp.int32, (qb, 2 * qb), 1)
    lane = lax.broadcasted_iota(jnp.int32, (qb, LANES), 1)

    def keys(cur_ref, prev_ref, blk, sl):
        own = cur_ref[blk * qb:(blk + 1) * qb, sl]
        before = prev_ref[:, sl] if blk == 0 else cur_ref[(blk - 1) * qb:blk * qb, sl]
        return jnp.concatenate([before, own], axis=0)

    for blk in range(nblk):
        for h in range(DIL_HEADS):
            sl = slice(h * DIL_HEAD_DIM, (h + 1) * DIL_HEAD_DIM)
            s_ref[blk, h] = _dot_nt(q_ref[blk * qb:(blk + 1) * qb, sl], keys(kc_ref, kp_ref, blk, sl))
    for blk in range(nblk):
        first_key = jnp.where(jb > 0, qi, qb) if blk == 0 else qi
        band = jnp.logical_and(ki >= first_key, ki <= qi + qb)
        s = jnp.where(band[None], s_ref[blk], -jnp.inf)
        m = jnp.max(s, axis=-1, keepdims=True)
        e = jnp.exp2((s - m) * (scale * LOG2_E))
        l = jnp.sum(e, axis=-1, keepdims=True)
        p_ref[blk] = e.astype(BF16)
        inv = 1.0 / l
        lse = scale * m + jnp.log(l)
        lse_all = jnp.zeros((qb, LANES), F32)
        for h in range(DIL_HEADS):
            lse_all = jnp.where(lane == h, lse[h], lse_all)
        lse_ref[blk * qb:(blk + 1) * qb, :] = lse_all
        for h in range(DIL_HEADS):
            sl = slice(h * DIL_HEAD_DIM, (h + 1) * DIL_HEAD_DIM)
            pv = _dot(p_ref[blk, h], keys(vc_ref, vp_ref, blk, sl))
            o_ref[blk * qb:(blk + 1) * qb, sl] = pv * inv[h]


def _dil_attention(qkv_g, d):
    b, _, l, _ = qkv_g.shape
    hw = DIL_HEADS * DIL_HEAD_DIM
    qb = DIL_BLOCK
    nblk = min(DIL_BLOCKS_PER_STEP, l // qb)
    rows = nblk * qb
    kern = functools.partial(_dil_attn_kernel, scale=DIL_HEAD_DIM ** -0.5, nblk=nblk)

    def cur(c):
        return pl.BlockSpec((None, None, rows, hw), lambda bi, r, jb: (bi, r, jb, c))

    def prev(c):
        return pl.BlockSpec((None, None, qb, hw), lambda bi, r, jb: (bi, r, jnp.maximum(jb * nblk - 1, 0), c))

    return pl.pallas_call(
        kern,
        grid=(b, d, l // rows),
        in_specs=[cur(0), cur(1), prev(1), cur(2), prev(2)],
        out_specs=[
            pl.BlockSpec((None, None, rows, hw), lambda bi, r, jb: (bi, r, jb, 0)),
            pl.BlockSpec((None, None, rows, LANES), lambda bi, r, jb: (bi, r, jb, 0)),
        ],
        out_shape=[
            jax.ShapeDtypeStruct((b, d, l, hw), F32),
            jax.ShapeDtypeStruct((b, d, l, LANES), F32),
        ],
        scratch_shapes=[
            pltpu.VMEM((nblk, DIL_HEADS, qb, 2 * qb), F32),
            pltpu.VMEM((nblk, DIL_HEADS, qb, 2 * qb), BF16),
        ],
        compiler_params=_params("parallel", "parallel", "arbitrary"),
        name=f"dil_attention_d{d}",
    )(qkv_g, qkv_g, qkv_g, qkv_g, qkv_g)


def _dil_merge_kernel(o0_ref, l0_ref, o1_ref, l1_ref, o2_ref, l2_ref, w_ref, x_ref, out_ref,
                      n1_ref, n2_ref, nl1_ref, nl2_ref, a_ref, *, tm, n):
    i = pl.program_id(1)
    d1 = DIL_GROUPS[1][1]
    d2 = DIL_GROUPS[2][1]

    def merge():
        for d, o_ref, l_ref, n_ref, nl_ref in ((d1, o1_ref, l1_ref, n1_ref, nl1_ref),
                                               (d2, o2_ref, l2_ref, n2_ref, nl2_ref)):
            for r in range(d):
                rows = pl.ds(r, tm // d, stride=d)
                nl_ref[rows, :] = l_ref[r]
                for h in range(DIL_HEADS):
                    n_ref[h, rows, :] = o_ref[r, :, h * DIL_HEAD_DIM:(h + 1) * DIL_HEAD_DIM]
        for h in range(DIL_HEADS):
            sl = slice(h * DIL_HEAD_DIM, (h + 1) * DIL_HEAD_DIM)
            a0 = l0_ref[:, h:h + 1]
            a1 = nl1_ref[:, h:h + 1]
            a2 = nl2_ref[:, h:h + 1]
            mx = jnp.maximum(jnp.maximum(a0, a1), a2)
            w0 = jnp.exp(a0 - mx)
            w1 = jnp.exp(a1 - mx)
            w2 = jnp.exp(a2 - mx)
            den = w0 + w1 + w2
            o = (w0 / den) * o0_ref[:, sl] + (w1 / den) * n1_ref[h] + (w2 / den) * n2_ref[h]
            a_ref[:, sl] = o.astype(BF16)

    def project(a):
        out_ref[...] = x_ref[...] + _dot(a, w_ref[...])

    @pl.when(i == 0)
    def _():
        merge()

    @pl.when(jnp.logical_and(i > 0, i < n))
    def _():
        a = a_ref[...]
        merge()
        project(a)

    @pl.when(i == n)
    def _():
        project(a_ref[...])


def _dil_merge_out(outs, lses, wo, layer, x):
    b, s, dm = x.shape
    hw = DIL_HEADS * DIL_HEAD_DIM
    tm = min(s, TM_MERGE)
    d1 = DIL_GROUPS[1][1]
    d2 = DIL_GROUPS[2][1]
    n = s // tm
    kern = functools.partial(_dil_merge_kernel, tm=tm, n=n)

    def merged(i):
        return jnp.minimum(i, n - 1)

    def projected(i):
        return jnp.maximum(i - 1, 0)

    def grp(d, w):
        return pl.BlockSpec((None, d, tm // d, w), lambda bi, i: (bi, 0, merged(i), 0))

    return pl.pallas_call(
        kern,
        grid=(b, n + 1),
        in_specs=[
            pl.BlockSpec((None, None, tm, hw), lambda bi, i: (bi, 0, merged(i), 0)),
            pl.BlockSpec((None, None, tm, LANES), lambda bi, i: (bi, 0, merged(i), 0)),
            grp(d1, hw), grp(d1, LANES), grp(d2, hw), grp(d2, LANES),
            pl.BlockSpec((None, hw, dm), lambda bi, i: (layer, 0, 0)),
            pl.BlockSpec((None, tm, dm), lambda bi, i: (bi, projected(i), 0)),
        ],
        out_specs=pl.BlockSpec((None, tm, dm), lambda bi, i: (bi, projected(i), 0)),
        out_shape=jax.ShapeDtypeStruct((b, s, dm), F32),
        scratch_shapes=[
            pltpu.VMEM((DIL_HEADS, tm, DIL_HEAD_DIM), F32), pltpu.VMEM((DIL_HEADS, tm, DIL_HEAD_DIM), F32),
            pltpu.VMEM((tm, LANES), F32), pltpu.VMEM((tm, LANES), F32),
            pltpu.VMEM((tm, hw), BF16),
        ],
        compiler_params=_params("parallel", "arbitrary"),
        name="dil_merge_out",
    )(outs[0], lses[0], outs[1], lses[1], outs[2], lses[2], wo, x)


def _ffn_kernel(x_ref, g_ref, og_ref, wg_ref, wv_ref, cp_ref, wd_ref, *rest, tm, tf, nf, out_norm, n_cast):
    cast_src = rest[:n_cast]
    o_ref = rest[n_cast]
    cast_dst = rest[n_cast + 1:2 * n_cast + 1]
    h_ref, u_ref, prev_ref = rest[2 * n_cast + 1:]

    def side_jobs():
        for src, dst in zip(cast_src, cast_dst):
            dst[...] = src[...].astype(BF16)
    i = pl.program_id(1)
    t = pl.program_id(2)
    row = lax.broadcasted_iota(jnp.int32, (CARRY_ROWS, 1), 0)

    def up():
        h = h_ref[...]
        u_ref[:, :tf] = _dot(h, wg_ref[...])
        u_ref[:, tf:] = _dot(h, wv_ref[...])

    def conv(u, c):
        prev = prev_ref[c]
        prev_ref[c] = u[tm - CARRY_ROWS:, :]
        p1 = prev[CARRY_ROWS - 1:CARRY_ROWS, :]
        p2 = prev[CARRY_ROWS - 2:CARRY_ROWS - 1, :]
        r1 = pltpu.roll(u, 1, 0)
        r2 = pltpu.roll(u, 2, 0)
        u1 = jnp.concatenate([jnp.where(row == 0, p1, r1[:CARRY_ROWS]), r1[CARRY_ROWS:]], axis=0)
        u2 = jnp.concatenate(
            [jnp.where(row == 0, p2, jnp.where(row == 1, p1, r2[:CARRY_ROWS])), r2[CARRY_ROWS:]], axis=0)
        cp = cp_ref[...]
        acc = cp[CONV_WIDTH:CONV_WIDTH + 1, :] + u2 * cp[0:1, :]
        acc = acc + u1 * cp[1:2, :]
        return acc + u * cp[2:3, :]

    def down(u):
        cv = conv(u, t - 1)
        gate = cv[:, :tf]
        val = cv[:, tf:]
        act = (gate * (1.0 / (1.0 + jnp.exp(-gate)))) * val
        o_ref[...] += _dot(act.astype(BF16), wd_ref[...])

    @pl.when(t == 0)
    def _():
        side_jobs()
        x = x_ref[...]
        h_ref[...] = _rms(x, g_ref[...]).astype(BF16)
        o_ref[...] = x
        up()

    @pl.when(jnp.logical_and(t == 0, i == 0))
    def _():
        prev_ref[...] = jnp.zeros(prev_ref.shape, F32)

    @pl.when(jnp.logical_and(t > 0, t < nf))
    def _():
        side_jobs()
        u = u_ref[...]
        up()
        down(u)

    @pl.when(t == nf)
    def _():
        side_jobs()
        down(u_ref[...])
        if out_norm:
            o_ref[...] = _rms(o_ref[...], og_ref[...])


def _ffn_chunk_order(a, tf):
    lead = a.shape[:-1]
    return a.reshape(*lead, 2, FFN_HIDDEN // tf, tf).swapaxes(-3, -2).reshape(*lead, 2 * FFN_HIDDEN)


def _ffn_conv_params(conv_w, conv_b):
    pad = jnp.zeros((conv_w.shape[0], SUBLANES - CONV_WIDTH - 1, conv_w.shape[2]), F32)
    conv_p = jnp.concatenate([conv_w, conv_b[:, None, :], pad], axis=1)
    return _ffn_chunk_order(conv_p, TF_FFN)


def _ffn(x, gain, out_gain, w_up, up_layer, conv_p, conv_layer, w_down, down_layer, out_norm, casts):
    b, s, _ = x.shape
    tm = min(s, TM_FFN)
    tf = TF_FFN
    nf = FFN_HIDDEN // tf
    steps = b * (s // tm) * (nf + 1)
    kern = functools.partial(_ffn_kernel, tm=tm, tf=tf, nf=nf, out_norm=out_norm, n_cast=len(casts))

    def flat_step(bi, i, t):
        return (bi * (s // tm) + i) * (nf + 1) + t

    cast_in, cast_out, cast_shapes = [], [], []
    for w, layer in casts:
        _, r, c = w.shape
        rb = CAST_ROWS
        assert r % rb == 0 and r // rb <= steps
        cast_in.append(pl.BlockSpec(
            (None, rb, c), lambda bi, i, t, layer=layer, n=r // rb: (layer, jnp.minimum(flat_step(bi, i, t), n - 1), 0)))
        cast_out.append(pl.BlockSpec(
            (None, rb, c), lambda bi, i, t, n=r // rb: (0, jnp.minimum(flat_step(bi, i, t), n - 1), 0)))
        cast_shapes.append(jax.ShapeDtypeStruct((1, r, c), BF16))

    def up_chunk(t):
        return jnp.where(t == nf, 0, t)

    def tile(bi, i, t):
        return _prefetch_tile(bi, i, t == nf, b, s // tm)

    def down_chunk(t):
        return jnp.maximum(t - 1, 0)

    outs = pl.pallas_call(
        kern,
        grid=(b, s // tm, nf + 1),
        in_specs=[
            pl.BlockSpec((None, tm, D_MODEL), lambda bi, i, t: (*tile(bi, i, t), 0)),
            pl.BlockSpec((1, D_MODEL), lambda bi, i, t: (0, 0)),
            pl.BlockSpec((1, D_MODEL), lambda bi, i, t: (0, 0)),
            pl.BlockSpec((None, D_MODEL, tf), lambda bi, i, t: (up_layer, 0, up_chunk(t))),
            pl.BlockSpec((None, D_MODEL, tf), lambda bi, i, t: (up_layer, 0, nf + up_chunk(t))),
            pl.BlockSpec((None, SUBLANES, 2 * tf), lambda bi, i, t: (conv_layer, 0, down_chunk(t))),
            pl.BlockSpec((None, tf, D_MODEL), lambda bi, i, t: (down_layer, down_chunk(t), 0)),
        ] + cast_in,
        out_specs=[pl.BlockSpec((None, tm, D_MODEL), lambda bi, i, t: (bi, i, 0))] + cast_out,
        out_shape=[jax.ShapeDtypeStruct(x.shape, F32)] + cast_shapes,
        scratch_shapes=[
            pltpu.VMEM((tm, D_MODEL), BF16),
            pltpu.VMEM((tm, 2 * tf), F32),
            pltpu.VMEM((nf, CARRY_ROWS, 2 * tf), F32),
        ],
        compiler_params=_params("arbitrary", "arbitrary", "arbitrary"),
        name="conv_ffn",
    )(x, gain, out_gain, w_up, w_up, conv_p, w_down, *[w for w, _ in casts])
    return outs[0], outs[1:]


def _rotate_half_cols(w):
    half = w.shape[-1] // 2
    return jnp.concatenate([-w[..., half:], w[..., :half]], axis=-1)


def _mla_weights(wq_a, wq_b, wkv_a, wkv_b):
    w_pe = wkv_a[:, MLA_KV_RANK:]
    w1 = jnp.concatenate([wq_a, wkv_a[:, :MLA_KV_RANK], w_pe, _rotate_half_cols(w_pe)], axis=1)
    qb = wq_b.reshape(MLA_Q_RANK, MLA_HEADS, MLA_NOPE + MLA_ROPE)
    nope = qb[:, :, :MLA_NOPE].reshape(MLA_Q_RANK, MLA_PAIRS, 2 * MLA_NOPE)
    pe = qb[:, :, MLA_NOPE:]
    rot = _rotate_half_cols(pe).reshape(MLA_Q_RANK, MLA_PAIRS, LANES)
    pe = pe.reshape(MLA_Q_RANK, MLA_PAIRS, LANES)
    wq = jnp.concatenate([nope, pe, rot], axis=2).reshape(MLA_Q_RANK, MLA_PAIRS * Q_PAIR_IN)
    kvb = wkv_b.reshape(MLA_KV_RANK, MLA_HEADS, MLA_NOPE + MLA_V)
    wkv = jnp.concatenate([kvb[:, :, :MLA_NOPE].reshape(MLA_KV_RANK, -1),
                           kvb[:, :, MLA_NOPE:].reshape(MLA_KV_RANK, -1)], axis=1)
    return w1.astype(BF16), wq.astype(BF16), wkv.astype(BF16)


def _mla_layer(x, gain, tabs, wq_a, q_norm, wq_b, wkv_a, kv_norm, wkv_b, wo, layer):
    b, s, dm = x.shape
    cm, sm, tk = tabs
    w1, wq, wkv = _mla_weights(wq_a, wq_b, wkv_a, wkv_b)
    x2 = x.reshape(b * s, dm)
    q_lat, ckv, kpe = _mla_down(x2, gain.reshape(1, dm), w1, q_norm.reshape(1, -1), kv_norm.reshape(1, -1), tk)
    q = _mla_q_up(q_lat, wq, cm, sm)
    kv = _mla_kv_up(ckv, wkv)
    o = _mla_attention(q.reshape(b, s, -1), kv.reshape(b, s, -1), kpe.reshape(b, s, -1), b, s)
    return _matmul_resid(o.reshape(b * s, -1), wo, layer, x2).reshape(b, s, dm)


def _dil_layer(x, gain, tabs, w_in, wo, layer):
    b, s, dm = x.shape
    cd, sa, sb = (t.reshape(b, s, LANES) for t in tabs)
    outs, lses = [], []
    for g, (_, d) in enumerate(DIL_GROUPS):
        qkv_g = _dil_qkv(x, gain.reshape(1, dm), w_in, 0, g, cd, sa, sb, d)
        o, lse = _dil_attention(qkv_g, d)
        outs.append(o)
        lses.append(lse)
    return _dil_merge_out(outs, lses, wo, layer, x)


def kernel(x, positions, attn_norm, ffn_norm, final_norm, mla_wq_a, mla_q_norm, mla_wq_b, mla_wkv_a,
           mla_kv_norm, mla_wkv_b, mla_wo, dil_w_in, dil_wo, ffn_w_up, ffn_conv_w, ffn_conv_b, ffn_w_down):
    b, s, dm = x.shape
    cm, sm, tk, cd, sa, sb = _rope_tables(positions)
    mla_wo_b = mla_wo.astype(BF16)
    dil_wo_b = dil_wo.astype(BF16)
    ffn_conv_p = _ffn_conv_params(ffn_conv_w, ffn_conv_b)
    up_b, down_b, w_in_b = ffn_w_up[:1].astype(BF16), ffn_w_down[:1].astype(BF16), None
    for i in range(DEPTH):
        j = i // N_MIXERS
        if i % N_MIXERS == 0:
            x = _mla_layer(x, attn_norm[i], (cm, sm, tk), mla_wq_a[j], mla_q_norm[j], mla_wq_b[j],
                           mla_wkv_a[j], mla_kv_norm[j], mla_wkv_b[j], mla_wo_b, j)
        else:
            x = _dil_layer(x, attn_norm[i], (cd, sa, sb), w_in_b, dil_wo_b, j)
        last = i == DEPTH - 1
        casts = [] if last else [(ffn_w_up, i + 1), (ffn_w_down, i + 1)]
        if not last and (i + 1) % N_MIXERS == 1:
            casts.append((dil_w_in, (i + 1) // N_MIXERS))
        x, cast = _ffn(x, ffn_norm[i].reshape(1, dm), final_norm.reshape(1, dm), up_b, 0, ffn_conv_p, i,
                       down_b, 0, last, casts)
        if not last:
            up_b, down_b = cast[0], cast[1]
            w_in_b = cast[2] if len(cast) > 2 else None
    return x
```

```python
import functools

import jax
import jax.numpy as jnp
from jax import lax
from jax.experimental import pallas as pl
from jax.experimental.pallas import tpu as pltpu

F32 = jnp.float32
BF16 = jnp.bfloat16

D_MODEL = 2048
DEPTH = 4
N_MIXERS = 2
ROPE_THETA = 500000.0
NORM_EPS = 1e-6
LOG2_E = 1.4426950408889634

MLA_HEADS = 16
MLA_Q_RANK = 512
MLA_KV_RANK = 512
MLA_NOPE = 128
MLA_ROPE = 64
MLA_V = 128

DIL_GROUPS = ((128, 1), (512, 4), (2048, 16))
DIL_HEADS = 16
DIL_HEAD_DIM = 128
DIL_ROT = DIL_HEAD_DIM // 4
DIL_BLOCK = 128
DIL_BLOCKS_PER_STEP = 4

FFN_HIDDEN = 5632
CONV_WIDTH = 3

LANES = 128
SUBLANES = 8
VMEM_BYTES_V7X = 64 * 1024 * 1024
VMEM_LIMIT = VMEM_BYTES_V7X - 8 * 1024 * 1024

TM_PROJ = 1024
TN_PROJ = 1024
TM_SMALL = 512
TM_MLA_PROJ = 1024
TM_MERGE = 256
TQ_MLA = 512
TM_FFN = 512
TF_FFN = 512
CARRY_ROWS = SUBLANES
PERM_SLOTS = 4
CAST_ROWS = 16


def _params(*sem):
    return pltpu.CompilerParams(dimension_semantics=sem, vmem_limit_bytes=VMEM_LIMIT)


def _prefetch_tile(bi, i, last, nb, ni):
    nxt = i + last.astype(jnp.int32)
    wrap = nxt == ni
    return jnp.minimum(bi + wrap.astype(jnp.int32), nb - 1), jnp.where(wrap, 0, nxt)


def _rms(x, g):
    ms = jnp.mean(x * x, axis=-1, keepdims=True)
    return (x * lax.rsqrt(ms + NORM_EPS)) * g


def _dot(a, b):
    return jnp.dot(a, b, preferred_element_type=F32)


def _dot_nt(a, b):
    return lax.dot_general(a, b, (((1,), (1,)), ((), ())), preferred_element_type=F32)


def _rope_tables_kernel(pos_ref, fm_ref, fd_ref, cm_ref, sm_ref, tk_ref, cd_ref, sa_ref, sb_ref):
    pos = pos_ref[...]
    lane = lax.broadcasted_iota(jnp.int32, (1, LANES), 1)
    am = pos * fm_ref[...]
    cm = jnp.cos(am)
    sm = jnp.sin(am)
    cm_ref[...] = cm
    sm_ref[...] = sm
    tk_ref[...] = jnp.where(lane < MLA_ROPE, cm, sm)
    ad = pos * fd_ref[...]
    cd = jnp.cos(ad)
    sd = jnp.sin(ad)
    half = DIL_ROT // 2
    cd_ref[...] = jnp.where(lane < DIL_ROT, cd, 1.0)
    sa_ref[...] = jnp.where(lane < half, 0.0, jnp.where(lane < DIL_ROT, sd, 0.0))
    sb_ref[...] = jnp.where(lane < half, -sd, 0.0)


def _rope_tables(positions):
    m = positions.size
    pos = positions.reshape(m, 1).astype(F32)
    inv_m = ROPE_THETA ** (-jnp.arange(0, MLA_ROPE, 2, dtype=F32) / MLA_ROPE)
    inv_d = ROPE_THETA ** (-jnp.arange(0, DIL_ROT, 2, dtype=F32) / DIL_ROT)
    fm = jnp.tile(inv_m, LANES // inv_m.size).reshape(1, LANES)
    fd = jnp.tile(inv_d, LANES // inv_d.size).reshape(1, LANES)
    tm = min(m, 2048)
    row = pl.BlockSpec((tm, LANES), lambda i: (i, 0))
    vec = pl.BlockSpec((1, LANES), lambda i: (0, 0))
    tab = jax.ShapeDtypeStruct((m, LANES), F32)
    return pl.pallas_call(
        _rope_tables_kernel,
        grid=(m // tm,),
        in_specs=[pl.BlockSpec((tm, 1), lambda i: (i, 0)), vec, vec],
        out_specs=[row] * 6,
        out_shape=[tab] * 6,
        compiler_params=_params("parallel"),
        name="rope_tables",
    )(pos, fm, fd)


def _mla_down_kernel(x_ref, g_ref, w_ref, qn_ref, kvn_ref, tk_ref, ql_ref, ckv_ref, kpe_ref):
    half = x_ref.shape[0] // 2
    for rows in (slice(0, half), slice(half, 2 * half)):
        h = _rms(x_ref[rows, :], g_ref[...]).astype(BF16)
        acc = _dot(h, w_ref[...])
        ql_ref[rows, :] = _rms(acc[:, :MLA_Q_RANK], qn_ref[...]).astype(BF16)
        ckv_ref[rows, :] = _rms(acc[:, MLA_Q_RANK:MLA_Q_RANK + MLA_KV_RANK], kvn_ref[...]).astype(BF16)
        y = acc[:, MLA_Q_RANK + MLA_KV_RANK:] * tk_ref[rows, :]
        z = y + pltpu.roll(y, MLA_ROPE, 1)
        lane = lax.broadcasted_iota(jnp.int32, z.shape, 1)
        kpe_ref[rows, :LANES] = jnp.where(lane < MLA_ROPE, z, 0.0).astype(BF16)
        kpe_ref[rows, LANES:] = jnp.where(lane < MLA_ROPE, 0.0, z).astype(BF16)


def _mla_down(x2, gain, w1, q_norm, kv_norm, tk):
    m = x2.shape[0]
    tm = min(m, TM_MLA_PROJ)
    n1 = w1.shape[1]
    return pl.pallas_call(
        _mla_down_kernel,
        grid=(m // tm,),
        in_specs=[
            pl.BlockSpec((tm, D_MODEL), lambda i: (i, 0)),
            pl.BlockSpec((1, D_MODEL), lambda i: (0, 0)),
            pl.BlockSpec((D_MODEL, n1), lambda i: (0, 0)),
            pl.BlockSpec((1, MLA_Q_RANK), lambda i: (0, 0)),
            pl.BlockSpec((1, MLA_KV_RANK), lambda i: (0, 0)),
            pl.BlockSpec((tm, LANES), lambda i: (i, 0)),
        ],
        out_specs=[
            pl.BlockSpec((tm, MLA_Q_RANK), lambda i: (i, 0)),
            pl.BlockSpec((tm, MLA_KV_RANK), lambda i: (i, 0)),
            pl.BlockSpec((tm, 2 * LANES), lambda i: (i, 0)),
        ],
        out_shape=[
            jax.ShapeDtypeStruct((m, MLA_Q_RANK), BF16),
            jax.ShapeDtypeStruct((m, MLA_KV_RANK), BF16),
            jax.ShapeDtypeStruct((m, 2 * LANES), BF16),
        ],
        compiler_params=_params("parallel"),
        name="mla_down",
    )(x2, gain, w1, q_norm, kv_norm, tk)


MLA_PAIRS = MLA_HEADS // 2
Q_PAIR_IN = 2 * MLA_NOPE + 2 * LANES
Q_PAIR_OUT = 2 * MLA_NOPE + LANES


def _mla_q_up_kernel(a_ref, w_ref, c_ref, s_ref, o_ref):
    a = a_ref[...]
    c = c_ref[...]
    s = s_ref[...]
    for p in range(MLA_PAIRS):
        acc = _dot(a, w_ref[:, p * Q_PAIR_IN:(p + 1) * Q_PAIR_IN])
        o_ref[:, p * Q_PAIR_OUT:p * Q_PAIR_OUT + 2 * MLA_NOPE] = acc[:, :2 * MLA_NOPE].astype(BF16)
        pe = acc[:, 2 * MLA_NOPE:2 * MLA_NOPE + LANES] * c + acc[:, 2 * MLA_NOPE + LANES:] * s
        o_ref[:, p * Q_PAIR_OUT + 2 * MLA_NOPE:(p + 1) * Q_PAIR_OUT] = pe.astype(BF16)


def _mla_q_up(q_lat, wq, cm, sm):
    m = q_lat.shape[0]
    tm = min(m, TM_MLA_PROJ)
    n_out = MLA_PAIRS * Q_PAIR_OUT
    return pl.pallas_call(
        _mla_q_up_kernel,
        grid=(m // tm,),
        in_specs=[
            pl.BlockSpec((tm, MLA_Q_RANK), lambda i: (i, 0)),
            pl.BlockSpec(wq.shape, lambda i: (0, 0)),
            pl.BlockSpec((tm, LANES), lambda i: (i, 0)),
            pl.BlockSpec((tm, LANES), lambda i: (i, 0)),
        ],
        out_specs=pl.BlockSpec((tm, n_out), lambda i: (i, 0)),
        out_shape=jax.ShapeDtypeStruct((m, n_out), BF16),
        compiler_params=_params("parallel"),
        name="mla_q_up",
    )(q_lat, wq, cm, sm)


def _matmul_bf16_kernel(a_ref, w_ref, o_ref):
    o_ref[...] = _dot(a_ref[...], w_ref[...]).astype(BF16)


def _mla_kv_up(ckv, wkv):
    m = ckv.shape[0]
    tm = min(m, TM_MLA_PROJ)
    n = wkv.shape[1]
    return pl.pallas_call(
        _matmul_bf16_kernel,
        grid=(m // tm,),
        in_specs=[
            pl.BlockSpec((tm, MLA_KV_RANK), lambda i: (i, 0)),
            pl.BlockSpec(wkv.shape, lambda i: (0, 0)),
        ],
        out_specs=pl.BlockSpec((tm, n), lambda i: (i, 0)),
        out_shape=jax.ShapeDtypeStruct((m, n), BF16),
        compiler_params=_params("parallel"),
        name="mla_kv_up",
    )(ckv, wkv)


def _mla_attn_kernel(q_ref, kn_ref, kpe_ref, v_ref, o_ref, m_ref, l_ref, acc_ref, *, tq, scale):
    g = pl.program_id(2)
    scale_log2e = scale * LOG2_E
    row = lax.broadcasted_iota(jnp.int32, (tq, tq), 0)
    col = lax.broadcasted_iota(jnp.int32, (tq, tq), 1)
    causal = col <= row
    heads = (0, 1)
    for sub in range(2):
        qi = 2 * g + sub
        rows = slice(sub * tq, (sub + 1) * tq)
        q_pe = q_ref[rows, 2 * MLA_NOPE:]
        qs = [jnp.concatenate([q_ref[rows, hh * LANES:(hh + 1) * LANES], q_pe], axis=1) for hh in heads]
        m_ref[...] = jnp.full(m_ref.shape, -jnp.inf, F32)
        l_ref[...] = jnp.zeros(l_ref.shape, F32)
        acc_ref[...] = jnp.zeros(acc_ref.shape, F32)

        def step(hh, kb, masked, nkb=1):
            lanes = slice(hh * LANES, (hh + 1) * LANES)
            tk = nkb * tq
            ks = pl.multiple_of(kb * tq, tq)
            k = jnp.concatenate([kn_ref[pl.ds(ks, tk), lanes], kpe_ref[pl.ds(ks, tk), lanes]], axis=1)
            s = _dot_nt(qs[hh], k)
            if masked:
                s = jnp.where(causal, s, -jnp.inf)
            m_prev = m_ref[hh]
            m_next = jnp.maximum(m_prev, jnp.max(s, axis=1, keepdims=True))
            m_wide = jnp.concatenate([m_next] * (tk // LANES), axis=1)
            p = jnp.exp2((s - m_wide) * scale_log2e)
            alpha = jnp.exp2((m_prev - m_next) * scale_log2e)
            l_ref[hh] = alpha * l_ref[hh] + jnp.sum(p, axis=1, keepdims=True)
            m_ref[hh] = m_next
            acc_ref[hh] = alpha * acc_ref[hh] + _dot(p.astype(BF16), v_ref[pl.ds(ks, tk), lanes])

        def trip(k2, carry):
            for hh in heads:
                step(hh, 2 * k2, False, nkb=2)
            return carry

        lax.fori_loop(0, g, trip, 0)
        if sub == 1:
            for hh in heads:
                step(hh, qi - 1, False)
        for hh in heads:
            step(hh, qi, True)
        for hh in heads:
            o_ref[rows, hh * LANES:(hh + 1) * LANES] = (acc_ref[hh] / l_ref[hh]).astype(BF16)


def _mla_attention(q, kv, kpe, b, s):
    tq = min(s // 2, TQ_MLA)
    scale = (MLA_NOPE + MLA_ROPE) ** -0.5
    kern = functools.partial(_mla_attn_kernel, tq=tq, scale=scale)
    pair_w = 2 * LANES
    return pl.pallas_call(
        kern,
        grid=(b, MLA_PAIRS, s // (2 * tq)),
        in_specs=[
            pl.BlockSpec((None, 2 * tq, Q_PAIR_OUT), lambda bi, p, g: (bi, g, p)),
            pl.BlockSpec((None, s, pair_w), lambda bi, p, g: (bi, 0, p)),
            pl.BlockSpec((None, s, pair_w), lambda bi, p, g: (bi, 0, 0)),
            pl.BlockSpec((None, s, pair_w), lambda bi, p, g: (bi, 0, MLA_PAIRS + p)),
        ],
        out_specs=pl.BlockSpec((None, 2 * tq, pair_w), lambda bi, p, g: (bi, g, p)),
        out_shape=jax.ShapeDtypeStruct((b, s, MLA_HEADS * MLA_V), BF16),
        scratch_shapes=[
            pltpu.VMEM((2, tq, LANES), F32),
            pltpu.VMEM((2, tq, LANES), F32),
            pltpu.VMEM((2, tq, LANES), F32),
        ],
        compiler_params=_params("parallel", "parallel", "arbitrary"),
        name="mla_attention",
    )(q, kv, kpe, kv)


def _matmul_resid_kernel(a_ref, w_ref, r_ref, o_ref):
    o_ref[...] = r_ref[...] + _dot(a_ref[...], w_ref[...])


def _matmul_resid(a, w, layer, resid):
    m, k = a.shape
    n = w.shape[2]
    tm = min(m, TM_SMALL)
    return pl.pallas_call(
        _matmul_resid_kernel,
        grid=(m // tm,),
        in_specs=[
            pl.BlockSpec((tm, k), lambda i: (i, 0)),
            pl.BlockSpec((None, k, n), lambda i: (layer, 0, 0)),
            pl.BlockSpec((tm, n), lambda i: (i, 0)),
        ],
        out_specs=pl.BlockSpec((tm, n), lambda i: (i, 0)),
        out_shape=jax.ShapeDtypeStruct((m, n), F32),
        compiler_params=_params("parallel"),
        name="out_proj_resid",
    )(a, w, resid)


def _dil_rope(xc, c, sa, sb):
    half = DIL_ROT // 2
    return xc * c + pltpu.roll(xc, half, 1) * sa + pltpu.roll(xc, LANES - half, 1) * sb


def _dil_qkv_kernel(x_ref, g_ref, w_ref, c_ref, sa_ref, sb_ref, o_ref, h_ref, acc_ref, *perm_refs,
                    d, tm, n_rope, nj):
    j = pl.program_id(2)
    rows = tm // d
    if d > 1:
        xs_ref, cs_ref, sas_ref, sbs_ref = perm_refs
    else:
        cs_ref, sas_ref, sbs_ref = c_ref, sa_ref, sb_ref

    def norm_input():
        if d == 1:
            h_ref[...] = _rms(x_ref[...], g_ref[...]).astype(BF16)
            return
        x = x_ref[...]
        rinv = lax.rsqrt(jnp.mean(x * x, axis=-1, keepdims=True) + NORM_EPS)
        for t in range(x_ref.shape[1] // LANES):
            lanes = slice(t * LANES, (t + 1) * LANES)
            slot = t % xs_ref.shape[0]
            xs_ref[slot] = (x_ref[:, lanes] * rinv) * g_ref[:, lanes]
            for r in range(d):
                h_ref[r * rows:(r + 1) * rows, lanes] = xs_ref[slot, pl.ds(r, rows, stride=d), :].astype(BF16)
        for r in range(d):
            sl = pl.ds(r, rows, stride=d)
            dst = slice(r * rows, (r + 1) * rows)
            cs_ref[dst, :] = c_ref[sl, :]
            sas_ref[dst, :] = sa_ref[sl, :]
            sbs_ref[dst, :] = sb_ref[sl, :]

    def matmul():
        acc_ref[...] = _dot(h_ref[...], w_ref[...])

    def emit(rope):
        for r in range(d):
            src = slice(r * rows, (r + 1) * rows)
            if rope:
                c, sa, sb = cs_ref[src, :], sas_ref[src, :], sbs_ref[src, :]
            for t in range(acc_ref.shape[1] // LANES):
                a = acc_ref[src, t * LANES:(t + 1) * LANES]
                if rope:
                    a = _dil_rope(a, c, sa, sb)
                o_ref[r, :, t * LANES:(t + 1) * LANES] = a.astype(BF16)

    @pl.when(j == 0)
    def _():
        norm_input()
        matmul()

    @pl.when(jnp.logical_and(j >= 1, j <= n_rope))
    def _():
        emit(True)
        matmul()

    @pl.when(jnp.logical_and(j > n_rope, j < nj))
    def _():
        emit(False)
        matmul()

    @pl.when(j == nj)
    def _():
        emit(False)


def _dil_qkv(x, gain, w_in, layer, g, cd, sa, sb, d):
    b, s, _ = x.shape
    n = 3 * DIL_HEADS * DIL_HEAD_DIM
    tm = min(s, TM_PROJ)
    tn = TN_PROJ
    col0 = g * (n // tn)
    n_rope = 2 * DIL_HEADS * DIL_HEAD_DIM // tn
    nj = n // tn
    assert n_rope < nj
    kern = functools.partial(_dil_qkv_kernel, d=d, tm=tm, n_rope=n_rope, nj=nj)
    ni = s // tm

    def tile(bi, i, j):
        return _prefetch_tile(bi, i, j == nj, b, ni)

    tab = pl.BlockSpec((None, tm, LANES), lambda bi, i, j: (*tile(bi, i, j), 0))
    perm_scratch = []
    if d > 1:
        perm_scratch = [pltpu.VMEM((PERM_SLOTS, tm, LANES), F32)] + [pltpu.VMEM((tm, LANES), F32)] * 3
    return pl.pallas_call(
        kern,
        grid=(b, s // tm, nj + 1),
        in_specs=[
            pl.BlockSpec((None, tm, D_MODEL), lambda bi, i, j: (*tile(bi, i, j), 0)),
            pl.BlockSpec((1, D_MODEL), lambda bi, i, j: (0, 0)),
            pl.BlockSpec((None, D_MODEL, tn), lambda bi, i, j: (layer, 0, col0 + jnp.where(j == nj, 0, j))),
            tab, tab, tab,
        ],
        out_specs=pl.BlockSpec((None, d, tm // d, tn), lambda bi, i, j: (bi, 0, i, jnp.maximum(j - 1, 0))),
        out_shape=jax.ShapeDtypeStruct((b, d, s // d, n), BF16),
        scratch_shapes=[pltpu.VMEM((tm, D_MODEL), BF16), pltpu.VMEM((tm, tn), F32)] + perm_scratch,
        compiler_params=_params("parallel", "parallel", "arbitrary"),
        name=f"dil_qkv_d{d}",
    )(x, gain, w_in, cd, sa, sb)


def _dil_attn_kernel(q_ref, kc_ref, kp_ref, vc_ref, vp_ref, o_ref, lse_ref, s_ref, p_ref, *, scale, nblk):
    jb = pl.program_id(2)
    qb = DIL_BLOCK
    qi = lax.broadcasted_iota(jnp.int32, (qb, 2 * qb), 0)
    ki = lax.broadcasted_iota(jnp.int32, (qb, 2 * qb), 1)
    lane = lax.broadcasted_iota(jnp.int32, (qb, LANES), 1)

    def keys(cur_ref, prev_ref, blk, sl):
        own = cur_ref[blk * qb:(blk + 1) * qb, sl]
        before = prev_ref[:, sl] if blk == 0 else cur_ref[(blk - 1) * qb:blk * qb, sl]
        return jnp.concatenate([before, own], axis=0)

    for blk in range(nblk):
        for h in range(DIL_HEADS):
            sl = slice(h * DIL_HEAD_DIM, (h + 1) * DIL_HEAD_DIM)
            s_ref[blk, h] = _dot_nt(q_ref[blk * qb:(blk + 1) * qb, sl], keys(kc_ref, kp_ref, blk, sl))
    for blk in range(nblk):
        first_key = jnp.where(jb > 0, qi, qb) if blk == 0 else qi
        band = jnp.logical_and(ki >= first_key, ki <= qi + qb)
        s = jnp.where(band[None], s_ref[blk], -jnp.inf)
        m = jnp.max(s, axis=-1, keepdims=True)
        e = jnp.exp2((s - m) * (scale * LOG2_E))
        l = jnp.sum(e, axis=-1, keepdims=True)
        p_ref[blk] = e.astype(BF16)
        inv = 1.0 / l
        lse = scale * m + jnp.log(l)
        lse_all = jnp.zeros((qb, LANES), F32)
        for h in range(DIL_HEADS):
            lse_all = jnp.where(lane == h, lse[h], lse_all)
        lse_ref[blk * qb:(blk + 1) * qb, :] = lse_all
        for h in range(DIL_HEADS):
            sl = slice(h * DIL_HEAD_DIM, (h + 1) * DIL_HEAD_DIM)
            pv = _dot(p_ref[blk, h], keys(vc_ref, vp_ref, blk, sl))
            o_ref[blk * qb:(blk + 1) * qb, sl] = pv * inv[h]


def _dil_attention(qkv_g, d):
    b, _, l, _ = qkv_g.shape
    hw = DIL_HEADS * DIL_HEAD_DIM
    qb = DIL_BLOCK
    nblk = min(DIL_BLOCKS_PER_STEP, l // qb)
    rows = nblk * qb
    kern = functools.partial(_dil_attn_kernel, scale=DIL_HEAD_DIM ** -0.5, nblk=nblk)

    def cur(c):
        return pl.BlockSpec((None, None, rows, hw), lambda bi, r, jb: (bi, r, jb, c))

    def prev(c):
        return pl.BlockSpec((None, None, qb, hw), lambda bi, r, jb: (bi, r, jnp.maximum(jb * nblk - 1, 0), c))

    return pl.pallas_call(
        kern,
        grid=(b, d, l // rows),
        in_specs=[cur(0), cur(1), prev(1), cur(2), prev(2)],
        out_specs=[
            pl.BlockSpec((None, None, rows, hw), lambda bi, r, jb: (bi, r, jb, 0)),
            pl.BlockSpec((None, None, rows, LANES), lambda bi, r, jb: (bi, r, jb, 0)),
        ],
        out_shape=[
            jax.ShapeDtypeStruct((b, d, l, hw), F32),
            jax.ShapeDtypeStruct((b, d, l, LANES), F32),
        ],
        scratch_shapes=[
            pltpu.VMEM((nblk, DIL_HEADS, qb, 2 * qb), F32),
            pltpu.VMEM((nblk, DIL_HEADS, qb, 2 * qb), BF16),
        ],
        compiler_params=_params("parallel", "parallel", "arbitrary"),
        name=f"dil_attention_d{d}",
    )(qkv_g, qkv_g, qkv_g, qkv_g, qkv_g)


def _dil_merge_kernel(o0_ref, l0_ref, o1_ref, l1_ref, o2_ref, l2_ref, w_ref, x_ref, out_ref,
                      n1_ref, n2_ref, nl1_ref, nl2_ref, a_ref, *, tm, n):
    i = pl.program_id(1)
    d1 = DIL_GROUPS[1][1]
    d2 = DIL_GROUPS[2][1]

    def merge():
        for d, o_ref, l_ref, n_ref, nl_ref in ((d1, o1_ref, l1_ref, n1_ref, nl1_ref),
                                               (d2, o2_ref, l2_ref, n2_ref, nl2_ref)):
            for r in range(d):
                rows = pl.ds(r, tm // d, stride=d)
                nl_ref[rows, :] = l_ref[r]
                for h in range(DIL_HEADS):
                    n_ref[h, rows, :] = o_ref[r, :, h * DIL_HEAD_DIM:(h + 1) * DIL_HEAD_DIM]
        for h in range(DIL_HEADS):
            sl = slice(h * DIL_HEAD_DIM, (h + 1) * DIL_HEAD_DIM)
            a0 = l0_ref[:, h:h + 1]
            a1 = nl1_ref[:, h:h + 1]
            a2 = nl2_ref[:, h:h + 1]
            mx = jnp.maximum(jnp.maximum(a0, a1), a2)
            w0 = jnp.exp(a0 - mx)
            w1 = jnp.exp(a1 - mx)
            w2 = jnp.exp(a2 - mx)
            den = w0 + w1 + w2
            o = (w0 / den) * o0_ref[:, sl] + (w1 / den) * n1_ref[h] + (w2 / den) * n2_ref[h]
            a_ref[:, sl] = o.astype(BF16)

    def project(a):
        out_ref[...] = x_ref[...] + _dot(a, w_ref[...])

    @pl.when(i == 0)
    def _():
        merge()

    @pl.when(jnp.logical_and(i > 0, i < n))
    def _():
        a = a_ref[...]
        merge()
        project(a)

    @pl.when(i == n)
    def _():
        project(a_ref[...])


def _dil_merge_out(outs, lses, wo, layer, x):
    b, s, dm = x.shape
    hw = DIL_HEADS * DIL_HEAD_DIM
    tm = min(s, TM_MERGE)
    d1 = DIL_GROUPS[1][1]
    d2 = DIL_GROUPS[2][1]
    n = s // tm
    kern = functools.partial(_dil_merge_kernel, tm=tm, n=n)

    def merged(i):
        return jnp.minimum(i, n - 1)

    def projected(i):
        return jnp.maximum(i - 1, 0)

    def grp(d, w):
        return pl.BlockSpec((None, d, tm // d, w), lambda bi, i: (bi, 0, merged(i), 0))

    return pl.pallas_call(
        kern,
        grid=(b, n + 1),
        in_specs=[
            pl.BlockSpec((None, None, tm, hw), lambda bi, i: (bi, 0, merged(i), 0)),
            pl.BlockSpec((None, None, tm, LANES), lambda bi, i: (bi, 0, merged(i), 0)),
            grp(d1, hw), grp(d1, LANES), grp(d2, hw), grp(d2, LANES),
            pl.BlockSpec((None, hw, dm), lambda bi, i: (layer, 0, 0)),
            pl.BlockSpec((None, tm, dm), lambda bi, i: (bi, projected(i), 0)),
        ],
        out_specs=pl.BlockSpec((None, tm, dm), lambda bi, i: (bi, projected(i), 0)),
        out_shape=jax.ShapeDtypeStruct((b, s, dm), F32),
        scratch_shapes=[
            pltpu.VMEM((DIL_HEADS, tm, DIL_HEAD_DIM), F32), pltpu.VMEM((DIL_HEADS, tm, DIL_HEAD_DIM), F32),
            pltpu.VMEM((tm, LANES), F32), pltpu.VMEM((tm, LANES), F32),
            pltpu.VMEM((tm, hw), BF16),
        ],
        compiler_params=_params("parallel", "arbitrary"),
        name="dil_merge_out",
    )(outs[0], lses[0], outs[1], lses[1], outs[2], lses[2], wo, x)


def _ffn_kernel(x_ref, g_ref, og_ref, wg_ref, wv_ref, cp_ref, wd_ref, *rest, tm, tf, nf, ni, out_norm, n_cast):
    cast_src = rest[:n_cast]
    o_ref = rest[n_cast]
    cast_dst = rest[n_cast + 1:2 * n_cast + 1]
    h_ref, u_ref, prev_ref = rest[2 * n_cast + 1:]
    k = pl.program_id(1)
    t = lax.rem(k, nf)
    c_down = lax.rem(k + nf - 1, nf)
    row = lax.broadcasted_iota(jnp.int32, (CARRY_ROWS, 1), 0)

    def side_jobs():
        for src, dst in zip(cast_src, cast_dst):
            dst[...] = src[...].astype(BF16)

    def up():
        h = h_ref[...]
        u_ref[:, :tf] = _dot(h, wg_ref[...])
        u_ref[:, tf:] = _dot(h, wv_ref[...])

    def start_tile():
        h_ref[...] = _rms(x_ref[...], g_ref[...]).astype(BF16)
        up()

    def conv(u):
        prev = prev_ref[c_down]
        prev_ref[c_down] = u[tm - CARRY_ROWS:, :]
        p1 = prev[CARRY_ROWS - 1:CARRY_ROWS, :]
        p2 = prev[CARRY_ROWS - 2:CARRY_ROWS - 1, :]
        r1 = pltpu.roll(u, 1, 0)
        r2 = pltpu.roll(u, 2, 0)
        u1 = jnp.concatenate([jnp.where(row == 0, p1, r1[:CARRY_ROWS]), r1[CARRY_ROWS:]], axis=0)
        u2 = jnp.concatenate(
            [jnp.where(row == 0, p2, jnp.where(row == 1, p1, r2[:CARRY_ROWS])), r2[CARRY_ROWS:]], axis=0)
        cp = cp_ref[...]
        acc = cp[CONV_WIDTH:CONV_WIDTH + 1, :] + u2 * cp[0:1, :]
        acc = acc + u1 * cp[1:2, :]
        return acc + u * cp[2:3, :]

    def down(u, first_chunk):
        cv = conv(u)
        gate = cv[:, :tf]
        val = cv[:, tf:]
        act = (gate * (1.0 / (1.0 + jnp.exp(-gate)))) * val
        d = _dot(act.astype(BF16), wd_ref[...])
        if first_chunk:
            o_ref[...] = x_ref[...] + d
        else:
            o_ref[...] += d

    def finish_tile():
        if out_norm:
            o_ref[...] = _rms(o_ref[...], og_ref[...])

    @pl.when(k == 0)
    def _():
        side_jobs()
        prev_ref[...] = jnp.zeros(prev_ref.shape, F32)
        start_tile()

    @pl.when(jnp.logical_and(t == 0, jnp.logical_and(k > 0, k < ni * nf)))
    def _():
        side_jobs()
        u = u_ref[...]
        start_tile()
        down(u, False)
        finish_tile()

    @pl.when(t == 1)
    def _():
        side_jobs()
        u = u_ref[...]
        up()
        down(u, True)

    @pl.when(t >= 2)
    def _():
        side_jobs()
        u = u_ref[...]
        up()
        down(u, False)

    @pl.when(k == ni * nf)
    def _():
        side_jobs()
        down(u_ref[...], False)
        finish_tile()


def _ffn_chunk_order(a, tf):
    lead = a.shape[:-1]
    return a.reshape(*lead, 2, FFN_HIDDEN // tf, tf).swapaxes(-3, -2).reshape(*lead, 2 * FFN_HIDDEN)


def _ffn_conv_params(conv_w, conv_b):
    pad = jnp.zeros((conv_w.shape[0], SUBLANES - CONV_WIDTH - 1, conv_w.shape[2]), F32)
    conv_p = jnp.concatenate([conv_w, conv_b[:, None, :], pad], axis=1)
    return _ffn_chunk_order(conv_p, TF_FFN)


def _ffn(x, gain, out_gain, w_up, up_layer, conv_p, conv_layer, w_down, down_layer, out_norm, casts):
    b, s, _ = x.shape
    tm = min(s, TM_FFN)
    tf = TF_FFN
    nf = FFN_HIDDEN // tf
    ni = s // tm
    assert nf >= 2
    nk = ni * nf + 1
    kern = functools.partial(_ffn_kernel, tm=tm, tf=tf, nf=nf, ni=ni, out_norm=out_norm, n_cast=len(casts))

    def in_tile(k):
        return jnp.minimum(lax.div(k, nf), ni - 1)

    def out_tile(k):
        return lax.div(jnp.maximum(k - 1, 0), nf)

    def up_chunk(k):
        return lax.rem(k, nf)

    def down_chunk(k):
        return lax.rem(k + nf - 1, nf)

    cast_in, cast_out, cast_shapes = [], [], []
    for w, layer in casts:
        _, r, c = w.shape
        rb = CAST_ROWS
        assert r % rb == 0 and r // rb <= b * nk
        cast_in.append(pl.BlockSpec(
            (None, rb, c), lambda bi, k, layer=layer, n=r // rb: (layer, jnp.minimum(bi * nk + k, n - 1), 0)))
        cast_out.append(pl.BlockSpec(
            (None, rb, c), lambda bi, k, n=r // rb: (0, jnp.minimum(bi * nk + k, n - 1), 0)))
        cast_shapes.append(jax.ShapeDtypeStruct((1, r, c), BF16))

    outs = pl.pallas_call(
        kern,
        grid=(b, nk),
        in_specs=[
            pl.BlockSpec((None, tm, D_MODEL), lambda bi, k: (bi, in_tile(k), 0)),
            pl.BlockSpec((1, D_MODEL), lambda bi, k: (0, 0)),
            pl.BlockSpec((1, D_MODEL), lambda bi, k: (0, 0)),
            pl.BlockSpec((None, D_MODEL, tf), lambda bi, k: (up_layer, 0, up_chunk(k))),
            pl.BlockSpec((None, D_MODEL, tf), lambda bi, k: (up_layer, 0, nf + up_chunk(k))),
            pl.BlockSpec((None, SUBLANES, 2 * tf), lambda bi, k: (conv_layer, 0, down_chunk(k))),
            pl.BlockSpec((None, tf, D_MODEL), lambda bi, k: (down_layer, down_chunk(k), 0)),
        ] + cast_in,
        out_specs=[pl.BlockSpec((None, tm, D_MODEL), lambda bi, k: (bi, out_tile(k), 0))] + cast_out,
        out_shape=[jax.ShapeDtypeStruct(x.shape, F32)] + cast_shapes,
        scratch_shapes=[
            pltpu.VMEM((tm, D_MODEL), BF16),
            pltpu.VMEM((tm, 2 * tf), F32),
            pltpu.VMEM((nf, CARRY_ROWS, 2 * tf), F32),
        ],
        compiler_params=_params("arbitrary", "arbitrary"),
        name="conv_ffn",
    )(x, gain, out_gain, w_up, w_up, conv_p, w_down, *[w for w, _ in casts])
    return outs[0], outs[1:]


def _rotate_half_cols(w):
    half = w.shape[-1] // 2
    return jnp.concatenate([-w[..., half:], w[..., :half]], axis=-1)


def _mla_weights(wq_a, wq_b, wkv_a, wkv_b):
    w_pe = wkv_a[:, MLA_KV_RANK:]
    w1 = jnp.concatenate([wq_a, wkv_a[:, :MLA_KV_RANK], w_pe, _rotate_half_cols(w_pe)], axis=1)
    qb = wq_b.reshape(MLA_Q_RANK, MLA_HEADS, MLA_NOPE + MLA_ROPE)
    nope = qb[:, :, :MLA_NOPE].reshape(MLA_Q_RANK, MLA_PAIRS, 2 * MLA_NOPE)
    pe = qb[:, :, MLA_NOPE:]
    rot = _rotate_half_cols(pe).reshape(MLA_Q_RANK, MLA_PAIRS, LANES)
    pe = pe.reshape(MLA_Q_RANK, MLA_PAIRS, LANES)
    wq = jnp.concatenate([nope, pe, rot], axis=2).reshape(MLA_Q_RANK, MLA_PAIRS * Q_PAIR_IN)
    kvb = wkv_b.reshape(MLA_KV_RANK, MLA_HEADS, MLA_NOPE + MLA_V)
    wkv = jnp.concatenate([kvb[:, :, :MLA_NOPE].reshape(MLA_KV_RANK, -1),
                           kvb[:, :, MLA_NOPE:].reshape(MLA_KV_RANK, -1)], axis=1)
    return w1.astype(BF16), wq.astype(BF16), wkv.astype(BF16)


def _mla_layer(x, gain, tabs, wq_a, q_norm, wq_b, wkv_a, kv_norm, wkv_b, wo, layer):
    b, s, dm = x.shape
    cm, sm, tk = tabs
    w1, wq, wkv = _mla_weights(wq_a, wq_b, wkv_a, wkv_b)
    x2 = x.reshape(b * s, dm)
    q_lat, ckv, kpe = _mla_down(x2, gain.reshape(1, dm), w1, q_norm.reshape(1, -1), kv_norm.reshape(1, -1), tk)
    q = _mla_q_up(q_lat, wq, cm, sm)
    kv = _mla_kv_up(ckv, wkv)
    o = _mla_attention(q.reshape(b, s, -1), kv.reshape(b, s, -1), kpe.reshape(b, s, -1), b, s)
    return _matmul_resid(o.reshape(b * s, -1), wo, layer, x2).reshape(b, s, dm)


def _dil_layer(x, gain, tabs, w_in, wo, layer):
    b, s, dm = x.shape
    cd, sa, sb = (t.reshape(b, s, LANES) for t in tabs)
    outs, lses = [], []
    for g, (_, d) in enumerate(DIL_GROUPS):
        qkv_g = _dil_qkv(x, gain.reshape(1, dm), w_in, 0, g, cd, sa, sb, d)
        o, lse = _dil_attention(qkv_g, d)
        outs.append(o)
        lses.append(lse)
    return _dil_merge_out(outs, lses, wo, layer, x)


def kernel(x, positions, attn_norm, ffn_norm, final_norm, mla_wq_a, mla_q_norm, mla_wq_b, mla_wkv_a,
           mla_kv_norm, mla_wkv_b, mla_wo, dil_w_in, dil_wo, ffn_w_up, ffn_conv_w, ffn_conv_b, ffn_w_down):
    b, s, dm = x.shape
    cm, sm, tk, cd, sa, sb = _rope_tables(positions)
    mla_wo_b = mla_wo.astype(BF16)
    dil_wo_b = dil_wo.astype(BF16)
    ffn_conv_p = _ffn_conv_params(ffn_conv_w, ffn_conv_b)
    up_b, down_b, w_in_b = ffn_w_up[:1].astype(BF16), ffn_w_down[:1].astype(BF16), None
    for i in range(DEPTH):
        j = i // N_MIXERS
        if i % N_MIXERS == 0:
            x = _mla_layer(x, attn_norm[i], (cm, sm, tk), mla_wq_a[j], mla_q_norm[j], mla_wq_b[j],
                           mla_wkv_a[j], mla_kv_norm[j], mla_wkv_b[j], mla_wo_b, j)
        else:
            x = _dil_layer(x, attn_norm[i], (cd, sa, sb), w_in_b, dil_wo_b, j)
        last = i == DEPTH - 1
        casts = [] if last else [(ffn_w_up, i + 1), (ffn_w_down, i + 1)]
        if not last and (i + 1) % N_MIXERS == 1:
            casts.append((dil_w_in, (i + 1) // N_MIXERS))
        x, cast = _ffn(x, ffn_norm[i].reshape(1, dm), final_norm.reshape(1, dm), up_b, 0, ffn_conv_p, i,
                       down_b, 0, last, casts)
        if not last:
            up_b, down_b = cast[0], cast[1]
            w_in_b = cast[2] if len(cast) > 2 else None
    return x
```

```python
import functools

import jax
import jax.numpy as jnp
from jax import lax
from jax.experimental import pallas as pl
from jax.experimental.pallas import tpu as pltpu

F32 = jnp.float32
BF16 = jnp.bfloat16

D_MODEL = 2048
DEPTH = 4
N_MIXERS = 2
ROPE_THETA = 500000.0
NORM_EPS = 1e-6
LOG2_E = 1.4426950408889634

MLA_HEADS = 16
MLA_Q_RANK = 512
MLA_KV_RANK = 512
MLA_NOPE = 128
MLA_ROPE = 64
MLA_V = 128

DIL_GROUPS = ((128, 1), (512, 4), (2048, 16))
DIL_HEADS = 16
DIL_HEAD_DIM = 128
DIL_ROT = DIL_HEAD_DIM // 4
DIL_BLOCK = 128
DIL_BLOCKS_PER_STEP = 4

FFN_HIDDEN = 5632
CONV_WIDTH = 3

LANES = 128
SUBLANES = 8
VMEM_BYTES_V7X = 64 * 1024 * 1024
VMEM_LIMIT = VMEM_BYTES_V7X - 8 * 1024 * 1024

TM_PROJ = 1024
TN_PROJ = 1024
TM_SMALL = 512
TM_MLA_PROJ = 1024
TM_MERGE = 256
TQ_MLA = 512
TM_FFN = 512
TF_FFN = 512
CARRY_ROWS = SUBLANES
PERM_SLOTS = 4
CAST_ROWS = 16


def _params(*sem):
    return pltpu.CompilerParams(dimension_semantics=sem, vmem_limit_bytes=VMEM_LIMIT)


def _prefetch_tile(bi, i, last, nb, ni):
    nxt = i + last.astype(jnp.int32)
    wrap = nxt == ni
    return jnp.minimum(bi + wrap.astype(jnp.int32), nb - 1), jnp.where(wrap, 0, nxt)


def _rms(x, g):
    ms = jnp.mean(x * x, axis=-1, keepdims=True)
    return (x * lax.rsqrt(ms + NORM_EPS)) * g


def _dot(a, b):
    return jnp.dot(a, b, preferred_element_type=F32)


def _dot_nt(a, b):
    return lax.dot_general(a, b, (((1,), (1,)), ((), ())), preferred_element_type=F32)


def _rope_tables_kernel(pos_ref, fm_ref, fd_ref, cm_ref, sm_ref, tk_ref, cd_ref, sa_ref, sb_ref):
    pos = pos_ref[...]
    lane = lax.broadcasted_iota(jnp.int32, (1, LANES), 1)
    am = pos * fm_ref[...]
    cm = jnp.cos(am)
    sm = jnp.sin(am)
    cm_ref[...] = cm
    sm_ref[...] = sm
    tk_ref[...] = jnp.where(lane < MLA_ROPE, cm, sm)
    ad = pos * fd_ref[...]
    cd = jnp.cos(ad)
    sd = jnp.sin(ad)
    half = DIL_ROT // 2
    cd_ref[...] = jnp.where(lane < DIL_ROT, cd, 1.0)
    sa_ref[...] = jnp.where(lane < half, 0.0, jnp.where(lane < DIL_ROT, sd, 0.0))
    sb_ref[...] = jnp.where(lane < half, -sd, 0.0)


def _rope_tables(positions):
    m = positions.size
    pos = positions.reshape(m, 1).astype(F32)
    inv_m = ROPE_THETA ** (-jnp.arange(0, MLA_ROPE, 2, dtype=F32) / MLA_ROPE)
    inv_d = ROPE_THETA ** (-jnp.arange(0, DIL_ROT, 2, dtype=F32) / DIL_ROT)
    fm = jnp.tile(inv_m, LANES // inv_m.size).reshape(1, LANES)
    fd = jnp.tile(inv_d, LANES // inv_d.size).reshape(1, LANES)
    tm = min(m, 2048)
    row = pl.BlockSpec((tm, LANES), lambda i: (i, 0))
    vec = pl.BlockSpec((1, LANES), lambda i: (0, 0))
    tab = jax.ShapeDtypeStruct((m, LANES), F32)
    return pl.pallas_call(
        _rope_tables_kernel,
        grid=(m // tm,),
        in_specs=[pl.BlockSpec((tm, 1), lambda i: (i, 0)), vec, vec],
        out_specs=[row] * 6,
        out_shape=[tab] * 6,
        compiler_params=_params("parallel"),
        name="rope_tables",
    )(pos, fm, fd)


def _mla_down_kernel(x_ref, g_ref, w_ref, qn_ref, kvn_ref, tk_ref, ql_ref, ckv_ref, kpe_ref):
    half = x_ref.shape[0] // 2
    for rows in (slice(0, half), slice(half, 2 * half)):
        h = _rms(x_ref[rows, :], g_ref[...]).astype(BF16)
        acc = _dot(h, w_ref[...])
        ql_ref[rows, :] = _rms(acc[:, :MLA_Q_RANK], qn_ref[...]).astype(BF16)
        ckv_ref[rows, :] = _rms(acc[:, MLA_Q_RANK:MLA_Q_RANK + MLA_KV_RANK], kvn_ref[...]).astype(BF16)
        y = acc[:, MLA_Q_RANK + MLA_KV_RANK:] * tk_ref[rows, :]
        z = y + pltpu.roll(y, MLA_ROPE, 1)
        lane = lax.broadcasted_iota(jnp.int32, z.shape, 1)
        kpe_ref[rows, :LANES] = jnp.where(lane < MLA_ROPE, z, 0.0).astype(BF16)
        kpe_ref[rows, LANES:] = jnp.where(lane < MLA_ROPE, 0.0, z).astype(BF16)


def _mla_down(x2, gain, w1, q_norm, kv_norm, tk):
    m = x2.shape[0]
    tm = min(m, TM_MLA_PROJ)
    n1 = w1.shape[1]
    return pl.pallas_call(
        _mla_down_kernel,
        grid=(m // tm,),
        in_specs=[
            pl.BlockSpec((tm, D_MODEL), lambda i: (i, 0)),
            pl.BlockSpec((1, D_MODEL), lambda i: (0, 0)),
            pl.BlockSpec((D_MODEL, n1), lambda i: (0, 0)),
            pl.BlockSpec((1, MLA_Q_RANK), lambda i: (0, 0)),
            pl.BlockSpec((1, MLA_KV_RANK), lambda i: (0, 0)),
            pl.BlockSpec((tm, LANES), lambda i: (i, 0)),
        ],
        out_specs=[
            pl.BlockSpec((tm, MLA_Q_RANK), lambda i: (i, 0)),
            pl.BlockSpec((tm, MLA_KV_RANK), lambda i: (i, 0)),
            pl.BlockSpec((tm, 2 * LANES), lambda i: (i, 0)),
        ],
        out_shape=[
            jax.ShapeDtypeStruct((m, MLA_Q_RANK), BF16),
            jax.ShapeDtypeStruct((m, MLA_KV_RANK), BF16),
            jax.ShapeDtypeStruct((m, 2 * LANES), BF16),
        ],
        compiler_params=_params("parallel"),
        name="mla_down",
    )(x2, gain, w1, q_norm, kv_norm, tk)


MLA_PAIRS = MLA_HEADS // 2
Q_PAIR_IN = 2 * MLA_NOPE + 2 * LANES
Q_PAIR_OUT = 2 * MLA_NOPE + LANES
MLA_Q_PRESCALE = (MLA_NOPE + MLA_ROPE) ** -0.5 * LOG2_E


def _mla_q_up_kernel(a_ref, w_ref, c_ref, s_ref, o_ref):
    a = a_ref[...]
    c = c_ref[...]
    s = s_ref[...]
    for p in range(MLA_PAIRS):
        acc = _dot(a, w_ref[:, p * Q_PAIR_IN:(p + 1) * Q_PAIR_IN])
        nope = acc[:, :2 * MLA_NOPE] * MLA_Q_PRESCALE
        o_ref[:, p * Q_PAIR_OUT:p * Q_PAIR_OUT + 2 * MLA_NOPE] = nope.astype(BF16)
        pe = acc[:, 2 * MLA_NOPE:2 * MLA_NOPE + LANES] * c + acc[:, 2 * MLA_NOPE + LANES:] * s
        o_ref[:, p * Q_PAIR_OUT + 2 * MLA_NOPE:(p + 1) * Q_PAIR_OUT] = (pe * MLA_Q_PRESCALE).astype(BF16)


def _mla_q_up(q_lat, wq, cm, sm):
    m = q_lat.shape[0]
    tm = min(m, TM_MLA_PROJ)
    n_out = MLA_PAIRS * Q_PAIR_OUT
    return pl.pallas_call(
        _mla_q_up_kernel,
        grid=(m // tm,),
        in_specs=[
            pl.BlockSpec((tm, MLA_Q_RANK), lambda i: (i, 0)),
            pl.BlockSpec(wq.shape, lambda i: (0, 0)),
            pl.BlockSpec((tm, LANES), lambda i: (i, 0)),
            pl.BlockSpec((tm, LANES), lambda i: (i, 0)),
        ],
        out_specs=pl.BlockSpec((tm, n_out), lambda i: (i, 0)),
        out_shape=jax.ShapeDtypeStruct((m, n_out), BF16),
        compiler_params=_params("parallel"),
        name="mla_q_up",
    )(q_lat, wq, cm, sm)


def _matmul_bf16_kernel(a_ref, w_ref, o_ref):
    o_ref[...] = _dot(a_ref[...], w_ref[...]).astype(BF16)


def _mla_kv_up(ckv, wkv):
    m = ckv.shape[0]
    tm = min(m, TM_MLA_PROJ)
    n = wkv.shape[1]
    return pl.pallas_call(
        _matmul_bf16_kernel,
        grid=(m // tm,),
        in_specs=[
            pl.BlockSpec((tm, MLA_KV_RANK), lambda i: (i, 0)),
            pl.BlockSpec(wkv.shape, lambda i: (0, 0)),
        ],
        out_specs=pl.BlockSpec((tm, n), lambda i: (i, 0)),
        out_shape=jax.ShapeDtypeStruct((m, n), BF16),
        compiler_params=_params("parallel"),
        name="mla_kv_up",
    )(ckv, wkv)


def _mla_attn_kernel(q_ref, kn_ref, kpe_ref, v_ref, o_ref, m_ref, l_ref, acc_ref, *, tq):
    g = pl.program_id(2)
    row = lax.broadcasted_iota(jnp.int32, (tq, tq), 0)
    col = lax.broadcasted_iota(jnp.int32, (tq, tq), 1)
    causal = col <= row
    heads = (0, 1)
    for sub in range(2):
        qi = 2 * g + sub
        rows = slice(sub * tq, (sub + 1) * tq)
        q_pe = q_ref[rows, 2 * MLA_NOPE:]
        qs = [jnp.concatenate([q_ref[rows, hh * LANES:(hh + 1) * LANES], q_pe], axis=1) for hh in heads]
        m_ref[...] = jnp.full(m_ref.shape, -jnp.inf, F32)
        l_ref[...] = jnp.zeros(l_ref.shape, F32)
        acc_ref[...] = jnp.zeros(acc_ref.shape, F32)

        def step(hh, kb, masked, nkb=1):
            lanes = slice(hh * LANES, (hh + 1) * LANES)
            tk = nkb * tq
            ks = pl.multiple_of(kb * tq, tq)
            k = jnp.concatenate([kn_ref[pl.ds(ks, tk), lanes], kpe_ref[pl.ds(ks, tk), lanes]], axis=1)
            s = _dot_nt(qs[hh], k)
            if masked:
                s = jnp.where(causal, s, -jnp.inf)
            m_prev = m_ref[hh]
            m_next = jnp.maximum(m_prev, jnp.max(s, axis=1, keepdims=True))
            m_wide = jnp.concatenate([m_next] * (tk // LANES), axis=1)
            p = jnp.exp2(s - m_wide)
            alpha = jnp.exp2(m_prev - m_next)
            l_ref[hh] = alpha * l_ref[hh] + jnp.sum(p, axis=1, keepdims=True)
            m_ref[hh] = m_next
            acc_ref[hh] = alpha * acc_ref[hh] + _dot(p.astype(BF16), v_ref[pl.ds(ks, tk), lanes])

        def trip(k2, carry):
            for hh in heads:
                step(hh, 2 * k2, False, nkb=2)
            return carry

        lax.fori_loop(0, g, trip, 0)
        if sub == 1:
            for hh in heads:
                step(hh, qi - 1, False)
        for hh in heads:
            step(hh, qi, True)
        for hh in heads:
            o_ref[rows, hh * LANES:(hh + 1) * LANES] = (acc_ref[hh] / l_ref[hh]).astype(BF16)


def _mla_attention(q, kv, kpe, b, s):
    tq = min(s // 2, TQ_MLA)
    kern = functools.partial(_mla_attn_kernel, tq=tq)
    pair_w = 2 * LANES
    return pl.pallas_call(
        kern,
        grid=(b, MLA_PAIRS, s // (2 * tq)),
        in_specs=[
            pl.BlockSpec((None, 2 * tq, Q_PAIR_OUT), lambda bi, p, g: (bi, g, p)),
            pl.BlockSpec((None, s, pair_w), lambda bi, p, g: (bi, 0, p)),
            pl.BlockSpec((None, s, pair_w), lambda bi, p, g: (bi, 0, 0)),
            pl.BlockSpec((None, s, pair_w), lambda bi, p, g: (bi, 0, MLA_PAIRS + p)),
        ],
        out_specs=pl.BlockSpec((None, 2 * tq, pair_w), lambda bi, p, g: (bi, g, p)),
        out_shape=jax.ShapeDtypeStruct((b, s, MLA_HEADS * MLA_V), BF16),
        scratch_shapes=[
            pltpu.VMEM((2, tq, LANES), F32),
            pltpu.VMEM((2, tq, LANES), F32),
            pltpu.VMEM((2, tq, LANES), F32),
        ],
        compiler_params=_params("parallel", "parallel", "arbitrary"),
        name="mla_attention",
    )(q, kv, kpe, kv)


def _matmul_resid_kernel(a_ref, w_ref, r_ref, o_ref):
    o_ref[...] = r_ref[...] + _dot(a_ref[...], w_ref[...])


def _matmul_resid(a, w, layer, resid):
    m, k = a.shape
    n = w.shape[2]
    tm = min(m, TM_SMALL)
    return pl.pallas_call(
        _matmul_resid_kernel,
        grid=(m // tm,),
        in_specs=[
            pl.BlockSpec((tm, k), lambda i: (i, 0)),
            pl.BlockSpec((None, k, n), lambda i: (layer, 0, 0)),
            pl.BlockSpec((tm, n), lambda i: (i, 0)),
        ],
        out_specs=pl.BlockSpec((tm, n), lambda i: (i, 0)),
        out_shape=jax.ShapeDtypeStruct((m, n), F32),
        compiler_params=_params("parallel"),
        name="out_proj_resid",
    )(a, w, resid)


def _dil_rope(xc, c, sa, sb):
    half = DIL_ROT // 2
    return xc * c + pltpu.roll(xc, half, 1) * sa + pltpu.roll(xc, LANES - half, 1) * sb


def _dil_qkv_kernel(x_ref, g_ref, w_ref, c_ref, sa_ref, sb_ref, o_ref, h_ref, acc_ref, *perm_refs,
                    d, tm, n_rope, nj):
    j = pl.program_id(2)
    rows = tm // d
    if d > 1:
        xs_ref, cs_ref, sas_ref, sbs_ref = perm_refs
    else:
        cs_ref, sas_ref, sbs_ref = c_ref, sa_ref, sb_ref

    def norm_input():
        if d == 1:
            h_ref[...] = _rms(x_ref[...], g_ref[...]).astype(BF16)
            return
        x = x_ref[...]
        rinv = lax.rsqrt(jnp.mean(x * x, axis=-1, keepdims=True) + NORM_EPS)
        for t in range(x_ref.shape[1] // LANES):
            lanes = slice(t * LANES, (t + 1) * LANES)
            slot = t % xs_ref.shape[0]
            xs_ref[slot] = (x_ref[:, lanes] * rinv) * g_ref[:, lanes]
            for r in range(d):
                h_ref[r * rows:(r + 1) * rows, lanes] = xs_ref[slot, pl.ds(r, rows, stride=d), :].astype(BF16)
        for r in range(d):
            sl = pl.ds(r, rows, stride=d)
            dst = slice(r * rows, (r + 1) * rows)
            cs_ref[dst, :] = c_ref[sl, :]
            sas_ref[dst, :] = sa_ref[sl, :]
            sbs_ref[dst, :] = sb_ref[sl, :]

    def matmul():
        acc_ref[...] = _dot(h_ref[...], w_ref[...])

    def emit(rope):
        for r in range(d):
            src = slice(r * rows, (r + 1) * rows)
            if rope:
                c, sa, sb = cs_ref[src, :], sas_ref[src, :], sbs_ref[src, :]
            for t in range(acc_ref.shape[1] // LANES):
                a = acc_ref[src, t * LANES:(t + 1) * LANES]
                if rope:
                    a = _dil_rope(a, c, sa, sb)
                o_ref[r, :, t * LANES:(t + 1) * LANES] = a.astype(BF16)

    @pl.when(j == 0)
    def _():
        norm_input()
        matmul()

    @pl.when(jnp.logical_and(j >= 1, j <= n_rope))
    def _():
        emit(True)
        matmul()

    @pl.when(jnp.logical_and(j > n_rope, j < nj))
    def _():
        emit(False)
        matmul()

    @pl.when(j == nj)
    def _():
        emit(False)


def _dil_qkv(x, gain, w_in, layer, g, cd, sa, sb, d):
    b, s, _ = x.shape
    n = 3 * DIL_HEADS * DIL_HEAD_DIM
    tm = min(s, TM_PROJ)
    tn = TN_PROJ
    col0 = g * (n // tn)
    n_rope = 2 * DIL_HEADS * DIL_HEAD_DIM // tn
    nj = n // tn
    assert n_rope < nj
    kern = functools.partial(_dil_qkv_kernel, d=d, tm=tm, n_rope=n_rope, nj=nj)
    ni = s // tm

    def tile(bi, i, j):
        return _prefetch_tile(bi, i, j == nj, b, ni)

    tab = pl.BlockSpec((None, tm, LANES), lambda bi, i, j: (*tile(bi, i, j), 0))
    perm_scratch = []
    if d > 1:
        perm_scratch = [pltpu.VMEM((PERM_SLOTS, tm, LANES), F32)] + [pltpu.VMEM((tm, LANES), F32)] * 3
    return pl.pallas_call(
        kern,
        grid=(b, s // tm, nj + 1),
        in_specs=[
            pl.BlockSpec((None, tm, D_MODEL), lambda bi, i, j: (*tile(bi, i, j), 0)),
            pl.BlockSpec((1, D_MODEL), lambda bi, i, j: (0, 0)),
            pl.BlockSpec((None, D_MODEL, tn), lambda bi, i, j: (layer, 0, col0 + jnp.where(j == nj, 0, j))),
            tab, tab, tab,
        ],
        out_specs=pl.BlockSpec((None, d, tm // d, tn), lambda bi, i, j: (bi, 0, i, jnp.maximum(j - 1, 0))),
        out_shape=jax.ShapeDtypeStruct((b, d, s // d, n), BF16),
        scratch_shapes=[pltpu.VMEM((tm, D_MODEL), BF16), pltpu.VMEM((tm, tn), F32)] + perm_scratch,
        compiler_params=_params("parallel", "parallel", "arbitrary"),
        name=f"dil_qkv_d{d}",
    )(x, gain, w_in, cd, sa, sb)


def _dil_attn_kernel(q_ref, kc_ref, kp_ref, vc_ref, vp_ref, o_ref, lse_ref, s_ref, p_ref, *, scale, nblk):
    jb = pl.program_id(2)
    qb = DIL_BLOCK
    qi = lax.broadcasted_iota(jnp.int32, (qb, 2 * qb), 0)
    ki = lax.broadcasted_iota(jnp.int32, (qb, 2 * qb), 1)
    lane = lax.broadcasted_iota(jnp.int32, (qb, LANES), 1)

    def keys(cur_ref, prev_ref, blk, sl):
        own = cur_ref[blk * qb:(blk + 1) * qb, sl]
        before = prev_ref[:, sl] if blk == 0 else cur_ref[(blk - 1) * qb:blk * qb, sl]
        return jnp.concatenate([before, own], axis=0)

    for blk in range(nblk):
        for h in range(DIL_HEADS):
            sl = slice(h * DIL_HEAD_DIM, (h + 1) * DIL_HEAD_DIM)
            s_ref[blk, h] = _dot_nt(q_ref[blk * qb:(blk + 1) * qb, sl], keys(kc_ref, kp_ref, blk, sl))
    for blk in range(nblk):
        first_key = jnp.where(jb > 0, qi, qb) if blk == 0 else qi
        band = jnp.logical_and(ki >= first_key, ki <= qi + qb)
        s = jnp.where(band[None], s_ref[blk], -jnp.inf)
        m = jnp.max(s, axis=-1, keepdims=True)
        e = jnp.exp2((s - m) * (scale * LOG2_E))
        l = jnp.sum(e, axis=-1, keepdims=True)
        p_ref[blk] = e.astype(BF16)
        inv = 1.0 / l
        lse = scale * m + jnp.log(l)
        lse_all = jnp.zeros((qb, LANES), F32)
        for h in range(DIL_HEADS):
            lse_all = jnp.where(lane == h, lse[h], lse_all)
        lse_ref[blk * qb:(blk + 1) * qb, :] = lse_all
        for h in range(DIL_HEADS):
            sl = slice(h * DIL_HEAD_DIM, (h + 1) * DIL_HEAD_DIM)
            pv = _dot(p_ref[blk, h], keys(vc_ref, vp_ref, blk, sl))
            o_ref[blk * qb:(blk + 1) * qb, sl] = pv * inv[h]


def _dil_attention(qkv_g, d):
    b, _, l, _ = qkv_g.shape
    hw = DIL_HEADS * DIL_HEAD_DIM
    qb = DIL_BLOCK
    nblk = min(DIL_BLOCKS_PER_STEP, l // qb)
    rows = nblk * qb
    kern = functools.partial(_dil_attn_kernel, scale=DIL_HEAD_DIM ** -0.5, nblk=nblk)

    def cur(c):
        return pl.BlockSpec((None, None, rows, hw), lambda bi, r, jb: (bi, r, jb, c))

    def prev(c):
        return pl.BlockSpec((None, None, qb, hw), lambda bi, r, jb: (bi, r, jnp.maximum(jb * nblk - 1, 0), c))

    return pl.pallas_call(
        kern,
        grid=(b, d, l // rows),
        in_specs=[cur(0), cur(1), prev(1), cur(2), prev(2)],
        out_specs=[
            pl.BlockSpec((None, None, rows, hw), lambda bi, r, jb: (bi, r, jb, 0)),
            pl.BlockSpec((None, None, rows, LANES), lambda bi, r, jb: (bi, r, jb, 0)),
        ],
        out_shape=[
            jax.ShapeDtypeStruct((b, d, l, hw), F32),
            jax.ShapeDtypeStruct((b, d, l, LANES), F32),
        ],
        scratch_shapes=[
            pltpu.VMEM((nblk, DIL_HEADS, qb, 2 * qb), F32),
            pltpu.VMEM((nblk, DIL_HEADS, qb, 2 * qb), BF16),
        ],
        compiler_params=_params("parallel", "parallel", "arbitrary"),
        name=f"dil_attention_d{d}",
    )(qkv_g, qkv_g, qkv_g, qkv_g, qkv_g)


def _dil_merge_kernel(o0_ref, l0_ref, o1_ref, l1_ref, o2_ref, l2_ref, w_ref, x_ref, out_ref,
                      n1_ref, n2_ref, nl1_ref, nl2_ref, a_ref, *, tm, n):
    i = pl.program_id(1)
    d1 = DIL_GROUPS[1][1]
    d2 = DIL_GROUPS[2][1]

    def merge():
        for d, o_ref, l_ref, n_ref, nl_ref in ((d1, o1_ref, l1_ref, n1_ref, nl1_ref),
                                               (d2, o2_ref, l2_ref, n2_ref, nl2_ref)):
            for r in range(d):
                rows = pl.ds(r, tm // d, stride=d)
                nl_ref[rows, :] = l_ref[r]
                for h in range(DIL_HEADS):
                    n_ref[h, rows, :] = o_ref[r, :, h * DIL_HEAD_DIM:(h + 1) * DIL_HEAD_DIM]
        for h in range(DIL_HEADS):
            sl = slice(h * DIL_HEAD_DIM, (h + 1) * DIL_HEAD_DIM)
            a0 = l0_ref[:, h:h + 1]
            a1 = nl1_ref[:, h:h + 1]
            a2 = nl2_ref[:, h:h + 1]
            mx = jnp.maximum(jnp.maximum(a0, a1), a2)
            w0 = jnp.exp(a0 - mx)
            w1 = jnp.exp(a1 - mx)
            w2 = jnp.exp(a2 - mx)
            den = w0 + w1 + w2
            o = (w0 / den) * o0_ref[:, sl] + (w1 / den) * n1_ref[h] + (w2 / den) * n2_ref[h]
            a_ref[:, sl] = o.astype(BF16)

    def project(a):
        out_ref[...] = x_ref[...] + _dot(a, w_ref[...])

    @pl.when(i == 0)
    def _():
        merge()

    @pl.when(jnp.logical_and(i > 0, i < n))
    def _():
        a = a_ref[...]
        merge()
        project(a)

    @pl.when(i == n)
    def _():
        project(a_ref[...])


def _dil_merge_out(outs, lses, wo, layer, x):
    b, s, dm = x.shape
    hw = DIL_HEADS * DIL_HEAD_DIM
    tm = min(s, TM_MERGE)
    d1 = DIL_GROUPS[1][1]
    d2 = DIL_GROUPS[2][1]
    n = s // tm
    kern = functools.partial(_dil_merge_kernel, tm=tm, n=n)

    def merged(i):
        return jnp.minimum(i, n - 1)

    def projected(i):
        return jnp.maximum(i - 1, 0)

    def grp(d, w):
        return pl.BlockSpec((None, d, tm // d, w), lambda bi, i: (bi, 0, merged(i), 0))

    return pl.pallas_call(
        kern,
        grid=(b, n + 1),
        in_specs=[
            pl.BlockSpec((None, None, tm, hw), lambda bi, i: (bi, 0, merged(i), 0)),
            pl.BlockSpec((None, None, tm, LANES), lambda bi, i: (bi, 0, merged(i), 0)),
            grp(d1, hw), grp(d1, LANES), grp(d2, hw), grp(d2, LANES),
            pl.BlockSpec((None, hw, dm), lambda bi, i: (layer, 0, 0)),
            pl.BlockSpec((None, tm, dm), lambda bi, i: (bi, projected(i), 0)),
        ],
        out_specs=pl.BlockSpec((None, tm, dm), lambda bi, i: (bi, projected(i), 0)),
        out_shape=jax.ShapeDtypeStruct((b, s, dm), F32),
        scratch_shapes=[
            pltpu.VMEM((DIL_HEADS, tm, DIL_HEAD_DIM), F32), pltpu.VMEM((DIL_HEADS, tm, DIL_HEAD_DIM), F32),
            pltpu.VMEM((tm, LANES), F32), pltpu.VMEM((tm, LANES), F32),
            pltpu.VMEM((tm, hw), BF16),
        ],
        compiler_params=_params("parallel", "arbitrary"),
        name="dil_merge_out",
    )(outs[0], lses[0], outs[1], lses[1], outs[2], lses[2], wo, x)


def _ffn_kernel(x_ref, g_ref, og_ref, wg_ref, wv_ref, cp_ref, wd_ref, *rest, tm, tf, nf, ni, out_norm, n_cast):
    cast_src = rest[:n_cast]
    o_ref = rest[n_cast]
    cast_dst = rest[n_cast + 1:2 * n_cast + 1]
    h_ref, u_ref, prev_ref = rest[2 * n_cast + 1:]
    k = pl.program_id(1)
    t = lax.rem(k, nf)
    c_down = lax.rem(k + nf - 1, nf)
    row = lax.broadcasted_iota(jnp.int32, (CARRY_ROWS, 1), 0)

    def side_jobs():
        for src, dst in zip(cast_src, cast_dst):
            dst[...] = src[...].astype(BF16)

    def up():
        h = h_ref[...]
        u_ref[:, :tf] = _dot(h, wg_ref[...])
        u_ref[:, tf:] = _dot(h, wv_ref[...])

    def start_tile():
        h_ref[...] = _rms(x_ref[...], g_ref[...]).astype(BF16)
        up()

    def conv(u):
        prev = prev_ref[c_down]
        prev_ref[c_down] = u[tm - CARRY_ROWS:, :]
        p1 = prev[CARRY_ROWS - 1:CARRY_ROWS, :]
        p2 = prev[CARRY_ROWS - 2:CARRY_ROWS - 1, :]
        r1 = pltpu.roll(u, 1, 0)
        r2 = pltpu.roll(u, 2, 0)
        u1 = jnp.concatenate([jnp.where(row == 0, p1, r1[:CARRY_ROWS]), r1[CARRY_ROWS:]], axis=0)
        u2 = jnp.concatenate(
            [jnp.where(row == 0, p2, jnp.where(row == 1, p1, r2[:CARRY_ROWS])), r2[CARRY_ROWS:]], axis=0)
        cp = cp_ref[...]
        acc = cp[CONV_WIDTH:CONV_WIDTH + 1, :] + u2 * cp[0:1, :]
        acc = acc + u1 * cp[1:2, :]
        return acc + u * cp[2:3, :]

    def down(u, first_chunk):
        cv = conv(u)
        gate = cv[:, :tf]
        val = cv[:, tf:]
        act = (gate * (1.0 / (1.0 + jnp.exp(-gate)))) * val
        d = _dot(act.astype(BF16), wd_ref[...])
        if first_chunk:
            o_ref[...] = x_ref[...] + d
        else:
            o_ref[...] += d

    def finish_tile():
        if out_norm:
            o_ref[...] = _rms(o_ref[...], og_ref[...])

    @pl.when(k == 0)
    def _():
        side_jobs()
        prev_ref[...] = jnp.zeros(prev_ref.shape, F32)
        start_tile()

    @pl.when(jnp.logical_and(t == 0, jnp.logical_and(k > 0, k < ni * nf)))
    def _():
        side_jobs()
        u = u_ref[...]
        start_tile()
        down(u, False)
        finish_tile()

    @pl.when(t == 1)
    def _():
        side_jobs()
        u = u_ref[...]
        up()
        down(u, True)

    @pl.when(t >= 2)
    def _():
        side_jobs()
        u = u_ref[...]
        up()
        down(u, False)

    @pl.when(k == ni * nf)
    def _():
        side_jobs()
        down(u_ref[...], False)
        finish_tile()


def _ffn_chunk_order(a, tf):
    lead = a.shape[:-1]
    return a.reshape(*lead, 2, FFN_HIDDEN // tf, tf).swapaxes(-3, -2).reshape(*lead, 2 * FFN_HIDDEN)


def _ffn_conv_params(conv_w, conv_b):
    pad = jnp.zeros((conv_w.shape[0], SUBLANES - CONV_WIDTH - 1, conv_w.shape[2]), F32)
    conv_p = jnp.concatenate([conv_w, conv_b[:, None, :], pad], axis=1)
    return _ffn_chunk_order(conv_p, TF_FFN)


def _ffn(x, gain, out_gain, w_up, up_layer, conv_p, conv_layer, w_down, down_layer, out_norm, casts):
    b, s, _ = x.shape
    tm = min(s, TM_FFN)
    tf = TF_FFN
    nf = FFN_HIDDEN // tf
    ni = s // tm
    assert nf >= 2
    nk = ni * nf + 1
    kern = functools.partial(_ffn_kernel, tm=tm, tf=tf, nf=nf, ni=ni, out_norm=out_norm, n_cast=len(casts))

    def in_tile(k):
        return jnp.minimum(lax.div(k, nf), ni - 1)

    def out_tile(k):
        return lax.div(jnp.maximum(k - 1, 0), nf)

    def up_chunk(k):
        return lax.rem(k, nf)

    def down_chunk(k):
        return lax.rem(k + nf - 1, nf)

    cast_in, cast_out, cast_shapes = [], [], []
    for w, layer in casts:
        _, r, c = w.shape
        rb = CAST_ROWS
        assert r % rb == 0 and r // rb <= b * nk
        cast_in.append(pl.BlockSpec(
            (None, rb, c), lambda bi, k, layer=layer, n=r // rb: (layer, jnp.minimum(bi * nk + k, n - 1), 0)))
        cast_out.append(pl.BlockSpec(
            (None, rb, c), lambda bi, k, n=r // rb: (0, jnp.minimum(bi * nk + k, n - 1), 0)))
        cast_shapes.append(jax.ShapeDtypeStruct((1, r, c), BF16))

    outs = pl.pallas_call(
        kern,
        grid=(b, nk),
        in_specs=[
            pl.BlockSpec((None, tm, D_MODEL), lambda bi, k: (bi, in_tile(k), 0)),
            pl.BlockSpec((1, D_MODEL), lambda bi, k: (0, 0)),
            pl.BlockSpec((1, D_MODEL), lambda bi, k: (0, 0)),
            pl.BlockSpec((None, D_MODEL, tf), lambda bi, k: (up_layer, 0, up_chunk(k))),
            pl.BlockSpec((None, D_MODEL, tf), lambda bi, k: (up_layer, 0, nf + up_chunk(k))),
            pl.BlockSpec((None, SUBLANES, 2 * tf), lambda bi, k: (conv_layer, 0, down_chunk(k))),
            pl.BlockSpec((None, tf, D_MODEL), lambda bi, k: (down_layer, down_chunk(k), 0)),
        ] + cast_in,
        out_specs=[pl.BlockSpec((None, tm, D_MODEL), lambda bi, k: (bi, out_tile(k), 0))] + cast_out,
        out_shape=[jax.ShapeDtypeStruct(x.shape, F32)] + cast_shapes,
        scratch_shapes=[
            pltpu.VMEM((tm, D_MODEL), BF16),
            pltpu.VMEM((tm, 2 * tf), F32),
            pltpu.VMEM((nf, CARRY_ROWS, 2 * tf), F32),
        ],
        compiler_params=_params("arbitrary", "arbitrary"),
        name="conv_ffn",
    )(x, gain, out_gain, w_up, w_up, conv_p, w_down, *[w for w, _ in casts])
    return outs[0], outs[1:]


def _rotate_half_cols(w):
    half = w.shape[-1] // 2
    return jnp.concatenate([-w[..., half:], w[..., :half]], axis=-1)


def _mla_weights(wq_a, wq_b, wkv_a, wkv_b):
    w_pe = wkv_a[:, MLA_KV_RANK:]
    w1 = jnp.concatenate([wq_a, wkv_a[:, :MLA_KV_RANK], w_pe, _rotate_half_cols(w_pe)], axis=1)
    qb = wq_b.reshape(MLA_Q_RANK, MLA_HEADS, MLA_NOPE + MLA_ROPE)
    nope = qb[:, :, :MLA_NOPE].reshape(MLA_Q_RANK, MLA_PAIRS, 2 * MLA_NOPE)
    pe = qb[:, :, MLA_NOPE:]
    rot = _rotate_half_cols(pe).reshape(MLA_Q_RANK, MLA_PAIRS, LANES)
    pe = pe.reshape(MLA_Q_RANK, MLA_PAIRS, LANES)
    wq = jnp.concatenate([nope, pe, rot], axis=2).reshape(MLA_Q_RANK, MLA_PAIRS * Q_PAIR_IN)
    kvb = wkv_b.reshape(MLA_KV_RANK, MLA_HEADS, MLA_NOPE + MLA_V)
    wkv = jnp.concatenate([kvb[:, :, :MLA_NOPE].reshape(MLA_KV_RANK, -1),
                           kvb[:, :, MLA_NOPE:].reshape(MLA_KV_RANK, -1)], axis=1)
    return w1.astype(BF16), wq.astype(BF16), wkv.astype(BF16)


def _mla_layer(x, gain, tabs, wq_a, q_norm, wq_b, wkv_a, kv_norm, wkv_b, wo, layer):
    b, s, dm = x.shape
    cm, sm, tk = tabs
    w1, wq, wkv = _mla_weights(wq_a, wq_b, wkv_a, wkv_b)
    x2 = x.reshape(b * s, dm)
    q_lat, ckv, kpe = _mla_down(x2, gain.reshape(1, dm), w1, q_norm.reshape(1, -1), kv_norm.reshape(1, -1), tk)
    q = _mla_q_up(q_lat, wq, cm, sm)
    kv = _mla_kv_up(ckv, wkv)
    o = _mla_attention(q.reshape(b, s, -1), kv.reshape(b, s, -1), kpe.reshape(b, s, -1), b, s)
    return _matmul_resid(o.reshape(b * s, -1), wo, layer, x2).reshape(b, s, dm)


def _dil_layer(x, gain, tabs, w_in, wo, layer):
    b, s, dm = x.shape
    cd, sa, sb = (t.reshape(b, s, LANES) for t in tabs)
    outs, lses = [], []
    for g, (_, d) in enumerate(DIL_GROUPS):
        qkv_g = _dil_qkv(x, gain.reshape(1, dm), w_in, 0, g, cd, sa, sb, d)
        o, lse = _dil_attention(qkv_g, d)
        outs.append(o)
        lses.append(lse)
    return _dil_merge_out(outs, lses, wo, layer, x)


def kernel(x, positions, attn_norm, ffn_norm, final_norm, mla_wq_a, mla_q_norm, mla_wq_b, mla_wkv_a,
           mla_kv_norm, mla_wkv_b, mla_wo, dil_w_in, dil_wo, ffn_w_up, ffn_conv_w, ffn_conv_b, ffn_w_down):
    b, s, dm = x.shape
    cm, sm, tk, cd, sa, sb = _rope_tables(positions)
    mla_wo_b = mla_wo.astype(BF16)
    dil_wo_b = dil_wo.astype(BF16)
    ffn_conv_p = _ffn_conv_params(ffn_conv_w, ffn_conv_b)
    up_b, down_b, w_in_b = ffn_w_up[:1].astype(BF16), ffn_w_down[:1].astype(BF16), None
    for i in range(DEPTH):
        j = i // N_MIXERS
        if i % N_MIXERS == 0:
            x = _mla_layer(x, attn_norm[i], (cm, sm, tk), mla_wq_a[j], mla_q_norm[j], mla_wq_b[j],
                           mla_wkv_a[j], mla_kv_norm[j], mla_wkv_b[j], mla_wo_b, j)
        else:
            x = _dil_layer(x, attn_norm[i], (cd, sa, sb), w_in_b, dil_wo_b, j)
        last = i == DEPTH - 1
        casts = [] if last else [(ffn_w_up, i + 1), (ffn_w_down, i + 1)]
        if not last and (i + 1) % N_MIXERS == 1:
            casts.append((dil_w_in, (i + 1) // N_MIXERS))
        x, cast = _ffn(x, ffn_norm[i].reshape(1, dm), final_norm.reshape(1, dm), up_b, 0, ffn_conv_p, i,
                       down_b, 0, last, casts)
        if not last:
            up_b, down_b = cast[0], cast[1]
            w_in_b = cast[2] if len(cast) > 2 else None
    return x
```

```python
import functools

import jax
import jax.numpy as jnp
from jax import lax
from jax.experimental import pallas as pl
from jax.experimental.pallas import tpu as pltpu

F32 = jnp.float32
BF16 = jnp.bfloat16

D_MODEL = 2048
DEPTH = 4
N_MIXERS = 2
ROPE_THETA = 500000.0
NORM_EPS = 1e-6
LOG2_E = 1.4426950408889634

MLA_HEADS = 16
MLA_Q_RANK = 512
MLA_KV_RANK = 512
MLA_NOPE = 128
MLA_ROPE = 64
MLA_V = 128

DIL_GROUPS = ((128, 1), (512, 4), (2048, 16))
DIL_HEADS = 16
DIL_HEAD_DIM = 128
DIL_ROT = DIL_HEAD_DIM // 4
DIL_BLOCK = 128
DIL_BLOCKS_PER_STEP = 4

FFN_HIDDEN = 5632
CONV_WIDTH = 3

LANES = 128
SUBLANES = 8
VMEM_BYTES_V7X = 64 * 1024 * 1024
VMEM_LIMIT = VMEM_BYTES_V7X - 8 * 1024 * 1024

TM_PROJ = 1024
TN_PROJ = 1024
TM_SMALL = 512
TM_MLA_PROJ = 1024
TM_MERGE = 256
TQ_MLA = 512
TM_FFN = 512
TF_FFN = 512
CARRY_ROWS = SUBLANES
PERM_SLOTS = 4
CAST_ROWS = 16


def _params(*sem):
    return pltpu.CompilerParams(dimension_semantics=sem, vmem_limit_bytes=VMEM_LIMIT)


def _prefetch_tile(bi, i, last, nb, ni):
    nxt = i + last.astype(jnp.int32)
    wrap = nxt == ni
    return jnp.minimum(bi + wrap.astype(jnp.int32), nb - 1), jnp.where(wrap, 0, nxt)


def _cast_jobs(casts, steps, flat_step):
    ins, outs, shapes = [], [], []
    for w, layer in casts:
        _, r, c = w.shape
        rb = next(v for v in range(CAST_ROWS, r + 1, CAST_ROWS) if r % v == 0 and r // v <= steps)
        n = r // rb
        ins.append(pl.BlockSpec(
            (None, rb, c), lambda *g, layer=layer, n=n: (layer, jnp.minimum(flat_step(*g), n - 1), 0)))
        outs.append(pl.BlockSpec((None, rb, c), lambda *g, n=n: (0, jnp.minimum(flat_step(*g), n - 1), 0)))
        shapes.append(jax.ShapeDtypeStruct((1, r, c), BF16))
    return ins, outs, shapes


def _run_casts(srcs, dsts):
    for src, dst in zip(srcs, dsts):
        dst[...] = src[...].astype(BF16)


def _rms(x, g):
    ms = jnp.mean(x * x, axis=-1, keepdims=True)
    return (x * lax.rsqrt(ms + NORM_EPS)) * g


def _dot(a, b):
    return jnp.dot(a, b, preferred_element_type=F32)


def _dot_nt(a, b):
    return lax.dot_general(a, b, (((1,), (1,)), ((), ())), preferred_element_type=F32)


def _rope_tables_kernel(pos_ref, fm_ref, fd_ref, cm_ref, sm_ref, tk_ref, cd_ref, sa_ref, sb_ref):
    pos = pos_ref[...]
    lane = lax.broadcasted_iota(jnp.int32, (1, LANES), 1)
    am = pos * fm_ref[...]
    cm = jnp.cos(am)
    sm = jnp.sin(am)
    cm_ref[...] = cm
    sm_ref[...] = sm
    tk_ref[...] = jnp.where(lane < MLA_ROPE, cm, sm)
    ad = pos * fd_ref[...]
    cd = jnp.cos(ad)
    sd = jnp.sin(ad)
    half = DIL_ROT // 2
    cd_ref[...] = jnp.where(lane < DIL_ROT, cd, 1.0)
    sa_ref[...] = jnp.where(lane < half, 0.0, jnp.where(lane < DIL_ROT, sd, 0.0))
    sb_ref[...] = jnp.where(lane < half, -sd, 0.0)


def _rope_tables(positions):
    m = positions.size
    pos = positions.reshape(m, 1).astype(F32)
    inv_m = ROPE_THETA ** (-jnp.arange(0, MLA_ROPE, 2, dtype=F32) / MLA_ROPE)
    inv_d = ROPE_THETA ** (-jnp.arange(0, DIL_ROT, 2, dtype=F32) / DIL_ROT)
    fm = jnp.tile(inv_m, LANES // inv_m.size).reshape(1, LANES)
    fd = jnp.tile(inv_d, LANES // inv_d.size).reshape(1, LANES)
    tm = min(m, 2048)
    row = pl.BlockSpec((tm, LANES), lambda i: (i, 0))
    vec = pl.BlockSpec((1, LANES), lambda i: (0, 0))
    tab = jax.ShapeDtypeStruct((m, LANES), F32)
    return pl.pallas_call(
        _rope_tables_kernel,
        grid=(m // tm,),
        in_specs=[pl.BlockSpec((tm, 1), lambda i: (i, 0)), vec, vec],
        out_specs=[row] * 6,
        out_shape=[tab] * 6,
        compiler_params=_params("parallel"),
        name="rope_tables",
    )(pos, fm, fd)


def _mla_down_kernel(x_ref, g_ref, w_ref, qn_ref, kvn_ref, tk_ref, ql_ref, ckv_ref, kpe_ref):
    half = x_ref.shape[0] // 2
    for rows in (slice(0, half), slice(half, 2 * half)):
        h = _rms(x_ref[rows, :], g_ref[...]).astype(BF16)
        acc = _dot(h, w_ref[...])
        ql_ref[rows, :] = _rms(acc[:, :MLA_Q_RANK], qn_ref[...]).astype(BF16)
        ckv_ref[rows, :] = _rms(acc[:, MLA_Q_RANK:MLA_Q_RANK + MLA_KV_RANK], kvn_ref[...]).astype(BF16)
        y = acc[:, MLA_Q_RANK + MLA_KV_RANK:] * tk_ref[rows, :]
        z = y + pltpu.roll(y, MLA_ROPE, 1)
        lane = lax.broadcasted_iota(jnp.int32, z.shape, 1)
        kpe_ref[rows, :LANES] = jnp.where(lane < MLA_ROPE, z, 0.0).astype(BF16)
        kpe_ref[rows, LANES:] = jnp.where(lane < MLA_ROPE, 0.0, z).astype(BF16)


def _mla_down(x2, gain, w1, q_norm, kv_norm, tk):
    m = x2.shape[0]
    tm = min(m, TM_MLA_PROJ)
    n1 = w1.shape[1]
    return pl.pallas_call(
        _mla_down_kernel,
        grid=(m // tm,),
        in_specs=[
            pl.BlockSpec((tm, D_MODEL), lambda i: (i, 0)),
            pl.BlockSpec((1, D_MODEL), lambda i: (0, 0)),
            pl.BlockSpec((D_MODEL, n1), lambda i: (0, 0)),
            pl.BlockSpec((1, MLA_Q_RANK), lambda i: (0, 0)),
            pl.BlockSpec((1, MLA_KV_RANK), lambda i: (0, 0)),
            pl.BlockSpec((tm, LANES), lambda i: (i, 0)),
        ],
        out_specs=[
            pl.BlockSpec((tm, MLA_Q_RANK), lambda i: (i, 0)),
            pl.BlockSpec((tm, MLA_KV_RANK), lambda i: (i, 0)),
            pl.BlockSpec((tm, 2 * LANES), lambda i: (i, 0)),
        ],
        out_shape=[
            jax.ShapeDtypeStruct((m, MLA_Q_RANK), BF16),
            jax.ShapeDtypeStruct((m, MLA_KV_RANK), BF16),
            jax.ShapeDtypeStruct((m, 2 * LANES), BF16),
        ],
        compiler_params=_params("parallel"),
        name="mla_down",
    )(x2, gain, w1, q_norm, kv_norm, tk)


MLA_PAIRS = MLA_HEADS // 2
Q_PAIR_IN = 2 * MLA_NOPE + 2 * LANES
Q_PAIR_OUT = 2 * MLA_NOPE + LANES
MLA_Q_PRESCALE = (MLA_NOPE + MLA_ROPE) ** -0.5 * LOG2_E


def _mla_q_up_kernel(a_ref, w_ref, c_ref, s_ref, o_ref):
    a = a_ref[...]
    c = c_ref[...]
    s = s_ref[...]
    for p in range(MLA_PAIRS):
        acc = _dot(a, w_ref[:, p * Q_PAIR_IN:(p + 1) * Q_PAIR_IN])
        nope = acc[:, :2 * MLA_NOPE] * MLA_Q_PRESCALE
        o_ref[:, p * Q_PAIR_OUT:p * Q_PAIR_OUT + 2 * MLA_NOPE] = nope.astype(BF16)
        pe = acc[:, 2 * MLA_NOPE:2 * MLA_NOPE + LANES] * c + acc[:, 2 * MLA_NOPE + LANES:] * s
        o_ref[:, p * Q_PAIR_OUT + 2 * MLA_NOPE:(p + 1) * Q_PAIR_OUT] = (pe * MLA_Q_PRESCALE).astype(BF16)


def _mla_q_up(q_lat, wq, cm, sm):
    m = q_lat.shape[0]
    tm = min(m, TM_MLA_PROJ)
    n_out = MLA_PAIRS * Q_PAIR_OUT
    return pl.pallas_call(
        _mla_q_up_kernel,
        grid=(m // tm,),
        in_specs=[
            pl.BlockSpec((tm, MLA_Q_RANK), lambda i: (i, 0)),
            pl.BlockSpec(wq.shape, lambda i: (0, 0)),
            pl.BlockSpec((tm, LANES), lambda i: (i, 0)),
            pl.BlockSpec((tm, LANES), lambda i: (i, 0)),
        ],
        out_specs=pl.BlockSpec((tm, n_out), lambda i: (i, 0)),
        out_shape=jax.ShapeDtypeStruct((m, n_out), BF16),
        compiler_params=_params("parallel"),
        name="mla_q_up",
    )(q_lat, wq, cm, sm)


def _matmul_bf16_kernel(a_ref, w_ref, o_ref):
    o_ref[...] = _dot(a_ref[...], w_ref[...]).astype(BF16)


def _mla_kv_up(ckv, wkv):
    m = ckv.shape[0]
    tm = min(m, TM_MLA_PROJ)
    n = wkv.shape[1]
    return pl.pallas_call(
        _matmul_bf16_kernel,
        grid=(m // tm,),
        in_specs=[
            pl.BlockSpec((tm, MLA_KV_RANK), lambda i: (i, 0)),
            pl.BlockSpec(wkv.shape, lambda i: (0, 0)),
        ],
        out_specs=pl.BlockSpec((tm, n), lambda i: (i, 0)),
        out_shape=jax.ShapeDtypeStruct((m, n), BF16),
        compiler_params=_params("parallel"),
        name="mla_kv_up",
    )(ckv, wkv)


def _mla_attn_kernel(q_ref, kn_ref, kpe_ref, v_ref, o_ref, m_ref, l_ref, acc_ref, *, tq):
    g = pl.program_id(2)
    row = lax.broadcasted_iota(jnp.int32, (tq, tq), 0)
    col = lax.broadcasted_iota(jnp.int32, (tq, tq), 1)
    causal = col <= row
    heads = (0, 1)
    for sub in range(2):
        qi = 2 * g + sub
        rows = slice(sub * tq, (sub + 1) * tq)
        q_pe = q_ref[rows, 2 * MLA_NOPE:]
        qs = [jnp.concatenate([q_ref[rows, hh * LANES:(hh + 1) * LANES], q_pe], axis=1) for hh in heads]
        m_ref[...] = jnp.full(m_ref.shape, -jnp.inf, F32)
        l_ref[...] = jnp.zeros(l_ref.shape, F32)
        acc_ref[...] = jnp.zeros(acc_ref.shape, F32)

        def step(hh, kb, masked, nkb=1):
            lanes = slice(hh * LANES, (hh + 1) * LANES)
            tk = nkb * tq
            ks = pl.multiple_of(kb * tq, tq)
            k = jnp.concatenate([kn_ref[pl.ds(ks, tk), lanes], kpe_ref[pl.ds(ks, tk), lanes]], axis=1)
            s = _dot_nt(qs[hh], k)
            if masked:
                s = jnp.where(causal, s, -jnp.inf)
            m_prev = m_ref[hh]
            m_next = jnp.maximum(m_prev, jnp.max(s, axis=1, keepdims=True))
            m_wide = jnp.concatenate([m_next] * (tk // LANES), axis=1)
            p = jnp.exp2(s - m_wide)
            alpha = jnp.exp2(m_prev - m_next)
            l_ref[hh] = alpha * l_ref[hh] + jnp.sum(p, axis=1, keepdims=True)
            m_ref[hh] = m_next
            acc_ref[hh] = alpha * acc_ref[hh] + _dot(p.astype(BF16), v_ref[pl.ds(ks, tk), lanes])

        def trip(k2, carry):
            for hh in heads:
                step(hh, 2 * k2, False, nkb=2)
            return carry

        lax.fori_loop(0, g, trip, 0)
        if sub == 1:
            for hh in heads:
                step(hh, qi - 1, False)
        for hh in heads:
            step(hh, qi, True)
        for hh in heads:
            o_ref[rows, hh * LANES:(hh + 1) * LANES] = (acc_ref[hh] / l_ref[hh]).astype(BF16)


def _mla_attention(q, kv, kpe, b, s):
    tq = min(s // 2, TQ_MLA)
    kern = functools.partial(_mla_attn_kernel, tq=tq)
    pair_w = 2 * LANES
    return pl.pallas_call(
        kern,
        grid=(b, MLA_PAIRS, s // (2 * tq)),
        in_specs=[
            pl.BlockSpec((None, 2 * tq, Q_PAIR_OUT), lambda bi, p, g: (bi, g, p)),
            pl.BlockSpec((None, s, pair_w), lambda bi, p, g: (bi, 0, p)),
            pl.BlockSpec((None, s, pair_w), lambda bi, p, g: (bi, 0, 0)),
            pl.BlockSpec((None, s, pair_w), lambda bi, p, g: (bi, 0, MLA_PAIRS + p)),
        ],
        out_specs=pl.BlockSpec((None, 2 * tq, pair_w), lambda bi, p, g: (bi, g, p)),
        out_shape=jax.ShapeDtypeStruct((b, s, MLA_HEADS * MLA_V), BF16),
        scratch_shapes=[
            pltpu.VMEM((2, tq, LANES), F32),
            pltpu.VMEM((2, tq, LANES), F32),
            pltpu.VMEM((2, tq, LANES), F32),
        ],
        compiler_params=_params("parallel", "parallel", "arbitrary"),
        name="mla_attention",
    )(q, kv, kpe, kv)


def _matmul_resid_kernel(a_ref, w_ref, r_ref, *rest, n_cast):
    o_ref = rest[n_cast]
    _run_casts(rest[:n_cast], rest[n_cast + 1:])
    o_ref[...] = r_ref[...] + _dot(a_ref[...], w_ref[...])


def _matmul_resid(a, w, layer, resid, casts=()):
    m, k = a.shape
    n = w.shape[2]
    tm = min(m, TM_SMALL)
    cast_in, cast_out, cast_shapes = _cast_jobs(casts, m // tm, lambda i: i)
    outs = pl.pallas_call(
        functools.partial(_matmul_resid_kernel, n_cast=len(casts)),
        grid=(m // tm,),
        in_specs=[
            pl.BlockSpec((tm, k), lambda i: (i, 0)),
            pl.BlockSpec((None, k, n), lambda i: (layer, 0, 0), pipeline_mode=pl.Buffered(1)),
            pl.BlockSpec((tm, n), lambda i: (i, 0)),
        ] + cast_in,
        out_specs=[pl.BlockSpec((tm, n), lambda i: (i, 0))] + cast_out,
        out_shape=[jax.ShapeDtypeStruct((m, n), F32)] + cast_shapes,
        compiler_params=_params("arbitrary"),
        name="out_proj_resid",
    )(a, w, resid, *[cw for cw, _ in casts])
    return outs[0], outs[1:]


def _dil_rope(xc, c, sa, sb):
    half = DIL_ROT // 2
    return xc * c + pltpu.roll(xc, half, 1) * sa + pltpu.roll(xc, LANES - half, 1) * sb


def _dil_qkv_kernel(x_ref, g_ref, w_ref, c_ref, sa_ref, sb_ref, o_ref, h_ref, acc_ref, *perm_refs,
                    d, tm, n_rope, nj):
    j = pl.program_id(2)
    rows = tm // d
    if d > 1:
        xs_ref, cs_ref, sas_ref, sbs_ref = perm_refs
    else:
        cs_ref, sas_ref, sbs_ref = c_ref, sa_ref, sb_ref

    def norm_input():
        if d == 1:
            h_ref[...] = _rms(x_ref[...], g_ref[...]).astype(BF16)
            return
        x = x_ref[...]
        rinv = lax.rsqrt(jnp.mean(x * x, axis=-1, keepdims=True) + NORM_EPS)
        for t in range(x_ref.shape[1] // LANES):
            lanes = slice(t * LANES, (t + 1) * LANES)
            slot = t % xs_ref.shape[0]
            xs_ref[slot] = (x_ref[:, lanes] * rinv) * g_ref[:, lanes]
            for r in range(d):
                h_ref[r * rows:(r + 1) * rows, lanes] = xs_ref[slot, pl.ds(r, rows, stride=d), :].astype(BF16)
        for r in range(d):
            sl = pl.ds(r, rows, stride=d)
            dst = slice(r * rows, (r + 1) * rows)
            cs_ref[dst, :] = c_ref[sl, :]
            sas_ref[dst, :] = sa_ref[sl, :]
            sbs_ref[dst, :] = sb_ref[sl, :]

    def matmul():
        acc_ref[...] = _dot(h_ref[...], w_ref[...])

    def emit(rope):
        for r in range(d):
            src = slice(r * rows, (r + 1) * rows)
            if rope:
                c, sa, sb = cs_ref[src, :], sas_ref[src, :], sbs_ref[src, :]
            for t in range(acc_ref.shape[1] // LANES):
                a = acc_ref[src, t * LANES:(t + 1) * LANES]
                if rope:
                    a = _dil_rope(a, c, sa, sb)
                o_ref[r, :, t * LANES:(t + 1) * LANES] = a.astype(BF16)

    @pl.when(j == 0)
    def _():
        norm_input()
        matmul()

    @pl.when(jnp.logical_and(j >= 1, j <= n_rope))
    def _():
        emit(True)
        matmul()

    @pl.when(jnp.logical_and(j > n_rope, j < nj))
    def _():
        emit(False)
        matmul()

    @pl.when(j == nj)
    def _():
        emit(False)


def _dil_qkv(x, gain, w_in, layer, g, cd, sa, sb, d):
    b, s, _ = x.shape
    n = 3 * DIL_HEADS * DIL_HEAD_DIM
    tm = min(s, TM_PROJ)
    tn = TN_PROJ
    col0 = g * (n // tn)
    n_rope = 2 * DIL_HEADS * DIL_HEAD_DIM // tn
    nj = n // tn
    assert n_rope < nj
    kern = functools.partial(_dil_qkv_kernel, d=d, tm=tm, n_rope=n_rope, nj=nj)
    ni = s // tm

    def tile(bi, i, j):
        return _prefetch_tile(bi, i, j == nj, b, ni)

    tab = pl.BlockSpec((None, tm, LANES), lambda bi, i, j: (*tile(bi, i, j), 0))
    perm_scratch = []
    if d > 1:
        perm_scratch = [pltpu.VMEM((PERM_SLOTS, tm, LANES), F32)] + [pltpu.VMEM((tm, LANES), F32)] * 3
    return pl.pallas_call(
        kern,
        grid=(b, s // tm, nj + 1),
        in_specs=[
            pl.BlockSpec((None, tm, D_MODEL), lambda bi, i, j: (*tile(bi, i, j), 0)),
            pl.BlockSpec((1, D_MODEL), lambda bi, i, j: (0, 0)),
            pl.BlockSpec((None, D_MODEL, tn), lambda bi, i, j: (layer, 0, col0 + jnp.where(j == nj, 0, j))),
            tab, tab, tab,
        ],
        out_specs=pl.BlockSpec((None, d, tm // d, tn), lambda bi, i, j: (bi, 0, i, jnp.maximum(j - 1, 0))),
        out_shape=jax.ShapeDtypeStruct((b, d, s // d, n), BF16),
        scratch_shapes=[pltpu.VMEM((tm, D_MODEL), BF16), pltpu.VMEM((tm, tn), F32)] + perm_scratch,
        compiler_params=_params("parallel", "parallel", "arbitrary"),
        name=f"dil_qkv_d{d}",
    )(x, gain, w_in, cd, sa, sb)


def _dil_attn_kernel(q_ref, kc_ref, kp_ref, vc_ref, vp_ref, o_ref, lse_ref, s_ref, p_ref, *, scale, nblk):
    jb = pl.program_id(2)
    qb = DIL_BLOCK
    qi = lax.broadcasted_iota(jnp.int32, (qb, 2 * qb), 0)
    ki = lax.broadcasted_iota(jnp.int32, (qb, 2 * qb), 1)
    lane = lax.broadcasted_iota(jnp.int32, (qb, LANES), 1)

    def keys(cur_ref, prev_ref, blk, sl):
        own = cur_ref[blk * qb:(blk + 1) * qb, sl]
        before = prev_ref[:, sl] if blk == 0 else cur_ref[(blk - 1) * qb:blk * qb, sl]
        return jnp.concatenate([before, own], axis=0)

    for blk in range(nblk):
        for h in range(DIL_HEADS):
            sl = slice(h * DIL_HEAD_DIM, (h + 1) * DIL_HEAD_DIM)
            s_ref[blk, h] = _dot_nt(q_ref[blk * qb:(blk + 1) * qb, sl], keys(kc_ref, kp_ref, blk, sl))
    for blk in range(nblk):
        first_key = jnp.where(jb > 0, qi, qb) if blk == 0 else qi
        band = jnp.logical_and(ki >= first_key, ki <= qi + qb)
        s = jnp.where(band[None], s_ref[blk], -jnp.inf)
        m = jnp.max(s, axis=-1, keepdims=True)
        e = jnp.exp2((s - m) * (scale * LOG2_E))
        l = jnp.sum(e, axis=-1, keepdims=True)
        p_ref[blk] = e.astype(BF16)
        inv = 1.0 / l
        lse = scale * m + jnp.log(l)
        lse_all = jnp.zeros((qb, LANES), F32)
        for h in range(DIL_HEADS):
            lse_all = jnp.where(lane == h, lse[h], lse_all)
        lse_ref[blk * qb:(blk + 1) * qb, :] = lse_all
        for h in range(DIL_HEADS):
            sl = slice(h * DIL_HEAD_DIM, (h + 1) * DIL_HEAD_DIM)
            pv = _dot(p_ref[blk, h], keys(vc_ref, vp_ref, blk, sl))
            o_ref[blk * qb:(blk + 1) * qb, sl] = pv * inv[h]


def _dil_attention(qkv_g, d):
    b, _, l, _ = qkv_g.shape
    hw = DIL_HEADS * DIL_HEAD_DIM
    qb = DIL_BLOCK
    nblk = min(DIL_BLOCKS_PER_STEP, l // qb)
    rows = nblk * qb
    kern = functools.partial(_dil_attn_kernel, scale=DIL_HEAD_DIM ** -0.5, nblk=nblk)

    def cur(c):
        return pl.BlockSpec((None, None, rows, hw), lambda bi, r, jb: (bi, r, jb, c))

    def prev(c):
        return pl.BlockSpec((None, None, qb, hw), lambda bi, r, jb: (bi, r, jnp.maximum(jb * nblk - 1, 0), c))

    return pl.pallas_call(
        kern,
        grid=(b, d, l // rows),
        in_specs=[cur(0), cur(1), prev(1), cur(2), prev(2)],
        out_specs=[
            pl.BlockSpec((None, None, rows, hw), lambda bi, r, jb: (bi, r, jb, 0)),
            pl.BlockSpec((None, None, rows, LANES), lambda bi, r, jb: (bi, r, jb, 0)),
        ],
        out_shape=[
            jax.ShapeDtypeStruct((b, d, l, hw), F32),
            jax.ShapeDtypeStruct((b, d, l, LANES), F32),
        ],
        scratch_shapes=[
            pltpu.VMEM((nblk, DIL_HEADS, qb, 2 * qb), F32),
            pltpu.VMEM((nblk, DIL_HEADS, qb, 2 * qb), BF16),
        ],
        compiler_params=_params("parallel", "parallel", "arbitrary"),
        name=f"dil_attention_d{d}",
    )(qkv_g, qkv_g, qkv_g, qkv_g, qkv_g)


def _dil_merge_kernel(o0_ref, l0_ref, o1_ref, l1_ref, o2_ref, l2_ref, w_ref, x_ref, out_ref,
                      n1_ref, n2_ref, nl1_ref, nl2_ref, a_ref, *, tm, n):
    i = pl.program_id(1)
    d1 = DIL_GROUPS[1][1]
    d2 = DIL_GROUPS[2][1]

    def merge():
        for d, o_ref, l_ref, n_ref, nl_ref in ((d1, o1_ref, l1_ref, n1_ref, nl1_ref),
                                               (d2, o2_ref, l2_ref, n2_ref, nl2_ref)):
            for r in range(d):
                rows = pl.ds(r, tm // d, stride=d)
                nl_ref[rows, :] = l_ref[r]
                for h in range(DIL_HEADS):
                    n_ref[h, rows, :] = o_ref[r, :, h * DIL_HEAD_DIM:(h + 1) * DIL_HEAD_DIM]
        for h in range(DIL_HEADS):
            sl = slice(h * DIL_HEAD_DIM, (h + 1) * DIL_HEAD_DIM)
            a0 = l0_ref[:, h:h + 1]
            a1 = nl1_ref[:, h:h + 1]
            a2 = nl2_ref[:, h:h + 1]
            mx = jnp.maximum(jnp.maximum(a0, a1), a2)
            w0 = jnp.exp(a0 - mx)
            w1 = jnp.exp(a1 - mx)
            w2 = jnp.exp(a2 - mx)
            den = w0 + w1 + w2
            o = (w0 / den) * o0_ref[:, sl] + (w1 / den) * n1_ref[h] + (w2 / den) * n2_ref[h]
            a_ref[:, sl] = o.astype(BF16)

    def project(a):
        out_ref[...] = x_ref[...] + _dot(a, w_ref[...])

    @pl.when(i == 0)
    def _():
        merge()

    @pl.when(jnp.logical_and(i > 0, i < n))
    def _():
        a = a_ref[...]
        merge()
        project(a)

    @pl.when(i == n)
    def _():
        project(a_ref[...])


def _dil_merge_out(outs, lses, wo, layer, x):
    b, s, dm = x.shape
    hw = DIL_HEADS * DIL_HEAD_DIM
    tm = min(s, TM_MERGE)
    d1 = DIL_GROUPS[1][1]
    d2 = DIL_GROUPS[2][1]
    n = s // tm
    kern = functools.partial(_dil_merge_kernel, tm=tm, n=n)

    def merged(i):
        return jnp.minimum(i, n - 1)

    def projected(i):
        return jnp.maximum(i - 1, 0)

    def grp(d, w):
        return pl.BlockSpec((None, d, tm // d, w), lambda bi, i: (bi, 0, merged(i), 0))

    return pl.pallas_call(
        kern,
        grid=(b, n + 1),
        in_specs=[
            pl.BlockSpec((None, None, tm, hw), lambda bi, i: (bi, 0, merged(i), 0)),
            pl.BlockSpec((None, None, tm, LANES), lambda bi, i: (bi, 0, merged(i), 0)),
            grp(d1, hw), grp(d1, LANES), grp(d2, hw), grp(d2, LANES),
            pl.BlockSpec((None, hw, dm), lambda bi, i: (layer, 0, 0)),
            pl.BlockSpec((None, tm, dm), lambda bi, i: (bi, projected(i), 0)),
        ],
        out_specs=pl.BlockSpec((None, tm, dm), lambda bi, i: (bi, projected(i), 0)),
        out_shape=jax.ShapeDtypeStruct((b, s, dm), F32),
        scratch_shapes=[
            pltpu.VMEM((DIL_HEADS, tm, DIL_HEAD_DIM), F32), pltpu.VMEM((DIL_HEADS, tm, DIL_HEAD_DIM), F32),
            pltpu.VMEM((tm, LANES), F32), pltpu.VMEM((tm, LANES), F32),
            pltpu.VMEM((tm, hw), BF16),
        ],
        compiler_params=_params("parallel", "arbitrary"),
        name="dil_merge_out",
    )(outs[0], lses[0], outs[1], lses[1], outs[2], lses[2], wo, x)


def _ffn_kernel(x_ref, g_ref, og_ref, wg_ref, wv_ref, cp_ref, wd_ref, *rest, tm, tf, nf, ni, out_norm, n_cast):
    cast_src = rest[:n_cast]
    o_ref = rest[n_cast]
    cast_dst = rest[n_cast + 1:2 * n_cast + 1]
    h_ref, u_ref, prev_ref = rest[2 * n_cast + 1:]
    k = pl.program_id(1)
    t = lax.rem(k, nf)
    c_down = lax.rem(k + nf - 1, nf)
    row = lax.broadcasted_iota(jnp.int32, (CARRY_ROWS, 1), 0)

    def side_jobs():
        _run_casts(cast_src, cast_dst)

    def up():
        h = h_ref[...]
        u_ref[:, :tf] = _dot(h, wg_ref[...])
        u_ref[:, tf:] = _dot(h, wv_ref[...])

    def start_tile():
        h_ref[...] = _rms(x_ref[...], g_ref[...]).astype(BF16)
        up()

    def conv(u):
        prev = prev_ref[c_down]
        prev_ref[c_down] = u[tm - CARRY_ROWS:, :]
        p1 = prev[CARRY_ROWS - 1:CARRY_ROWS, :]
        p2 = prev[CARRY_ROWS - 2:CARRY_ROWS - 1, :]
        r1 = pltpu.roll(u, 1, 0)
        r2 = pltpu.roll(u, 2, 0)
        u1 = jnp.concatenate([jnp.where(row == 0, p1, r1[:CARRY_ROWS]), r1[CARRY_ROWS:]], axis=0)
        u2 = jnp.concatenate(
            [jnp.where(row == 0, p2, jnp.where(row == 1, p1, r2[:CARRY_ROWS])), r2[CARRY_ROWS:]], axis=0)
        cp = cp_ref[...]
        acc = cp[CONV_WIDTH:CONV_WIDTH + 1, :] + u2 * cp[0:1, :]
        acc = acc + u1 * cp[1:2, :]
        return acc + u * cp[2:3, :]

    def down(u, first_chunk):
        cv = conv(u)
        gate = cv[:, :tf]
        val = cv[:, tf:]
        act = (gate * (1.0 / (1.0 + jnp.exp(-gate)))) * val
        d = _dot(act.astype(BF16), wd_ref[...])
        if first_chunk:
            o_ref[...] = x_ref[...] + d
        else:
            o_ref[...] += d

    def finish_tile():
        if out_norm:
            o_ref[...] = _rms(o_ref[...], og_ref[...])

    @pl.when(k == 0)
    def _():
        side_jobs()
        prev_ref[...] = jnp.zeros(prev_ref.shape, F32)
        start_tile()

    @pl.when(jnp.logical_and(t == 0, jnp.logical_and(k > 0, k < ni * nf)))
    def _():
        side_jobs()
        u = u_ref[...]
        start_tile()
        down(u, False)
        finish_tile()

    @pl.when(t == 1)
    def _():
        side_jobs()
        u = u_ref[...]
        up()
        down(u, True)

    @pl.when(t >= 2)
    def _():
        side_jobs()
        u = u_ref[...]
        up()
        down(u, False)

    @pl.when(k == ni * nf)
    def _():
        side_jobs()
        down(u_ref[...], False)
        finish_tile()


def _ffn_chunk_order(a, tf):
    lead = a.shape[:-1]
    return a.reshape(*lead, 2, FFN_HIDDEN // tf, tf).swapaxes(-3, -2).reshape(*lead, 2 * FFN_HIDDEN)


def _ffn_conv_params(conv_w, conv_b):
    pad = jnp.zeros((conv_w.shape[0], SUBLANES - CONV_WIDTH - 1, conv_w.shape[2]), F32)
    conv_p = jnp.concatenate([conv_w, conv_b[:, None, :], pad], axis=1)
    return _ffn_chunk_order(conv_p, TF_FFN)


def _ffn(x, gain, out_gain, w_up, up_layer, conv_p, conv_layer, w_down, down_layer, out_norm, casts):
    b, s, _ = x.shape
    tm = min(s, TM_FFN)
    tf = TF_FFN
    nf = FFN_HIDDEN // tf
    ni = s // tm
    assert nf >= 2
    nk = ni * nf + 1
    kern = functools.partial(_ffn_kernel, tm=tm, tf=tf, nf=nf, ni=ni, out_norm=out_norm, n_cast=len(casts))

    def in_tile(k):
        return jnp.minimum(lax.div(k, nf), ni - 1)

    def out_tile(k):
        return lax.div(jnp.maximum(k - 1, 0), nf)

    def up_chunk(k):
        return lax.rem(k, nf)

    def down_chunk(k):
        return lax.rem(k + nf - 1, nf)

    cast_in, cast_out, cast_shapes = _cast_jobs(casts, b * nk, lambda bi, k: bi * nk + k)

    outs = pl.pallas_call(
        kern,
        grid=(b, nk),
        in_specs=[
            pl.BlockSpec((None, tm, D_MODEL), lambda bi, k: (bi, in_tile(k), 0)),
            pl.BlockSpec((1, D_MODEL), lambda bi, k: (0, 0)),
            pl.BlockSpec((1, D_MODEL), lambda bi, k: (0, 0)),
            pl.BlockSpec((None, D_MODEL, tf), lambda bi, k: (up_layer, 0, up_chunk(k))),
            pl.BlockSpec((None, D_MODEL, tf), lambda bi, k: (up_layer, 0, nf + up_chunk(k))),
            pl.BlockSpec((None, SUBLANES, 2 * tf), lambda bi, k: (conv_layer, 0, down_chunk(k))),
            pl.BlockSpec((None, tf, D_MODEL), lambda bi, k: (down_layer, down_chunk(k), 0)),
        ] + cast_in,
        out_specs=[pl.BlockSpec((None, tm, D_MODEL), lambda bi, k: (bi, out_tile(k), 0))] + cast_out,
        out_shape=[jax.ShapeDtypeStruct(x.shape, F32)] + cast_shapes,
        scratch_shapes=[
            pltpu.VMEM((tm, D_MODEL), BF16),
            pltpu.VMEM((tm, 2 * tf), F32),
            pltpu.VMEM((nf, CARRY_ROWS, 2 * tf), F32),
        ],
        compiler_params=_params("arbitrary", "arbitrary"),
        name="conv_ffn",
    )(x, gain, out_gain, w_up, w_up, conv_p, w_down, *[w for w, _ in casts])
    return outs[0], outs[1:]


def _rotate_half_cols(w):
    half = w.shape[-1] // 2
    return jnp.concatenate([-w[..., half:], w[..., :half]], axis=-1)


def _mla_weights(wq_a, wq_b, wkv_a, wkv_b):
    w_pe = wkv_a[:, MLA_KV_RANK:]
    w1 = jnp.concatenate([wq_a, wkv_a[:, :MLA_KV_RANK], w_pe, _rotate_half_cols(w_pe)], axis=1)
    qb = wq_b.reshape(MLA_Q_RANK, MLA_HEADS, MLA_NOPE + MLA_ROPE)
    nope = qb[:, :, :MLA_NOPE].reshape(MLA_Q_RANK, MLA_PAIRS, 2 * MLA_NOPE)
    pe = qb[:, :, MLA_NOPE:]
    rot = _rotate_half_cols(pe).reshape(MLA_Q_RANK, MLA_PAIRS, LANES)
    pe = pe.reshape(MLA_Q_RANK, MLA_PAIRS, LANES)
    wq = jnp.concatenate([nope, pe, rot], axis=2).reshape(MLA_Q_RANK, MLA_PAIRS * Q_PAIR_IN)
    kvb = wkv_b.reshape(MLA_KV_RANK, MLA_HEADS, MLA_NOPE + MLA_V)
    wkv = jnp.concatenate([kvb[:, :, :MLA_NOPE].reshape(MLA_KV_RANK, -1),
                           kvb[:, :, MLA_NOPE:].reshape(MLA_KV_RANK, -1)], axis=1)
    return w1.astype(BF16), wq.astype(BF16), wkv.astype(BF16)


def _mla_layer(x, gain, tabs, wq_a, q_norm, wq_b, wkv_a, kv_norm, wkv_b, wo, layer, casts=()):
    b, s, dm = x.shape
    cm, sm, tk = tabs
    w1, wq, wkv = _mla_weights(wq_a, wq_b, wkv_a, wkv_b)
    x2 = x.reshape(b * s, dm)
    q_lat, ckv, kpe = _mla_down(x2, gain.reshape(1, dm), w1, q_norm.reshape(1, -1), kv_norm.reshape(1, -1), tk)
    q = _mla_q_up(q_lat, wq, cm, sm)
    kv = _mla_kv_up(ckv, wkv)
    o = _mla_attention(q.reshape(b, s, -1), kv.reshape(b, s, -1), kpe.reshape(b, s, -1), b, s)
    out, cast = _matmul_resid(o.reshape(b * s, -1), wo, layer, x2, casts)
    return out.reshape(b, s, dm), cast


def _dil_layer(x, gain, tabs, w_in, wo, layer):
    b, s, dm = x.shape
    cd, sa, sb = (t.reshape(b, s, LANES) for t in tabs)
    outs, lses = [], []
    for g, (_, d) in enumerate(DIL_GROUPS):
        qkv_g = _dil_qkv(x, gain.reshape(1, dm), w_in, 0, g, cd, sa, sb, d)
        o, lse = _dil_attention(qkv_g, d)
        outs.append(o)
        lses.append(lse)
    return _dil_merge_out(outs, lses, wo, layer, x)


def kernel(x, positions, attn_norm, ffn_norm, final_norm, mla_wq_a, mla_q_norm, mla_wq_b, mla_wkv_a,
           mla_kv_norm, mla_wkv_b, mla_wo, dil_w_in, dil_wo, ffn_w_up, ffn_conv_w, ffn_conv_b, ffn_w_down):
    b, s, dm = x.shape
    cm, sm, tk, cd, sa, sb = _rope_tables(positions)
    mla_wo_b = mla_wo.astype(BF16)
    dil_wo_b = dil_wo.astype(BF16)
    ffn_conv_p = _ffn_conv_params(ffn_conv_w, ffn_conv_b)
    up_b = down_b = w_in_b = None
    for i in range(DEPTH):
        j = i // N_MIXERS
        if i % N_MIXERS == 0:
            first = [(ffn_w_up, 0), (ffn_w_down, 0)] if i == 0 else []
            x, cast = _mla_layer(x, attn_norm[i], (cm, sm, tk), mla_wq_a[j], mla_q_norm[j], mla_wq_b[j],
                                 mla_wkv_a[j], mla_kv_norm[j], mla_wkv_b[j], mla_wo_b, j, first)
            if first:
                up_b, down_b = cast
        else:
            x = _dil_layer(x, attn_norm[i], (cd, sa, sb), w_in_b, dil_wo_b, j)
        last = i == DEPTH - 1
        casts = [] if last else [(ffn_w_up, i + 1), (ffn_w_down, i + 1)]
        if not last and (i + 1) % N_MIXERS == 1:
            casts.append((dil_w_in, (i + 1) // N_MIXERS))
        x, cast = _ffn(x, ffn_norm[i].reshape(1, dm), final_norm.reshape(1, dm), up_b, 0, ffn_conv_p, i,
                       down_b, 0, last, casts)
        if not last:
            up_b, down_b = cast[0], cast[1]
            w_in_b = cast[2] if len(cast) > 2 else None
    return x
```

```python
import functools

import jax
import jax.numpy as jnp
from jax import lax
from jax.experimental import pallas as pl
from jax.experimental.pallas import tpu as pltpu

F32 = jnp.float32
BF16 = jnp.bfloat16

D_MODEL = 2048
DEPTH = 4
N_MIXERS = 2
ROPE_THETA = 500000.0
NORM_EPS = 1e-6
LOG2_E = 1.4426950408889634
LN_2 = 0.6931471805599453

MLA_HEADS = 16
MLA_Q_RANK = 512
MLA_KV_RANK = 512
MLA_NOPE = 128
MLA_ROPE = 64
MLA_V = 128

DIL_GROUPS = ((128, 1), (512, 4), (2048, 16))
DIL_HEADS = 16
DIL_HEAD_DIM = 128
DIL_ROT = DIL_HEAD_DIM // 4
DIL_BLOCK = 128
DIL_Q_PRESCALE = DIL_HEAD_DIM ** -0.5 * LOG2_E
DIL_BLOCKS_PER_STEP = 4

FFN_HIDDEN = 5632
CONV_WIDTH = 3

LANES = 128
SUBLANES = 8
VMEM_BYTES_V7X = 64 * 1024 * 1024
VMEM_LIMIT = VMEM_BYTES_V7X - 8 * 1024 * 1024

TM_PROJ = 1024
TN_PROJ = 1024
TM_SMALL = 512
TM_MLA_PROJ = 1024
TM_MERGE = 256
TQ_MLA = 512
TM_FFN = 512
TF_FFN = 512
CARRY_ROWS = SUBLANES
PERM_SLOTS = 4
CAST_ROWS = 16


def _params(*sem):
    return pltpu.CompilerParams(dimension_semantics=sem, vmem_limit_bytes=VMEM_LIMIT)


def _prefetch_tile(bi, i, last, nb, ni):
    nxt = i + last.astype(jnp.int32)
    wrap = nxt == ni
    return jnp.minimum(bi + wrap.astype(jnp.int32), nb - 1), jnp.where(wrap, 0, nxt)


def _cast_jobs(casts, steps, flat_step):
    ins, outs, shapes = [], [], []
    for w, layer in casts:
        _, r, c = w.shape
        rb = next(v for v in range(CAST_ROWS, r + 1, CAST_ROWS) if r % v == 0 and r // v <= steps)
        n = r // rb
        ins.append(pl.BlockSpec(
            (None, rb, c), lambda *g, layer=layer, n=n: (layer, jnp.minimum(flat_step(*g), n - 1), 0)))
        outs.append(pl.BlockSpec((None, rb, c), lambda *g, n=n: (0, jnp.minimum(flat_step(*g), n - 1), 0)))
        shapes.append(jax.ShapeDtypeStruct((1, r, c), BF16))
    return ins, outs, shapes


def _run_casts(srcs, dsts):
    for src, dst in zip(srcs, dsts):
        dst[...] = src[...].astype(BF16)


def _rms(x, g):
    ms = jnp.mean(x * x, axis=-1, keepdims=True)
    return (x * lax.rsqrt(ms + NORM_EPS)) * g


def _dot(a, b):
    return jnp.dot(a, b, preferred_element_type=F32)


def _dot_nt(a, b):
    return lax.dot_general(a, b, (((1,), (1,)), ((), ())), preferred_element_type=F32)


def _rope_tables_kernel(pos_ref, fm_ref, fd_ref, cm_ref, sm_ref, tk_ref, cd_ref, sa_ref, sb_ref):
    pos = pos_ref[...]
    lane = lax.broadcasted_iota(jnp.int32, (1, LANES), 1)
    am = pos * fm_ref[...]
    cm = jnp.cos(am)
    sm = jnp.sin(am)
    cm_ref[...] = cm
    sm_ref[...] = sm
    tk_ref[...] = jnp.where(lane < MLA_ROPE, cm, sm)
    ad = pos * fd_ref[...]
    cd = jnp.cos(ad)
    sd = jnp.sin(ad)
    half = DIL_ROT // 2
    cd_ref[...] = jnp.where(lane < DIL_ROT, cd, 1.0)
    sa_ref[...] = jnp.where(lane < half, 0.0, jnp.where(lane < DIL_ROT, sd, 0.0))
    sb_ref[...] = jnp.where(lane < half, -sd, 0.0)


def _rope_tables(positions):
    m = positions.size
    pos = positions.reshape(m, 1).astype(F32)
    inv_m = ROPE_THETA ** (-jnp.arange(0, MLA_ROPE, 2, dtype=F32) / MLA_ROPE)
    inv_d = ROPE_THETA ** (-jnp.arange(0, DIL_ROT, 2, dtype=F32) / DIL_ROT)
    fm = jnp.tile(inv_m, LANES // inv_m.size).reshape(1, LANES)
    fd = jnp.tile(inv_d, LANES // inv_d.size).reshape(1, LANES)
    tm = min(m, 2048)
    row = pl.BlockSpec((tm, LANES), lambda i: (i, 0))
    vec = pl.BlockSpec((1, LANES), lambda i: (0, 0))
    tab = jax.ShapeDtypeStruct((m, LANES), F32)
    return pl.pallas_call(
        _rope_tables_kernel,
        grid=(m // tm,),
        in_specs=[pl.BlockSpec((tm, 1), lambda i: (i, 0)), vec, vec],
        out_specs=[row] * 6,
        out_shape=[tab] * 6,
        compiler_params=_params("parallel"),
        name="rope_tables",
    )(pos, fm, fd)


def _mla_down_kernel(x_ref, g_ref, w_ref, qn_ref, kvn_ref, tk_ref, ql_ref, ckv_ref, kpe_ref):
    half = x_ref.shape[0] // 2
    for rows in (slice(0, half), slice(half, 2 * half)):
        h = _rms(x_ref[rows, :], g_ref[...]).astype(BF16)
        acc = _dot(h, w_ref[...])
        ql_ref[rows, :] = _rms(acc[:, :MLA_Q_RANK], qn_ref[...]).astype(BF16)
        ckv_ref[rows, :] = _rms(acc[:, MLA_Q_RANK:MLA_Q_RANK + MLA_KV_RANK], kvn_ref[...]).astype(BF16)
        y = acc[:, MLA_Q_RANK + MLA_KV_RANK:] * tk_ref[rows, :]
        z = y + pltpu.roll(y, MLA_ROPE, 1)
        lane = lax.broadcasted_iota(jnp.int32, z.shape, 1)
        kpe_ref[rows, :LANES] = jnp.where(lane < MLA_ROPE, z, 0.0).astype(BF16)
        kpe_ref[rows, LANES:] = jnp.where(lane < MLA_ROPE, 0.0, z).astype(BF16)


def _mla_down(x2, gain, w1, q_norm, kv_norm, tk):
    m = x2.shape[0]
    tm = min(m, TM_MLA_PROJ)
    n1 = w1.shape[1]
    return pl.pallas_call(
        _mla_down_kernel,
        grid=(m // tm,),
        in_specs=[
            pl.BlockSpec((tm, D_MODEL), lambda i: (i, 0)),
            pl.BlockSpec((1, D_MODEL), lambda i: (0, 0)),
            pl.BlockSpec((D_MODEL, n1), lambda i: (0, 0)),
            pl.BlockSpec((1, MLA_Q_RANK), lambda i: (0, 0)),
            pl.BlockSpec((1, MLA_KV_RANK), lambda i: (0, 0)),
            pl.BlockSpec((tm, LANES), lambda i: (i, 0)),
        ],
        out_specs=[
            pl.BlockSpec((tm, MLA_Q_RANK), lambda i: (i, 0)),
            pl.BlockSpec((tm, MLA_KV_RANK), lambda i: (i, 0)),
            pl.BlockSpec((tm, 2 * LANES), lambda i: (i, 0)),
        ],
        out_shape=[
            jax.ShapeDtypeStruct((m, MLA_Q_RANK), BF16),
            jax.ShapeDtypeStruct((m, MLA_KV_RANK), BF16),
            jax.ShapeDtypeStruct((m, 2 * LANES), BF16),
        ],
        compiler_params=_params("parallel"),
        name="mla_down",
    )(x2, gain, w1, q_norm, kv_norm, tk)


MLA_PAIRS = MLA_HEADS // 2
Q_PAIR_IN = 2 * MLA_NOPE + 2 * LANES
Q_PAIR_OUT = 2 * MLA_NOPE + LANES
MLA_Q_PRESCALE = (MLA_NOPE + MLA_ROPE) ** -0.5 * LOG2_E


def _mla_q_up_kernel(a_ref, w_ref, c_ref, s_ref, o_ref):
    a = a_ref[...]
    c = c_ref[...]
    s = s_ref[...]
    for p in range(MLA_PAIRS):
        acc = _dot(a, w_ref[:, p * Q_PAIR_IN:(p + 1) * Q_PAIR_IN])
        nope = acc[:, :2 * MLA_NOPE] * MLA_Q_PRESCALE
        o_ref[:, p * Q_PAIR_OUT:p * Q_PAIR_OUT + 2 * MLA_NOPE] = nope.astype(BF16)
        pe = acc[:, 2 * MLA_NOPE:2 * MLA_NOPE + LANES] * c + acc[:, 2 * MLA_NOPE + LANES:] * s
        o_ref[:, p * Q_PAIR_OUT + 2 * MLA_NOPE:(p + 1) * Q_PAIR_OUT] = (pe * MLA_Q_PRESCALE).astype(BF16)


def _mla_q_up(q_lat, wq, cm, sm):
    m = q_lat.shape[0]
    tm = min(m, TM_MLA_PROJ)
    n_out = MLA_PAIRS * Q_PAIR_OUT
    return pl.pallas_call(
        _mla_q_up_kernel,
        grid=(m // tm,),
        in_specs=[
            pl.BlockSpec((tm, MLA_Q_RANK), lambda i: (i, 0)),
            pl.BlockSpec(wq.shape, lambda i: (0, 0)),
            pl.BlockSpec((tm, LANES), lambda i: (i, 0)),
            pl.BlockSpec((tm, LANES), lambda i: (i, 0)),
        ],
        out_specs=pl.BlockSpec((tm, n_out), lambda i: (i, 0)),
        out_shape=jax.ShapeDtypeStruct((m, n_out), BF16),
        compiler_params=_params("parallel"),
        name="mla_q_up",
    )(q_lat, wq, cm, sm)


def _matmul_bf16_kernel(a_ref, w_ref, o_ref):
    o_ref[...] = _dot(a_ref[...], w_ref[...]).astype(BF16)


def _mla_kv_up(ckv, wkv):
    m = ckv.shape[0]
    tm = min(m, TM_MLA_PROJ)
    n = wkv.shape[1]
    return pl.pallas_call(
        _matmul_bf16_kernel,
        grid=(m // tm,),
        in_specs=[
            pl.BlockSpec((tm, MLA_KV_RANK), lambda i: (i, 0)),
            pl.BlockSpec(wkv.shape, lambda i: (0, 0)),
        ],
        out_specs=pl.BlockSpec((tm, n), lambda i: (i, 0)),
        out_shape=jax.ShapeDtypeStruct((m, n), BF16),
        compiler_params=_params("parallel"),
        name="mla_kv_up",
    )(ckv, wkv)


def _mla_attn_kernel(q_ref, kn_ref, kpe_ref, v_ref, o_ref, m_ref, l_ref, acc_ref, *, tq):
    g = pl.program_id(2)
    row = lax.broadcasted_iota(jnp.int32, (tq, tq), 0)
    col = lax.broadcasted_iota(jnp.int32, (tq, tq), 1)
    causal = col <= row
    heads = (0, 1)
    for sub in range(2):
        qi = 2 * g + sub
        rows = slice(sub * tq, (sub + 1) * tq)
        q_pe = q_ref[rows, 2 * MLA_NOPE:]
        qs = [jnp.concatenate([q_ref[rows, hh * LANES:(hh + 1) * LANES], q_pe], axis=1) for hh in heads]
        m_ref[...] = jnp.full(m_ref.shape, -jnp.inf, F32)
        l_ref[...] = jnp.zeros(l_ref.shape, F32)
        acc_ref[...] = jnp.zeros(acc_ref.shape, F32)

        def step(hh, kb, masked, nkb=1):
            lanes = slice(hh * LANES, (hh + 1) * LANES)
            tk = nkb * tq
            ks = pl.multiple_of(kb * tq, tq)
            k = jnp.concatenate([kn_ref[pl.ds(ks, tk), lanes], kpe_ref[pl.ds(ks, tk), lanes]], axis=1)
            s = _dot_nt(qs[hh], k)
            if masked:
                diag = jnp.where(causal, s[:, tk - tq:], -jnp.inf)
                s = diag if nkb == 1 else jnp.concatenate([s[:, :tk - tq], diag], axis=1)
            m_prev = m_ref[hh]
            m_next = jnp.maximum(m_prev, jnp.max(s, axis=1, keepdims=True))
            m_wide = jnp.concatenate([m_next] * (tk // LANES), axis=1)
            p = jnp.exp2(s - m_wide)
            alpha = jnp.exp2(m_prev - m_next)
            l_ref[hh] = alpha * l_ref[hh] + jnp.sum(p, axis=1, keepdims=True)
            m_ref[hh] = m_next
            acc_ref[hh] = alpha * acc_ref[hh] + _dot(p.astype(BF16), v_ref[pl.ds(ks, tk), lanes])

        def trip(k2, carry):
            for hh in heads:
                step(hh, 2 * k2, False, nkb=2)
            return carry

        lax.fori_loop(0, g, trip, 0)
        for hh in heads:
            step(hh, qi - sub, True, nkb=1 + sub)
        for hh in heads:
            o_ref[rows, hh * LANES:(hh + 1) * LANES] = (acc_ref[hh] / l_ref[hh]).astype(BF16)


def _mla_attention(q, kv, kpe, b, s):
    tq = min(s // 2, TQ_MLA)
    kern = functools.partial(_mla_attn_kernel, tq=tq)
    pair_w = 2 * LANES
    return pl.pallas_call(
        kern,
        grid=(b, MLA_PAIRS, s // (2 * tq)),
        in_specs=[
            pl.BlockSpec((None, 2 * tq, Q_PAIR_OUT), lambda bi, p, g: (bi, g, p)),
            pl.BlockSpec((None, s, pair_w), lambda bi, p, g: (bi, 0, p)),
            pl.BlockSpec((None, s, pair_w), lambda bi, p, g: (bi, 0, 0)),
            pl.BlockSpec((None, s, pair_w), lambda bi, p, g: (bi, 0, MLA_PAIRS + p)),
        ],
        out_specs=pl.BlockSpec((None, 2 * tq, pair_w), lambda bi, p, g: (bi, g, p)),
        out_shape=jax.ShapeDtypeStruct((b, s, MLA_HEADS * MLA_V), BF16),
        scratch_shapes=[
            pltpu.VMEM((2, tq, LANES), F32),
            pltpu.VMEM((2, tq, LANES), F32),
            pltpu.VMEM((2, tq, LANES), F32),
        ],
        compiler_params=_params("parallel", "parallel", "arbitrary"),
        name="mla_attention",
    )(q, kv, kpe, kv)


def _matmul_resid_kernel(a_ref, w_ref, r_ref, *rest, n_cast):
    o_ref = rest[n_cast]
    _run_casts(rest[:n_cast], rest[n_cast + 1:])
    o_ref[...] = r_ref[...] + _dot(a_ref[...], w_ref[...])


def _matmul_resid(a, w, layer, resid, casts=()):
    m, k = a.shape
    n = w.shape[2]
    tm = min(m, TM_SMALL)
    cast_in, cast_out, cast_shapes = _cast_jobs(casts, m // tm, lambda i: i)
    outs = pl.pallas_call(
        functools.partial(_matmul_resid_kernel, n_cast=len(casts)),
        grid=(m // tm,),
        in_specs=[
            pl.BlockSpec((tm, k), lambda i: (i, 0)),
            pl.BlockSpec((None, k, n), lambda i: (layer, 0, 0), pipeline_mode=pl.Buffered(1)),
            pl.BlockSpec((tm, n), lambda i: (i, 0)),
        ] + cast_in,
        out_specs=[pl.BlockSpec((tm, n), lambda i: (i, 0))] + cast_out,
        out_shape=[jax.ShapeDtypeStruct((m, n), F32)] + cast_shapes,
        compiler_params=_params("arbitrary"),
        name="out_proj_resid",
    )(a, w, resid, *[cw for cw, _ in casts])
    return outs[0], outs[1:]


def _dil_rope(xc, c, sa, sb):
    half = DIL_ROT // 2
    return xc * c + pltpu.roll(xc, half, 1) * sa + pltpu.roll(xc, LANES - half, 1) * sb


def _dil_qkv_kernel(x_ref, g_ref, w_ref, c_ref, sa_ref, sb_ref, o_ref, h_ref, acc_ref, *perm_refs,
                    d, tm, n_query, n_rope, nj):
    j = pl.program_id(2)
    rows = tm // d
    if d > 1:
        xs_ref, cs_ref, sas_ref, sbs_ref = perm_refs
    else:
        cs_ref, sas_ref, sbs_ref = c_ref, sa_ref, sb_ref

    def norm_input():
        if d == 1:
            h_ref[...] = _rms(x_ref[...], g_ref[...]).astype(BF16)
            return
        x = x_ref[...]
        rinv = lax.rsqrt(jnp.mean(x * x, axis=-1, keepdims=True) + NORM_EPS)
        for t in range(x_ref.shape[1] // LANES):
            lanes = slice(t * LANES, (t + 1) * LANES)
            slot = t % xs_ref.shape[0]
            xs_ref[slot] = (x_ref[:, lanes] * rinv) * g_ref[:, lanes]
            for r in range(d):
                h_ref[r * rows:(r + 1) * rows, lanes] = xs_ref[slot, pl.ds(r, rows, stride=d), :].astype(BF16)
        for r in range(d):
            sl = pl.ds(r, rows, stride=d)
            dst = slice(r * rows, (r + 1) * rows)
            cs_ref[dst, :] = c_ref[sl, :]
            sas_ref[dst, :] = sa_ref[sl, :]
            sbs_ref[dst, :] = sb_ref[sl, :]

    def matmul():
        acc_ref[...] = _dot(h_ref[...], w_ref[...])

    def emit(rope, query=False):
        for r in range(d):
            src = slice(r * rows, (r + 1) * rows)
            if rope:
                c, sa, sb = cs_ref[src, :], sas_ref[src, :], sbs_ref[src, :]
            for t in range(acc_ref.shape[1] // LANES):
                a = acc_ref[src, t * LANES:(t + 1) * LANES]
                if rope:
                    a = _dil_rope(a, c, sa, sb)
                if query:
                    a = a * DIL_Q_PRESCALE
                o_ref[r, :, t * LANES:(t + 1) * LANES] = a.astype(BF16)

    @pl.when(j == 0)
    def _():
        norm_input()
        matmul()

    @pl.when(jnp.logical_and(j >= 1, j <= n_query))
    def _():
        emit(True, query=True)
        matmul()

    @pl.when(jnp.logical_and(j > n_query, j <= n_rope))
    def _():
        emit(True)
        matmul()

    @pl.when(jnp.logical_and(j > n_rope, j < nj))
    def _():
        emit(False)
        matmul()

    @pl.when(j == nj)
    def _():
        emit(False)


def _dil_qkv(x, gain, w_in, layer, g, cd, sa, sb, d):
    b, s, _ = x.shape
    n = 3 * DIL_HEADS * DIL_HEAD_DIM
    tm = min(s, TM_PROJ)
    tn = TN_PROJ
    col0 = g * (n // tn)
    n_rope = 2 * DIL_HEADS * DIL_HEAD_DIM // tn
    nj = n // tn
    assert n_rope < nj
    n_query = DIL_HEADS * DIL_HEAD_DIM // tn
    kern = functools.partial(_dil_qkv_kernel, d=d, tm=tm, n_query=n_query, n_rope=n_rope, nj=nj)
    ni = s // tm

    def tile(bi, i, j):
        return _prefetch_tile(bi, i, j == nj, b, ni)

    tab = pl.BlockSpec((None, tm, LANES), lambda bi, i, j: (*tile(bi, i, j), 0))
    perm_scratch = []
    if d > 1:
        perm_scratch = [pltpu.VMEM((PERM_SLOTS, tm, LANES), F32)] + [pltpu.VMEM((tm, LANES), F32)] * 3
    return pl.pallas_call(
        kern,
        grid=(b, s // tm, nj + 1),
        in_specs=[
            pl.BlockSpec((None, tm, D_MODEL), lambda bi, i, j: (*tile(bi, i, j), 0)),
            pl.BlockSpec((1, D_MODEL), lambda bi, i, j: (0, 0)),
            pl.BlockSpec((None, D_MODEL, tn), lambda bi, i, j: (layer, 0, col0 + jnp.where(j == nj, 0, j))),
            tab, tab, tab,
        ],
        out_specs=pl.BlockSpec((None, d, tm // d, tn), lambda bi, i, j: (bi, 0, i, jnp.maximum(j - 1, 0))),
        out_shape=jax.ShapeDtypeStruct((b, d, s // d, n), BF16),
        scratch_shapes=[pltpu.VMEM((tm, D_MODEL), BF16), pltpu.VMEM((tm, tn), F32)] + perm_scratch,
        compiler_params=_params("parallel", "parallel", "arbitrary"),
        name=f"dil_qkv_d{d}",
    )(x, gain, w_in, cd, sa, sb)


def _dil_attn_kernel(q_ref, kc_ref, kp_ref, vc_ref, vp_ref, o_ref, lse_ref, s_ref, p_ref, *, nblk):
    jb = pl.program_id(2)
    qb = DIL_BLOCK
    qi = lax.broadcasted_iota(jnp.int32, (qb, 2 * qb), 0)
    ki = lax.broadcasted_iota(jnp.int32, (qb, 2 * qb), 1)
    lane = lax.broadcasted_iota(jnp.int32, (qb, LANES), 1)

    def keys(cur_ref, prev_ref, blk, sl):
        own = cur_ref[blk * qb:(blk + 1) * qb, sl]
        before = prev_ref[:, sl] if blk == 0 else cur_ref[(blk - 1) * qb:blk * qb, sl]
        return jnp.concatenate([before, own], axis=0)

    for blk in range(nblk):
        for h in range(DIL_HEADS):
            sl = slice(h * DIL_HEAD_DIM, (h + 1) * DIL_HEAD_DIM)
            s_ref[blk, h] = _dot_nt(q_ref[blk * qb:(blk + 1) * qb, sl], keys(kc_ref, kp_ref, blk, sl))
    for blk in range(nblk):
        first_key = jnp.where(jb > 0, qi, qb) if blk == 0 else qi
        band = jnp.logical_and(ki >= first_key, ki <= qi + qb)
        s = jnp.where(band[None], s_ref[blk], -jnp.inf)
        m = jnp.max(s, axis=-1, keepdims=True)
        e = jnp.exp2(s - m)
        l = jnp.sum(e, axis=-1, keepdims=True)
        p_ref[blk] = e.astype(BF16)
        inv = 1.0 / l
        lse = LN_2 * m + jnp.log(l)
        lse_all = jnp.zeros((qb, LANES), F32)
        for h in range(DIL_HEADS):
            lse_all = jnp.where(lane == h, lse[h], lse_all)
        lse_ref[blk * qb:(blk + 1) * qb, :] = lse_all
        for h in range(DIL_HEADS):
            sl = slice(h * DIL_HEAD_DIM, (h + 1) * DIL_HEAD_DIM)
            pv = _dot(p_ref[blk, h], keys(vc_ref, vp_ref, blk, sl))
            o_ref[blk * qb:(blk + 1) * qb, sl] = pv * inv[h]


def _dil_attention(qkv_g, d):
    b, _, l, _ = qkv_g.shape
    hw = DIL_HEADS * DIL_HEAD_DIM
    qb = DIL_BLOCK
    nblk = min(DIL_BLOCKS_PER_STEP, l // qb)
    rows = nblk * qb
    kern = functools.partial(_dil_attn_kernel, nblk=nblk)

    def cur(c):
        return pl.BlockSpec((None, None, rows, hw), lambda bi, r, jb: (bi, r, jb, c))

    def prev(c):
        return pl.BlockSpec((None, None, qb, hw), lambda bi, r, jb: (bi, r, jnp.maximum(jb * nblk - 1, 0), c))

    return pl.pallas_call(
        kern,
        grid=(b, d, l // rows),
        in_specs=[cur(0), cur(1), prev(1), cur(2), prev(2)],
        out_specs=[
            pl.BlockSpec((None, None, rows, hw), lambda bi, r, jb: (bi, r, jb, 0)),
            pl.BlockSpec((None, None, rows, LANES), lambda bi, r, jb: (bi, r, jb, 0)),
        ],
        out_shape=[
            jax.ShapeDtypeStruct((b, d, l, hw), F32),
            jax.ShapeDtypeStruct((b, d, l, LANES), F32),
        ],
        scratch_shapes=[
            pltpu.VMEM((nblk, DIL_HEADS, qb, 2 * qb), F32),
            pltpu.VMEM((nblk, DIL_HEADS, qb, 2 * qb), BF16),
        ],
        compiler_params=_params("parallel", "parallel", "arbitrary"),
        name=f"dil_attention_d{d}",
    )(qkv_g, qkv_g, qkv_g, qkv_g, qkv_g)


def _dil_merge_kernel(o0_ref, l0_ref, o1_ref, l1_ref, o2_ref, l2_ref, w_ref, x_ref, out_ref,
                      n1_ref, n2_ref, nl1_ref, nl2_ref, a_ref, *, tm, n):
    i = pl.program_id(1)
    d1 = DIL_GROUPS[1][1]
    d2 = DIL_GROUPS[2][1]

    def merge():
        for d, o_ref, l_ref, n_ref, nl_ref in ((d1, o1_ref, l1_ref, n1_ref, nl1_ref),
                                               (d2, o2_ref, l2_ref, n2_ref, nl2_ref)):
            for r in range(d):
                rows = pl.ds(r, tm // d, stride=d)
                nl_ref[rows, :] = l_ref[r]
                for h in range(DIL_HEADS):
                    n_ref[h, rows, :] = o_ref[r, :, h * DIL_HEAD_DIM:(h + 1) * DIL_HEAD_DIM]
        for h in range(DIL_HEADS):
            sl = slice(h * DIL_HEAD_DIM, (h + 1) * DIL_HEAD_DIM)
            a0 = l0_ref[:, h:h + 1]
            a1 = nl1_ref[:, h:h + 1]
            a2 = nl2_ref[:, h:h + 1]
            mx = jnp.maximum(jnp.maximum(a0, a1), a2)
            w0 = jnp.exp(a0 - mx)
            w1 = jnp.exp(a1 - mx)
            w2 = jnp.exp(a2 - mx)
            den = w0 + w1 + w2
            o = (w0 / den) * o0_ref[:, sl] + (w1 / den) * n1_ref[h] + (w2 / den) * n2_ref[h]
            a_ref[:, sl] = o.astype(BF16)

    def project(a):
        out_ref[...] = x_ref[...] + _dot(a, w_ref[...])

    @pl.when(i == 0)
    def _():
        merge()

    @pl.when(jnp.logical_and(i > 0, i < n))
    def _():
        a = a_ref[...]
        merge()
        project(a)

    @pl.when(i == n)
    def _():
        project(a_ref[...])


def _dil_merge_out(outs, lses, wo, layer, x):
    b, s, dm = x.shape
    hw = DIL_HEADS * DIL_HEAD_DIM
    tm = min(s, TM_MERGE)
    d1 = DIL_GROUPS[1][1]
    d2 = DIL_GROUPS[2][1]
    n = s // tm
    kern = functools.partial(_dil_merge_kernel, tm=tm, n=n)

    def merged(i):
        return jnp.minimum(i, n - 1)

    def projected(i):
        return jnp.maximum(i - 1, 0)

    def grp(d, w):
        return pl.BlockSpec((None, d, tm // d, w), lambda bi, i: (bi, 0, merged(i), 0))

    return pl.pallas_call(
        kern,
        grid=(b, n + 1),
        in_specs=[
            pl.BlockSpec((None, None, tm, hw), lambda bi, i: (bi, 0, merged(i), 0)),
            pl.BlockSpec((None, None, tm, LANES), lambda bi, i: (bi, 0, merged(i), 0)),
            grp(d1, hw), grp(d1, LANES), grp(d2, hw), grp(d2, LANES),
            pl.BlockSpec((None, hw, dm), lambda bi, i: (layer, 0, 0)),
            pl.BlockSpec((None, tm, dm), lambda bi, i: (bi, projected(i), 0)),
        ],
        out_specs=pl.BlockSpec((None, tm, dm), lambda bi, i: (bi, projected(i), 0)),
        out_shape=jax.ShapeDtypeStruct((b, s, dm), F32),
        scratch_shapes=[
            pltpu.VMEM((DIL_HEADS, tm, DIL_HEAD_DIM), F32), pltpu.VMEM((DIL_HEADS, tm, DIL_HEAD_DIM), F32),
            pltpu.VMEM((tm, LANES), F32), pltpu.VMEM((tm, LANES), F32),
            pltpu.VMEM((tm, hw), BF16),
        ],
        compiler_params=_params("parallel", "arbitrary"),
        name="dil_merge_out",
    )(outs[0], lses[0], outs[1], lses[1], outs[2], lses[2], wo, x)


def _ffn_kernel(x_ref, g_ref, og_ref, wg_ref, wv_ref, cp_ref, wd_ref, *rest, tm, tf, nf, ni, out_norm, n_cast):
    cast_src = rest[:n_cast]
    o_ref = rest[n_cast]
    cast_dst = rest[n_cast + 1:2 * n_cast + 1]
    h_ref, u_ref, prev_ref = rest[2 * n_cast + 1:]
    k = pl.program_id(1)
    t = lax.rem(k, nf)
    c_down = lax.rem(k + nf - 1, nf)
    row = lax.broadcasted_iota(jnp.int32, (CARRY_ROWS, 1), 0)

    def side_jobs():
        _run_casts(cast_src, cast_dst)

    def up():
        h = h_ref[...]
        u_ref[:, :tf] = _dot(h, wg_ref[...])
        u_ref[:, tf:] = _dot(h, wv_ref[...])

    def start_tile():
        h_ref[...] = _rms(x_ref[...], g_ref[...]).astype(BF16)
        up()

    def conv(u):
        prev = prev_ref[c_down]
        prev_ref[c_down] = u[tm - CARRY_ROWS:, :]
        p1 = prev[CARRY_ROWS - 1:CARRY_ROWS, :]
        p2 = prev[CARRY_ROWS - 2:CARRY_ROWS - 1, :]
        r1 = pltpu.roll(u, 1, 0)
        r2 = pltpu.roll(u, 2, 0)
        u1 = jnp.concatenate([jnp.where(row == 0, p1, r1[:CARRY_ROWS]), r1[CARRY_ROWS:]], axis=0)
        u2 = jnp.concatenate(
            [jnp.where(row == 0, p2, jnp.where(row == 1, p1, r2[:CARRY_ROWS])), r2[CARRY_ROWS:]], axis=0)
        cp = cp_ref[...]
        acc = cp[CONV_WIDTH:CONV_WIDTH + 1, :] + u2 * cp[0:1, :]
        acc = acc + u1 * cp[1:2, :]
        return acc + u * cp[2:3, :]

    def down(u, first_chunk):
        cv = conv(u)
        gate = cv[:, :tf]
        val = cv[:, tf:]
        act = (gate * (1.0 / (1.0 + jnp.exp(-gate)))) * val
        d = _dot(act.astype(BF16), wd_ref[...])
        if first_chunk:
            o_ref[...] = x_ref[...] + d
        else:
            o_ref[...] += d

    def finish_tile():
        if out_norm:
            o_ref[...] = _rms(o_ref[...], og_ref[...])

    @pl.when(k == 0)
    def _():
        side_jobs()
        prev_ref[...] = jnp.zeros(prev_ref.shape, F32)
        start_tile()

    @pl.when(jnp.logical_and(t == 0, jnp.logical_and(k > 0, k < ni * nf)))
    def _():
        side_jobs()
        u = u_ref[...]
        start_tile()
        down(u, False)
        finish_tile()

    @pl.when(t == 1)
    def _():
        side_jobs()
        u = u_ref[...]
        up()
        down(u, True)

    @pl.when(t >= 2)
    def _():
        side_jobs()
        u = u_ref[...]
        up()
        down(u, False)

    @pl.when(k == ni * nf)
    def _():
        side_jobs()
        down(u_ref[...], False)
        finish_tile()


def _ffn_chunk_order(a, tf):
    lead = a.shape[:-1]
    return a.reshape(*lead, 2, FFN_HIDDEN // tf, tf).swapaxes(-3, -2).reshape(*lead, 2 * FFN_HIDDEN)


def _ffn_conv_params(conv_w, conv_b):
    pad = jnp.zeros((conv_w.shape[0], SUBLANES - CONV_WIDTH - 1, conv_w.shape[2]), F32)
    conv_p = jnp.concatenate([conv_w, conv_b[:, None, :], pad], axis=1)
    return _ffn_chunk_order(conv_p, TF_FFN)


def _ffn(x, gain, out_gain, w_up, up_layer, conv_p, conv_layer, w_down, down_layer, out_norm, casts):
    b, s, _ = x.shape
    tm = min(s, TM_FFN)
    tf = TF_FFN
    nf = FFN_HIDDEN // tf
    ni = s // tm
    assert nf >= 2
    nk = ni * nf + 1
    kern = functools.partial(_ffn_kernel, tm=tm, tf=tf, nf=nf, ni=ni, out_norm=out_norm, n_cast=len(casts))

    def in_tile(k):
        return jnp.minimum(lax.div(k, nf), ni - 1)

    def out_tile(k):
        return lax.div(jnp.maximum(k - 1, 0), nf)

    def up_chunk(k):
        return lax.rem(k, nf)

    def down_chunk(k):
        return lax.rem(k + nf - 1, nf)

    cast_in, cast_out, cast_shapes = _cast_jobs(casts, b * nk, lambda bi, k: bi * nk + k)

    outs = pl.pallas_call(
        kern,
        grid=(b, nk),
        in_specs=[
            pl.BlockSpec((None, tm, D_MODEL), lambda bi, k: (bi, in_tile(k), 0)),
            pl.BlockSpec((1, D_MODEL), lambda bi, k: (0, 0)),
            pl.BlockSpec((1, D_MODEL), lambda bi, k: (0, 0)),
            pl.BlockSpec((None, D_MODEL, tf), lambda bi, k: (up_layer, 0, up_chunk(k))),
            pl.BlockSpec((None, D_MODEL, tf), lambda bi, k: (up_layer, 0, nf + up_chunk(k))),
            pl.BlockSpec((None, SUBLANES, 2 * tf), lambda bi, k: (conv_layer, 0, down_chunk(k))),
            pl.BlockSpec((None, tf, D_MODEL), lambda bi, k: (down_layer, down_chunk(k), 0)),
        ] + cast_in,
        out_specs=[pl.BlockSpec((None, tm, D_MODEL), lambda bi, k: (bi, out_tile(k), 0))] + cast_out,
        out_shape=[jax.ShapeDtypeStruct(x.shape, F32)] + cast_shapes,
        scratch_shapes=[
            pltpu.VMEM((tm, D_MODEL), BF16),
            pltpu.VMEM((tm, 2 * tf), F32),
            pltpu.VMEM((nf, CARRY_ROWS, 2 * tf), F32),
        ],
        compiler_params=_params("arbitrary", "arbitrary"),
        name="conv_ffn",
    )(x, gain, out_gain, w_up, w_up, conv_p, w_down, *[w for w, _ in casts])
    return outs[0], outs[1:]


def _rotate_half_cols(w):
    half = w.shape[-1] // 2
    return jnp.concatenate([-w[..., half:], w[..., :half]], axis=-1)


def _mla_weights(wq_a, wq_b, wkv_a, wkv_b):
    w_pe = wkv_a[:, MLA_KV_RANK:]
    w1 = jnp.concatenate([wq_a, wkv_a[:, :MLA_KV_RANK], w_pe, _rotate_half_cols(w_pe)], axis=1)
    qb = wq_b.reshape(MLA_Q_RANK, MLA_HEADS, MLA_NOPE + MLA_ROPE)
    nope = qb[:, :, :MLA_NOPE].reshape(MLA_Q_RANK, MLA_PAIRS, 2 * MLA_NOPE)
    pe = qb[:, :, MLA_NOPE:]
    rot = _rotate_half_cols(pe).reshape(MLA_Q_RANK, MLA_PAIRS, LANES)
    pe = pe.reshape(MLA_Q_RANK, MLA_PAIRS, LANES)
    wq = jnp.concatenate([nope, pe, rot], axis=2).reshape(MLA_Q_RANK, MLA_PAIRS * Q_PAIR_IN)
    kvb = wkv_b.reshape(MLA_KV_RANK, MLA_HEADS, MLA_NOPE + MLA_V)
    wkv = jnp.concatenate([kvb[:, :, :MLA_NOPE].reshape(MLA_KV_RANK, -1),
                           kvb[:, :, MLA_NOPE:].reshape(MLA_KV_RANK, -1)], axis=1)
    return w1.astype(BF16), wq.astype(BF16), wkv.astype(BF16)


def _mla_layer(x, gain, tabs, wq_a, q_norm, wq_b, wkv_a, kv_norm, wkv_b, wo, layer, casts=()):
    b, s, dm = x.shape
    cm, sm, tk = tabs
    w1, wq, wkv = _mla_weights(wq_a, wq_b, wkv_a, wkv_b)
    x2 = x.reshape(b * s, dm)
    q_lat, ckv, kpe = _mla_down(x2, gain.reshape(1, dm), w1, q_norm.reshape(1, -1), kv_norm.reshape(1, -1), tk)
    q = _mla_q_up(q_lat, wq, cm, sm)
    kv = _mla_kv_up(ckv, wkv)
    o = _mla_attention(q.reshape(b, s, -1), kv.reshape(b, s, -1), kpe.reshape(b, s, -1), b, s)
    out, cast = _matmul_resid(o.reshape(b * s, -1), wo, layer, x2, casts)
    return out.reshape(b, s, dm), cast


def _dil_layer(x, gain, tabs, w_in, wo, layer):
    b, s, dm = x.shape
    cd, sa, sb = (t.reshape(b, s, LANES) for t in tabs)
    outs, lses = [], []
    for g, (_, d) in enumerate(DIL_GROUPS):
        qkv_g = _dil_qkv(x, gain.reshape(1, dm), w_in, 0, g, cd, sa, sb, d)
        o, lse = _dil_attention(qkv_g, d)
        outs.append(o)
        lses.append(lse)
    return _dil_merge_out(outs, lses, wo, layer, x)


def kernel(x, positions, attn_norm, ffn_norm, final_norm, mla_wq_a, mla_q_norm, mla_wq_b, mla_wkv_a,
           mla_kv_norm, mla_wkv_b, mla_wo, dil_w_in, dil_wo, ffn_w_up, ffn_conv_w, ffn_conv_b, ffn_w_down):
    b, s, dm = x.shape
    cm, sm, tk, cd, sa, sb = _rope_tables(positions)
    mla_wo_b = mla_wo.astype(BF16)
    dil_wo_b = dil_wo.astype(BF16)
    ffn_conv_p = _ffn_conv_params(ffn_conv_w, ffn_conv_b)
    up_b = down_b = w_in_b = None
    for i in range(DEPTH):
        j = i // N_MIXERS
        if i % N_MIXERS == 0:
            first = [(ffn_w_up, 0), (ffn_w_down, 0)] if i == 0 else []
            x, cast = _mla_layer(x, attn_norm[i], (cm, sm, tk), mla_wq_a[j], mla_q_norm[j], mla_wq_b[j],
                                 mla_wkv_a[j], mla_kv_norm[j], mla_wkv_b[j], mla_wo_b, j, first)
            if first:
                up_b, down_b = cast
        else:
            x = _dil_layer(x, attn_norm[i], (cd, sa, sb), w_in_b, dil_wo_b, j)
        last = i == DEPTH - 1
        casts = [] if last else [(ffn_w_up, i + 1), (ffn_w_down, i + 1)]
        if not last and (i + 1) % N_MIXERS == 1:
            casts.append((dil_w_in, (i + 1) // N_MIXERS))
        x, cast = _ffn(x, ffn_norm[i].reshape(1, dm), final_norm.reshape(1, dm), up_b, 0, ffn_conv_p, i,
                       down_b, 0, last, casts)
        if not last:
            up_b, down_b = cast[0], cast[1]
            w_in_b = cast[2] if len(cast) > 2 else None
    return x
```

```python
import functools

import jax
import jax.numpy as jnp
from jax import lax
from jax.experimental import pallas as pl
from jax.experimental.pallas import tpu as pltpu

F32 = jnp.float32
BF16 = jnp.bfloat16

D_MODEL = 2048
DEPTH = 4
N_MIXERS = 2
ROPE_THETA = 500000.0
NORM_EPS = 1e-6
LOG2_E = 1.4426950408889634
LN_2 = 0.6931471805599453

MLA_HEADS = 16
MLA_Q_RANK = 512
MLA_KV_RANK = 512
MLA_NOPE = 128
MLA_ROPE = 64
MLA_V = 128

DIL_GROUPS = ((128, 1), (512, 4), (2048, 16))
DIL_HEADS = 16
DIL_HEAD_DIM = 128
DIL_ROT = DIL_HEAD_DIM // 4
DIL_BLOCK = 128
DIL_Q_PRESCALE = DIL_HEAD_DIM ** -0.5 * LOG2_E
DIL_BLOCKS_PER_STEP = 4

FFN_HIDDEN = 5632
CONV_WIDTH = 3

LANES = 128
SUBLANES = 8
VMEM_BYTES_V7X = 64 * 1024 * 1024
VMEM_LIMIT = VMEM_BYTES_V7X - 8 * 1024 * 1024

TM_PROJ = 1024
TN_PROJ = 1024
TM_SMALL = 512
TM_MLA_PROJ = 1024
TM_MERGE = 256
TQ_MLA = 512
TM_FFN = 512
TF_FFN = 512
CARRY_ROWS = SUBLANES
PERM_SLOTS = 4
CAST_ROWS = 16


def _params(*sem):
    return pltpu.CompilerParams(dimension_semantics=sem, vmem_limit_bytes=VMEM_LIMIT)


def _prefetch_tile(bi, i, last, nb, ni):
    nxt = i + last.astype(jnp.int32)
    wrap = nxt == ni
    return jnp.minimum(bi + wrap.astype(jnp.int32), nb - 1), jnp.where(wrap, 0, nxt)


def _cast_jobs(casts, steps, flat_step):
    ins, outs, shapes = [], [], []
    for w, layer in casts:
        _, r, c = w.shape
        rb = next(v for v in range(CAST_ROWS, r + 1, CAST_ROWS) if r % v == 0 and r // v <= steps)
        n = r // rb
        ins.append(pl.BlockSpec(
            (None, rb, c), lambda *g, layer=layer, n=n: (layer, jnp.minimum(flat_step(*g), n - 1), 0)))
        outs.append(pl.BlockSpec((None, rb, c), lambda *g, n=n: (0, jnp.minimum(flat_step(*g), n - 1), 0)))
        shapes.append(jax.ShapeDtypeStruct((1, r, c), BF16))
    return ins, outs, shapes


def _run_casts(srcs, dsts):
    for src, dst in zip(srcs, dsts):
        dst[...] = src[...].astype(BF16)


def _rms(x, g):
    ms = jnp.mean(x * x, axis=-1, keepdims=True)
    return (x * lax.rsqrt(ms + NORM_EPS)) * g


def _dot(a, b):
    return jnp.dot(a, b, preferred_element_type=F32)


def _dot_nt(a, b):
    return lax.dot_general(a, b, (((1,), (1,)), ((), ())), preferred_element_type=F32)


def _rope_tables_kernel(pos_ref, fm_ref, fd_ref, cm_ref, sm_ref, tk_ref, cd_ref, sa_ref, sb_ref):
    pos = pos_ref[...]
    lane = lax.broadcasted_iota(jnp.int32, (1, LANES), 1)
    am = pos * fm_ref[...]
    cm = jnp.cos(am)
    sm = jnp.sin(am)
    cm_ref[...] = cm
    sm_ref[...] = sm
    tk_ref[...] = jnp.where(lane < MLA_ROPE, cm, sm)
    ad = pos * fd_ref[...]
    cd = jnp.cos(ad)
    sd = jnp.sin(ad)
    half = DIL_ROT // 2
    cd_ref[...] = jnp.where(lane < DIL_ROT, cd, 1.0)
    sa_ref[...] = jnp.where(lane < half, 0.0, jnp.where(lane < DIL_ROT, sd, 0.0))
    sb_ref[...] = jnp.where(lane < half, -sd, 0.0)


def _rope_tables(positions):
    m = positions.size
    pos = positions.reshape(m, 1).astype(F32)
    inv_m = ROPE_THETA ** (-jnp.arange(0, MLA_ROPE, 2, dtype=F32) / MLA_ROPE)
    inv_d = ROPE_THETA ** (-jnp.arange(0, DIL_ROT, 2, dtype=F32) / DIL_ROT)
    fm = jnp.tile(inv_m, LANES // inv_m.size).reshape(1, LANES)
    fd = jnp.tile(inv_d, LANES // inv_d.size).reshape(1, LANES)
    tm = min(m, 2048)
    row = pl.BlockSpec((tm, LANES), lambda i: (i, 0))
    vec = pl.BlockSpec((1, LANES), lambda i: (0, 0))
    tab = jax.ShapeDtypeStruct((m, LANES), F32)
    return pl.pallas_call(
        _rope_tables_kernel,
        grid=(m // tm,),
        in_specs=[pl.BlockSpec((tm, 1), lambda i: (i, 0)), vec, vec],
        out_specs=[row] * 6,
        out_shape=[tab] * 6,
        compiler_params=_params("parallel"),
        name="rope_tables",
    )(pos, fm, fd)


def _mla_down_kernel(x_ref, g_ref, w_ref, qn_ref, kvn_ref, tk_ref, ql_ref, ckv_ref, kpe_ref):
    half = x_ref.shape[0] // 2
    for rows in (slice(0, half), slice(half, 2 * half)):
        h = _rms(x_ref[rows, :], g_ref[...]).astype(BF16)
        acc = _dot(h, w_ref[...])
        ql_ref[rows, :] = _rms(acc[:, :MLA_Q_RANK], qn_ref[...]).astype(BF16)
        ckv_ref[rows, :] = _rms(acc[:, MLA_Q_RANK:MLA_Q_RANK + MLA_KV_RANK], kvn_ref[...]).astype(BF16)
        y = acc[:, MLA_Q_RANK + MLA_KV_RANK:] * tk_ref[rows, :]
        z = y + pltpu.roll(y, MLA_ROPE, 1)
        lane = lax.broadcasted_iota(jnp.int32, z.shape, 1)
        kpe_ref[rows, :LANES] = jnp.where(lane < MLA_ROPE, z, 0.0).astype(BF16)
        kpe_ref[rows, LANES:] = jnp.where(lane < MLA_ROPE, 0.0, z).astype(BF16)


def _mla_down(x2, gain, w1, q_norm, kv_norm, tk):
    m = x2.shape[0]
    tm = min(m, TM_MLA_PROJ)
    n1 = w1.shape[1]
    return pl.pallas_call(
        _mla_down_kernel,
        grid=(m // tm,),
        in_specs=[
            pl.BlockSpec((tm, D_MODEL), lambda i: (i, 0)),
            pl.BlockSpec((1, D_MODEL), lambda i: (0, 0)),
            pl.BlockSpec((D_MODEL, n1), lambda i: (0, 0)),
            pl.BlockSpec((1, MLA_Q_RANK), lambda i: (0, 0)),
            pl.BlockSpec((1, MLA_KV_RANK), lambda i: (0, 0)),
            pl.BlockSpec((tm, LANES), lambda i: (i, 0)),
        ],
        out_specs=[
            pl.BlockSpec((tm, MLA_Q_RANK), lambda i: (i, 0)),
            pl.BlockSpec((tm, MLA_KV_RANK), lambda i: (i, 0)),
            pl.BlockSpec((tm, 2 * LANES), lambda i: (i, 0)),
        ],
        out_shape=[
            jax.ShapeDtypeStruct((m, MLA_Q_RANK), BF16),
            jax.ShapeDtypeStruct((m, MLA_KV_RANK), BF16),
            jax.ShapeDtypeStruct((m, 2 * LANES), BF16),
        ],
        compiler_params=_params("parallel"),
        name="mla_down",
    )(x2, gain, w1, q_norm, kv_norm, tk)


MLA_PAIRS = MLA_HEADS // 2
Q_PAIR_IN = 2 * MLA_NOPE + 2 * LANES
Q_PAIR_OUT = 2 * MLA_NOPE + LANES
MLA_Q_PRESCALE = (MLA_NOPE + MLA_ROPE) ** -0.5 * LOG2_E


def _mla_q_up_kernel(a_ref, w_ref, c_ref, s_ref, o_ref):
    a = a_ref[...]
    c = c_ref[...]
    s = s_ref[...]
    for p in range(MLA_PAIRS):
        acc = _dot(a, w_ref[:, p * Q_PAIR_IN:(p + 1) * Q_PAIR_IN])
        nope = acc[:, :2 * MLA_NOPE] * MLA_Q_PRESCALE
        o_ref[:, p * Q_PAIR_OUT:p * Q_PAIR_OUT + 2 * MLA_NOPE] = nope.astype(BF16)
        pe = acc[:, 2 * MLA_NOPE:2 * MLA_NOPE + LANES] * c + acc[:, 2 * MLA_NOPE + LANES:] * s
        o_ref[:, p * Q_PAIR_OUT + 2 * MLA_NOPE:(p + 1) * Q_PAIR_OUT] = (pe * MLA_Q_PRESCALE).astype(BF16)


def _mla_q_up(q_lat, wq, cm, sm):
    m = q_lat.shape[0]
    tm = min(m, TM_MLA_PROJ)
    n_out = MLA_PAIRS * Q_PAIR_OUT
    return pl.pallas_call(
        _mla_q_up_kernel,
        grid=(m // tm,),
        in_specs=[
            pl.BlockSpec((tm, MLA_Q_RANK), lambda i: (i, 0)),
            pl.BlockSpec(wq.shape, lambda i: (0, 0)),
            pl.BlockSpec((tm, LANES), lambda i: (i, 0)),
            pl.BlockSpec((tm, LANES), lambda i: (i, 0)),
        ],
        out_specs=pl.BlockSpec((tm, n_out), lambda i: (i, 0)),
        out_shape=jax.ShapeDtypeStruct((m, n_out), BF16),
        compiler_params=_params("parallel"),
        name="mla_q_up",
    )(q_lat, wq, cm, sm)


def _matmul_bf16_kernel(a_ref, w_ref, o_ref):
    o_ref[...] = _dot(a_ref[...], w_ref[...]).astype(BF16)


def _mla_kv_up(ckv, wkv):
    m = ckv.shape[0]
    tm = min(m, TM_MLA_PROJ)
    n = wkv.shape[1]
    return pl.pallas_call(
        _matmul_bf16_kernel,
        grid=(m // tm,),
        in_specs=[
            pl.BlockSpec((tm, MLA_KV_RANK), lambda i: (i, 0)),
            pl.BlockSpec(wkv.shape, lambda i: (0, 0)),
        ],
        out_specs=pl.BlockSpec((tm, n), lambda i: (i, 0)),
        out_shape=jax.ShapeDtypeStruct((m, n), BF16),
        compiler_params=_params("parallel"),
        name="mla_kv_up",
    )(ckv, wkv)


def _mla_attn_kernel(q_ref, kn_ref, kpe_ref, v_ref, o_ref, m_ref, l_ref, acc_ref, *, tq):
    g = pl.program_id(2)
    row = lax.broadcasted_iota(jnp.int32, (tq, tq), 0)
    col = lax.broadcasted_iota(jnp.int32, (tq, tq), 1)
    causal = col <= row
    heads = (0, 1)
    for sub in range(2):
        qi = 2 * g + sub
        rows = slice(sub * tq, (sub + 1) * tq)
        q_pe = q_ref[rows, 2 * MLA_NOPE:]
        qs = [jnp.concatenate([q_ref[rows, hh * LANES:(hh + 1) * LANES], q_pe], axis=1) for hh in heads]
        m_ref[...] = jnp.full(m_ref.shape, -jnp.inf, F32)
        l_ref[...] = jnp.zeros(l_ref.shape, F32)
        acc_ref[...] = jnp.zeros(acc_ref.shape, F32)

        def step(hh, kb, masked, nkb=1):
            lanes = slice(hh * LANES, (hh + 1) * LANES)
            tk = nkb * tq
            ks = pl.multiple_of(kb * tq, tq)
            k = jnp.concatenate([kn_ref[pl.ds(ks, tk), lanes], kpe_ref[pl.ds(ks, tk), lanes]], axis=1)
            s = _dot_nt(qs[hh], k)
            if masked:
                diag = jnp.where(causal, s[:, tk - tq:], -jnp.inf)
                s = diag if nkb == 1 else jnp.concatenate([s[:, :tk - tq], diag], axis=1)
            m_prev = m_ref[hh]
            m_next = jnp.maximum(m_prev, jnp.max(s, axis=1, keepdims=True))
            m_wide = jnp.concatenate([m_next] * (tk // LANES), axis=1)
            p = jnp.exp2(s - m_wide)
            alpha = jnp.exp2(m_prev - m_next)
            l_ref[hh] = alpha * l_ref[hh] + jnp.sum(p, axis=1, keepdims=True)
            m_ref[hh] = m_next
            acc_ref[hh] = alpha * acc_ref[hh] + _dot(p.astype(BF16), v_ref[pl.ds(ks, tk), lanes])

        def trip(k2, carry):
            for hh in heads:
                step(hh, 2 * k2, False, nkb=2)
            return carry

        lax.fori_loop(0, g, trip, 0)
        for hh in heads:
            step(hh, qi - sub, True, nkb=1 + sub)
        for hh in heads:
            o_ref[rows, hh * LANES:(hh + 1) * LANES] = (acc_ref[hh] / l_ref[hh]).astype(BF16)


def _mla_attention(q, kv, kpe, b, s):
    tq = min(s // 2, TQ_MLA)
    kern = functools.partial(_mla_attn_kernel, tq=tq)
    pair_w = 2 * LANES
    return pl.pallas_call(
        kern,
        grid=(b, MLA_PAIRS, s // (2 * tq)),
        in_specs=[
            pl.BlockSpec((None, 2 * tq, Q_PAIR_OUT), lambda bi, p, g: (bi, g, p)),
            pl.BlockSpec((None, s, pair_w), lambda bi, p, g: (bi, 0, p)),
            pl.BlockSpec((None, s, pair_w), lambda bi, p, g: (bi, 0, 0)),
            pl.BlockSpec((None, s, pair_w), lambda bi, p, g: (bi, 0, MLA_PAIRS + p)),
        ],
        out_specs=pl.BlockSpec((None, 2 * tq, pair_w), lambda bi, p, g: (bi, g, p)),
        out_shape=jax.ShapeDtypeStruct((b, s, MLA_HEADS * MLA_V), BF16),
        scratch_shapes=[
            pltpu.VMEM((2, tq, LANES), F32),
            pltpu.VMEM((2, tq, LANES), F32),
            pltpu.VMEM((2, tq, LANES), F32),
        ],
        compiler_params=_params("parallel", "parallel", "arbitrary"),
        name="mla_attention",
    )(q, kv, kpe, kv)


def _matmul_resid_kernel(a_ref, w_ref, r_ref, *rest, n_cast):
    o_ref = rest[n_cast]
    _run_casts(rest[:n_cast], rest[n_cast + 1:])
    o_ref[...] = r_ref[...] + _dot(a_ref[...], w_ref[...])


def _matmul_resid(a, w, layer, resid, casts=()):
    m, k = a.shape
    n = w.shape[2]
    tm = min(m, TM_SMALL)
    cast_in, cast_out, cast_shapes = _cast_jobs(casts, m // tm, lambda i: i)
    outs = pl.pallas_call(
        functools.partial(_matmul_resid_kernel, n_cast=len(casts)),
        grid=(m // tm,),
        in_specs=[
            pl.BlockSpec((tm, k), lambda i: (i, 0)),
            pl.BlockSpec((None, k, n), lambda i: (layer, 0, 0), pipeline_mode=pl.Buffered(1)),
            pl.BlockSpec((tm, n), lambda i: (i, 0)),
        ] + cast_in,
        out_specs=[pl.BlockSpec((tm, n), lambda i: (i, 0))] + cast_out,
        out_shape=[jax.ShapeDtypeStruct((m, n), F32)] + cast_shapes,
        compiler_params=_params("arbitrary"),
        name="out_proj_resid",
    )(a, w, resid, *[cw for cw, _ in casts])
    return outs[0], outs[1:]


def _dil_rope(xc, c, sa, sb):
    half = DIL_ROT // 2
    return xc * c + pltpu.roll(xc, half, 1) * sa + pltpu.roll(xc, LANES - half, 1) * sb


def _dil_qkv_kernel(x_ref, g_ref, w_ref, c_ref, sa_ref, sb_ref, o_ref, h_ref, acc_ref, *perm_refs,
                    d, tm, n_query, n_rope, nj):
    j = pl.program_id(2)
    rows = tm // d
    if d > 1:
        xs_ref, cs_ref, sas_ref, sbs_ref = perm_refs
    else:
        cs_ref, sas_ref, sbs_ref = c_ref, sa_ref, sb_ref

    def norm_input():
        if d == 1:
            h_ref[...] = _rms(x_ref[...], g_ref[...]).astype(BF16)
            return
        x = x_ref[...]
        rinv = lax.rsqrt(jnp.mean(x * x, axis=-1, keepdims=True) + NORM_EPS)
        for t in range(x_ref.shape[1] // LANES):
            lanes = slice(t * LANES, (t + 1) * LANES)
            slot = t % xs_ref.shape[0]
            xs_ref[slot] = (x_ref[:, lanes] * rinv) * g_ref[:, lanes]
            for r in range(d):
                h_ref[r * rows:(r + 1) * rows, lanes] = xs_ref[slot, pl.ds(r, rows, stride=d), :].astype(BF16)
        for r in range(d):
            sl = pl.ds(r, rows, stride=d)
            dst = slice(r * rows, (r + 1) * rows)
            cs_ref[dst, :] = c_ref[sl, :]
            sas_ref[dst, :] = sa_ref[sl, :]
            sbs_ref[dst, :] = sb_ref[sl, :]

    def matmul():
        acc_ref[...] = _dot(h_ref[...], w_ref[...])

    def emit(rope, query=False):
        for r in range(d):
            src = slice(r * rows, (r + 1) * rows)
            if rope:
                c, sa, sb = cs_ref[src, :], sas_ref[src, :], sbs_ref[src, :]
            for t in range(acc_ref.shape[1] // LANES):
                a = acc_ref[src, t * LANES:(t + 1) * LANES]
                if rope:
                    a = _dil_rope(a, c, sa, sb)
                if query:
                    a = a * DIL_Q_PRESCALE
                o_ref[r, :, t * LANES:(t + 1) * LANES] = a.astype(BF16)

    @pl.when(j == 0)
    def _():
        norm_input()
        matmul()

    @pl.when(jnp.logical_and(j >= 1, j <= n_query))
    def _():
        emit(True, query=True)
        matmul()

    @pl.when(jnp.logical_and(j > n_query, j <= n_rope))
    def _():
        emit(True)
        matmul()

    @pl.when(jnp.logical_and(j > n_rope, j < nj))
    def _():
        emit(False)
        matmul()

    @pl.when(j == nj)
    def _():
        emit(False)


def _dil_qkv(x, gain, w_in, layer, g, cd, sa, sb, d):
    b, s, _ = x.shape
    n = 3 * DIL_HEADS * DIL_HEAD_DIM
    tm = min(s, TM_PROJ)
    tn = TN_PROJ
    col0 = g * (n // tn)
    n_rope = 2 * DIL_HEADS * DIL_HEAD_DIM // tn
    nj = n // tn
    assert n_rope < nj
    n_query = DIL_HEADS * DIL_HEAD_DIM // tn
    kern = functools.partial(_dil_qkv_kernel, d=d, tm=tm, n_query=n_query, n_rope=n_rope, nj=nj)
    ni = s // tm

    def tile(bi, i, j):
        return _prefetch_tile(bi, i, j == nj, b, ni)

    tab = pl.BlockSpec((None, tm, LANES), lambda bi, i, j: (*tile(bi, i, j), 0))
    perm_scratch = []
    if d > 1:
        perm_scratch = [pltpu.VMEM((PERM_SLOTS, tm, LANES), F32)] + [pltpu.VMEM((tm, LANES), F32)] * 3
    return pl.pallas_call(
        kern,
        grid=(b, s // tm, nj + 1),
        in_specs=[
            pl.BlockSpec((None, tm, D_MODEL), lambda bi, i, j: (*tile(bi, i, j), 0)),
            pl.BlockSpec((1, D_MODEL), lambda bi, i, j: (0, 0)),
            pl.BlockSpec((None, D_MODEL, tn), lambda bi, i, j: (layer, 0, col0 + jnp.where(j == nj, 0, j))),
            tab, tab, tab,
        ],
        out_specs=pl.BlockSpec((None, d, tm // d, tn), lambda bi, i, j: (bi, 0, i, jnp.maximum(j - 1, 0))),
        out_shape=jax.ShapeDtypeStruct((b, d, s // d, n), BF16),
        scratch_shapes=[pltpu.VMEM((tm, D_MODEL), BF16), pltpu.VMEM((tm, tn), F32)] + perm_scratch,
        compiler_params=_params("parallel", "parallel", "arbitrary"),
        name=f"dil_qkv_d{d}",
    )(x, gain, w_in, cd, sa, sb)


def _dil_attn_kernel(q_ref, kc_ref, kp_ref, vc_ref, vp_ref, o_ref, lse_ref, s_ref, p_ref, *, nblk):
    jb = pl.program_id(2)
    qb = DIL_BLOCK
    qi = lax.broadcasted_iota(jnp.int32, (qb, 2 * qb), 0)
    ki = lax.broadcasted_iota(jnp.int32, (qb, 2 * qb), 1)
    lane = lax.broadcasted_iota(jnp.int32, (qb, LANES), 1)

    def keys(cur_ref, prev_ref, blk, sl):
        own = cur_ref[blk * qb:(blk + 1) * qb, sl]
        before = prev_ref[:, sl] if blk == 0 else cur_ref[(blk - 1) * qb:blk * qb, sl]
        return jnp.concatenate([before, own], axis=0)

    for blk in range(nblk):
        for h in range(DIL_HEADS):
            sl = slice(h * DIL_HEAD_DIM, (h + 1) * DIL_HEAD_DIM)
            s_ref[blk, h] = _dot_nt(q_ref[blk * qb:(blk + 1) * qb, sl], keys(kc_ref, kp_ref, blk, sl))
    for blk in range(nblk):
        first_key = jnp.where(jb > 0, qi, qb) if blk == 0 else qi
        band = jnp.logical_and(ki >= first_key, ki <= qi + qb)
        s = jnp.where(band[None], s_ref[blk], -jnp.inf)
        m = jnp.max(s, axis=-1, keepdims=True)
        e = jnp.exp2(s - m)
        l = jnp.sum(e, axis=-1, keepdims=True)
        p_ref[blk] = e.astype(BF16)
        inv = 1.0 / l
        lse = LN_2 * m + jnp.log(l)
        lse_all = jnp.zeros((qb, LANES), F32)
        for h in range(DIL_HEADS):
            lse_all = jnp.where(lane == h, lse[h], lse_all)
        lse_ref[blk * qb:(blk + 1) * qb, :] = lse_all
        for h in range(DIL_HEADS):
            sl = slice(h * DIL_HEAD_DIM, (h + 1) * DIL_HEAD_DIM)
            pv = _dot(p_ref[blk, h], keys(vc_ref, vp_ref, blk, sl))
            o_ref[blk * qb:(blk + 1) * qb, sl] = pv * inv[h]


def _dil_attention(qkv_g, d):
    b, _, l, _ = qkv_g.shape
    hw = DIL_HEADS * DIL_HEAD_DIM
    qb = DIL_BLOCK
    nblk = min(DIL_BLOCKS_PER_STEP, l // qb)
    rows = nblk * qb
    kern = functools.partial(_dil_attn_kernel, nblk=nblk)

    def cur(c):
        return pl.BlockSpec((None, None, rows, hw), lambda bi, r, jb: (bi, r, jb, c))

    def prev(c):
        return pl.BlockSpec((None, None, qb, hw), lambda bi, r, jb: (bi, r, jnp.maximum(jb * nblk - 1, 0), c))

    return pl.pallas_call(
        kern,
        grid=(b, d, l // rows),
        in_specs=[cur(0), cur(1), prev(1), cur(2), prev(2)],
        out_specs=[
            pl.BlockSpec((None, None, rows, hw), lambda bi, r, jb: (bi, r, jb, 0)),
            pl.BlockSpec((None, None, rows, LANES), lambda bi, r, jb: (bi, r, jb, 0)),
        ],
        out_shape=[
            jax.ShapeDtypeStruct((b, d, l, hw), F32),
            jax.ShapeDtypeStruct((b, d, l, LANES), F32),
        ],
        scratch_shapes=[
            pltpu.VMEM((nblk, DIL_HEADS, qb, 2 * qb), F32),
            pltpu.VMEM((nblk, DIL_HEADS, qb, 2 * qb), BF16),
        ],
        compiler_params=_params("parallel", "parallel", "arbitrary"),
        name=f"dil_attention_d{d}",
    )(qkv_g, qkv_g, qkv_g, qkv_g, qkv_g)


def _dil_merge_kernel(o0_ref, l0_ref, o1_ref, l1_ref, o2_ref, l2_ref, w_ref, x_ref, out_ref,
                      n1_ref, n2_ref, nl1_ref, nl2_ref, a_ref, *, tm, n):
    i = pl.program_id(1)
    d1 = DIL_GROUPS[1][1]
    d2 = DIL_GROUPS[2][1]

    def merge():
        for d, o_ref, l_ref, n_ref, nl_ref in ((d1, o1_ref, l1_ref, n1_ref, nl1_ref),
                                               (d2, o2_ref, l2_ref, n2_ref, nl2_ref)):
            for r in range(d):
                rows = pl.ds(r, tm // d, stride=d)
                nl_ref[rows, :] = l_ref[r]
                for h in range(DIL_HEADS):
                    n_ref[h, rows, :] = o_ref[r, :, h * DIL_HEAD_DIM:(h + 1) * DIL_HEAD_DIM]
        a0, a1, a2 = l0_ref[...], nl1_ref[...], nl2_ref[...]
        mx = jnp.maximum(jnp.maximum(a0, a1), a2)
        w0 = jnp.exp(a0 - mx)
        w1 = jnp.exp(a1 - mx)
        w2 = jnp.exp(a2 - mx)
        den = w0 + w1 + w2
        w0, w1, w2 = w0 / den, w1 / den, w2 / den
        for h in range(DIL_HEADS):
            sl = slice(h * DIL_HEAD_DIM, (h + 1) * DIL_HEAD_DIM)
            o = w0[:, h:h + 1] * o0_ref[:, sl] + w1[:, h:h + 1] * n1_ref[h] + w2[:, h:h + 1] * n2_ref[h]
            a_ref[:, sl] = o.astype(BF16)

    def project(a):
        out_ref[...] = x_ref[...] + _dot(a, w_ref[...])

    @pl.when(i == 0)
    def _():
        merge()

    @pl.when(jnp.logical_and(i > 0, i < n))
    def _():
        a = a_ref[...]
        merge()
        project(a)

    @pl.when(i == n)
    def _():
        project(a_ref[...])


def _dil_merge_out(outs, lses, wo, layer, x):
    b, s, dm = x.shape
    hw = DIL_HEADS * DIL_HEAD_DIM
    tm = min(s, TM_MERGE)
    d1 = DIL_GROUPS[1][1]
    d2 = DIL_GROUPS[2][1]
    n = s // tm
    kern = functools.partial(_dil_merge_kernel, tm=tm, n=n)

    def merged(i):
        return jnp.minimum(i, n - 1)

    def projected(i):
        return jnp.maximum(i - 1, 0)

    def grp(d, w):
        return pl.BlockSpec((None, d, tm // d, w), lambda bi, i: (bi, 0, merged(i), 0))

    return pl.pallas_call(
        kern,
        grid=(b, n + 1),
        in_specs=[
            pl.BlockSpec((None, None, tm, hw), lambda bi, i: (bi, 0, merged(i), 0)),
            pl.BlockSpec((None, None, tm, LANES), lambda bi, i: (bi, 0, merged(i), 0)),
            grp(d1, hw), grp(d1, LANES), grp(d2, hw), grp(d2, LANES),
            pl.BlockSpec((None, hw, dm), lambda bi, i: (layer, 0, 0)),
            pl.BlockSpec((None, tm, dm), lambda bi, i: (bi, projected(i), 0)),
        ],
        out_specs=pl.BlockSpec((None, tm, dm), lambda bi, i: (bi, projected(i), 0)),
        out_shape=jax.ShapeDtypeStruct((b, s, dm), F32),
        scratch_shapes=[
            pltpu.VMEM((DIL_HEADS, tm, DIL_HEAD_DIM), F32), pltpu.VMEM((DIL_HEADS, tm, DIL_HEAD_DIM), F32),
            pltpu.VMEM((tm, LANES), F32), pltpu.VMEM((tm, LANES), F32),
            pltpu.VMEM((tm, hw), BF16),
        ],
        compiler_params=_params("parallel", "arbitrary"),
        name="dil_merge_out",
    )(outs[0], lses[0], outs[1], lses[1], outs[2], lses[2], wo, x)


def _ffn_kernel(x_ref, g_ref, og_ref, wg_ref, wv_ref, cp_ref, wd_ref, *rest, tm, tf, nf, ni, out_norm, n_cast):
    cast_src = rest[:n_cast]
    o_ref = rest[n_cast]
    cast_dst = rest[n_cast + 1:2 * n_cast + 1]
    h_ref, u_ref, prev_ref = rest[2 * n_cast + 1:]
    k = pl.program_id(1)
    t = lax.rem(k, nf)
    c_down = lax.rem(k + nf - 1, nf)
    row = lax.broadcasted_iota(jnp.int32, (CARRY_ROWS, 1), 0)

    def side_jobs():
        _run_casts(cast_src, cast_dst)

    def up():
        h = h_ref[...]
        u_ref[:, :tf] = _dot(h, wg_ref[...])
        u_ref[:, tf:] = _dot(h, wv_ref[...])

    def start_tile():
        h_ref[...] = _rms(x_ref[...], g_ref[...]).astype(BF16)
        up()

    def conv(u):
        prev = prev_ref[c_down]
        prev_ref[c_down] = u[tm - CARRY_ROWS:, :]
        p1 = prev[CARRY_ROWS - 1:CARRY_ROWS, :]
        p2 = prev[CARRY_ROWS - 2:CARRY_ROWS - 1, :]
        r1 = pltpu.roll(u, 1, 0)
        r2 = pltpu.roll(u, 2, 0)
        u1 = jnp.concatenate([jnp.where(row == 0, p1, r1[:CARRY_ROWS]), r1[CARRY_ROWS:]], axis=0)
        u2 = jnp.concatenate(
            [jnp.where(row == 0, p2, jnp.where(row == 1, p1, r2[:CARRY_ROWS])), r2[CARRY_ROWS:]], axis=0)
        cp = cp_ref[...]
        acc = cp[CONV_WIDTH:CONV_WIDTH + 1, :] + u2 * cp[0:1, :]
        acc = acc + u1 * cp[1:2, :]
        return acc + u * cp[2:3, :]

    def down(u, first_chunk):
        cv = conv(u)
        gate = cv[:, :tf]
        val = cv[:, tf:]
        act = (gate * (1.0 / (1.0 + jnp.exp(-gate)))) * val
        d = _dot(act.astype(BF16), wd_ref[...])
        if first_chunk:
            o_ref[...] = x_ref[...] + d
        else:
            o_ref[...] += d

    def finish_tile():
        if out_norm:
            o_ref[...] = _rms(o_ref[...], og_ref[...])

    @pl.when(k == 0)
    def _():
        side_jobs()
        prev_ref[...] = jnp.zeros(prev_ref.shape, F32)
        start_tile()

    @pl.when(jnp.logical_and(t == 0, jnp.logical_and(k > 0, k < ni * nf)))
    def _():
        side_jobs()
        u = u_ref[...]
        start_tile()
        down(u, False)
        finish_tile()

    @pl.when(t == 1)
    def _():
        side_jobs()
        u = u_ref[...]
        up()
        down(u, True)

    @pl.when(t >= 2)
    def _():
        side_jobs()
        u = u_ref[...]
        up()
        down(u, False)

    @pl.when(k == ni * nf)
    def _():
        side_jobs()
        down(u_ref[...], False)
        finish_tile()


def _ffn_chunk_order(a, tf):
    lead = a.shape[:-1]
    return a.reshape(*lead, 2, FFN_HIDDEN // tf, tf).swapaxes(-3, -2).reshape(*lead, 2 * FFN_HIDDEN)


def _ffn_conv_params(conv_w, conv_b):
    pad = jnp.zeros((conv_w.shape[0], SUBLANES - CONV_WIDTH - 1, conv_w.shape[2]), F32)
    conv_p = jnp.concatenate([conv_w, conv_b[:, None, :], pad], axis=1)
    return _ffn_chunk_order(conv_p, TF_FFN)


def _ffn(x, gain, out_gain, w_up, up_layer, conv_p, conv_layer, w_down, down_layer, out_norm, casts):
    b, s, _ = x.shape
    tm = min(s, TM_FFN)
    tf = TF_FFN
    nf = FFN_HIDDEN // tf
    ni = s // tm
    assert nf >= 2
    nk = ni * nf + 1
    kern = functools.partial(_ffn_kernel, tm=tm, tf=tf, nf=nf, ni=ni, out_norm=out_norm, n_cast=len(casts))

    def in_tile(k):
        return jnp.minimum(lax.div(k, nf), ni - 1)

    def out_tile(k):
        return lax.div(jnp.maximum(k - 1, 0), nf)

    def up_chunk(k):
        return lax.rem(k, nf)

    def down_chunk(k):
        return lax.rem(k + nf - 1, nf)

    cast_in, cast_out, cast_shapes = _cast_jobs(casts, b * nk, lambda bi, k: bi * nk + k)

    outs = pl.pallas_call(
        kern,
        grid=(b, nk),
        in_specs=[
            pl.BlockSpec((None, tm, D_MODEL), lambda bi, k: (bi, in_tile(k), 0)),
            pl.BlockSpec((1, D_MODEL), lambda bi, k: (0, 0)),
            pl.BlockSpec((1, D_MODEL), lambda bi, k: (0, 0)),
            pl.BlockSpec((None, D_MODEL, tf), lambda bi, k: (up_layer, 0, up_chunk(k))),
            pl.BlockSpec((None, D_MODEL, tf), lambda bi, k: (up_layer, 0, nf + up_chunk(k))),
            pl.BlockSpec((None, SUBLANES, 2 * tf), lambda bi, k: (conv_layer, 0, down_chunk(k))),
            pl.BlockSpec((None, tf, D_MODEL), lambda bi, k: (down_layer, down_chunk(k), 0)),
        ] + cast_in,
        out_specs=[pl.BlockSpec((None, tm, D_MODEL), lambda bi, k: (bi, out_tile(k), 0))] + cast_out,
        out_shape=[jax.ShapeDtypeStruct(x.shape, F32)] + cast_shapes,
        scratch_shapes=[
            pltpu.VMEM((tm, D_MODEL), BF16),
            pltpu.VMEM((tm, 2 * tf), F32),
            pltpu.VMEM((nf, CARRY_ROWS, 2 * tf), F32),
        ],
        compiler_params=_params("arbitrary", "arbitrary"),
        name="conv_ffn",
    )(x, gain, out_gain, w_up, w_up, conv_p, w_down, *[w for w, _ in casts])
    return outs[0], outs[1:]


def _rotate_half_cols(w):
    half = w.shape[-1] // 2
    return jnp.concatenate([-w[..., half:], w[..., :half]], axis=-1)


def _mla_weights(wq_a, wq_b, wkv_a, wkv_b):
    w_pe = wkv_a[:, MLA_KV_RANK:]
    w1 = jnp.concatenate([wq_a, wkv_a[:, :MLA_KV_RANK], w_pe, _rotate_half_cols(w_pe)], axis=1)
    qb = wq_b.reshape(MLA_Q_RANK, MLA_HEADS, MLA_NOPE + MLA_ROPE)
    nope = qb[:, :, :MLA_NOPE].reshape(MLA_Q_RANK, MLA_PAIRS, 2 * MLA_NOPE)
    pe = qb[:, :, MLA_NOPE:]
    rot = _rotate_half_cols(pe).reshape(MLA_Q_RANK, MLA_PAIRS, LANES)
    pe = pe.reshape(MLA_Q_RANK, MLA_PAIRS, LANES)
    wq = jnp.concatenate([nope, pe, rot], axis=2).reshape(MLA_Q_RANK, MLA_PAIRS * Q_PAIR_IN)
    kvb = wkv_b.reshape(MLA_KV_RANK, MLA_HEADS, MLA_NOPE + MLA_V)
    wkv = jnp.concatenate([kvb[:, :, :MLA_NOPE].reshape(MLA_KV_RANK, -1),
                           kvb[:, :, MLA_NOPE:].reshape(MLA_KV_RANK, -1)], axis=1)
    return w1.astype(BF16), wq.astype(BF16), wkv.astype(BF16)


def _mla_layer(x, gain, tabs, wq_a, q_norm, wq_b, wkv_a, kv_norm, wkv_b, wo, layer, casts=()):
    b, s, dm = x.shape
    cm, sm, tk = tabs
    w1, wq, wkv = _mla_weights(wq_a, wq_b, wkv_a, wkv_b)
    x2 = x.reshape(b * s, dm)
    q_lat, ckv, kpe = _mla_down(x2, gain.reshape(1, dm), w1, q_norm.reshape(1, -1), kv_norm.reshape(1, -1), tk)
    q = _mla_q_up(q_lat, wq, cm, sm)
    kv = _mla_kv_up(ckv, wkv)
    o = _mla_attention(q.reshape(b, s, -1), kv.reshape(b, s, -1), kpe.reshape(b, s, -1), b, s)
    out, cast = _matmul_resid(o.reshape(b * s, -1), wo, layer, x2, casts)
    return out.reshape(b, s, dm), cast


def _dil_layer(x, gain, tabs, w_in, wo, layer):
    b, s, dm = x.shape
    cd, sa, sb = (t.reshape(b, s, LANES) for t in tabs)
    outs, lses = [], []
    for g, (_, d) in enumerate(DIL_GROUPS):
        qkv_g = _dil_qkv(x, gain.reshape(1, dm), w_in, 0, g, cd, sa, sb, d)
        o, lse = _dil_attention(qkv_g, d)
        outs.append(o)
        lses.append(lse)
    return _dil_merge_out(outs, lses, wo, layer, x)


def kernel(x, positions, attn_norm, ffn_norm, final_norm, mla_wq_a, mla_q_norm, mla_wq_b, mla_wkv_a,
           mla_kv_norm, mla_wkv_b, mla_wo, dil_w_in, dil_wo, ffn_w_up, ffn_conv_w, ffn_conv_b, ffn_w_down):
    b, s, dm = x.shape
    cm, sm, tk, cd, sa, sb = _rope_tables(positions)
    mla_wo_b = mla_wo.astype(BF16)
    dil_wo_b = dil_wo.astype(BF16)
    ffn_conv_p = _ffn_conv_params(ffn_conv_w, ffn_conv_b)
    up_b = down_b = w_in_b = None
    for i in range(DEPTH):
        j = i // N_MIXERS
        if i % N_MIXERS == 0:
            first = [(ffn_w_up, 0), (ffn_w_down, 0)] if i == 0 else []
            x, cast = _mla_layer(x, attn_norm[i], (cm, sm, tk), mla_wq_a[j], mla_q_norm[j], mla_wq_b[j],
                                 mla_wkv_a[j], mla_kv_norm[j], mla_wkv_b[j], mla_wo_b, j, first)
            if first:
                up_b, down_b = cast
        else:
            x = _dil_layer(x, attn_norm[i], (cd, sa, sb), w_in_b, dil_wo_b, j)
        last = i == DEPTH - 1
        casts = [] if last else [(ffn_w_up, i + 1), (ffn_w_down, i + 1)]
        if not last and (i + 1) % N_MIXERS == 1:
            casts.append((dil_w_in, (i + 1) // N_MIXERS))
        x, cast = _ffn(x, ffn_norm[i].reshape(1, dm), final_norm.reshape(1, dm), up_b, 0, ffn_conv_p, i,
                       down_b, 0, last, casts)
        if not last:
            up_b, down_b = cast[0], cast[1]
            w_in_b = cast[2] if len(cast) > 2 else None
    return x
```

```python
import functools

import jax
import jax.numpy as jnp
from jax import lax
from jax.experimental import pallas as pl
from jax.experimental.pallas import tpu as pltpu

F32 = jnp.float32
BF16 = jnp.bfloat16

D_MODEL = 2048
DEPTH = 4
N_MIXERS = 2
ROPE_THETA = 500000.0
NORM_EPS = 1e-6
LOG2_E = 1.4426950408889634
LN_2 = 0.6931471805599453

MLA_HEADS = 16
MLA_Q_RANK = 512
MLA_KV_RANK = 512
MLA_NOPE = 128
MLA_ROPE = 64
MLA_V = 128

DIL_GROUPS = ((128, 1), (512, 4), (2048, 16))
DIL_HEADS = 16
DIL_HEAD_DIM = 128
DIL_ROT = DIL_HEAD_DIM // 4
DIL_BLOCK = 128
DIL_Q_PRESCALE = DIL_HEAD_DIM ** -0.5 * LOG2_E
DIL_BLOCKS_PER_STEP = 4

FFN_HIDDEN = 5632
CONV_WIDTH = 3

LANES = 128
SUBLANES = 8
VMEM_BYTES_V7X = 64 * 1024 * 1024
VMEM_LIMIT = VMEM_BYTES_V7X - 8 * 1024 * 1024

TM_PROJ = 1024
TN_PROJ = 1024
TM_SMALL = 512
TM_MLA_PROJ = 1024
TM_MERGE = 256
TQ_MLA = 512
TM_FFN = 512
TF_FFN = 512
CARRY_ROWS = SUBLANES
PERM_SLOTS = 4
CAST_ROWS = 16


def _params(*sem):
    return pltpu.CompilerParams(dimension_semantics=sem, vmem_limit_bytes=VMEM_LIMIT)


def _prefetch_tile(bi, i, last, nb, ni):
    nxt = i + last.astype(jnp.int32)
    wrap = nxt == ni
    return jnp.minimum(bi + wrap.astype(jnp.int32), nb - 1), jnp.where(wrap, 0, nxt)


def _cast_jobs(casts, steps, flat_step):
    ins, outs, shapes = [], [], []
    for w, layer in casts:
        _, r, c = w.shape
        rb = next(v for v in range(CAST_ROWS, r + 1, CAST_ROWS) if r % v == 0 and r // v <= steps)
        n = r // rb
        ins.append(pl.BlockSpec(
            (None, rb, c), lambda *g, layer=layer, n=n: (layer, jnp.minimum(flat_step(*g), n - 1), 0)))
        outs.append(pl.BlockSpec((None, rb, c), lambda *g, n=n: (0, jnp.minimum(flat_step(*g), n - 1), 0)))
        shapes.append(jax.ShapeDtypeStruct((1, r, c), BF16))
    return ins, outs, shapes


def _run_casts(srcs, dsts):
    for src, dst in zip(srcs, dsts):
        dst[...] = src[...].astype(BF16)


def _rms(x, g):
    ms = jnp.mean(x * x, axis=-1, keepdims=True)
    return (x * lax.rsqrt(ms + NORM_EPS)) * g


def _dot(a, b):
    return jnp.dot(a, b, preferred_element_type=F32)


def _dot_nt(a, b):
    return lax.dot_general(a, b, (((1,), (1,)), ((), ())), preferred_element_type=F32)


def _rope_tables_kernel(pos_ref, f_ref, cm_ref, sm_ref, tk_ref, cd_ref, sa_ref, sb_ref):
    pos = pos_ref[...]
    lane = lax.broadcasted_iota(jnp.int32, (1, LANES), 1)
    ang = pos * f_ref[...]
    c = jnp.cos(ang)
    s = jnp.sin(ang)
    nm = MLA_ROPE // 2
    second = jnp.logical_and(lane >= nm, lane < 2 * nm)
    cm = jnp.where(second, pltpu.roll(c, nm, 1), c)
    sm = jnp.where(second, pltpu.roll(s, nm, 1), s)
    cm_ref[...] = cm
    sm_ref[...] = sm
    tk_ref[...] = jnp.where(lane < MLA_ROPE, cm, sm)
    cd = pltpu.roll(c, LANES - nm, 1)
    sd = pltpu.roll(s, LANES - nm, 1)
    half = DIL_ROT // 2
    cd_ref[...] = jnp.where(lane < DIL_ROT, cd, 1.0)
    sa_ref[...] = jnp.where(lane < half, 0.0, jnp.where(lane < DIL_ROT, sd, 0.0))
    sb_ref[...] = jnp.where(lane < half, -sd, 0.0)


def _rope_tables(positions):
    m = positions.size
    pos = positions.reshape(m, 1).astype(F32)
    inv_m = ROPE_THETA ** (-jnp.arange(0, MLA_ROPE, 2, dtype=F32) / MLA_ROPE)
    inv_d = ROPE_THETA ** (-jnp.arange(0, DIL_ROT, 2, dtype=F32) / DIL_ROT)
    assert 2 * inv_d.size == inv_m.size and 4 * inv_m.size == LANES
    freqs = jnp.concatenate([inv_m, inv_d, inv_d, inv_m, inv_m]).reshape(1, LANES)
    tm = min(m, 2048)
    row = pl.BlockSpec((tm, LANES), lambda i: (i, 0))
    vec = pl.BlockSpec((1, LANES), lambda i: (0, 0))
    tab = jax.ShapeDtypeStruct((m, LANES), F32)
    return pl.pallas_call(
        _rope_tables_kernel,
        grid=(m // tm,),
        in_specs=[pl.BlockSpec((tm, 1), lambda i: (i, 0)), vec],
        out_specs=[row] * 6,
        out_shape=[tab] * 6,
        compiler_params=_params("parallel"),
        name="rope_tables",
    )(pos, freqs)


def _mla_down_kernel(x_ref, g_ref, w_ref, qn_ref, kvn_ref, tk_ref, ql_ref, ckv_ref, kpe_ref):
    half = x_ref.shape[0] // 2
    for rows in (slice(0, half), slice(half, 2 * half)):
        h = _rms(x_ref[rows, :], g_ref[...]).astype(BF16)
        acc = _dot(h, w_ref[...])
        ql_ref[rows, :] = _rms(acc[:, :MLA_Q_RANK], qn_ref[...]).astype(BF16)
        ckv_ref[rows, :] = _rms(acc[:, MLA_Q_RANK:MLA_Q_RANK + MLA_KV_RANK], kvn_ref[...]).astype(BF16)
        y = acc[:, MLA_Q_RANK + MLA_KV_RANK:] * tk_ref[rows, :]
        z = y + pltpu.roll(y, MLA_ROPE, 1)
        lane = lax.broadcasted_iota(jnp.int32, z.shape, 1)
        kpe_ref[rows, :LANES] = jnp.where(lane < MLA_ROPE, z, 0.0).astype(BF16)
        kpe_ref[rows, LANES:] = jnp.where(lane < MLA_ROPE, 0.0, z).astype(BF16)


def _mla_down(x2, gain, w1, q_norm, kv_norm, tk):
    m = x2.shape[0]
    tm = min(m, TM_MLA_PROJ)
    n1 = w1.shape[1]
    return pl.pallas_call(
        _mla_down_kernel,
        grid=(m // tm,),
        in_specs=[
            pl.BlockSpec((tm, D_MODEL), lambda i: (i, 0)),
            pl.BlockSpec((1, D_MODEL), lambda i: (0, 0)),
            pl.BlockSpec((D_MODEL, n1), lambda i: (0, 0)),
            pl.BlockSpec((1, MLA_Q_RANK), lambda i: (0, 0)),
            pl.BlockSpec((1, MLA_KV_RANK), lambda i: (0, 0)),
            pl.BlockSpec((tm, LANES), lambda i: (i, 0)),
        ],
        out_specs=[
            pl.BlockSpec((tm, MLA_Q_RANK), lambda i: (i, 0)),
            pl.BlockSpec((tm, MLA_KV_RANK), lambda i: (i, 0)),
            pl.BlockSpec((tm, 2 * LANES), lambda i: (i, 0)),
        ],
        out_shape=[
            jax.ShapeDtypeStruct((m, MLA_Q_RANK), BF16),
            jax.ShapeDtypeStruct((m, MLA_KV_RANK), BF16),
            jax.ShapeDtypeStruct((m, 2 * LANES), BF16),
        ],
        compiler_params=_params("parallel"),
        name="mla_down",
    )(x2, gain, w1, q_norm, kv_norm, tk)


MLA_PAIRS = MLA_HEADS // 2
Q_PAIR_IN = 2 * MLA_NOPE + 2 * LANES
Q_PAIR_OUT = 2 * MLA_NOPE + LANES
MLA_Q_PRESCALE = (MLA_NOPE + MLA_ROPE) ** -0.5 * LOG2_E


def _mla_q_up_kernel(a_ref, w_ref, c_ref, s_ref, o_ref):
    a = a_ref[...]
    c = c_ref[...]
    s = s_ref[...]
    for p in range(MLA_PAIRS):
        acc = _dot(a, w_ref[:, p * Q_PAIR_IN:(p + 1) * Q_PAIR_IN])
        nope = acc[:, :2 * MLA_NOPE] * MLA_Q_PRESCALE
        o_ref[:, p * Q_PAIR_OUT:p * Q_PAIR_OUT + 2 * MLA_NOPE] = nope.astype(BF16)
        pe = acc[:, 2 * MLA_NOPE:2 * MLA_NOPE + LANES] * c + acc[:, 2 * MLA_NOPE + LANES:] * s
        o_ref[:, p * Q_PAIR_OUT + 2 * MLA_NOPE:(p + 1) * Q_PAIR_OUT] = (pe * MLA_Q_PRESCALE).astype(BF16)


def _mla_q_up(q_lat, wq, cm, sm):
    m = q_lat.shape[0]
    tm = min(m, TM_MLA_PROJ)
    n_out = MLA_PAIRS * Q_PAIR_OUT
    return pl.pallas_call(
        _mla_q_up_kernel,
        grid=(m // tm,),
        in_specs=[
            pl.BlockSpec((tm, MLA_Q_RANK), lambda i: (i, 0)),
            pl.BlockSpec(wq.shape, lambda i: (0, 0)),
            pl.BlockSpec((tm, LANES), lambda i: (i, 0)),
            pl.BlockSpec((tm, LANES), lambda i: (i, 0)),
        ],
        out_specs=pl.BlockSpec((tm, n_out), lambda i: (i, 0)),
        out_shape=jax.ShapeDtypeStruct((m, n_out), BF16),
        compiler_params=_params("parallel"),
        name="mla_q_up",
    )(q_lat, wq, cm, sm)


def _matmul_bf16_kernel(a_ref, w_ref, o_ref):
    o_ref[...] = _dot(a_ref[...], w_ref[...]).astype(BF16)


def _mla_kv_up(ckv, wkv):
    m = ckv.shape[0]
    tm = min(m, TM_MLA_PROJ)
    n = wkv.shape[1]
    return pl.pallas_call(
        _matmul_bf16_kernel,
        grid=(m // tm,),
        in_specs=[
            pl.BlockSpec((tm, MLA_KV_RANK), lambda i: (i, 0)),
            pl.BlockSpec(wkv.shape, lambda i: (0, 0)),
        ],
        out_specs=pl.BlockSpec((tm, n), lambda i: (i, 0)),
        out_shape=jax.ShapeDtypeStruct((m, n), BF16),
        compiler_params=_params("parallel"),
        name="mla_kv_up",
    )(ckv, wkv)


def _mla_attn_kernel(q_ref, kn_ref, kpe_ref, v_ref, o_ref, m_ref, l_ref, acc_ref, *, tq):
    g = pl.program_id(2)
    row = lax.broadcasted_iota(jnp.int32, (tq, tq), 0)
    col = lax.broadcasted_iota(jnp.int32, (tq, tq), 1)
    causal = col <= row
    heads = (0, 1)
    for sub in range(2):
        qi = 2 * g + sub
        rows = slice(sub * tq, (sub + 1) * tq)
        q_pe = q_ref[rows, 2 * MLA_NOPE:]
        qs = [jnp.concatenate([q_ref[rows, hh * LANES:(hh + 1) * LANES], q_pe], axis=1) for hh in heads]
        m_ref[...] = jnp.full(m_ref.shape, -jnp.inf, F32)
        l_ref[...] = jnp.zeros(l_ref.shape, F32)
        acc_ref[...] = jnp.zeros(acc_ref.shape, F32)

        def step(hh, kb, masked, nkb=1):
            lanes = slice(hh * LANES, (hh + 1) * LANES)
            tk = nkb * tq
            ks = pl.multiple_of(kb * tq, tq)
            k = jnp.concatenate([kn_ref[pl.ds(ks, tk), lanes], kpe_ref[pl.ds(ks, tk), lanes]], axis=1)
            s = _dot_nt(qs[hh], k)
            if masked:
                diag = jnp.where(causal, s[:, tk - tq:], -jnp.inf)
                s = diag if nkb == 1 else jnp.concatenate([s[:, :tk - tq], diag], axis=1)
            m_prev = m_ref[hh]
            m_next = jnp.maximum(m_prev, jnp.max(s, axis=1, keepdims=True))
            m_wide = jnp.concatenate([m_next] * (tk // LANES), axis=1)
            p = jnp.exp2(s - m_wide)
            alpha = jnp.exp2(m_prev - m_next)
            l_ref[hh] = alpha * l_ref[hh] + jnp.sum(p, axis=1, keepdims=True)
            m_ref[hh] = m_next
            acc_ref[hh] = alpha * acc_ref[hh] + _dot(p.astype(BF16), v_ref[pl.ds(ks, tk), lanes])

        def trip(k2, carry):
            for hh in heads:
                step(hh, 2 * k2, False, nkb=2)
            return carry

        lax.fori_loop(0, g, trip, 0)
        for hh in heads:
            step(hh, qi - sub, True, nkb=1 + sub)
        for hh in heads:
            o_ref[rows, hh * LANES:(hh + 1) * LANES] = (acc_ref[hh] / l_ref[hh]).astype(BF16)


def _mla_attention(q, kv, kpe, b, s):
    tq = min(s // 2, TQ_MLA)
    kern = functools.partial(_mla_attn_kernel, tq=tq)
    pair_w = 2 * LANES
    return pl.pallas_call(
        kern,
        grid=(b, MLA_PAIRS, s // (2 * tq)),
        in_specs=[
            pl.BlockSpec((None, 2 * tq, Q_PAIR_OUT), lambda bi, p, g: (bi, g, p)),
            pl.BlockSpec((None, s, pair_w), lambda bi, p, g: (bi, 0, p)),
            pl.BlockSpec((None, s, pair_w), lambda bi, p, g: (bi, 0, 0)),
            pl.BlockSpec((None, s, pair_w), lambda bi, p, g: (bi, 0, MLA_PAIRS + p)),
        ],
        out_specs=pl.BlockSpec((None, 2 * tq, pair_w), lambda bi, p, g: (bi, g, p)),
        out_shape=jax.ShapeDtypeStruct((b, s, MLA_HEADS * MLA_V), BF16),
        scratch_shapes=[
            pltpu.VMEM((2, tq, LANES), F32),
            pltpu.VMEM((2, tq, LANES), F32),
            pltpu.VMEM((2, tq, LANES), F32),
        ],
        compiler_params=_params("parallel", "parallel", "arbitrary"),
        name="mla_attention",
    )(q, kv, kpe, kv)


def _matmul_resid_kernel(a_ref, w_ref, r_ref, *rest, n_cast):
    o_ref = rest[n_cast]
    _run_casts(rest[:n_cast], rest[n_cast + 1:])
    o_ref[...] = r_ref[...] + _dot(a_ref[...], w_ref[...])


def _matmul_resid(a, w, layer, resid, casts=()):
    m, k = a.shape
    n = w.shape[2]
    tm = min(m, TM_SMALL)
    cast_in, cast_out, cast_shapes = _cast_jobs(casts, m // tm, lambda i: i)
    outs = pl.pallas_call(
        functools.partial(_matmul_resid_kernel, n_cast=len(casts)),
        grid=(m // tm,),
        in_specs=[
            pl.BlockSpec((tm, k), lambda i: (i, 0)),
            pl.BlockSpec((None, k, n), lambda i: (layer, 0, 0), pipeline_mode=pl.Buffered(1)),
            pl.BlockSpec((tm, n), lambda i: (i, 0)),
        ] + cast_in,
        out_specs=[pl.BlockSpec((tm, n), lambda i: (i, 0))] + cast_out,
        out_shape=[jax.ShapeDtypeStruct((m, n), F32)] + cast_shapes,
        compiler_params=_params("arbitrary"),
        name="out_proj_resid",
    )(a, w, resid, *[cw for cw, _ in casts])
    return outs[0], outs[1:]


def _dil_rope(xc, c, sa, sb):
    half = DIL_ROT // 2
    return xc * c + pltpu.roll(xc, half, 1) * sa + pltpu.roll(xc, LANES - half, 1) * sb


def _dil_qkv_kernel(x_ref, g_ref, w_ref, c_ref, sa_ref, sb_ref, o_ref, h_ref, acc_ref, *perm_refs,
                    d, tm, n_query, n_rope, nj):
    j = pl.program_id(2)
    rows = tm // d
    if d > 1:
        xs_ref, cs_ref, sas_ref, sbs_ref = perm_refs
    else:
        cs_ref, sas_ref, sbs_ref = c_ref, sa_ref, sb_ref

    def norm_input():
        if d == 1:
            h_ref[...] = _rms(x_ref[...], g_ref[...]).astype(BF16)
            return
        x = x_ref[...]
        rinv = lax.rsqrt(jnp.mean(x * x, axis=-1, keepdims=True) + NORM_EPS)
        for t in range(x_ref.shape[1] // LANES):
            lanes = slice(t * LANES, (t + 1) * LANES)
            slot = t % xs_ref.shape[0]
            xs_ref[slot] = (x_ref[:, lanes] * rinv) * g_ref[:, lanes]
            for r in range(d):
                h_ref[r * rows:(r + 1) * rows, lanes] = xs_ref[slot, pl.ds(r, rows, stride=d), :].astype(BF16)
        for r in range(d):
            sl = pl.ds(r, rows, stride=d)
            dst = slice(r * rows, (r + 1) * rows)
            cs_ref[dst, :] = c_ref[sl, :]
            sas_ref[dst, :] = sa_ref[sl, :]
            sbs_ref[dst, :] = sb_ref[sl, :]

    def matmul():
        acc_ref[...] = _dot(h_ref[...], w_ref[...])

    def emit(rope, query=False):
        for r in range(d):
            src = slice(r * rows, (r + 1) * rows)
            if rope:
                c, sa, sb = cs_ref[src, :], sas_ref[src, :], sbs_ref[src, :]
            for t in range(acc_ref.shape[1] // LANES):
                a = acc_ref[src, t * LANES:(t + 1) * LANES]
                if rope:
                    a = _dil_rope(a, c, sa, sb)
                if query:
                    a = a * DIL_Q_PRESCALE
                o_ref[r, :, t * LANES:(t + 1) * LANES] = a.astype(BF16)

    @pl.when(j == 0)
    def _():
        norm_input()
        matmul()

    @pl.when(jnp.logical_and(j >= 1, j <= n_query))
    def _():
        emit(True, query=True)
        matmul()

    @pl.when(jnp.logical_and(j > n_query, j <= n_rope))
    def _():
        emit(True)
        matmul()

    @pl.when(jnp.logical_and(j > n_rope, j < nj))
    def _():
        emit(False)
        matmul()

    @pl.when(j == nj)
    def _():
        emit(False)


def _dil_qkv(x, gain, w_in, layer, g, cd, sa, sb, d):
    b, s, _ = x.shape
    n = 3 * DIL_HEADS * DIL_HEAD_DIM
    tm = min(s, TM_PROJ)
    tn = TN_PROJ
    col0 = g * (n // tn)
    n_rope = 2 * DIL_HEADS * DIL_HEAD_DIM // tn
    nj = n // tn
    assert n_rope < nj
    n_query = DIL_HEADS * DIL_HEAD_DIM // tn
    kern = functools.partial(_dil_qkv_kernel, d=d, tm=tm, n_query=n_query, n_rope=n_rope, nj=nj)
    ni = s // tm

    def tile(bi, i, j):
        return _prefetch_tile(bi, i, j == nj, b, ni)

    tab = pl.BlockSpec((None, tm, LANES), lambda bi, i, j: (*tile(bi, i, j), 0))
    perm_scratch = []
    if d > 1:
        perm_scratch = [pltpu.VMEM((PERM_SLOTS, tm, LANES), F32)] + [pltpu.VMEM((tm, LANES), F32)] * 3
    return pl.pallas_call(
        kern,
        grid=(b, s // tm, nj + 1),
        in_specs=[
            pl.BlockSpec((None, tm, D_MODEL), lambda bi, i, j: (*tile(bi, i, j), 0)),
            pl.BlockSpec((1, D_MODEL), lambda bi, i, j: (0, 0)),
            pl.BlockSpec((None, D_MODEL, tn), lambda bi, i, j: (layer, 0, col0 + jnp.where(j == nj, 0, j))),
            tab, tab, tab,
        ],
        out_specs=pl.BlockSpec((None, d, tm // d, tn), lambda bi, i, j: (bi, 0, i, jnp.maximum(j - 1, 0))),
        out_shape=jax.ShapeDtypeStruct((b, d, s // d, n), BF16),
        scratch_shapes=[pltpu.VMEM((tm, D_MODEL), BF16), pltpu.VMEM((tm, tn), F32)] + perm_scratch,
        compiler_params=_params("parallel", "parallel", "arbitrary"),
        name=f"dil_qkv_d{d}",
    )(x, gain, w_in, cd, sa, sb)


def _dil_attn_kernel(q_ref, kc_ref, kp_ref, vc_ref, vp_ref, o_ref, lse_ref, s_ref, p_ref, *, nblk):
    jb = pl.program_id(2)
    qb = DIL_BLOCK
    qi = lax.broadcasted_iota(jnp.int32, (qb, 2 * qb), 0)
    ki = lax.broadcasted_iota(jnp.int32, (qb, 2 * qb), 1)
    lane = lax.broadcasted_iota(jnp.int32, (qb, LANES), 1)

    def keys(cur_ref, prev_ref, blk, sl):
        own = cur_ref[blk * qb:(blk + 1) * qb, sl]
        before = prev_ref[:, sl] if blk == 0 else cur_ref[(blk - 1) * qb:blk * qb, sl]
        return jnp.concatenate([before, own], axis=0)

    for blk in range(nblk):
        for h in range(DIL_HEADS):
            sl = slice(h * DIL_HEAD_DIM, (h + 1) * DIL_HEAD_DIM)
            s_ref[blk, h] = _dot_nt(q_ref[blk * qb:(blk + 1) * qb, sl], keys(kc_ref, kp_ref, blk, sl))
    for blk in range(nblk):
        first_key = jnp.where(jb > 0, qi, qb) if blk == 0 else qi
        band = jnp.logical_and(ki >= first_key, ki <= qi + qb)
        s = jnp.where(band[None], s_ref[blk], -jnp.inf)
        m = jnp.max(s, axis=-1, keepdims=True)
        e = jnp.exp2(s - m)
        l = jnp.sum(e, axis=-1, keepdims=True)
        p_ref[blk] = e.astype(BF16)
        inv = 1.0 / l
        lse = LN_2 * m + jnp.log(l)
        lse_all = jnp.zeros((qb, LANES), F32)
        for h in range(DIL_HEADS):
            lse_all = jnp.where(lane == h, lse[h], lse_all)
        lse_ref[blk * qb:(blk + 1) * qb, :] = lse_all
        for h in range(DIL_HEADS):
            sl = slice(h * DIL_HEAD_DIM, (h + 1) * DIL_HEAD_DIM)
            pv = _dot(p_ref[blk, h], keys(vc_ref, vp_ref, blk, sl))
            o_ref[blk * qb:(blk + 1) * qb, sl] = pv * inv[h]


def _dil_attention(qkv_g, d):
    b, _, l, _ = qkv_g.shape
    hw = DIL_HEADS * DIL_HEAD_DIM
    qb = DIL_BLOCK
    nblk = min(DIL_BLOCKS_PER_STEP, l // qb)
    rows = nblk * qb
    kern = functools.partial(_dil_attn_kernel, nblk=nblk)

    def cur(c):
        return pl.BlockSpec((None, None, rows, hw), lambda bi, r, jb: (bi, r, jb, c))

    def prev(c):
        return pl.BlockSpec((None, None, qb, hw), lambda bi, r, jb: (bi, r, jnp.maximum(jb * nblk - 1, 0), c))

    return pl.pallas_call(
        kern,
        grid=(b, d, l // rows),
        in_specs=[cur(0), cur(1), prev(1), cur(2), prev(2)],
        out_specs=[
            pl.BlockSpec((None, None, rows, hw), lambda bi, r, jb: (bi, r, jb, 0)),
            pl.BlockSpec((None, None, rows, LANES), lambda bi, r, jb: (bi, r, jb, 0)),
        ],
        out_shape=[
            jax.ShapeDtypeStruct((b, d, l, hw), F32),
            jax.ShapeDtypeStruct((b, d, l, LANES), F32),
        ],
        scratch_shapes=[
            pltpu.VMEM((nblk, DIL_HEADS, qb, 2 * qb), F32),
            pltpu.VMEM((nblk, DIL_HEADS, qb, 2 * qb), BF16),
        ],
        compiler_params=_params("parallel", "parallel", "arbitrary"),
        name=f"dil_attention_d{d}",
    )(qkv_g, qkv_g, qkv_g, qkv_g, qkv_g)


def _dil_merge_kernel(o0_ref, l0_ref, o1_ref, l1_ref, o2_ref, l2_ref, w_ref, x_ref, out_ref,
                      n1_ref, n2_ref, nl1_ref, nl2_ref, a_ref, *, tm, n):
    i = pl.program_id(1)
    d1 = DIL_GROUPS[1][1]
    d2 = DIL_GROUPS[2][1]

    def merge():
        for d, o_ref, l_ref, n_ref, nl_ref in ((d1, o1_ref, l1_ref, n1_ref, nl1_ref),
                                               (d2, o2_ref, l2_ref, n2_ref, nl2_ref)):
            for r in range(d):
                rows = pl.ds(r, tm // d, stride=d)
                nl_ref[rows, :] = l_ref[r]
                for h in range(DIL_HEADS):
                    n_ref[h, rows, :] = o_ref[r, :, h * DIL_HEAD_DIM:(h + 1) * DIL_HEAD_DIM]
        a0, a1, a2 = l0_ref[...], nl1_ref[...], nl2_ref[...]
        mx = jnp.maximum(jnp.maximum(a0, a1), a2)
        w0 = jnp.exp(a0 - mx)
        w1 = jnp.exp(a1 - mx)
        w2 = jnp.exp(a2 - mx)
        den = w0 + w1 + w2
        w0, w1, w2 = w0 / den, w1 / den, w2 / den
        for h in range(DIL_HEADS):
            sl = slice(h * DIL_HEAD_DIM, (h + 1) * DIL_HEAD_DIM)
            o = w0[:, h:h + 1] * o0_ref[:, sl] + w1[:, h:h + 1] * n1_ref[h] + w2[:, h:h + 1] * n2_ref[h]
            a_ref[:, sl] = o.astype(BF16)

    def project(a):
        out_ref[...] = x_ref[...] + _dot(a, w_ref[...])

    @pl.when(i == 0)
    def _():
        merge()

    @pl.when(jnp.logical_and(i > 0, i < n))
    def _():
        a = a_ref[...]
        merge()
        project(a)

    @pl.when(i == n)
    def _():
        project(a_ref[...])


def _dil_merge_out(outs, lses, wo, layer, x):
    b, s, dm = x.shape
    hw = DIL_HEADS * DIL_HEAD_DIM
    tm = min(s, TM_MERGE)
    d1 = DIL_GROUPS[1][1]
    d2 = DIL_GROUPS[2][1]
    n = s // tm
    kern = functools.partial(_dil_merge_kernel, tm=tm, n=n)

    def merged(i):
        return jnp.minimum(i, n - 1)

    def projected(i):
        return jnp.maximum(i - 1, 0)

    def grp(d, w):
        return pl.BlockSpec((None, d, tm // d, w), lambda bi, i: (bi, 0, merged(i), 0))

    return pl.pallas_call(
        kern,
        grid=(b, n + 1),
        in_specs=[
            pl.BlockSpec((None, None, tm, hw), lambda bi, i: (bi, 0, merged(i), 0)),
            pl.BlockSpec((None, None, tm, LANES), lambda bi, i: (bi, 0, merged(i), 0)),
            grp(d1, hw), grp(d1, LANES), grp(d2, hw), grp(d2, LANES),
            pl.BlockSpec((None, hw, dm), lambda bi, i: (layer, 0, 0)),
            pl.BlockSpec((None, tm, dm), lambda bi, i: (bi, projected(i), 0)),
        ],
        out_specs=pl.BlockSpec((None, tm, dm), lambda bi, i: (bi, projected(i), 0)),
        out_shape=jax.ShapeDtypeStruct((b, s, dm), F32),
        scratch_shapes=[
            pltpu.VMEM((DIL_HEADS, tm, DIL_HEAD_DIM), F32), pltpu.VMEM((DIL_HEADS, tm, DIL_HEAD_DIM), F32),
            pltpu.VMEM((tm, LANES), F32), pltpu.VMEM((tm, LANES), F32),
            pltpu.VMEM((tm, hw), BF16),
        ],
        compiler_params=_params("parallel", "arbitrary"),
        name="dil_merge_out",
    )(outs[0], lses[0], outs[1], lses[1], outs[2], lses[2], wo, x)


def _ffn_kernel(x_ref, g_ref, og_ref, wg_ref, wv_ref, cp_ref, wd_ref, *rest, tm, tf, nf, ni, out_norm, n_cast):
    cast_src = rest[:n_cast]
    o_ref = rest[n_cast]
    cast_dst = rest[n_cast + 1:2 * n_cast + 1]
    h_ref, u_ref, prev_ref = rest[2 * n_cast + 1:]
    k = pl.program_id(1)
    t = lax.rem(k, nf)
    c_down = lax.rem(k + nf - 1, nf)
    row = lax.broadcasted_iota(jnp.int32, (CARRY_ROWS, 1), 0)

    def side_jobs():
        _run_casts(cast_src, cast_dst)

    def up():
        h = h_ref[...]
        u_ref[:, :tf] = _dot(h, wg_ref[...])
        u_ref[:, tf:] = _dot(h, wv_ref[...])

    def start_tile():
        h_ref[...] = _rms(x_ref[...], g_ref[...]).astype(BF16)
        up()

    def conv(u):
        prev = prev_ref[c_down]
        prev_ref[c_down] = u[tm - CARRY_ROWS:, :]
        p1 = prev[CARRY_ROWS - 1:CARRY_ROWS, :]
        p2 = prev[CARRY_ROWS - 2:CARRY_ROWS - 1, :]
        r1 = pltpu.roll(u, 1, 0)
        r2 = pltpu.roll(u, 2, 0)
        u1 = jnp.concatenate([jnp.where(row == 0, p1, r1[:CARRY_ROWS]), r1[CARRY_ROWS:]], axis=0)
        u2 = jnp.concatenate(
            [jnp.where(row == 0, p2, jnp.where(row == 1, p1, r2[:CARRY_ROWS])), r2[CARRY_ROWS:]], axis=0)
        cp = cp_ref[...]
        acc = cp[CONV_WIDTH:CONV_WIDTH + 1, :] + u2 * cp[0:1, :]
        acc = acc + u1 * cp[1:2, :]
        return acc + u * cp[2:3, :]

    def down(u, first_chunk):
        cv = conv(u)
        gate = cv[:, :tf]
        val = cv[:, tf:]
        act = (gate * (1.0 / (1.0 + jnp.exp(-gate)))) * val
        d = _dot(act.astype(BF16), wd_ref[...])
        if first_chunk:
            o_ref[...] = x_ref[...] + d
        else:
            o_ref[...] += d

    def finish_tile():
        if out_norm:
            o_ref[...] = _rms(o_ref[...], og_ref[...])

    @pl.when(k == 0)
    def _():
        side_jobs()
        prev_ref[...] = jnp.zeros(prev_ref.shape, F32)
        start_tile()

    @pl.when(jnp.logical_and(t == 0, jnp.logical_and(k > 0, k < ni * nf)))
    def _():
        side_jobs()
        u = u_ref[...]
        start_tile()
        down(u, False)
        finish_tile()

    @pl.when(t == 1)
    def _():
        side_jobs()
        u = u_ref[...]
        up()
        down(u, True)

    @pl.when(t >= 2)
    def _():
        side_jobs()
        u = u_ref[...]
        up()
        down(u, False)

    @pl.when(k == ni * nf)
    def _():
        side_jobs()
        down(u_ref[...], False)
        finish_tile()


def _ffn_chunk_order(a, tf):
    lead = a.shape[:-1]
    return a.reshape(*lead, 2, FFN_HIDDEN // tf, tf).swapaxes(-3, -2).reshape(*lead, 2 * FFN_HIDDEN)


def _ffn_conv_params(conv_w, conv_b):
    pad = jnp.zeros((conv_w.shape[0], SUBLANES - CONV_WIDTH - 1, conv_w.shape[2]), F32)
    conv_p = jnp.concatenate([conv_w, conv_b[:, None, :], pad], axis=1)
    return _ffn_chunk_order(conv_p, TF_FFN)


def _ffn(x, gain, out_gain, w_up, up_layer, conv_p, conv_layer, w_down, down_layer, out_norm, casts):
    b, s, _ = x.shape
    tm = min(s, TM_FFN)
    tf = TF_FFN
    nf = FFN_HIDDEN // tf
    ni = s // tm
    assert nf >= 2
    nk = ni * nf + 1
    kern = functools.partial(_ffn_kernel, tm=tm, tf=tf, nf=nf, ni=ni, out_norm=out_norm, n_cast=len(casts))

    def in_tile(k):
        return jnp.minimum(lax.div(k, nf), ni - 1)

    def out_tile(k):
        return lax.div(jnp.maximum(k - 1, 0), nf)

    def up_chunk(k):
        return lax.rem(k, nf)

    def down_chunk(k):
        return lax.rem(k + nf - 1, nf)

    cast_in, cast_out, cast_shapes = _cast_jobs(casts, b * nk, lambda bi, k: bi * nk + k)

    outs = pl.pallas_call(
        kern,
        grid=(b, nk),
        in_specs=[
            pl.BlockSpec((None, tm, D_MODEL), lambda bi, k: (bi, in_tile(k), 0)),
            pl.BlockSpec((1, D_MODEL), lambda bi, k: (0, 0)),
            pl.BlockSpec((1, D_MODEL), lambda bi, k: (0, 0)),
            pl.BlockSpec((None, D_MODEL, tf), lambda bi, k: (up_layer, 0, up_chunk(k))),
            pl.BlockSpec((None, D_MODEL, tf), lambda bi, k: (up_layer, 0, nf + up_chunk(k))),
            pl.BlockSpec((None, SUBLANES, 2 * tf), lambda bi, k: (conv_layer, 0, down_chunk(k))),
            pl.BlockSpec((None, tf, D_MODEL), lambda bi, k: (down_layer, down_chunk(k), 0)),
        ] + cast_in,
        out_specs=[pl.BlockSpec((None, tm, D_MODEL), lambda bi, k: (bi, out_tile(k), 0))] + cast_out,
        out_shape=[jax.ShapeDtypeStruct(x.shape, F32)] + cast_shapes,
        scratch_shapes=[
            pltpu.VMEM((tm, D_MODEL), BF16),
            pltpu.VMEM((tm, 2 * tf), F32),
            pltpu.VMEM((nf, CARRY_ROWS, 2 * tf), F32),
        ],
        compiler_params=_params("arbitrary", "arbitrary"),
        name="conv_ffn",
    )(x, gain, out_gain, w_up, w_up, conv_p, w_down, *[w for w, _ in casts])
    return outs[0], outs[1:]


def _rotate_half_cols(w):
    half = w.shape[-1] // 2
    return jnp.concatenate([-w[..., half:], w[..., :half]], axis=-1)


def _mla_weights(wq_a, wq_b, wkv_a, wkv_b):
    w_pe = wkv_a[:, MLA_KV_RANK:]
    w1 = jnp.concatenate([wq_a, wkv_a[:, :MLA_KV_RANK], w_pe, _rotate_half_cols(w_pe)], axis=1)
    qb = wq_b.reshape(MLA_Q_RANK, MLA_HEADS, MLA_NOPE + MLA_ROPE)
    nope = qb[:, :, :MLA_NOPE].reshape(MLA_Q_RANK, MLA_PAIRS, 2 * MLA_NOPE)
    pe = qb[:, :, MLA_NOPE:]
    rot = _rotate_half_cols(pe).reshape(MLA_Q_RANK, MLA_PAIRS, LANES)
    pe = pe.reshape(MLA_Q_RANK, MLA_PAIRS, LANES)
    wq = jnp.concatenate([nope, pe, rot], axis=2).reshape(MLA_Q_RANK, MLA_PAIRS * Q_PAIR_IN)
    kvb = wkv_b.reshape(MLA_KV_RANK, MLA_HEADS, MLA_NOPE + MLA_V)
    wkv = jnp.concatenate([kvb[:, :, :MLA_NOPE].reshape(MLA_KV_RANK, -1),
                           kvb[:, :, MLA_NOPE:].reshape(MLA_KV_RANK, -1)], axis=1)
    return w1.astype(BF16), wq.astype(BF16), wkv.astype(BF16)


def _mla_layer(x, gain, tabs, wq_a, q_norm, wq_b, wkv_a, kv_norm, wkv_b, wo, layer, casts=()):
    b, s, dm = x.shape
    cm, sm, tk = tabs
    w1, wq, wkv = _mla_weights(wq_a, wq_b, wkv_a, wkv_b)
    x2 = x.reshape(b * s, dm)
    q_lat, ckv, kpe = _mla_down(x2, gain.reshape(1, dm), w1, q_norm.reshape(1, -1), kv_norm.reshape(1, -1), tk)
    q = _mla_q_up(q_lat, wq, cm, sm)
    kv = _mla_kv_up(ckv, wkv)
    o = _mla_attention(q.reshape(b, s, -1), kv.reshape(b, s, -1), kpe.reshape(b, s, -1), b, s)
    out, cast = _matmul_resid(o.reshape(b * s, -1), wo, layer, x2, casts)
    return out.reshape(b, s, dm), cast


def _dil_layer(x, gain, tabs, w_in, wo, layer):
    b, s, dm = x.shape
    cd, sa, sb = (t.reshape(b, s, LANES) for t in tabs)
    outs, lses = [], []
    for g, (_, d) in enumerate(DIL_GROUPS):
        qkv_g = _dil_qkv(x, gain.reshape(1, dm), w_in, 0, g, cd, sa, sb, d)
        o, lse = _dil_attention(qkv_g, d)
        outs.append(o)
        lses.append(lse)
    return _dil_merge_out(outs, lses, wo, layer, x)


def kernel(x, positions, attn_norm, ffn_norm, final_norm, mla_wq_a, mla_q_norm, mla_wq_b, mla_wkv_a,
           mla_kv_norm, mla_wkv_b, mla_wo, dil_w_in, dil_wo, ffn_w_up, ffn_conv_w, ffn_conv_b, ffn_w_down):
    b, s, dm = x.shape
    cm, sm, tk, cd, sa, sb = _rope_tables(positions)
    mla_wo_b = mla_wo.astype(BF16)
    dil_wo_b = dil_wo.astype(BF16)
    ffn_conv_p = _ffn_conv_params(ffn_conv_w, ffn_conv_b)
    up_b = down_b = w_in_b = None
    for i in range(DEPTH):
        j = i // N_MIXERS
        if i % N_MIXERS == 0:
            first = [(ffn_w_up, 0), (ffn_w_down, 0)] if i == 0 else []
            x, cast = _mla_layer(x, attn_norm[i], (cm, sm, tk), mla_wq_a[j], mla_q_norm[j], mla_wq_b[j],
                                 mla_wkv_a[j], mla_kv_norm[j], mla_wkv_b[j], mla_wo_b, j, first)
            if first:
                up_b, down_b = cast
        else:
            x = _dil_layer(x, attn_norm[i], (cd, sa, sb), w_in_b, dil_wo_b, j)
        last = i == DEPTH - 1
        casts = [] if last else [(ffn_w_up, i + 1), (ffn_w_down, i + 1)]
        if not last and (i + 1) % N_MIXERS == 1:
            casts.append((dil_w_in, (i + 1) // N_MIXERS))
        x, cast = _ffn(x, ffn_norm[i].reshape(1, dm), final_norm.reshape(1, dm), up_b, 0, ffn_conv_p, i,
                       down_b, 0, last, casts)
        if not last:
            up_b, down_b = cast[0], cast[1]
            w_in_b = cast[2] if len(cast) > 2 else None
    return x
```

```python
import functools

import jax
import jax.numpy as jnp
from jax import lax
from jax.experimental import pallas as pl
from jax.experimental.pallas import tpu as pltpu

F32 = jnp.float32
BF16 = jnp.bfloat16

D_MODEL = 2048
DEPTH = 4
N_MIXERS = 2
ROPE_THETA = 500000.0
NORM_EPS = 1e-6
LOG2_E = 1.4426950408889634
LN_2 = 0.6931471805599453

MLA_HEADS = 16
MLA_Q_RANK = 512
MLA_KV_RANK = 512
MLA_NOPE = 128
MLA_ROPE = 64
MLA_V = 128

DIL_GROUPS = ((128, 1), (512, 4), (2048, 16))
DIL_HEADS = 16
DIL_HEAD_DIM = 128
DIL_ROT = DIL_HEAD_DIM // 4
DIL_BLOCK = 128
DIL_Q_PRESCALE = DIL_HEAD_DIM ** -0.5 * LOG2_E
DIL_BLOCKS_PER_STEP = 4

FFN_HIDDEN = 5632
CONV_WIDTH = 3

LANES = 128
SUBLANES = 8
VMEM_BYTES_V7X = 64 * 1024 * 1024
VMEM_LIMIT = VMEM_BYTES_V7X - 8 * 1024 * 1024

TM_PROJ = 1024
TN_PROJ = 1024
TM_SMALL = 512
TM_MLA_PROJ = 1024
MLA_DOWN_CHUNKS = 4
TM_MERGE = 256
TQ_MLA = 512
TM_FFN = 512
TF_FFN = 512
CARRY_ROWS = SUBLANES
PERM_SLOTS = 4
CAST_ROWS = 16


def _params(*sem):
    return pltpu.CompilerParams(dimension_semantics=sem, vmem_limit_bytes=VMEM_LIMIT)


def _prefetch_tile(bi, i, last, nb, ni):
    nxt = i + last.astype(jnp.int32)
    wrap = nxt == ni
    return jnp.minimum(bi + wrap.astype(jnp.int32), nb - 1), jnp.where(wrap, 0, nxt)


def _cast_jobs(casts, steps, flat_step):
    ins, outs, shapes = [], [], []
    for w, layer in casts:
        _, r, c = w.shape
        rb = next(v for v in range(CAST_ROWS, r + 1, CAST_ROWS) if r % v == 0 and r // v <= steps)
        n = r // rb
        ins.append(pl.BlockSpec(
            (None, rb, c), lambda *g, layer=layer, n=n: (layer, jnp.minimum(flat_step(*g), n - 1), 0)))
        outs.append(pl.BlockSpec((None, rb, c), lambda *g, n=n: (0, jnp.minimum(flat_step(*g), n - 1), 0)))
        shapes.append(jax.ShapeDtypeStruct((1, r, c), BF16))
    return ins, outs, shapes


def _run_casts(srcs, dsts):
    for src, dst in zip(srcs, dsts):
        dst[...] = src[...].astype(BF16)


def _rms(x, g):
    ms = jnp.mean(x * x, axis=-1, keepdims=True)
    return (x * lax.rsqrt(ms + NORM_EPS)) * g


def _dot(a, b):
    return jnp.dot(a, b, preferred_element_type=F32)


def _dot_nt(a, b):
    return lax.dot_general(a, b, (((1,), (1,)), ((), ())), preferred_element_type=F32)


def _rope_tables_kernel(pos_ref, f_ref, cm_ref, sm_ref, tk_ref, cd_ref, sa_ref, sb_ref):
    pos = pos_ref[...]
    lane = lax.broadcasted_iota(jnp.int32, (1, LANES), 1)
    ang = pos * f_ref[...]
    c = jnp.cos(ang)
    s = jnp.sin(ang)
    nm = MLA_ROPE // 2
    second = jnp.logical_and(lane >= nm, lane < 2 * nm)
    cm = jnp.where(second, pltpu.roll(c, nm, 1), c)
    sm = jnp.where(second, pltpu.roll(s, nm, 1), s)
    cm_ref[...] = cm
    sm_ref[...] = sm
    tk_ref[...] = jnp.where(lane < MLA_ROPE, cm, sm)
    cd = pltpu.roll(c, LANES - nm, 1)
    sd = pltpu.roll(s, LANES - nm, 1)
    half = DIL_ROT // 2
    cd_ref[...] = jnp.where(lane < DIL_ROT, cd, 1.0)
    sa_ref[...] = jnp.where(lane < half, 0.0, jnp.where(lane < DIL_ROT, sd, 0.0))
    sb_ref[...] = jnp.where(lane < half, -sd, 0.0)


def _rope_tables(positions):
    m = positions.size
    pos = positions.reshape(m, 1).astype(F32)
    inv_m = ROPE_THETA ** (-jnp.arange(0, MLA_ROPE, 2, dtype=F32) / MLA_ROPE)
    inv_d = ROPE_THETA ** (-jnp.arange(0, DIL_ROT, 2, dtype=F32) / DIL_ROT)
    assert 2 * inv_d.size == inv_m.size and 4 * inv_m.size == LANES
    freqs = jnp.concatenate([inv_m, inv_d, inv_d, inv_m, inv_m]).reshape(1, LANES)
    tm = min(m, 2048)
    row = pl.BlockSpec((tm, LANES), lambda i: (i, 0))
    vec = pl.BlockSpec((1, LANES), lambda i: (0, 0))
    tab = jax.ShapeDtypeStruct((m, LANES), F32)
    return pl.pallas_call(
        _rope_tables_kernel,
        grid=(m // tm,),
        in_specs=[pl.BlockSpec((tm, 1), lambda i: (i, 0)), vec],
        out_specs=[row] * 6,
        out_shape=[tab] * 6,
        compiler_params=_params("parallel"),
        name="rope_tables",
    )(pos, freqs)


def _mla_down_kernel(x_ref, g_ref, w_ref, qn_ref, kvn_ref, tk_ref, ql_ref, ckv_ref, kpe_ref):
    chunk = x_ref.shape[0] // MLA_DOWN_CHUNKS
    for rows in (slice(c * chunk, (c + 1) * chunk) for c in range(MLA_DOWN_CHUNKS)):
        h = _rms(x_ref[rows, :], g_ref[...]).astype(BF16)
        acc = _dot(h, w_ref[...])
        ql_ref[rows, :] = _rms(acc[:, :MLA_Q_RANK], qn_ref[...]).astype(BF16)
        ckv_ref[rows, :] = _rms(acc[:, MLA_Q_RANK:MLA_Q_RANK + MLA_KV_RANK], kvn_ref[...]).astype(BF16)
        y = acc[:, MLA_Q_RANK + MLA_KV_RANK:] * tk_ref[rows, :]
        z = y + pltpu.roll(y, MLA_ROPE, 1)
        lane = lax.broadcasted_iota(jnp.int32, z.shape, 1)
        kpe_ref[rows, :LANES] = jnp.where(lane < MLA_ROPE, z, 0.0).astype(BF16)
        kpe_ref[rows, LANES:] = jnp.where(lane < MLA_ROPE, 0.0, z).astype(BF16)


def _mla_down(x2, gain, w1, q_norm, kv_norm, tk):
    m = x2.shape[0]
    tm = min(m, TM_MLA_PROJ)
    n1 = w1.shape[1]
    return pl.pallas_call(
        _mla_down_kernel,
        grid=(m // tm,),
        in_specs=[
            pl.BlockSpec((tm, D_MODEL), lambda i: (i, 0)),
            pl.BlockSpec((1, D_MODEL), lambda i: (0, 0)),
            pl.BlockSpec((D_MODEL, n1), lambda i: (0, 0)),
            pl.BlockSpec((1, MLA_Q_RANK), lambda i: (0, 0)),
            pl.BlockSpec((1, MLA_KV_RANK), lambda i: (0, 0)),
            pl.BlockSpec((tm, LANES), lambda i: (i, 0)),
        ],
        out_specs=[
            pl.BlockSpec((tm, MLA_Q_RANK), lambda i: (i, 0)),
            pl.BlockSpec((tm, MLA_KV_RANK), lambda i: (i, 0)),
            pl.BlockSpec((tm, 2 * LANES), lambda i: (i, 0)),
        ],
        out_shape=[
            jax.ShapeDtypeStruct((m, MLA_Q_RANK), BF16),
            jax.ShapeDtypeStruct((m, MLA_KV_RANK), BF16),
            jax.ShapeDtypeStruct((m, 2 * LANES), BF16),
        ],
        compiler_params=_params("parallel"),
        name="mla_down",
    )(x2, gain, w1, q_norm, kv_norm, tk)


MLA_PAIRS = MLA_HEADS // 2
Q_PAIR_IN = 2 * MLA_NOPE + 2 * LANES
Q_PAIR_OUT = 2 * MLA_NOPE + LANES
MLA_Q_PRESCALE = (MLA_NOPE + MLA_ROPE) ** -0.5 * LOG2_E


def _mla_q_up_kernel(a_ref, w_ref, c_ref, s_ref, o_ref):
    a = a_ref[...]
    c = c_ref[...]
    s = s_ref[...]
    for p in range(MLA_PAIRS):
        acc = _dot(a, w_ref[:, p * Q_PAIR_IN:(p + 1) * Q_PAIR_IN])
        nope = acc[:, :2 * MLA_NOPE] * MLA_Q_PRESCALE
        o_ref[:, p * Q_PAIR_OUT:p * Q_PAIR_OUT + 2 * MLA_NOPE] = nope.astype(BF16)
        pe = acc[:, 2 * MLA_NOPE:2 * MLA_NOPE + LANES] * c + acc[:, 2 * MLA_NOPE + LANES:] * s
        o_ref[:, p * Q_PAIR_OUT + 2 * MLA_NOPE:(p + 1) * Q_PAIR_OUT] = (pe * MLA_Q_PRESCALE).astype(BF16)


def _mla_q_up(q_lat, wq, cm, sm):
    m = q_lat.shape[0]
    tm = min(m, TM_MLA_PROJ)
    n_out = MLA_PAIRS * Q_PAIR_OUT
    return pl.pallas_call(
        _mla_q_up_kernel,
        grid=(m // tm,),
        in_specs=[
            pl.BlockSpec((tm, MLA_Q_RANK), lambda i: (i, 0)),
            pl.BlockSpec(wq.shape, lambda i: (0, 0)),
            pl.BlockSpec((tm, LANES), lambda i: (i, 0)),
            pl.BlockSpec((tm, LANES), lambda i: (i, 0)),
        ],
        out_specs=pl.BlockSpec((tm, n_out), lambda i: (i, 0)),
        out_shape=jax.ShapeDtypeStruct((m, n_out), BF16),
        compiler_params=_params("parallel"),
        name="mla_q_up",
    )(q_lat, wq, cm, sm)


def _matmul_bf16_kernel(a_ref, w_ref, o_ref):
    o_ref[...] = _dot(a_ref[...], w_ref[...]).astype(BF16)


def _mla_kv_up(ckv, wkv):
    m = ckv.shape[0]
    tm = min(m, TM_MLA_PROJ)
    n = wkv.shape[1]
    return pl.pallas_call(
        _matmul_bf16_kernel,
        grid=(m // tm,),
        in_specs=[
            pl.BlockSpec((tm, MLA_KV_RANK), lambda i: (i, 0)),
            pl.BlockSpec(wkv.shape, lambda i: (0, 0)),
        ],
        out_specs=pl.BlockSpec((tm, n), lambda i: (i, 0)),
        out_shape=jax.ShapeDtypeStruct((m, n), BF16),
        compiler_params=_params("parallel"),
        name="mla_kv_up",
    )(ckv, wkv)


def _mla_attn_kernel(q_ref, kn_ref, kpe_ref, v_ref, o_ref, m_ref, l_ref, acc_ref, *, tq):
    g = pl.program_id(2)
    row = lax.broadcasted_iota(jnp.int32, (tq, tq), 0)
    col = lax.broadcasted_iota(jnp.int32, (tq, tq), 1)
    causal = col <= row
    heads = (0, 1)
    for sub in range(2):
        qi = 2 * g + sub
        rows = slice(sub * tq, (sub + 1) * tq)
        q_pe = q_ref[rows, 2 * MLA_NOPE:]
        qs = [jnp.concatenate([q_ref[rows, hh * LANES:(hh + 1) * LANES], q_pe], axis=1) for hh in heads]
        m_ref[...] = jnp.full(m_ref.shape, -jnp.inf, F32)
        l_ref[...] = jnp.zeros(l_ref.shape, F32)
        acc_ref[...] = jnp.zeros(acc_ref.shape, F32)

        def step(hh, kb, masked, nkb=1):
            lanes = slice(hh * LANES, (hh + 1) * LANES)
            tk = nkb * tq
            ks = pl.multiple_of(kb * tq, tq)
            k = jnp.concatenate([kn_ref[pl.ds(ks, tk), lanes], kpe_ref[pl.ds(ks, tk), lanes]], axis=1)
            s = _dot_nt(qs[hh], k)
            if masked:
                diag = jnp.where(causal, s[:, tk - tq:], -jnp.inf)
                s = diag if nkb == 1 else jnp.concatenate([s[:, :tk - tq], diag], axis=1)
            m_prev = m_ref[hh]
            m_next = jnp.maximum(m_prev, jnp.max(s, axis=1, keepdims=True))
            m_wide = jnp.concatenate([m_next] * (tk // LANES), axis=1)
            p = jnp.exp2(s - m_wide)
            alpha = jnp.exp2(m_prev - m_next)
            l_ref[hh] = alpha * l_ref[hh] + jnp.sum(p, axis=1, keepdims=True)
            m_ref[hh] = m_next
            acc_ref[hh] = alpha * acc_ref[hh] + _dot(p.astype(BF16), v_ref[pl.ds(ks, tk), lanes])

        def trip(k2, carry):
            for hh in heads:
                step(hh, 2 * k2, False, nkb=2)
            return carry

        lax.fori_loop(0, g, trip, 0)
        for hh in heads:
            step(hh, qi - sub, True, nkb=1 + sub)
        for hh in heads:
            o_ref[rows, hh * LANES:(hh + 1) * LANES] = (acc_ref[hh] / l_ref[hh]).astype(BF16)


def _mla_attention(q, kv, kpe, b, s):
    tq = min(s // 2, TQ_MLA)
    kern = functools.partial(_mla_attn_kernel, tq=tq)
    pair_w = 2 * LANES
    return pl.pallas_call(
        kern,
        grid=(b, MLA_PAIRS, s // (2 * tq)),
        in_specs=[
            pl.BlockSpec((None, 2 * tq, Q_PAIR_OUT), lambda bi, p, g: (bi, g, p)),
            pl.BlockSpec((None, s, pair_w), lambda bi, p, g: (bi, 0, p)),
            pl.BlockSpec((None, s, pair_w), lambda bi, p, g: (bi, 0, 0)),
            pl.BlockSpec((None, s, pair_w), lambda bi, p, g: (bi, 0, MLA_PAIRS + p)),
        ],
        out_specs=pl.BlockSpec((None, 2 * tq, pair_w), lambda bi, p, g: (bi, g, p)),
        out_shape=jax.ShapeDtypeStruct((b, s, MLA_HEADS * MLA_V), BF16),
        scratch_shapes=[
            pltpu.VMEM((2, tq, LANES), F32),
            pltpu.VMEM((2, tq, LANES), F32),
            pltpu.VMEM((2, tq, LANES), F32),
        ],
        compiler_params=_params("parallel", "parallel", "arbitrary"),
        name="mla_attention",
    )(q, kv, kpe, kv)


def _matmul_resid_kernel(a_ref, w_ref, r_ref, *rest, n_cast):
    o_ref = rest[n_cast]
    _run_casts(rest[:n_cast], rest[n_cast + 1:])
    o_ref[...] = r_ref[...] + _dot(a_ref[...], w_ref[...])


def _matmul_resid(a, w, layer, resid, casts=()):
    m, k = a.shape
    n = w.shape[2]
    tm = min(m, TM_SMALL)
    cast_in, cast_out, cast_shapes = _cast_jobs(casts, m // tm, lambda i: i)
    outs = pl.pallas_call(
        functools.partial(_matmul_resid_kernel, n_cast=len(casts)),
        grid=(m // tm,),
        in_specs=[
            pl.BlockSpec((tm, k), lambda i: (i, 0)),
            pl.BlockSpec((None, k, n), lambda i: (layer, 0, 0), pipeline_mode=pl.Buffered(1)),
            pl.BlockSpec((tm, n), lambda i: (i, 0)),
        ] + cast_in,
        out_specs=[pl.BlockSpec((tm, n), lambda i: (i, 0))] + cast_out,
        out_shape=[jax.ShapeDtypeStruct((m, n), F32)] + cast_shapes,
        compiler_params=_params("arbitrary"),
        name="out_proj_resid",
    )(a, w, resid, *[cw for cw, _ in casts])
    return outs[0], outs[1:]


def _dil_rope(xc, c, sa, sb):
    half = DIL_ROT // 2
    return xc * c + pltpu.roll(xc, half, 1) * sa + pltpu.roll(xc, LANES - half, 1) * sb


def _dil_qkv_kernel(x_ref, g_ref, w_ref, c_ref, sa_ref, sb_ref, o_ref, h_ref, acc_ref, *perm_refs,
                    d, tm, n_query, n_rope, nj):
    j = pl.program_id(2)
    rows = tm // d
    if d > 1:
        xs_ref, cs_ref, sas_ref, sbs_ref = perm_refs
    else:
        cs_ref, sas_ref, sbs_ref = c_ref, sa_ref, sb_ref

    def norm_input():
        if d == 1:
            h_ref[...] = _rms(x_ref[...], g_ref[...]).astype(BF16)
            return
        x = x_ref[...]
        rinv = lax.rsqrt(jnp.mean(x * x, axis=-1, keepdims=True) + NORM_EPS)
        for t in range(x_ref.shape[1] // LANES):
            lanes = slice(t * LANES, (t + 1) * LANES)
            slot = t % xs_ref.shape[0]
            xs_ref[slot] = (x_ref[:, lanes] * rinv) * g_ref[:, lanes]
            for r in range(d):
                h_ref[r * rows:(r + 1) * rows, lanes] = xs_ref[slot, pl.ds(r, rows, stride=d), :].astype(BF16)
        for r in range(d):
            sl = pl.ds(r, rows, stride=d)
            dst = slice(r * rows, (r + 1) * rows)
            cs_ref[dst, :] = c_ref[sl, :]
            sas_ref[dst, :] = sa_ref[sl, :]
            sbs_ref[dst, :] = sb_ref[sl, :]

    def matmul():
        acc_ref[...] = _dot(h_ref[...], w_ref[...])

    def emit(rope, query=False):
        for r in range(d):
            src = slice(r * rows, (r + 1) * rows)
            if rope:
                c, sa, sb = cs_ref[src, :], sas_ref[src, :], sbs_ref[src, :]
            for t in range(acc_ref.shape[1] // LANES):
                a = acc_ref[src, t * LANES:(t + 1) * LANES]
                if rope:
                    a = _dil_rope(a, c, sa, sb)
                if query:
                    a = a * DIL_Q_PRESCALE
                o_ref[r, :, t * LANES:(t + 1) * LANES] = a.astype(BF16)

    @pl.when(j == 0)
    def _():
        norm_input()
        matmul()

    @pl.when(jnp.logical_and(j >= 1, j <= n_query))
    def _():
        emit(True, query=True)
        matmul()

    @pl.when(jnp.logical_and(j > n_query, j <= n_rope))
    def _():
        emit(True)
        matmul()

    @pl.when(jnp.logical_and(j > n_rope, j < nj))
    def _():
        emit(False)
        matmul()

    @pl.when(j == nj)
    def _():
        emit(False)


def _dil_qkv(x, gain, w_in, layer, g, cd, sa, sb, d):
    b, s, _ = x.shape
    n = 3 * DIL_HEADS * DIL_HEAD_DIM
    tm = min(s, TM_PROJ)
    tn = TN_PROJ
    col0 = g * (n // tn)
    n_rope = 2 * DIL_HEADS * DIL_HEAD_DIM // tn
    nj = n // tn
    assert n_rope < nj
    n_query = DIL_HEADS * DIL_HEAD_DIM // tn
    kern = functools.partial(_dil_qkv_kernel, d=d, tm=tm, n_query=n_query, n_rope=n_rope, nj=nj)
    ni = s // tm

    def tile(bi, i, j):
        return _prefetch_tile(bi, i, j == nj, b, ni)

    tab = pl.BlockSpec((None, tm, LANES), lambda bi, i, j: (*tile(bi, i, j), 0))
    perm_scratch = []
    if d > 1:
        perm_scratch = [pltpu.VMEM((PERM_SLOTS, tm, LANES), F32)] + [pltpu.VMEM((tm, LANES), F32)] * 3
    return pl.pallas_call(
        kern,
        grid=(b, s // tm, nj + 1),
        in_specs=[
            pl.BlockSpec((None, tm, D_MODEL), lambda bi, i, j: (*tile(bi, i, j), 0)),
            pl.BlockSpec((1, D_MODEL), lambda bi, i, j: (0, 0)),
            pl.BlockSpec((None, D_MODEL, tn), lambda bi, i, j: (layer, 0, col0 + jnp.where(j == nj, 0, j))),
            tab, tab, tab,
        ],
        out_specs=pl.BlockSpec((None, d, tm // d, tn), lambda bi, i, j: (bi, 0, i, jnp.maximum(j - 1, 0))),
        out_shape=jax.ShapeDtypeStruct((b, d, s // d, n), BF16),
        scratch_shapes=[pltpu.VMEM((tm, D_MODEL), BF16), pltpu.VMEM((tm, tn), F32)] + perm_scratch,
        compiler_params=_params("parallel", "parallel", "arbitrary"),
        name=f"dil_qkv_d{d}",
    )(x, gain, w_in, cd, sa, sb)


def _dil_attn_kernel(q_ref, kc_ref, kp_ref, vc_ref, vp_ref, o_ref, lse_ref, s_ref, p_ref, *, nblk):
    jb = pl.program_id(2)
    qb = DIL_BLOCK
    qi = lax.broadcasted_iota(jnp.int32, (qb, 2 * qb), 0)
    ki = lax.broadcasted_iota(jnp.int32, (qb, 2 * qb), 1)
    lane = lax.broadcasted_iota(jnp.int32, (qb, LANES), 1)

    def keys(cur_ref, prev_ref, blk, sl):
        own = cur_ref[blk * qb:(blk + 1) * qb, sl]
        before = prev_ref[:, sl] if blk == 0 else cur_ref[(blk - 1) * qb:blk * qb, sl]
        return jnp.concatenate([before, own], axis=0)

    for blk in range(nblk):
        for h in range(DIL_HEADS):
            sl = slice(h * DIL_HEAD_DIM, (h + 1) * DIL_HEAD_DIM)
            s_ref[blk, h] = _dot_nt(q_ref[blk * qb:(blk + 1) * qb, sl], keys(kc_ref, kp_ref, blk, sl))
    for blk in range(nblk):
        first_key = jnp.where(jb > 0, qi, qb) if blk == 0 else qi
        band = jnp.logical_and(ki >= first_key, ki <= qi + qb)
        s = jnp.where(band[None], s_ref[blk], -jnp.inf)
        m = jnp.max(s, axis=-1, keepdims=True)
        e = jnp.exp2(s - m)
        l = jnp.sum(e, axis=-1, keepdims=True)
        p_ref[blk] = e.astype(BF16)
        inv = 1.0 / l
        lse = LN_2 * m + jnp.log(l)
        lse_all = jnp.zeros((qb, LANES), F32)
        for h in range(DIL_HEADS):
            lse_all = jnp.where(lane == h, lse[h], lse_all)
        lse_ref[blk * qb:(blk + 1) * qb, :] = lse_all
        for h in range(DIL_HEADS):
            sl = slice(h * DIL_HEAD_DIM, (h + 1) * DIL_HEAD_DIM)
            pv = _dot(p_ref[blk, h], keys(vc_ref, vp_ref, blk, sl))
            o_ref[blk * qb:(blk + 1) * qb, sl] = pv * inv[h]


def _dil_attention(qkv_g, d):
    b, _, l, _ = qkv_g.shape
    hw = DIL_HEADS * DIL_HEAD_DIM
    qb = DIL_BLOCK
    nblk = min(DIL_BLOCKS_PER_STEP, l // qb)
    rows = nblk * qb
    kern = functools.partial(_dil_attn_kernel, nblk=nblk)

    def cur(c):
        return pl.BlockSpec((None, None, rows, hw), lambda bi, r, jb: (bi, r, jb, c))

    def prev(c):
        return pl.BlockSpec((None, None, qb, hw), lambda bi, r, jb: (bi, r, jnp.maximum(jb * nblk - 1, 0), c))

    return pl.pallas_call(
        kern,
        grid=(b, d, l // rows),
        in_specs=[cur(0), cur(1), prev(1), cur(2), prev(2)],
        out_specs=[
            pl.BlockSpec((None, None, rows, hw), lambda bi, r, jb: (bi, r, jb, 0)),
            pl.BlockSpec((None, None, rows, LANES), lambda bi, r, jb: (bi, r, jb, 0)),
        ],
        out_shape=[
            jax.ShapeDtypeStruct((b, d, l, hw), F32),
            jax.ShapeDtypeStruct((b, d, l, LANES), F32),
        ],
        scratch_shapes=[
            pltpu.VMEM((nblk, DIL_HEADS, qb, 2 * qb), F32),
            pltpu.VMEM((nblk, DIL_HEADS, qb, 2 * qb), BF16),
        ],
        compiler_params=_params("parallel", "parallel", "arbitrary"),
        name=f"dil_attention_d{d}",
    )(qkv_g, qkv_g, qkv_g, qkv_g, qkv_g)


def _dil_merge_kernel(o0_ref, l0_ref, o1_ref, l1_ref, o2_ref, l2_ref, w_ref, x_ref, out_ref,
                      n1_ref, n2_ref, nl1_ref, nl2_ref, a_ref, *, tm, n):
    i = pl.program_id(1)
    d1 = DIL_GROUPS[1][1]
    d2 = DIL_GROUPS[2][1]

    def merge():
        for d, o_ref, l_ref, n_ref, nl_ref in ((d1, o1_ref, l1_ref, n1_ref, nl1_ref),
                                               (d2, o2_ref, l2_ref, n2_ref, nl2_ref)):
            for r in range(d):
                rows = pl.ds(r, tm // d, stride=d)
                nl_ref[rows, :] = l_ref[r]
                for h in range(DIL_HEADS):
                    n_ref[h, rows, :] = o_ref[r, :, h * DIL_HEAD_DIM:(h + 1) * DIL_HEAD_DIM]
        a0, a1, a2 = l0_ref[...], nl1_ref[...], nl2_ref[...]
        mx = jnp.maximum(jnp.maximum(a0, a1), a2)
        w0 = jnp.exp(a0 - mx)
        w1 = jnp.exp(a1 - mx)
        w2 = jnp.exp(a2 - mx)
        den = w0 + w1 + w2
        w0, w1, w2 = w0 / den, w1 / den, w2 / den
        for h in range(DIL_HEADS):
            sl = slice(h * DIL_HEAD_DIM, (h + 1) * DIL_HEAD_DIM)
            o = w0[:, h:h + 1] * o0_ref[:, sl] + w1[:, h:h + 1] * n1_ref[h] + w2[:, h:h + 1] * n2_ref[h]
            a_ref[:, sl] = o.astype(BF16)

    def project(a):
        out_ref[...] = x_ref[...] + _dot(a, w_ref[...])

    @pl.when(i == 0)
    def _():
        merge()

    @pl.when(jnp.logical_and(i > 0, i < n))
    def _():
        a = a_ref[...]
        merge()
        project(a)

    @pl.when(i == n)
    def _():
        project(a_ref[...])


def _dil_merge_out(outs, lses, wo, layer, x):
    b, s, dm = x.shape
    hw = DIL_HEADS * DIL_HEAD_DIM
    tm = min(s, TM_MERGE)
    d1 = DIL_GROUPS[1][1]
    d2 = DIL_GROUPS[2][1]
    n = s // tm
    kern = functools.partial(_dil_merge_kernel, tm=tm, n=n)

    def merged(i):
        return jnp.minimum(i, n - 1)

    def projected(i):
        return jnp.maximum(i - 1, 0)

    def grp(d, w):
        return pl.BlockSpec((None, d, tm // d, w), lambda bi, i: (bi, 0, merged(i), 0))

    return pl.pallas_call(
        kern,
        grid=(b, n + 1),
        in_specs=[
            pl.BlockSpec((None, None, tm, hw), lambda bi, i: (bi, 0, merged(i), 0)),
            pl.BlockSpec((None, None, tm, LANES), lambda bi, i: (bi, 0, merged(i), 0)),
            grp(d1, hw), grp(d1, LANES), grp(d2, hw), grp(d2, LANES),
            pl.BlockSpec((None, hw, dm), lambda bi, i: (layer, 0, 0)),
            pl.BlockSpec((None, tm, dm), lambda bi, i: (bi, projected(i), 0)),
        ],
        out_specs=pl.BlockSpec((None, tm, dm), lambda bi, i: (bi, projected(i), 0)),
        out_shape=jax.ShapeDtypeStruct((b, s, dm), F32),
        scratch_shapes=[
            pltpu.VMEM((DIL_HEADS, tm, DIL_HEAD_DIM), F32), pltpu.VMEM((DIL_HEADS, tm, DIL_HEAD_DIM), F32),
            pltpu.VMEM((tm, LANES), F32), pltpu.VMEM((tm, LANES), F32),
            pltpu.VMEM((tm, hw), BF16),
        ],
        compiler_params=_params("parallel", "arbitrary"),
        name="dil_merge_out",
    )(outs[0], lses[0], outs[1], lses[1], outs[2], lses[2], wo, x)


def _ffn_kernel(x_ref, g_ref, og_ref, wg_ref, wv_ref, cp_ref, wd_ref, *rest, tm, tf, nf, ni, out_norm, n_cast):
    cast_src = rest[:n_cast]
    o_ref = rest[n_cast]
    cast_dst = rest[n_cast + 1:2 * n_cast + 1]
    h_ref, u_ref, prev_ref = rest[2 * n_cast + 1:]
    k = pl.program_id(1)
    t = lax.rem(k, nf)
    c_down = lax.rem(k + nf - 1, nf)
    row = lax.broadcasted_iota(jnp.int32, (CARRY_ROWS, 1), 0)

    def side_jobs():
        _run_casts(cast_src, cast_dst)

    def up():
        h = h_ref[...]
        u_ref[:, :tf] = _dot(h, wg_ref[...])
        u_ref[:, tf:] = _dot(h, wv_ref[...])

    def start_tile():
        h_ref[...] = _rms(x_ref[...], g_ref[...]).astype(BF16)
        up()

    def conv(u):
        prev = prev_ref[c_down]
        prev_ref[c_down] = u[tm - CARRY_ROWS:, :]
        p1 = prev[CARRY_ROWS - 1:CARRY_ROWS, :]
        p2 = prev[CARRY_ROWS - 2:CARRY_ROWS - 1, :]
        r1 = pltpu.roll(u, 1, 0)
        r2 = pltpu.roll(u, 2, 0)
        u1 = jnp.concatenate([jnp.where(row == 0, p1, r1[:CARRY_ROWS]), r1[CARRY_ROWS:]], axis=0)
        u2 = jnp.concatenate(
            [jnp.where(row == 0, p2, jnp.where(row == 1, p1, r2[:CARRY_ROWS])), r2[CARRY_ROWS:]], axis=0)
        cp = cp_ref[...]
        acc = cp[CONV_WIDTH:CONV_WIDTH + 1, :] + u2 * cp[0:1, :]
        acc = acc + u1 * cp[1:2, :]
        return acc + u * cp[2:3, :]

    def down(u, first_chunk):
        cv = conv(u)
        gate = cv[:, :tf]
        val = cv[:, tf:]
        act = (gate * (1.0 / (1.0 + jnp.exp(-gate)))) * val
        d = _dot(act.astype(BF16), wd_ref[...])
        if first_chunk:
            o_ref[...] = x_ref[...] + d
        else:
            o_ref[...] += d

    def finish_tile():
        if out_norm:
            o_ref[...] = _rms(o_ref[...], og_ref[...])

    @pl.when(k == 0)
    def _():
        side_jobs()
        prev_ref[...] = jnp.zeros(prev_ref.shape, F32)
        start_tile()

    @pl.when(jnp.logical_and(t == 0, jnp.logical_and(k > 0, k < ni * nf)))
    def _():
        side_jobs()
        u = u_ref[...]
        start_tile()
        down(u, False)
        finish_tile()

    @pl.when(t == 1)
    def _():
        side_jobs()
        u = u_ref[...]
        up()
        down(u, True)

    @pl.when(t >= 2)
    def _():
        side_jobs()
        u = u_ref[...]
        up()
        down(u, False)

    @pl.when(k == ni * nf)
    def _():
        side_jobs()
        down(u_ref[...], False)
        finish_tile()


def _ffn_chunk_order(a, tf):
    lead = a.shape[:-1]
    return a.reshape(*lead, 2, FFN_HIDDEN // tf, tf).swapaxes(-3, -2).reshape(*lead, 2 * FFN_HIDDEN)


def _ffn_conv_params(conv_w, conv_b):
    pad = jnp.zeros((conv_w.shape[0], SUBLANES - CONV_WIDTH - 1, conv_w.shape[2]), F32)
    conv_p = jnp.concatenate([conv_w, conv_b[:, None, :], pad], axis=1)
    return _ffn_chunk_order(conv_p, TF_FFN)


def _ffn(x, gain, out_gain, w_up, up_layer, conv_p, conv_layer, w_down, down_layer, out_norm, casts):
    b, s, _ = x.shape
    tm = min(s, TM_FFN)
    tf = TF_FFN
    nf = FFN_HIDDEN // tf
    ni = s // tm
    assert nf >= 2
    nk = ni * nf + 1
    kern = functools.partial(_ffn_kernel, tm=tm, tf=tf, nf=nf, ni=ni, out_norm=out_norm, n_cast=len(casts))

    def in_tile(k):
        return jnp.minimum(lax.div(k, nf), ni - 1)

    def out_tile(k):
        return lax.div(jnp.maximum(k - 1, 0), nf)

    def up_chunk(k):
        return lax.rem(k, nf)

    def down_chunk(k):
        return lax.rem(k + nf - 1, nf)

    cast_in, cast_out, cast_shapes = _cast_jobs(casts, b * nk, lambda bi, k: bi * nk + k)

    outs = pl.pallas_call(
        kern,
        grid=(b, nk),
        in_specs=[
            pl.BlockSpec((None, tm, D_MODEL), lambda bi, k: (bi, in_tile(k), 0)),
            pl.BlockSpec((1, D_MODEL), lambda bi, k: (0, 0)),
            pl.BlockSpec((1, D_MODEL), lambda bi, k: (0, 0)),
            pl.BlockSpec((None, D_MODEL, tf), lambda bi, k: (up_layer, 0, up_chunk(k))),
            pl.BlockSpec((None, D_MODEL, tf), lambda bi, k: (up_layer, 0, nf + up_chunk(k))),
            pl.BlockSpec((None, SUBLANES, 2 * tf), lambda bi, k: (conv_layer, 0, down_chunk(k))),
            pl.BlockSpec((None, tf, D_MODEL), lambda bi, k: (down_layer, down_chunk(k), 0)),
        ] + cast_in,
        out_specs=[pl.BlockSpec((None, tm, D_MODEL), lambda bi, k: (bi, out_tile(k), 0))] + cast_out,
        out_shape=[jax.ShapeDtypeStruct(x.shape, F32)] + cast_shapes,
        scratch_shapes=[
            pltpu.VMEM((tm, D_MODEL), BF16),
            pltpu.VMEM((tm, 2 * tf), F32),
            pltpu.VMEM((nf, CARRY_ROWS, 2 * tf), F32),
        ],
        compiler_params=_params("arbitrary", "arbitrary"),
        name="conv_ffn",
    )(x, gain, out_gain, w_up, w_up, conv_p, w_down, *[w for w, _ in casts])
    return outs[0], outs[1:]


def _rotate_half_cols(w):
    half = w.shape[-1] // 2
    return jnp.concatenate([-w[..., half:], w[..., :half]], axis=-1)


def _mla_weights(wq_a, wq_b, wkv_a, wkv_b):
    w_pe = wkv_a[:, MLA_KV_RANK:]
    w1 = jnp.concatenate([wq_a, wkv_a[:, :MLA_KV_RANK], w_pe, _rotate_half_cols(w_pe)], axis=1)
    qb = wq_b.reshape(MLA_Q_RANK, MLA_HEADS, MLA_NOPE + MLA_ROPE)
    nope = qb[:, :, :MLA_NOPE].reshape(MLA_Q_RANK, MLA_PAIRS, 2 * MLA_NOPE)
    pe = qb[:, :, MLA_NOPE:]
    rot = _rotate_half_cols(pe).reshape(MLA_Q_RANK, MLA_PAIRS, LANES)
    pe = pe.reshape(MLA_Q_RANK, MLA_PAIRS, LANES)
    wq = jnp.concatenate([nope, pe, rot], axis=2).reshape(MLA_Q_RANK, MLA_PAIRS * Q_PAIR_IN)
    kvb = wkv_b.reshape(MLA_KV_RANK, MLA_HEADS, MLA_NOPE + MLA_V)
    wkv = jnp.concatenate([kvb[:, :, :MLA_NOPE].reshape(MLA_KV_RANK, -1),
                           kvb[:, :, MLA_NOPE:].reshape(MLA_KV_RANK, -1)], axis=1)
    return w1.astype(BF16), wq.astype(BF16), wkv.astype(BF16)


def _mla_layer(x, gain, tabs, wq_a, q_norm, wq_b, wkv_a, kv_norm, wkv_b, wo, layer, casts=()):
    b, s, dm = x.shape
    cm, sm, tk = tabs
    w1, wq, wkv = _mla_weights(wq_a, wq_b, wkv_a, wkv_b)
    x2 = x.reshape(b * s, dm)
    q_lat, ckv, kpe = _mla_down(x2, gain.reshape(1, dm), w1, q_norm.reshape(1, -1), kv_norm.reshape(1, -1), tk)
    q = _mla_q_up(q_lat, wq, cm, sm)
    kv = _mla_kv_up(ckv, wkv)
    o = _mla_attention(q.reshape(b, s, -1), kv.reshape(b, s, -1), kpe.reshape(b, s, -1), b, s)
    out, cast = _matmul_resid(o.reshape(b * s, -1), wo, layer, x2, casts)
    return out.reshape(b, s, dm), cast


def _dil_layer(x, gain, tabs, w_in, wo, layer):
    b, s, dm = x.shape
    cd, sa, sb = (t.reshape(b, s, LANES) for t in tabs)
    outs, lses = [], []
    for g, (_, d) in enumerate(DIL_GROUPS):
        qkv_g = _dil_qkv(x, gain.reshape(1, dm), w_in, 0, g, cd, sa, sb, d)
        o, lse = _dil_attention(qkv_g, d)
        outs.append(o)
        lses.append(lse)
    return _dil_merge_out(outs, lses, wo, layer, x)


def kernel(x, positions, attn_norm, ffn_norm, final_norm, mla_wq_a, mla_q_norm, mla_wq_b, mla_wkv_a,
           mla_kv_norm, mla_wkv_b, mla_wo, dil_w_in, dil_wo, ffn_w_up, ffn_conv_w, ffn_conv_b, ffn_w_down):
    b, s, dm = x.shape
    cm, sm, tk, cd, sa, sb = _rope_tables(positions)
    mla_wo_b = mla_wo.astype(BF16)
    dil_wo_b = dil_wo.astype(BF16)
    ffn_conv_p = _ffn_conv_params(ffn_conv_w, ffn_conv_b)
    up_b = down_b = w_in_b = None
    for i in range(DEPTH):
        j = i // N_MIXERS
        if i % N_MIXERS == 0:
            first = [(ffn_w_up, 0), (ffn_w_down, 0)] if i == 0 else []
            x, cast = _mla_layer(x, attn_norm[i], (cm, sm, tk), mla_wq_a[j], mla_q_norm[j], mla_wq_b[j],
                                 mla_wkv_a[j], mla_kv_norm[j], mla_wkv_b[j], mla_wo_b, j, first)
            if first:
                up_b, down_b = cast
        else:
            x = _dil_layer(x, attn_norm[i], (cd, sa, sb), w_in_b, dil_wo_b, j)
        last = i == DEPTH - 1
        casts = [] if last else [(ffn_w_up, i + 1), (ffn_w_down, i + 1)]
        if not last and (i + 1) % N_MIXERS == 1:
            casts.append((dil_w_in, (i + 1) // N_MIXERS))
        x, cast = _ffn(x, ffn_norm[i].reshape(1, dm), final_norm.reshape(1, dm), up_b, 0, ffn_conv_p, i,
                       down_b, 0, last, casts)
        if not last:
            up_b, down_b = cast[0], cast[1]
            w_in_b = cast[2] if len(cast) > 2 else None
    return x
```
